```python
import jax, jax.numpy as jnp
from jax import lax
import numpy as np

D_MODEL = 1024
BATCH = 8
SEQ = 4096
DEPTH = 4

MIX_WIDTH = D_MODEL
A_WIDTH = MIX_WIDTH // 2
B_WIDTH = MIX_WIDTH - A_WIDTH
A_HEADS = 8
A_HEAD_DIM = A_WIDTH // A_HEADS
B_HEADS = 8
CHUNK = 128
CONV_WIDTH = 3
IN_COLS = 2 * A_WIDTH + 3 * B_WIDTH
POOL_WINDOWS = (2, 4, 8, 16)
POOL_GROUPS = len(POOL_WINDOWS)
POOL_GROUP_DIM = D_MODEL // POOL_GROUPS
N_GROUPS = 4
EXPERTS_PER_GROUP = 8
N_EXPERTS = N_GROUPS * EXPERTS_PER_GROUP
TOP_K = 2
D_EXPERT = D_MODEL // 2
DISPATCH_BLOCK = 128
EPS = 1e-6
N_EVEN = (DEPTH + 1) // 2
N_ODD = DEPTH // 2

kernel_name = "hybrid_sgu_conv_pool_hmoe_adaln"


def rmsnorm(x, g):
    xf = x.astype(jnp.float32)
    y = xf * lax.rsqrt(jnp.mean(xf * xf, axis=-1, keepdims=True) + EPS)
    return (y * g.astype(jnp.float32)).astype(x.dtype)


def chunk_sgu(u, v, g_v, w_s, b_s):
    bsz, s, _ = v.shape
    n_chunks = s // CHUNK
    vh = v.reshape(bsz, n_chunks, CHUNK, A_HEADS, A_HEAD_DIM)
    vh = rmsnorm(vh, g_v.reshape(A_HEADS, A_HEAD_DIM))
    causal = jnp.tril(jnp.ones((CHUNK, CHUNK), dtype=bool))
    w = jnp.where(causal[None], w_s, 0).astype(v.dtype)
    sv = jnp.einsum('hts,bnshd->bnthd', w, vh)
    sv = sv + b_s.T[None, None, :, :, None].astype(v.dtype)
    return u * sv.reshape(bsz, s, A_WIDTH)


def short_gated_conv(b_gate, c_gate, x_in, conv_w):
    z = c_gate * x_in
    s = z.shape[1]
    zp = jnp.pad(z, ((0, 0), (CONV_WIDTH - 1, 0), (0, 0)))
    conv = sum(conv_w[k] * zp[:, k:k + s] for k in range(CONV_WIDTH))
    return b_gate * conv


def pool_mixer(h, w_pool, pool_scale):
    s = h.shape[1]
    hf = h.astype(jnp.float32)
    csum = jnp.cumsum(hf, axis=1)
    t = jnp.arange(s)
    outs = []
    for g, win in enumerate(POOL_WINDOWS):
        sl = slice(g * POOL_GROUP_DIM, (g + 1) * POOL_GROUP_DIM)
        cg = csum[..., sl]
        lagged = jnp.pad(cg, ((0, 0), (win, 0), (0, 0)))[:, :s]
        count = jnp.minimum(t + 1, win).astype(jnp.float32)[None, :, None]
        pooled = (cg - lagged) / count - hf[..., sl]
        outs.append(pooled.astype(h.dtype) @ w_pool[g])
    return jnp.concatenate(outs, axis=-1) * pool_scale


def routed_experts(hf, expert_id, gate, w_gate, w_up, w_down):
    t_tok, d = hf.shape
    n_assign = t_tok * TOP_K
    flat_e = expert_id.reshape(n_assign)
    flat_g = gate.reshape(n_assign)
    flat_tok = jnp.repeat(jnp.arange(t_tok, dtype=jnp.int32), TOP_K)
    order = jnp.argsort(flat_e)
    sorted_e = flat_e[order]
    counts = jnp.bincount(flat_e, length=N_EXPERTS)
    starts = jnp.cumsum(counts) - counts
    padded = (counts + DISPATCH_BLOCK - 1) // DISPATCH_BLOCK * DISPATCH_BLOCK
    pad_ends = jnp.cumsum(padded)
    pad_starts = pad_ends - padded
    dest = pad_starts[sorted_e] + jnp.arange(n_assign) - starts[sorted_e]
    n_blocks = (n_assign + DISPATCH_BLOCK - 1) // DISPATCH_BLOCK + N_EXPERTS
    n_rows = n_blocks * DISPATCH_BLOCK
    buf_tok = jnp.full((n_rows,), t_tok, jnp.int32).at[dest].set(flat_tok[order])
    buf_gate = jnp.zeros((n_rows,), hf.dtype).at[dest].set(flat_g[order].astype(hf.dtype))
    blk_expert = jnp.minimum(
        jnp.searchsorted(pad_ends, jnp.arange(n_blocks) * DISPATCH_BLOCK, side='right'),
        N_EXPERTS - 1)
    x_pad = jnp.concatenate([hf, jnp.zeros((1, d), hf.dtype)], axis=0)
    xb = x_pad[buf_tok].reshape(n_blocks, DISPATCH_BLOCK, d)

    def run_block(args):
        xblk, e = args
        a = xblk @ w_gate[e]
        b = xblk @ w_up[e]
        return (jax.nn.silu(a) * b) @ w_down[e]

    yb = lax.map(run_block, (xb, blk_expert)).reshape(n_rows, d)
    out = jax.ops.segment_sum(yb * buf_gate[:, None], buf_tok, num_segments=t_tok + 1)
    return out[:t_tok]


def hier_moe(h, w_gr, b_gr, w_er, b_er, w_gate, w_up, w_down):
    bsz, s, d = h.shape
    t_tok = bsz * s
    hf = h.reshape(t_tok, d)
    rows = jnp.arange(t_tok)
    g_logits = (hf @ w_gr + b_gr).astype(jnp.float32)
    g_prob = jax.nn.softmax(g_logits, axis=-1)
    g_sel = jnp.argmax(g_logits, axis=-1).astype(jnp.int32)
    g_w = g_prob[rows, g_sel][:, None]
    e_logits = (hf @ w_er + b_er).astype(jnp.float32).reshape(t_tok, N_GROUPS, EXPERTS_PER_GROUP)
    e_logits = e_logits[rows, g_sel]
    top_v, top_i = lax.top_k(e_logits, TOP_K)
    gate = g_w * jax.nn.softmax(top_v, axis=-1)
    expert_id = g_sel[:, None] * EXPERTS_PER_GROUP + top_i.astype(jnp.int32)
    y = routed_experts(hf, expert_id, gate, w_gate, w_up, w_down)
    return y.reshape(bsz, s, d)


def setup_inputs(seed: int = 0) -> dict:
    key = jax.random.key(seed)
    ks = jax.random.split(key, 24)
    D = D_MODEL

    def nrm(k, shape, scale):
        return jax.random.normal(k, shape, jnp.float32) * scale

    gate_offset = jnp.concatenate([jnp.zeros((2 * D,), jnp.float32), jnp.ones((D,), jnp.float32),
                                   jnp.zeros((2 * D,), jnp.float32), jnp.ones((D,), jnp.float32)])
    return {
        "x": nrm(ks[0], (BATCH, SEQ, D), 1.0),
        "c": nrm(ks[1], (BATCH, D), 1.0),
        "w_ada": nrm(ks[2], (DEPTH, D, 6 * D), 0.25 * D ** -0.5),
        "b_ada": nrm(ks[3], (DEPTH, 6 * D), 0.02) + gate_offset,
        "norm_mix_g": 1.0 + nrm(ks[4], (DEPTH, D), 0.02),
        "norm_ffn_g": 1.0 + nrm(ks[5], (DEPTH, D), 0.02),
        "w_in_even": nrm(ks[6], (N_EVEN, D, IN_COLS), D ** -0.5),
        "sgu_norm_g": 1.0 + nrm(ks[7], (N_EVEN, A_WIDTH), 0.02),
        "w_spatial": nrm(ks[8], (N_EVEN, A_HEADS, CHUNK, CHUNK), CHUNK ** -0.5),
        "b_spatial": 1.0 + nrm(ks[9], (N_EVEN, A_HEADS, CHUNK), 0.02),
        "conv_w": nrm(ks[10], (N_EVEN, CONV_WIDTH, B_WIDTH), CONV_WIDTH ** -0.5),
        "w_out_even": nrm(ks[11], (N_EVEN, MIX_WIDTH, D), MIX_WIDTH ** -0.5),
        "w_pool": nrm(ks[12], (N_ODD, POOL_GROUPS, POOL_GROUP_DIM, POOL_GROUP_DIM), POOL_GROUP_DIM ** -0.5),
        "pool_scale": 1.0 + nrm(ks[13], (N_ODD, D), 0.02),
        "w_group_router": nrm(ks[14], (DEPTH, D, N_GROUPS), D ** -0.5),
        "b_group_router": nrm(ks[15], (DEPTH, N_GROUPS), 0.01),
        "w_expert_router": nrm(ks[16], (DEPTH, D, N_EXPERTS), D ** -0.5),
        "b_expert_router": nrm(ks[17], (DEPTH, N_EXPERTS), 0.01),
        "moe_w_gate": nrm(ks[18], (DEPTH, N_EXPERTS, D, D_EXPERT), D ** -0.5),
        "moe_w_up": nrm(ks[19], (DEPTH, N_EXPERTS, D, D_EXPERT), D ** -0.5),
        "moe_w_down": nrm(ks[20], (DEPTH, N_EXPERTS, D_EXPERT, D), D_EXPERT ** -0.5),
        "final_norm_g": 1.0 + nrm(ks[21], (D,), 0.02),
    }


def reference(x, c, w_ada, b_ada, norm_mix_g, norm_ffn_g, w_in_even, sgu_norm_g,
              w_spatial, b_spatial, conv_w, w_out_even, w_pool, pool_scale,
              w_group_router, b_group_router, w_expert_router, b_expert_router,
              moe_w_gate, moe_w_up, moe_w_down, final_norm_g):
    c_act = jax.nn.silu(c)
    split_pts = [A_WIDTH, 2 * A_WIDTH, 2 * A_WIDTH + B_WIDTH, 2 * A_WIDTH + 2 * B_WIDTH]
    for l in range(DEPTH):
        mod = c_act @ w_ada[l] + b_ada[l]
        sh_m, sc_m, g_m, sh_f, sc_f, g_f = [m[:, None, :] for m in jnp.split(mod, 6, axis=-1)]
        h = rmsnorm(x, norm_mix_g[l]) * (1 + sc_m) + sh_m
        i = l // 2
        if l % 2 == 0:
            z = h @ w_in_even[i]
            u, v, b_gate, c_gate, x_in = jnp.split(z, split_pts, axis=-1)
            y_a = chunk_sgu(u, v, sgu_norm_g[i], w_spatial[i], b_spatial[i])
            y_b = short_gated_conv(b_gate, c_gate, x_in, conv_w[i])
            y = jnp.concatenate([y_a, y_b], axis=-1) @ w_out_even[i]
        else:
            y = pool_mixer(h, w_pool[i], pool_scale[i])
        x = x + g_m * y
        h = rmsnorm(x, norm_ffn_g[l]) * (1 + sc_f) + sh_f
        y = hier_moe(h, w_group_router[l], b_group_router[l], w_expert_router[l],
                     b_expert_router[l], moe_w_gate[l], moe_w_up[l], moe_w_down[l])
        x = x + g_f * y
    return rmsnorm(x, final_norm_g)
```

```python
import functools

import jax
import jax.numpy as jnp
from jax import lax
from jax.experimental import pallas as pl
from jax.experimental.pallas import tpu as pltpu

F32 = jnp.float32
BF16 = jnp.bfloat16

EPS = 1e-6
LANES = 128
SUBLANES = 8
CHUNK = 128
A_HEADS = 8
N_GROUPS = 4
EXPERTS_PER_GROUP = 8
N_EXPERTS = N_GROUPS * EXPERTS_PER_GROUP
POOL_WINDOWS = (2, 4, 8, 16)
CONV_WIDTH = 3
ROUTE_COL0 = N_GROUPS

TM = 512
EXPERT_BLK = 256
DISPATCH_TILE = 1024
COMBINE_TILE = 512
MOD_TN = 1536
VMEM_LIMIT = 56 * 1024 * 1024


def _rms(x):
    return x * lax.rsqrt(jnp.mean(x * x, axis=-1, keepdims=True) + EPS)


def _dot(a, b):
    return jnp.dot(a, b, preferred_element_type=F32)


def _mod_kernel(c_ref, w_ref, b_ref, o_ref):
    c = c_ref[...]
    ca = c * jax.nn.sigmoid(c)
    o_ref[0] = jnp.dot(ca, w_ref[0], precision=lax.Precision.HIGHEST,
                       preferred_element_type=F32) + b_ref[0]


def _modulation(c, w_ada, b_ada):
    depth, d, n = w_ada.shape
    bsz = c.shape[0]
    return pl.pallas_call(
        _mod_kernel,
        grid=(depth, n // MOD_TN),
        in_specs=[
            pl.BlockSpec((bsz, d), lambda l, j: (0, 0)),
            pl.BlockSpec((1, d, MOD_TN), lambda l, j: (l, 0, j)),
            pl.BlockSpec((1, 1, MOD_TN), lambda l, j: (l, 0, j)),
        ],
        out_specs=pl.BlockSpec((1, bsz, MOD_TN), lambda l, j: (l, 0, j)),
        out_shape=jax.ShapeDtypeStruct((depth, bsz, n), F32),
        compiler_params=pltpu.CompilerParams(
            dimension_semantics=("arbitrary", "arbitrary"),
            vmem_limit_bytes=VMEM_LIMIT),
        name="modulation",
    )(c, w_ada, b_ada.reshape(depth, 1, n))


def _route(x_new, mod_ref, gffn_ref, rw_hi_ref, rw_lo_ref, rb_ref, ltri_ref,
           h2_ref, route_ref, cnt_ref, carry_scr):
    d = x_new.shape[-1]
    sh_f = mod_ref[0, 3:4, :]
    sc_f = mod_ref[0, 4:5, :]
    h2 = _rms(x_new) * gffn_ref[...] * (1.0 + sc_f) + sh_f
    for c in range(d // LANES):
        h2_ref[:, c, :] = h2[:, c * LANES:(c + 1) * LANES]

    hh = h2.astype(BF16)
    hl = (h2 - hh.astype(F32)).astype(BF16)
    w_hi = rw_hi_ref[...]
    logits = _dot(hh, w_hi) + _dot(hl, w_hi) + _dot(hh, rw_lo_ref[...]) + rb_ref[...]

    lane = lax.broadcasted_iota(jnp.int32, logits.shape, 1).astype(F32)
    neg = jnp.float32(-jnp.inf)
    big = jnp.float32(LANES)

    gl = jnp.where(lane < N_GROUPS, logits, neg)
    gmax = jnp.max(gl, axis=-1, keepdims=True)
    g_sel = jnp.min(jnp.where(gl == gmax, lane, big), axis=-1, keepdims=True)
    g_w = 1.0 / jnp.sum(jnp.exp(gl - gmax), axis=-1, keepdims=True)

    lo = ROUTE_COL0 + EXPERTS_PER_GROUP * g_sel
    el = jnp.where((lane >= lo) & (lane < lo + EXPERTS_PER_GROUP), logits, neg)
    m1 = jnp.max(el, axis=-1, keepdims=True)
    i1 = jnp.min(jnp.where(el == m1, lane, big), axis=-1, keepdims=True)
    el2 = jnp.where(lane == i1, neg, el)
    m2 = jnp.max(el2, axis=-1, keepdims=True)
    i2 = jnp.min(jnp.where(el2 == m2, lane, big), axis=-1, keepdims=True)
    t = jnp.exp(m2 - m1)
    gate1 = g_w / (1.0 + t)
    gate2 = g_w * t / (1.0 + t)

    is1 = lane == i1
    is2 = lane == i2
    onehot = jnp.where(is1 | is2, 1.0, 0.0)
    before = _dot(ltri_ref[...], onehot.astype(BF16)) + carry_scr[...]
    rank1 = jnp.sum(jnp.where(is1, before, 0.0), axis=-1, keepdims=True)
    rank2 = jnp.sum(jnp.where(is2, before, 0.0), axis=-1, keepdims=True)
    carry = carry_scr[...] + jnp.sum(onehot, axis=0, keepdims=True)
    carry_scr[...] = carry
    cnt_ref[...] = carry

    route = jnp.where(lane == 0, i1 - ROUTE_COL0, 0.0)
    route = jnp.where(lane == 1, i2 - ROUTE_COL0, route)
    route = jnp.where(lane == 2, rank1, route)
    route = jnp.where(lane == 3, rank2, route)
    route = jnp.where(lane == 4, gate1, route)
    route = jnp.where(lane == 5, gate2, route)
    route_ref[...] = route


def _even_kernel(tiles_per_seq,
                 x_ref, mod_ref, gmix_ref, gffn_ref, win_ref, gv_ref, ws_ref, bs_ref,
                 cw_ref, wout_ref, rw_hi_ref, rw_lo_ref, rb_ref, ltri_ref,
                 xo_ref, h2_ref, route_ref, cnt_ref,
                 zc_scr, carry_scr):
    i = pl.program_id(0)
    tm = x_ref.shape[0]
    aw = gv_ref.shape[0]
    hd = aw // A_HEADS
    n_chunks = tm // CHUNK

    @pl.when(i == 0)
    def _():
        carry_scr[...] = jnp.zeros_like(carry_scr)
        zc_scr[...] = jnp.zeros_like(zc_scr)

    x = x_ref[...]
    sh_m = mod_ref[0, 0:1, :]
    sc_m = mod_ref[0, 1:2, :]
    g_m = mod_ref[0, 2:3, :]
    h = _rms(x) * gmix_ref[...] * (1.0 + sc_m) + sh_m
    z = _dot(h.astype(BF16), win_ref[...])
    u = z[:, 0:aw]
    v = z[:, aw:2 * aw]
    b_gate = z[:, 2 * aw:3 * aw]
    c_gate = z[:, 3 * aw:4 * aw]
    x_in = z[:, 4 * aw:5 * aw]

    v_t = v.T
    row = lax.broadcasted_iota(jnp.int32, (CHUNK, CHUNK), 0)
    col = lax.broadcasted_iota(jnp.int32, (CHUNK, CHUNK), 1)
    causal = col <= row
    head_rows = []
    for hh in range(A_HEADS):
        vh = v_t[hh * hd:(hh + 1) * hd, :]
        msv = jnp.mean(vh * vh, axis=0, keepdims=True)
        vn = (vh * lax.rsqrt(msv + EPS) * gv_ref[hh * hd:(hh + 1) * hd, :]).astype(BF16)
        lhs = jnp.concatenate(
            [vn[:, c * CHUNK:(c + 1) * CHUNK] for c in range(n_chunks)], axis=0)
        w_m = jnp.where(causal, ws_ref[hh], 0.0).astype(BF16)
        sv_h = lax.dot_general(lhs, w_m, (((1,), (1,)), ((), ())),
                               preferred_element_type=F32)
        sv_h = sv_h + bs_ref[hh]
        head_rows.append(jnp.concatenate(
            [sv_h[c * hd:(c + 1) * hd, :] for c in range(n_chunks)], axis=1))
    sv = jnp.concatenate(head_rows, axis=0).T
    y_a = u * sv

    zc = c_gate * x_in
    first = (i % tiles_per_seq) == 0
    halo = zc_scr[tm:tm + SUBLANES, :]
    zc_scr[0:SUBLANES, :] = jnp.where(first, 0.0, halo)
    zc_scr[SUBLANES:SUBLANES + tm, :] = zc
    conv = cw_ref[2:3, :] * zc
    for k in range(CONV_WIDTH - 1):
        shift = CONV_WIDTH - 1 - k
        conv = conv + cw_ref[k:k + 1, :] * zc_scr[SUBLANES - shift:SUBLANES - shift + tm, :]
    y_b = b_gate * conv

    y = _dot(y_a.astype(BF16), wout_ref[0:aw, :]) + _dot(y_b.astype(BF16), wout_ref[aw:, :])
    x_new = x + g_m * y
    xo_ref[...] = x_new
    _route(x_new, mod_ref, gffn_ref, rw_hi_ref, rw_lo_ref, rb_ref, ltri_ref,
           h2_ref, route_ref, cnt_ref, carry_scr)


def _odd_kernel(tiles_per_seq,
                x_ref, mod_ref, gmix_ref, gffn_ref, wpool_ref, pscale_ref,
                rw_hi_ref, rw_lo_ref, rb_ref, ltri_ref,
                xo_ref, h2_ref, route_ref, cnt_ref,
                h_scr, carry_scr):
    i = pl.program_id(0)
    tm, d = x_ref.shape
    halo_rows = max(POOL_WINDOWS)
    gd = d // len(POOL_WINDOWS)

    @pl.when(i == 0)
    def _():
        carry_scr[...] = jnp.zeros_like(carry_scr)
        h_scr[...] = jnp.zeros_like(h_scr)

    x = x_ref[...]
    sh_m = mod_ref[0, 0:1, :]
    sc_m = mod_ref[0, 1:2, :]
    g_m = mod_ref[0, 2:3, :]
    h = _rms(x) * gmix_ref[...] * (1.0 + sc_m) + sh_m

    tile_in_seq = i % tiles_per_seq
    first = tile_in_seq == 0
    halo = h_scr[tm:tm + halo_rows, :]
    h_scr[0:halo_rows, :] = jnp.where(first, 0.0, halo)
    h_scr[halo_rows:halo_rows + tm, :] = h

    pos = (tile_in_seq * tm + lax.broadcasted_iota(jnp.int32, (tm, 1), 0)).astype(F32)
    outs = []
    for g, win in enumerate(POOL_WINDOWS):
        cs = slice(g * gd, (g + 1) * gd)
        acc = h[:, cs]
        for j in range(1, win):
            acc = acc + h_scr[halo_rows - j:halo_rows - j + tm, cs]
        count = jnp.minimum(pos + 1.0, jnp.float32(win))
        pooled = acc / count - h[:, cs]
        outs.append(_dot(pooled.astype(BF16), wpool_ref[g]))
    y = jnp.concatenate(outs, axis=-1) * pscale_ref[...]
    x_new = x + g_m * y
    xo_ref[...] = x_new
    _route(x_new, mod_ref, gffn_ref, rw_hi_ref, rw_lo_ref, rb_ref, ltri_ref,
           h2_ref, route_ref, cnt_ref, carry_scr)


def _mixer_call(kernel_fn, x, mod_l, seq, weights, scratch):
    t_tok, d = x.shape
    n_tiles = t_tok // TM
    tiles_per_seq = seq // TM

    def const_spec(a):
        return pl.BlockSpec(a.shape, lambda i, nd=a.ndim: (0,) * nd)

    in_specs = [
        pl.BlockSpec((TM, d), lambda i: (i, 0)),
        pl.BlockSpec((1,) + mod_l.shape[1:], lambda i: (i // tiles_per_seq, 0, 0)),
    ] + [const_spec(w) for w in weights]
    out_shape = (
        jax.ShapeDtypeStruct((t_tok, d), F32),
        jax.ShapeDtypeStruct((t_tok, d // LANES, LANES), F32),
        jax.ShapeDtypeStruct((t_tok, LANES), F32),
        jax.ShapeDtypeStruct((1, LANES), F32),
    )
    out_specs = (
        pl.BlockSpec((TM, d), lambda i: (i, 0)),
        pl.BlockSpec((TM, d // LANES, LANES), lambda i: (i, 0, 0)),
        pl.BlockSpec((TM, LANES), lambda i: (i, 0)),
        pl.BlockSpec((1, LANES), lambda i: (0, 0)),
    )
    return pl.pallas_call(
        functools.partial(kernel_fn, tiles_per_seq),
        grid=(n_tiles,),
        in_specs=in_specs,
        out_specs=out_specs,
        out_shape=out_shape,
        scratch_shapes=scratch + [pltpu.VMEM((1, LANES), F32)],
        compiler_params=pltpu.CompilerParams(
            dimension_semantics=("arbitrary",), vmem_limit_bytes=VMEM_LIMIT),
        name=kernel_fn.__name__.strip("_"),
    )(x, mod_l, *weights)


def _dispatch_kernel(dest_ref, h2_hbm, xs_in_hbm, xs_hbm, sem):
    del xs_in_hbm
    i = pl.program_id(0)
    base = i * DISPATCH_TILE

    def row_copy(t, k):
        return pltpu.make_async_copy(h2_hbm.at[base + t], xs_hbm.at[dest_ref[2 * t + k]], sem)

    def issue(t, carry):
        row_copy(t, 0).start()
        row_copy(t, 1).start()
        return carry

    lax.fori_loop(0, DISPATCH_TILE, issue, 0)

    def drain(t, carry):
        row_copy(t, 0).wait()
        row_copy(t, 1).wait()
        return carry

    lax.fori_loop(0, DISPATCH_TILE, drain, 0)


def _dispatch(dest, h2s, n_rows):
    t_tok = h2s.shape[0]
    xs0 = jnp.zeros((n_rows,) + h2s.shape[1:], F32)
    return pl.pallas_call(
        _dispatch_kernel,
        grid=(t_tok // DISPATCH_TILE,),
        in_specs=[
            pl.BlockSpec((2 * DISPATCH_TILE,), lambda i: (i,), memory_space=pltpu.SMEM),
            pl.BlockSpec(memory_space=pl.ANY),
            pl.BlockSpec(memory_space=pl.ANY),
        ],
        out_specs=pl.BlockSpec(memory_space=pl.ANY),
        out_shape=jax.ShapeDtypeStruct(xs0.shape, F32),
        scratch_shapes=[pltpu.SemaphoreType.DMA],
        input_output_aliases={2: 0},
        compiler_params=pltpu.CompilerParams(dimension_semantics=("arbitrary",)),
        name="dispatch",
    )(dest, h2s, xs0)


def _expert_kernel(blk_expert_ref, n_valid_ref, xs_ref, wg_ref, wu_ref, wd_ref, ys_ref):
    del blk_expert_ref
    j = pl.program_id(0)
    n_slabs = xs_ref.shape[1]

    @pl.when(j < n_valid_ref[0])
    def _():
        xb = jnp.concatenate([xs_ref[:, c, :] for c in range(n_slabs)], axis=-1).astype(BF16)
        a = _dot(xb, wg_ref[0].astype(BF16))
        b = _dot(xb, wu_ref[0].astype(BF16))
        hm = (a * jax.nn.sigmoid(a) * b).astype(BF16)
        y = _dot(hm, wd_ref[0].astype(BF16))
        for c in range(n_slabs):
            ys_ref[:, c, :] = y[:, c * LANES:(c + 1) * LANES]

    @pl.when(j >= n_valid_ref[0])
    def _():
        ys_ref[...] = jnp.zeros_like(ys_ref)


def _experts(blk_expert, n_valid, xs, w_gate, w_up, w_down):
    n_rows, n_slabs, _ = xs.shape
    _, d, de = w_gate.shape
    n_blocks = n_rows // EXPERT_BLK

    def row_map(j, be, nv):
        return (jnp.minimum(j, nv[0] - 1), 0, 0)

    def w_map(j, be, nv):
        return (be[j], 0, 0)

    return pl.pallas_call(
        _expert_kernel,
        grid_spec=pltpu.PrefetchScalarGridSpec(
            num_scalar_prefetch=2,
            grid=(n_blocks,),
            in_specs=[
                pl.BlockSpec((EXPERT_BLK, n_slabs, LANES), row_map),
                pl.BlockSpec((1, d, de), w_map),
                pl.BlockSpec((1, d, de), w_map),
                pl.BlockSpec((1, de, d), w_map),
            ],
            out_specs=pl.BlockSpec((EXPERT_BLK, n_slabs, LANES), lambda j, be, nv: (j, 0, 0)),
        ),
        out_shape=jax.ShapeDtypeStruct(xs.shape, F32),
        compiler_params=pltpu.CompilerParams(
            dimension_semantics=("arbitrary",), vmem_limit_bytes=VMEM_LIMIT),
        name="experts",
    )(blk_expert, n_valid, xs, w_gate, w_up, w_down)


def _combine_kernel(final, dest_ref, ys_hbm, x_ref, route_ref, mod_ref, gfin_ref,
                    xo_ref, buf0, buf1, sem):
    tc = x_ref.shape[0]
    n_slabs = buf0.shape[1]

    def row_copy(t, k):
        buf = buf0 if k == 0 else buf1
        return pltpu.make_async_copy(ys_hbm.at[dest_ref[2 * t + k]], buf.at[t], sem)

    def issue(t, carry):
        row_copy(t, 0).start()
        row_copy(t, 1).start()
        return carry

    lax.fori_loop(0, tc, issue, 0)

    def drain(t, carry):
        row_copy(t, 0).wait()
        row_copy(t, 1).wait()
        return carry

    lax.fori_loop(0, tc, drain, 0)

    gate1 = route_ref[:, 4:5]
    gate2 = route_ref[:, 5:6]
    g_f = mod_ref[0, 5:6, :]
    x = x_ref[...]
    cols = []
    for c in range(n_slabs):
        cs = slice(c * LANES, (c + 1) * LANES)
        y_c = buf0[:, c, :] * gate1 + buf1[:, c, :] * gate2
        cols.append(x[:, cs] + g_f[:, cs] * y_c)
    x_new = jnp.concatenate(cols, axis=-1)
    if final:
        x_new = _rms(x_new) * gfin_ref[...]
    xo_ref[...] = x_new


def _combine(dest, ys, x, route, mod_l, seq, g_final, final):
    t_tok, d = x.shape
    n_slabs = ys.shape[1]
    tiles_per_seq = seq // COMBINE_TILE
    return pl.pallas_call(
        functools.partial(_combine_kernel, final),
        grid=(t_tok // COMBINE_TILE,),
        in_specs=[
            pl.BlockSpec((2 * COMBINE_TILE,), lambda i: (i,), memory_space=pltpu.SMEM),
            pl.BlockSpec(memory_space=pl.ANY),
            pl.BlockSpec((COMBINE_TILE, d), lambda i: (i, 0)),
            pl.BlockSpec((COMBINE_TILE, LANES), lambda i: (i, 0)),
            pl.BlockSpec((1,) + mod_l.shape[1:], lambda i: (i // tiles_per_seq, 0, 0)),
            pl.BlockSpec((1, d), lambda i: (0, 0)),
        ],
        out_specs=pl.BlockSpec((COMBINE_TILE, d), lambda i: (i, 0)),
        out_shape=jax.ShapeDtypeStruct((t_tok, d), F32),
        scratch_shapes=[
            pltpu.VMEM((COMBINE_TILE, n_slabs, LANES), F32),
            pltpu.VMEM((COMBINE_TILE, n_slabs, LANES), F32),
            pltpu.SemaphoreType.DMA,
        ],
        compiler_params=pltpu.CompilerParams(
            dimension_semantics=("arbitrary",), vmem_limit_bytes=VMEM_LIMIT),
        name="combine_final" if final else "combine",
    )(dest, ys, x, route, mod_l, g_final)


def _moe(x, h2s, route, cnt, mod_l, seq, w_gate, w_up, w_down, g_final, final):
    t_tok = x.shape[0]
    n_blocks = (2 * t_tok) // EXPERT_BLK + N_EXPERTS
    n_rows = n_blocks * EXPERT_BLK

    counts = cnt[0, ROUTE_COL0:ROUTE_COL0 + N_EXPERTS].astype(jnp.int32)
    padded = (counts + EXPERT_BLK - 1) // EXPERT_BLK * EXPERT_BLK
    pad_ends = jnp.cumsum(padded)
    pad_starts = pad_ends - padded
    expert_id = route[:, 0:2].astype(jnp.int32)
    rank = route[:, 2:4].astype(jnp.int32)
    dest = (pad_starts[expert_id] + rank).reshape(2 * t_tok)
    n_valid = (pad_ends[-1:] // EXPERT_BLK).astype(jnp.int32)
    blk_expert = jnp.searchsorted(
        pad_ends, jnp.arange(n_blocks, dtype=jnp.int32) * EXPERT_BLK, side="right")
    last_used = jnp.max(jnp.where(counts > 0, jnp.arange(N_EXPERTS, dtype=jnp.int32), 0))
    blk_expert = jnp.minimum(blk_expert, last_used).astype(jnp.int32)

    xs = _dispatch(dest, h2s, n_rows)
    ys = _experts(blk_expert, n_valid, xs, w_gate, w_up, w_down)
    return _combine(dest, ys, x, route, mod_l, seq, g_final, final)


def kernel(x, c, w_ada, b_ada, norm_mix_g, norm_ffn_g, w_in_even, sgu_norm_g, w_spatial, b_spatial, conv_w, w_out_even, w_pool, pool_scale, w_group_router, b_group_router, w_expert_router, b_expert_router, moe_w_gate, moe_w_up, moe_w_down, final_norm_g):
    bsz, seq, d = x.shape
    depth = w_ada.shape[0]
    t_tok = bsz * seq
    assert seq % TM == 0 and seq % COMBINE_TILE == 0 and t_tok % DISPATCH_TILE == 0
    assert d % LANES == 0 and w_spatial.shape[-1] == CHUNK

    mod = _modulation(c, w_ada, b_ada).reshape(depth, bsz, 6, d)
    ltri = jnp.tril(jnp.ones((TM, TM), BF16), -1)
    g_final = final_norm_g.reshape(1, d)

    xf = x.reshape(t_tok, d)
    for l in range(depth):
        i = l // 2
        rw = jnp.concatenate([w_group_router[l], w_expert_router[l]], axis=1)
        rw = jnp.pad(rw, ((0, 0), (0, LANES - rw.shape[1])))
        rw_hi = rw.astype(BF16)
        rw_lo = (rw - rw_hi.astype(F32)).astype(BF16)
        rb = jnp.concatenate([b_group_router[l], b_expert_router[l]])
        rb = jnp.pad(rb, (0, LANES - rb.shape[0])).reshape(1, LANES)
        route_w = [rw_hi, rw_lo, rb, ltri]
        gmix = norm_mix_g[l].reshape(1, d)
        gffn = norm_ffn_g[l].reshape(1, d)
        if l % 2 == 0:
            aw = sgu_norm_g.shape[1]
            weights = [gmix, gffn, w_in_even[i].astype(BF16), sgu_norm_g[i].reshape(aw, 1),
                       w_spatial[i], b_spatial[i].reshape(A_HEADS, 1, CHUNK), conv_w[i],
                       w_out_even[i].astype(BF16)] + route_w
            scratch = [pltpu.VMEM((TM + SUBLANES, conv_w.shape[-1]), F32)]
            xf, h2s, route, cnt = _mixer_call(_even_kernel, xf, mod[l], seq, weights, scratch)
        else:
            weights = [gmix, gffn, w_pool[i].astype(BF16), pool_scale[i].reshape(1, d)] + route_w
            scratch = [pltpu.VMEM((TM + max(POOL_WINDOWS), d), F32)]
            xf, h2s, route, cnt = _mixer_call(_odd_kernel, xf, mod[l], seq, weights, scratch)
        xf = _moe(xf, h2s, route, cnt, mod[l], seq, moe_w_gate[l], moe_w_up[l], moe_w_down[l],
                  g_final, l == depth - 1)
    return xf.reshape(bsz, seq, d)
```

```python
import functools

import jax
import jax.numpy as jnp
from jax import lax
from jax.experimental import pallas as pl
from jax.experimental.pallas import tpu as pltpu

F32 = jnp.float32
BF16 = jnp.bfloat16

EPS = 1e-6
LANES = 128
SUBLANES = 8
CHUNK = 128
A_HEADS = 8
N_GROUPS = 4
EXPERTS_PER_GROUP = 8
N_EXPERTS = N_GROUPS * EXPERTS_PER_GROUP
POOL_WINDOWS = (2, 4, 8, 16)
CONV_WIDTH = 3
ROUTE_COL0 = N_GROUPS

TM = 512
EXPERT_BLK = 256
DISPATCH_TILE = 1024
COMBINE_TILE = 512
MOD_TN = 1536
VMEM_LIMIT = 56 * 1024 * 1024


def _rms(x):
    return x * lax.rsqrt(jnp.mean(x * x, axis=-1, keepdims=True) + EPS)


def _dot(a, b):
    return jnp.dot(a, b, preferred_element_type=F32)


def _mod_kernel(c_ref, w_ref, b_ref, o_ref):
    c = c_ref[...]
    ca = c * jax.nn.sigmoid(c)
    o_ref[0] = jnp.dot(ca, w_ref[0], precision=lax.Precision.HIGHEST,
                       preferred_element_type=F32) + b_ref[0]


def _modulation(c, w_ada, b_ada):
    depth, d, n = w_ada.shape
    bsz = c.shape[0]
    return pl.pallas_call(
        _mod_kernel,
        grid=(depth, n // MOD_TN),
        in_specs=[
            pl.BlockSpec((bsz, d), lambda l, j: (0, 0)),
            pl.BlockSpec((1, d, MOD_TN), lambda l, j: (l, 0, j)),
            pl.BlockSpec((1, 1, MOD_TN), lambda l, j: (l, 0, j)),
        ],
        out_specs=pl.BlockSpec((1, bsz, MOD_TN), lambda l, j: (l, 0, j)),
        out_shape=jax.ShapeDtypeStruct((depth, bsz, n), F32),
        compiler_params=pltpu.CompilerParams(
            dimension_semantics=("arbitrary", "arbitrary"),
            vmem_limit_bytes=VMEM_LIMIT),
        name="modulation",
    )(c, w_ada, b_ada.reshape(depth, 1, n))


def _route(x_new, mod_ref, gffn_ref, rw_hi_ref, rw_lo_ref, rb_ref, ltri_ref,
           h2_ref, route_ref, cnt_ref, carry_scr):
    d = x_new.shape[-1]
    sh_f = mod_ref[0, 3:4, :]
    sc_f = mod_ref[0, 4:5, :]
    h2 = _rms(x_new) * gffn_ref[...] * (1.0 + sc_f) + sh_f
    for c in range(d // LANES):
        h2_ref[:, c, :] = h2[:, c * LANES:(c + 1) * LANES]

    hh = h2.astype(BF16)
    hl = (h2 - hh.astype(F32)).astype(BF16)
    w_hi = rw_hi_ref[...]
    logits = _dot(hh, w_hi) + _dot(hl, w_hi) + _dot(hh, rw_lo_ref[...]) + rb_ref[...]

    lane = lax.broadcasted_iota(jnp.int32, logits.shape, 1).astype(F32)
    neg = jnp.float32(-jnp.inf)
    big = jnp.float32(LANES)

    gl = jnp.where(lane < N_GROUPS, logits, neg)
    gmax = jnp.max(gl, axis=-1, keepdims=True)
    g_sel = jnp.min(jnp.where(gl == gmax, lane, big), axis=-1, keepdims=True)
    g_w = 1.0 / jnp.sum(jnp.exp(gl - gmax), axis=-1, keepdims=True)

    lo = ROUTE_COL0 + EXPERTS_PER_GROUP * g_sel
    el = jnp.where((lane >= lo) & (lane < lo + EXPERTS_PER_GROUP), logits, neg)
    m1 = jnp.max(el, axis=-1, keepdims=True)
    i1 = jnp.min(jnp.where(el == m1, lane, big), axis=-1, keepdims=True)
    el2 = jnp.where(lane == i1, neg, el)
    m2 = jnp.max(el2, axis=-1, keepdims=True)
    i2 = jnp.min(jnp.where(el2 == m2, lane, big), axis=-1, keepdims=True)
    t = jnp.exp(m2 - m1)
    gate1 = g_w / (1.0 + t)
    gate2 = g_w * t / (1.0 + t)

    is1 = lane == i1
    is2 = lane == i2
    onehot = jnp.where(is1 | is2, 1.0, 0.0)
    before = _dot(ltri_ref[...], onehot.astype(BF16)) + carry_scr[...]
    rank1 = jnp.sum(jnp.where(is1, before, 0.0), axis=-1, keepdims=True)
    rank2 = jnp.sum(jnp.where(is2, before, 0.0), axis=-1, keepdims=True)
    carry = carry_scr[...] + jnp.sum(onehot, axis=0, keepdims=True)
    carry_scr[...] = carry
    cnt_ref[...] = carry

    route = jnp.where(lane == 0, i1 - ROUTE_COL0, 0.0)
    route = jnp.where(lane == 1, i2 - ROUTE_COL0, route)
    route = jnp.where(lane == 2, rank1, route)
    route = jnp.where(lane == 3, rank2, route)
    route = jnp.where(lane == 4, gate1, route)
    route = jnp.where(lane == 5, gate2, route)
    route_ref[...] = route


def _even_kernel(tiles_per_seq,
                 x_ref, mod_ref, gmix_ref, gffn_ref, win_ref, gv_ref, ws_ref, bs_ref,
                 cw_ref, wout_ref, rw_hi_ref, rw_lo_ref, rb_ref, ltri_ref,
                 xo_ref, h2_ref, route_ref, cnt_ref,
                 zc_scr, carry_scr):
    i = pl.program_id(0)
    tm = x_ref.shape[0]
    aw = gv_ref.shape[0]
    hd = aw // A_HEADS
    n_chunks = tm // CHUNK

    @pl.when(i == 0)
    def _():
        carry_scr[...] = jnp.zeros_like(carry_scr)
        zc_scr[...] = jnp.zeros_like(zc_scr)

    x = x_ref[...]
    sh_m = mod_ref[0, 0:1, :]
    sc_m = mod_ref[0, 1:2, :]
    g_m = mod_ref[0, 2:3, :]
    h = _rms(x) * gmix_ref[...] * (1.0 + sc_m) + sh_m
    z = _dot(h.astype(BF16), win_ref[...])
    u = z[:, 0:aw]
    v = z[:, aw:2 * aw]
    b_gate = z[:, 2 * aw:3 * aw]
    c_gate = z[:, 3 * aw:4 * aw]
    x_in = z[:, 4 * aw:5 * aw]

    v_t = v.T
    row = lax.broadcasted_iota(jnp.int32, (CHUNK, CHUNK), 0)
    col = lax.broadcasted_iota(jnp.int32, (CHUNK, CHUNK), 1)
    causal = col <= row
    head_rows = []
    for hh in range(A_HEADS):
        vh = v_t[hh * hd:(hh + 1) * hd, :]
        msv = jnp.mean(vh * vh, axis=0, keepdims=True)
        vn = (vh * lax.rsqrt(msv + EPS) * gv_ref[hh * hd:(hh + 1) * hd, :]).astype(BF16)
        lhs = jnp.concatenate(
            [vn[:, c * CHUNK:(c + 1) * CHUNK] for c in range(n_chunks)], axis=0)
        w_m = jnp.where(causal, ws_ref[hh], 0.0).astype(BF16)
        sv_h = lax.dot_general(lhs, w_m, (((1,), (1,)), ((), ())),
                               preferred_element_type=F32)
        sv_h = sv_h + bs_ref[hh]
        head_rows.append(jnp.concatenate(
            [sv_h[c * hd:(c + 1) * hd, :] for c in range(n_chunks)], axis=1))
    sv = jnp.concatenate(head_rows, axis=0).T
    y_a = u * sv

    zc = c_gate * x_in
    first = (i % tiles_per_seq) == 0
    halo = zc_scr[tm:tm + SUBLANES, :]
    zc_scr[0:SUBLANES, :] = jnp.where(first, 0.0, halo)
    zc_scr[SUBLANES:SUBLANES + tm, :] = zc
    conv = cw_ref[2:3, :] * zc
    for k in range(CONV_WIDTH - 1):
        shift = CONV_WIDTH - 1 - k
        conv = conv + cw_ref[k:k + 1, :] * zc_scr[SUBLANES - shift:SUBLANES - shift + tm, :]
    y_b = b_gate * conv

    y = _dot(y_a.astype(BF16), wout_ref[0:aw, :]) + _dot(y_b.astype(BF16), wout_ref[aw:, :])
    x_new = x + g_m * y
    xo_ref[...] = x_new
    _route(x_new, mod_ref, gffn_ref, rw_hi_ref, rw_lo_ref, rb_ref, ltri_ref,
           h2_ref, route_ref, cnt_ref, carry_scr)


def _odd_kernel(tiles_per_seq,
                x_ref, mod_ref, gmix_ref, gffn_ref, wpool_ref, pscale_ref,
                rw_hi_ref, rw_lo_ref, rb_ref, ltri_ref,
                xo_ref, h2_ref, route_ref, cnt_ref,
                h_scr, carry_scr):
    i = pl.program_id(0)
    tm, d = x_ref.shape
    halo_rows = max(POOL_WINDOWS)
    gd = d // len(POOL_WINDOWS)

    @pl.when(i == 0)
    def _():
        carry_scr[...] = jnp.zeros_like(carry_scr)
        h_scr[...] = jnp.zeros_like(h_scr)

    x = x_ref[...]
    sh_m = mod_ref[0, 0:1, :]
    sc_m = mod_ref[0, 1:2, :]
    g_m = mod_ref[0, 2:3, :]
    h = _rms(x) * gmix_ref[...] * (1.0 + sc_m) + sh_m

    tile_in_seq = i % tiles_per_seq
    first = tile_in_seq == 0
    halo = h_scr[tm:tm + halo_rows, :]
    h_scr[0:halo_rows, :] = jnp.where(first, 0.0, halo)
    h_scr[halo_rows:halo_rows + tm, :] = h

    pos = (tile_in_seq * tm + lax.broadcasted_iota(jnp.int32, (tm, 1), 0)).astype(F32)
    outs = []
    for g, win in enumerate(POOL_WINDOWS):
        cs = slice(g * gd, (g + 1) * gd)
        acc = h[:, cs]
        for j in range(1, win):
            acc = acc + h_scr[halo_rows - j:halo_rows - j + tm, cs]
        count = jnp.minimum(pos + 1.0, jnp.float32(win))
        pooled = acc / count - h[:, cs]
        outs.append(_dot(pooled.astype(BF16), wpool_ref[g]))
    y = jnp.concatenate(outs, axis=-1) * pscale_ref[...]
    x_new = x + g_m * y
    xo_ref[...] = x_new
    _route(x_new, mod_ref, gffn_ref, rw_hi_ref, rw_lo_ref, rb_ref, ltri_ref,
           h2_ref, route_ref, cnt_ref, carry_scr)


def _mixer_call(kernel_fn, x, mod_l, seq, weights, scratch):
    t_tok, d = x.shape
    n_tiles = t_tok // TM
    tiles_per_seq = seq // TM

    def const_spec(a):
        return pl.BlockSpec(a.shape, lambda i, nd=a.ndim: (0,) * nd)

    in_specs = [
        pl.BlockSpec((TM, d), lambda i: (i, 0)),
        pl.BlockSpec((1,) + mod_l.shape[1:], lambda i: (i // tiles_per_seq, 0, 0)),
    ] + [const_spec(w) for w in weights]
    out_shape = (
        jax.ShapeDtypeStruct((t_tok, d), F32),
        jax.ShapeDtypeStruct((t_tok, d // LANES, LANES), F32),
        jax.ShapeDtypeStruct((t_tok, LANES), F32),
        jax.ShapeDtypeStruct((1, LANES), F32),
    )
    out_specs = (
        pl.BlockSpec((TM, d), lambda i: (i, 0)),
        pl.BlockSpec((TM, d // LANES, LANES), lambda i: (i, 0, 0)),
        pl.BlockSpec((TM, LANES), lambda i: (i, 0)),
        pl.BlockSpec((1, LANES), lambda i: (0, 0)),
    )
    return pl.pallas_call(
        functools.partial(kernel_fn, tiles_per_seq),
        grid=(n_tiles,),
        in_specs=in_specs,
        out_specs=out_specs,
        out_shape=out_shape,
        scratch_shapes=scratch + [pltpu.VMEM((1, LANES), F32)],
        compiler_params=pltpu.CompilerParams(
            dimension_semantics=("arbitrary",), vmem_limit_bytes=VMEM_LIMIT),
        name=kernel_fn.__name__.strip("_"),
    )(x, mod_l, *weights)


def _dispatch_kernel(dest_ref, h2_ref, xs_in_hbm, xs_hbm, sem):
    del xs_in_hbm
    n_tok = h2_ref.shape[0]

    def issue(t, carry):
        for k in range(2):
            pltpu.make_async_copy(h2_ref.at[t], xs_hbm.at[dest_ref[2 * t + k]], sem).start()
        return carry

    lax.fori_loop(0, n_tok, issue, 0)
    for k in range(2):
        pltpu.make_async_copy(h2_ref, xs_hbm.at[pl.ds(0, n_tok)], sem).wait()


def _dispatch(dest, h2s, n_rows):
    t_tok = h2s.shape[0]
    xs0 = jnp.zeros((n_rows,) + h2s.shape[1:], F32)
    return pl.pallas_call(
        _dispatch_kernel,
        grid=(t_tok // DISPATCH_TILE,),
        in_specs=[
            pl.BlockSpec((2 * DISPATCH_TILE,), lambda i: (i,), memory_space=pltpu.SMEM),
            pl.BlockSpec((DISPATCH_TILE,) + h2s.shape[1:], lambda i: (i, 0, 0)),
            pl.BlockSpec(memory_space=pl.ANY),
        ],
        out_specs=pl.BlockSpec(memory_space=pl.ANY),
        out_shape=jax.ShapeDtypeStruct(xs0.shape, F32),
        scratch_shapes=[pltpu.SemaphoreType.DMA],
        input_output_aliases={2: 0},
        compiler_params=pltpu.CompilerParams(
            dimension_semantics=("arbitrary",), vmem_limit_bytes=VMEM_LIMIT),
        name="dispatch",
    )(dest, h2s, xs0)


def _expert_kernel(blk_expert_ref, n_valid_ref, xs_ref, wg_ref, wu_ref, wd_ref, ys_ref):
    del blk_expert_ref
    j = pl.program_id(0)
    n_slabs = xs_ref.shape[1]

    @pl.when(j < n_valid_ref[0])
    def _():
        xb = jnp.concatenate([xs_ref[:, c, :] for c in range(n_slabs)], axis=-1).astype(BF16)
        a = _dot(xb, wg_ref[0, 0].astype(BF16))
        b = _dot(xb, wu_ref[0, 0].astype(BF16))
        hm = (a * jax.nn.sigmoid(a) * b).astype(BF16)
        y = _dot(hm, wd_ref[0, 0].astype(BF16))
        for c in range(n_slabs):
            ys_ref[:, c, :] = y[:, c * LANES:(c + 1) * LANES]

    @pl.when(j >= n_valid_ref[0])
    def _():
        ys_ref[...] = jnp.zeros_like(ys_ref)


def _experts(blk_expert, n_valid, xs, layer, w_gate, w_up, w_down):
    n_rows, n_slabs, _ = xs.shape
    _, _, d, de = w_gate.shape
    n_blocks = n_rows // EXPERT_BLK

    def row_map(j, be, nv):
        return (jnp.minimum(j, nv[0] - 1), 0, 0)

    def w_map(j, be, nv):
        return (layer, be[j], 0, 0)

    return pl.pallas_call(
        _expert_kernel,
        grid_spec=pltpu.PrefetchScalarGridSpec(
            num_scalar_prefetch=2,
            grid=(n_blocks,),
            in_specs=[
                pl.BlockSpec((EXPERT_BLK, n_slabs, LANES), row_map),
                pl.BlockSpec((1, 1, d, de), w_map),
                pl.BlockSpec((1, 1, d, de), w_map),
                pl.BlockSpec((1, 1, de, d), w_map),
            ],
            out_specs=pl.BlockSpec((EXPERT_BLK, n_slabs, LANES), lambda j, be, nv: (j, 0, 0)),
        ),
        out_shape=jax.ShapeDtypeStruct(xs.shape, F32),
        compiler_params=pltpu.CompilerParams(
            dimension_semantics=("arbitrary",), vmem_limit_bytes=VMEM_LIMIT),
        name="experts",
    )(blk_expert, n_valid, xs, w_gate, w_up, w_down)


def _combine_kernel(final, dest_ref, ys_hbm, x_ref, route_ref, mod_ref, gfin_ref,
                    xo_ref, buf0, buf1, sem):
    tc = x_ref.shape[0]
    n_slabs = buf0.shape[1]

    def issue(t, carry):
        for k, buf in enumerate((buf0, buf1)):
            pltpu.make_async_copy(ys_hbm.at[dest_ref[2 * t + k]], buf.at[t], sem).start()
        return carry

    lax.fori_loop(0, tc, issue, 0)
    for buf in (buf0, buf1):
        pltpu.make_async_copy(ys_hbm.at[pl.ds(0, tc)], buf, sem).wait()

    gate1 = route_ref[:, 4:5]
    gate2 = route_ref[:, 5:6]
    g_f = mod_ref[0, 5:6, :]
    x = x_ref[...]
    cols = []
    for c in range(n_slabs):
        cs = slice(c * LANES, (c + 1) * LANES)
        y_c = buf0[:, c, :] * gate1 + buf1[:, c, :] * gate2
        cols.append(x[:, cs] + g_f[:, cs] * y_c)
    x_new = jnp.concatenate(cols, axis=-1)
    if final:
        x_new = _rms(x_new) * gfin_ref[...]
    xo_ref[...] = x_new


def _combine(dest, ys, x, route, mod_l, seq, g_final, final):
    t_tok, d = x.shape
    n_slabs = ys.shape[1]
    tiles_per_seq = seq // COMBINE_TILE
    return pl.pallas_call(
        functools.partial(_combine_kernel, final),
        grid=(t_tok // COMBINE_TILE,),
        in_specs=[
            pl.BlockSpec((2 * COMBINE_TILE,), lambda i: (i,), memory_space=pltpu.SMEM),
            pl.BlockSpec(memory_space=pl.ANY),
            pl.BlockSpec((COMBINE_TILE, d), lambda i: (i, 0)),
            pl.BlockSpec((COMBINE_TILE, LANES), lambda i: (i, 0)),
            pl.BlockSpec((1,) + mod_l.shape[1:], lambda i: (i // tiles_per_seq, 0, 0)),
            pl.BlockSpec((1, d), lambda i: (0, 0)),
        ],
        out_specs=pl.BlockSpec((COMBINE_TILE, d), lambda i: (i, 0)),
        out_shape=jax.ShapeDtypeStruct((t_tok, d), F32),
        scratch_shapes=[
            pltpu.VMEM((COMBINE_TILE, n_slabs, LANES), F32),
            pltpu.VMEM((COMBINE_TILE, n_slabs, LANES), F32),
            pltpu.SemaphoreType.DMA,
        ],
        compiler_params=pltpu.CompilerParams(
            dimension_semantics=("arbitrary",), vmem_limit_bytes=VMEM_LIMIT),
        name="combine_final" if final else "combine",
    )(dest, ys, x, route, mod_l, g_final)


def _moe(x, h2s, route, cnt, mod_l, seq, layer, w_gate, w_up, w_down, g_final, final):
    t_tok = x.shape[0]
    n_blocks = (2 * t_tok) // EXPERT_BLK + N_EXPERTS
    n_rows = n_blocks * EXPERT_BLK

    experts = jnp.arange(N_EXPERTS, dtype=jnp.int32)
    counts = cnt[0, ROUTE_COL0:ROUTE_COL0 + N_EXPERTS].astype(jnp.int32)
    padded = (counts + EXPERT_BLK - 1) // EXPERT_BLK * EXPERT_BLK
    pad_ends = jnp.sum(jnp.where(experts[None, :] <= experts[:, None], padded[None, :], 0), axis=1)
    pad_starts = pad_ends - padded
    expert_id = route[:, 0:2].astype(jnp.int32)
    rank = route[:, 2:4].astype(jnp.int32)
    start_of = jnp.sum(jnp.where(expert_id[..., None] == experts, pad_starts, 0), axis=-1)
    dest = (start_of + rank).reshape(2 * t_tok)
    n_valid = (pad_ends[-1:] // EXPERT_BLK).astype(jnp.int32)
    blk_row0 = jnp.arange(n_blocks, dtype=jnp.int32) * EXPERT_BLK
    blk_expert = jnp.sum((pad_ends[None, :] <= blk_row0[:, None]).astype(jnp.int32), axis=1)
    last_used = jnp.max(jnp.where(counts > 0, experts, 0))
    blk_expert = jnp.minimum(blk_expert, last_used).astype(jnp.int32)

    xs = _dispatch(dest, h2s, n_rows)
    ys = _experts(blk_expert, n_valid, xs, layer, w_gate, w_up, w_down)
    return _combine(dest, ys, x, route, mod_l, seq, g_final, final)


def kernel(x, c, w_ada, b_ada, norm_mix_g, norm_ffn_g, w_in_even, sgu_norm_g, w_spatial, b_spatial, conv_w, w_out_even, w_pool, pool_scale, w_group_router, b_group_router, w_expert_router, b_expert_router, moe_w_gate, moe_w_up, moe_w_down, final_norm_g):
    bsz, seq, d = x.shape
    depth = w_ada.shape[0]
    t_tok = bsz * seq
    assert seq % TM == 0 and seq % COMBINE_TILE == 0 and t_tok % DISPATCH_TILE == 0
    assert d % LANES == 0 and w_spatial.shape[-1] == CHUNK

    mod = _modulation(c, w_ada, b_ada).reshape(depth, bsz, 6, d)
    ltri = jnp.tril(jnp.ones((TM, TM), BF16), -1)
    g_final = final_norm_g.reshape(1, d)

    xf = x.reshape(t_tok, d)
    for l in range(depth):
        i = l // 2
        rw = jnp.concatenate([w_group_router[l], w_expert_router[l]], axis=1)
        rw = jnp.pad(rw, ((0, 0), (0, LANES - rw.shape[1])))
        rw_hi = rw.astype(BF16)
        rw_lo = (rw - rw_hi.astype(F32)).astype(BF16)
        rb = jnp.concatenate([b_group_router[l], b_expert_router[l]])
        rb = jnp.pad(rb, (0, LANES - rb.shape[0])).reshape(1, LANES)
        route_w = [rw_hi, rw_lo, rb, ltri]
        gmix = norm_mix_g[l].reshape(1, d)
        gffn = norm_ffn_g[l].reshape(1, d)
        if l % 2 == 0:
            aw = sgu_norm_g.shape[1]
            weights = [gmix, gffn, w_in_even[i].astype(BF16), sgu_norm_g[i].reshape(aw, 1),
                       w_spatial[i], b_spatial[i].reshape(A_HEADS, 1, CHUNK), conv_w[i],
                       w_out_even[i].astype(BF16)] + route_w
            scratch = [pltpu.VMEM((TM + SUBLANES, conv_w.shape[-1]), F32)]
            xf, h2s, route, cnt = _mixer_call(_even_kernel, xf, mod[l], seq, weights, scratch)
        else:
            weights = [gmix, gffn, w_pool[i].astype(BF16), pool_scale[i].reshape(1, d)] + route_w
            scratch = [pltpu.VMEM((TM + max(POOL_WINDOWS), d), F32)]
            xf, h2s, route, cnt = _mixer_call(_odd_kernel, xf, mod[l], seq, weights, scratch)
        xf = _moe(xf, h2s, route, cnt, mod[l], seq, l, moe_w_gate, moe_w_up, moe_w_down,
                  g_final, l == depth - 1)
    return xf.reshape(bsz, seq, d)
```

```python
import functools

import jax
import jax.numpy as jnp
from jax import lax
from jax.experimental import pallas as pl
from jax.experimental.pallas import tpu as pltpu

F32 = jnp.float32
BF16 = jnp.bfloat16
U32 = jnp.uint32
I32 = jnp.int32

EPS = 1e-6
LANES = 128
SUBLANES = 8
CHUNK = 128
A_HEADS = 8
N_GROUPS = 4
EXPERTS_PER_GROUP = 8
N_EXPERTS = N_GROUPS * EXPERTS_PER_GROUP
POOL_WINDOWS = (2, 4, 8, 16)
CONV_WIDTH = 3
ROUTE_COL0 = N_GROUPS

TM = 512
LOCAL_ROWS = 2 * TM + LANES
LOCAL_PAIRS = LOCAL_ROWS // 2
EXPERT_BLK = 256
BLK_PAIRS = EXPERT_BLK // 2
RUN_BITS = (LOCAL_PAIRS - 1).bit_length()
TAIL_BITS = (BLK_PAIRS - 1).bit_length()
MOD_TN = 1536
VMEM_LIMIT = 56 * 1024 * 1024


def _rms(x):
    return x * lax.rsqrt(jnp.mean(x * x, axis=-1, keepdims=True) + EPS)


def _dot(a, b):
    return jnp.dot(a, b, preferred_element_type=F32)


def _store_pairs(ref, rows_bf16):
    words = pltpu.bitcast(rows_bf16, U32)
    for c in range(ref.shape[1]):
        ref[:, c, :] = words[:, c * LANES:(c + 1) * LANES]


def _load_pairs(ref):
    words = jnp.concatenate([ref[:, c, :] for c in range(ref.shape[1])], axis=-1)
    return pltpu.bitcast(words, BF16)


def _mod_kernel(c_ref, w_ref, b_ref, o_ref):
    c = c_ref[...]
    ca = c * jax.nn.sigmoid(c)
    o_ref[0] = jnp.dot(ca, w_ref[0], precision=lax.Precision.HIGHEST,
                       preferred_element_type=F32) + b_ref[0]


def _modulation(c, w_ada, b_ada):
    depth, d, n = w_ada.shape
    bsz = c.shape[0]
    return pl.pallas_call(
        _mod_kernel,
        grid=(depth, n // MOD_TN),
        in_specs=[
            pl.BlockSpec((bsz, d), lambda l, j: (0, 0)),
            pl.BlockSpec((1, d, MOD_TN), lambda l, j: (l, 0, j)),
            pl.BlockSpec((1, 1, MOD_TN), lambda l, j: (l, 0, j)),
        ],
        out_specs=pl.BlockSpec((1, bsz, MOD_TN), lambda l, j: (l, 0, j)),
        out_shape=jax.ShapeDtypeStruct((depth, bsz, n), F32),
        compiler_params=pltpu.CompilerParams(
            dimension_semantics=("arbitrary", "arbitrary"),
            vmem_limit_bytes=VMEM_LIMIT),
        name="modulation",
    )(c, w_ada, b_ada.reshape(depth, 1, n))


def _route_and_sort(x_new, mod_ref, gffn_ref, rw_hi_ref, rw_lo_ref, rb_ref, ltri_ref, utri_ref,
                    xsl_ref, route_ref, cnt_ref):
    tm = x_new.shape[0]
    sh_f = mod_ref[0, 3:4, :]
    sc_f = mod_ref[0, 4:5, :]
    h2 = _rms(x_new) * gffn_ref[...] * (1.0 + sc_f) + sh_f

    hh = h2.astype(BF16)
    hl = (h2 - hh.astype(F32)).astype(BF16)
    w_hi = rw_hi_ref[...]
    logits = _dot(hh, w_hi) + _dot(hl, w_hi) + _dot(hh, rw_lo_ref[...]) + rb_ref[...]

    lane = lax.broadcasted_iota(I32, logits.shape, 1).astype(F32)
    neg = jnp.float32(-jnp.inf)
    big = jnp.float32(LANES)

    gl = jnp.where(lane < N_GROUPS, logits, neg)
    gmax = jnp.max(gl, axis=-1, keepdims=True)
    g_sel = jnp.min(jnp.where(gl == gmax, lane, big), axis=-1, keepdims=True)
    g_w = 1.0 / jnp.sum(jnp.exp(gl - gmax), axis=-1, keepdims=True)

    lo = ROUTE_COL0 + EXPERTS_PER_GROUP * g_sel
    el = jnp.where((lane >= lo) & (lane < lo + EXPERTS_PER_GROUP), logits, neg)
    m1 = jnp.max(el, axis=-1, keepdims=True)
    i1 = jnp.min(jnp.where(el == m1, lane, big), axis=-1, keepdims=True)
    el2 = jnp.where(lane == i1, neg, el)
    m2 = jnp.max(el2, axis=-1, keepdims=True)
    i2 = jnp.min(jnp.where(el2 == m2, lane, big), axis=-1, keepdims=True)
    t = jnp.exp(m2 - m1)
    gate1 = g_w / (1.0 + t)
    gate2 = g_w * t / (1.0 + t)

    is1 = lane == i1
    is2 = lane == i2
    onehot = jnp.where(is1 | is2, 1.0, 0.0)
    before = _dot(ltri_ref[...], onehot.astype(BF16))
    cnt = jnp.sum(onehot, axis=0, keepdims=True)
    pairs = jnp.floor((cnt + 1.0) * 0.5)
    pair_start = jnp.dot(jnp.broadcast_to(pairs, (SUBLANES, LANES)), utri_ref[...],
                         precision=lax.Precision.HIGHEST, preferred_element_type=F32)[0:1]
    pos = before + 2.0 * pair_start
    pos1 = jnp.sum(jnp.where(is1, pos, 0.0), axis=-1, keepdims=True)
    pos2 = jnp.sum(jnp.where(is2, pos, 0.0), axis=-1, keepdims=True)
    cnt_ref[0] = jnp.broadcast_to(cnt, (SUBLANES, LANES))

    route = jnp.where(lane == 0, pos1, 0.0)
    route = jnp.where(lane == 1, pos2, route)
    route = jnp.where(lane == 4, gate1, route)
    route = jnp.where(lane == 5, gate2, route)
    route_ref[...] = route

    pos_t = route.T
    srow = lax.broadcasted_iota(I32, (LOCAL_ROWS, tm), 0).astype(F32)
    perm = jnp.where((srow == pos_t[0:1, :]) | (srow == pos_t[1:2, :]), 1.0, 0.0).astype(BF16)
    _store_pairs(xsl_ref, _dot(perm, hh).astype(BF16))


def _even_kernel(tiles_per_seq,
                 x_ref, mod_ref, gmix_ref, gffn_ref, win_ref, gv_ref, ws_ref, bs_ref,
                 cw_ref, wout_ref, rw_hi_ref, rw_lo_ref, rb_ref, ltri_ref, utri_ref,
                 xo_ref, xsl_ref, route_ref, cnt_ref,
                 zc_scr):
    i = pl.program_id(0)
    tm = x_ref.shape[0]
    aw = gv_ref.shape[0]
    hd = aw // A_HEADS
    n_chunks = tm // CHUNK

    @pl.when(i == 0)
    def _():
        zc_scr[...] = jnp.zeros_like(zc_scr)

    x = x_ref[...]
    sh_m = mod_ref[0, 0:1, :]
    sc_m = mod_ref[0, 1:2, :]
    g_m = mod_ref[0, 2:3, :]
    h = _rms(x) * gmix_ref[...] * (1.0 + sc_m) + sh_m
    z = _dot(h.astype(BF16), win_ref[...])
    u = z[:, 0:aw]
    v = z[:, aw:2 * aw]
    b_gate = z[:, 2 * aw:3 * aw]
    c_gate = z[:, 3 * aw:4 * aw]
    x_in = z[:, 4 * aw:5 * aw]

    v_t = v.T
    row = lax.broadcasted_iota(I32, (CHUNK, CHUNK), 0)
    col = lax.broadcasted_iota(I32, (CHUNK, CHUNK), 1)
    causal = col <= row
    head_rows = []
    for hh in range(A_HEADS):
        vh = v_t[hh * hd:(hh + 1) * hd, :]
        msv = jnp.mean(vh * vh, axis=0, keepdims=True)
        vn = (vh * lax.rsqrt(msv + EPS) * gv_ref[hh * hd:(hh + 1) * hd, :]).astype(BF16)
        lhs = jnp.concatenate(
            [vn[:, c * CHUNK:(c + 1) * CHUNK] for c in range(n_chunks)], axis=0)
        w_m = jnp.where(causal, ws_ref[hh], 0.0).astype(BF16)
        sv_h = lax.dot_general(lhs, w_m, (((1,), (1,)), ((), ())),
                               preferred_element_type=F32)
        sv_h = sv_h + bs_ref[hh]
        head_rows.append(jnp.concatenate(
            [sv_h[c * hd:(c + 1) * hd, :] for c in range(n_chunks)], axis=1))
    sv = jnp.concatenate(head_rows, axis=0).T
    y_a = u * sv

    zc = c_gate * x_in
    first = (i % tiles_per_seq) == 0
    halo = zc_scr[tm:tm + SUBLANES, :]
    zc_scr[0:SUBLANES, :] = jnp.where(first, 0.0, halo)
    zc_scr[SUBLANES:SUBLANES + tm, :] = zc
    conv = cw_ref[2:3, :] * zc
    for k in range(CONV_WIDTH - 1):
        shift = CONV_WIDTH - 1 - k
        conv = conv + cw_ref[k:k + 1, :] * zc_scr[SUBLANES - shift:SUBLANES - shift + tm, :]
    y_b = b_gate * conv

    y = _dot(y_a.astype(BF16), wout_ref[0:aw, :]) + _dot(y_b.astype(BF16), wout_ref[aw:, :])
    x_new = x + g_m * y
    xo_ref[...] = x_new
    _route_and_sort(x_new, mod_ref, gffn_ref, rw_hi_ref, rw_lo_ref, rb_ref, ltri_ref, utri_ref,
                    xsl_ref, route_ref, cnt_ref)


def _odd_kernel(tiles_per_seq,
                x_ref, mod_ref, gmix_ref, gffn_ref, wpool_ref, pscale_ref,
                rw_hi_ref, rw_lo_ref, rb_ref, ltri_ref, utri_ref,
                xo_ref, xsl_ref, route_ref, cnt_ref,
                h_scr):
    i = pl.program_id(0)
    tm, d = x_ref.shape
    halo_rows = max(POOL_WINDOWS)
    gd = d // len(POOL_WINDOWS)

    @pl.when(i == 0)
    def _():
        h_scr[...] = jnp.zeros_like(h_scr)

    x = x_ref[...]
    sh_m = mod_ref[0, 0:1, :]
    sc_m = mod_ref[0, 1:2, :]
    g_m = mod_ref[0, 2:3, :]
    h = _rms(x) * gmix_ref[...] * (1.0 + sc_m) + sh_m

    tile_in_seq = i % tiles_per_seq
    first = tile_in_seq == 0
    halo = h_scr[tm:tm + halo_rows, :]
    h_scr[0:halo_rows, :] = jnp.where(first, 0.0, halo)
    h_scr[halo_rows:halo_rows + tm, :] = h

    pos = (tile_in_seq * tm + lax.broadcasted_iota(I32, (tm, 1), 0)).astype(F32)
    outs = []
    for g, win in enumerate(POOL_WINDOWS):
        cs = slice(g * gd, (g + 1) * gd)
        acc = h[:, cs]
        for j in range(1, win):
            acc = acc + h_scr[halo_rows - j:halo_rows - j + tm, cs]
        count = jnp.minimum(pos + 1.0, jnp.float32(win))
        pooled = acc / count - h[:, cs]
        outs.append(_dot(pooled.astype(BF16), wpool_ref[g]))
    y = jnp.concatenate(outs, axis=-1) * pscale_ref[...]
    x_new = x + g_m * y
    xo_ref[...] = x_new
    _route_and_sort(x_new, mod_ref, gffn_ref, rw_hi_ref, rw_lo_ref, rb_ref, ltri_ref, utri_ref,
                    xsl_ref, route_ref, cnt_ref)


def _mixer_call(kernel_fn, x, mod_l, seq, weights, scratch):
    t_tok, d = x.shape
    n_tiles = t_tok // TM
    tiles_per_seq = seq // TM
    n_slabs = d // LANES

    def const_spec(a):
        return pl.BlockSpec(a.shape, lambda i, nd=a.ndim: (0,) * nd)

    in_specs = [
        pl.BlockSpec((TM, d), lambda i: (i, 0)),
        pl.BlockSpec((1,) + mod_l.shape[1:], lambda i: (i // tiles_per_seq, 0, 0)),
    ] + [const_spec(w) for w in weights]
    out_shape = (
        jax.ShapeDtypeStruct((t_tok, d), F32),
        jax.ShapeDtypeStruct((n_tiles * LOCAL_PAIRS, n_slabs, LANES), U32),
        jax.ShapeDtypeStruct((t_tok, LANES), F32),
        jax.ShapeDtypeStruct((n_tiles, SUBLANES, LANES), F32),
    )
    out_specs = (
        pl.BlockSpec((TM, d), lambda i: (i, 0)),
        pl.BlockSpec((LOCAL_PAIRS, n_slabs, LANES), lambda i: (i, 0, 0)),
        pl.BlockSpec((TM, LANES), lambda i: (i, 0)),
        pl.BlockSpec((1, SUBLANES, LANES), lambda i: (i, 0, 0)),
    )
    return pl.pallas_call(
        functools.partial(kernel_fn, tiles_per_seq),
        grid=(n_tiles,),
        in_specs=in_specs,
        out_specs=out_specs,
        out_shape=out_shape,
        scratch_shapes=scratch,
        compiler_params=pltpu.CompilerParams(
            dimension_semantics=("arbitrary",), vmem_limit_bytes=VMEM_LIMIT),
        name=kernel_fn.__name__.strip("_"),
    )(x, mod_l, *weights)


def _for_each_piece(length, n_bits, fn):
    for bit in range(n_bits):
        size = 1 << bit
        offset = (length >> (bit + 1)) << (bit + 1)

        @pl.when(((length >> bit) & 1) == 1)
        def _(offset=offset, size=size):
            fn(offset, size)


def _dispatch_kernel(len_ref, loc_ref, glob_ref, tail_len_ref, tail_ref, n_valid_ref,
                     xsl_ref, xs_hbm, zero_scr, sem):
    i = pl.program_id(0)
    n_blocks = xs_hbm.shape[0] // BLK_PAIRS

    def run_copy(loc, glob):
        def make(offset, size):
            return pltpu.make_async_copy(xsl_ref.at[pl.ds(loc + offset, size)],
                                         xs_hbm.at[pl.ds(glob + offset, size)], sem)
        return make

    def for_each_run(action):
        def body(e, carry):
            k = i * N_EXPERTS + e
            make = run_copy(loc_ref[k], glob_ref[k])
            _for_each_piece(len_ref[k], RUN_BITS, lambda o, s: action(make(o, s)))
            return carry
        lax.fori_loop(0, N_EXPERTS, body, 0)

    for_each_run(lambda cp: cp.start())
    for_each_run(lambda cp: cp.wait())

    @pl.when(i == pl.num_programs(0) - 1)
    def _():
        zero_scr[...] = jnp.zeros_like(zero_scr)

        def for_each_tail(action):
            def body(e, carry):
                def make(offset, size):
                    return pltpu.make_async_copy(zero_scr.at[pl.ds(0, size)],
                                                 xs_hbm.at[pl.ds(tail_ref[e] + offset, size)], sem)
                _for_each_piece(tail_len_ref[e], TAIL_BITS, lambda o, s: action(make(o, s)))
                return carry
            lax.fori_loop(0, N_EXPERTS, body, 0)

        for_each_tail(lambda cp: cp.start())
        for_each_tail(lambda cp: cp.wait())

        def for_each_spare(action):
            def body(b, carry):
                action(pltpu.make_async_copy(
                    zero_scr, xs_hbm.at[pl.ds(b * BLK_PAIRS, BLK_PAIRS)], sem))
                return carry
            lax.fori_loop(n_valid_ref[0], n_blocks, body, 0)

        for_each_spare(lambda cp: cp.start())
        for_each_spare(lambda cp: cp.wait())


def _dispatch(tables, xsl, n_pairs):
    n_tiles = xsl.shape[0] // LOCAL_PAIRS
    slab = xsl.shape[1:]
    return pl.pallas_call(
        _dispatch_kernel,
        grid_spec=pltpu.PrefetchScalarGridSpec(
            num_scalar_prefetch=6,
            grid=(n_tiles,),
            in_specs=[pl.BlockSpec((LOCAL_PAIRS,) + slab, lambda i, *_: (i, 0, 0))],
            out_specs=pl.BlockSpec(memory_space=pl.ANY),
            scratch_shapes=[pltpu.VMEM((BLK_PAIRS,) + slab, U32), pltpu.SemaphoreType.DMA],
        ),
        out_shape=jax.ShapeDtypeStruct((n_pairs,) + slab, U32),
        compiler_params=pltpu.CompilerParams(
            dimension_semantics=("arbitrary",), vmem_limit_bytes=VMEM_LIMIT),
        name="dispatch",
    )(*tables, xsl)


def _expert_kernel(blk_expert_ref, n_valid_ref, xs_ref, wg_ref, wu_ref, wd_ref, ys_ref,
                   wg_scr, wu_scr, wd_scr):
    j = pl.program_id(0)
    prev = blk_expert_ref[jnp.maximum(j - 1, 0)]

    @pl.when((j == 0) | (blk_expert_ref[j] != prev))
    def _():
        wg_scr[...] = wg_ref[0, 0].astype(BF16)
        wu_scr[...] = wu_ref[0, 0].astype(BF16)
        wd_scr[...] = wd_ref[0, 0].astype(BF16)

    @pl.when(j < n_valid_ref[0])
    def _():
        xb = _load_pairs(xs_ref)
        a = _dot(xb, wg_scr[...])
        b = _dot(xb, wu_scr[...])
        hm = (a * jax.nn.sigmoid(a) * b).astype(BF16)
        _store_pairs(ys_ref, _dot(hm, wd_scr[...]).astype(BF16))

    @pl.when(j >= n_valid_ref[0])
    def _():
        ys_ref[...] = jnp.zeros_like(ys_ref)


def _experts(blk_expert, n_valid, xs, layer, w_gate, w_up, w_down):
    n_pairs, n_slabs, _ = xs.shape
    _, _, d, de = w_gate.shape
    n_blocks = n_pairs // BLK_PAIRS

    def row_map(j, be, nv):
        return (jnp.minimum(j, nv[0] - 1), 0, 0)

    def w_map(j, be, nv):
        return (layer, be[j], 0, 0)

    return pl.pallas_call(
        _expert_kernel,
        grid_spec=pltpu.PrefetchScalarGridSpec(
            num_scalar_prefetch=2,
            grid=(n_blocks,),
            in_specs=[
                pl.BlockSpec((BLK_PAIRS, n_slabs, LANES), row_map),
                pl.BlockSpec((1, 1, d, de), w_map),
                pl.BlockSpec((1, 1, d, de), w_map),
                pl.BlockSpec((1, 1, de, d), w_map),
            ],
            out_specs=pl.BlockSpec((BLK_PAIRS, n_slabs, LANES), lambda j, be, nv: (j, 0, 0)),
            scratch_shapes=[pltpu.VMEM((d, de), BF16), pltpu.VMEM((d, de), BF16),
                            pltpu.VMEM((de, d), BF16)],
        ),
        out_shape=jax.ShapeDtypeStruct(xs.shape, U32),
        compiler_params=pltpu.CompilerParams(
            dimension_semantics=("arbitrary",), vmem_limit_bytes=VMEM_LIMIT),
        name="experts",
    )(blk_expert, n_valid, xs, w_gate, w_up, w_down)


def _combine_kernel(final, len_ref, loc_ref, glob_ref, ys_hbm, x_ref, route_ref, mod_ref,
                    gfin_ref, xo_ref, ysl_scr, sem):
    i = pl.program_id(0)
    tm = x_ref.shape[0]

    @pl.when(i == 0)
    def _():
        ysl_scr[...] = jnp.zeros_like(ysl_scr)

    def for_each_run(action):
        def body(e, carry):
            k = i * N_EXPERTS + e
            loc = loc_ref[k]
            glob = glob_ref[k]

            def make(offset, size):
                return pltpu.make_async_copy(ys_hbm.at[pl.ds(glob + offset, size)],
                                             ysl_scr.at[pl.ds(loc + offset, size)], sem)
            _for_each_piece(len_ref[k], RUN_BITS, lambda o, s: action(make(o, s)))
            return carry
        lax.fori_loop(0, N_EXPERTS, body, 0)

    for_each_run(lambda cp: cp.start())
    for_each_run(lambda cp: cp.wait())

    ysl = _load_pairs(ysl_scr)
    srow = lax.broadcasted_iota(I32, (tm, LOCAL_ROWS), 1).astype(F32)
    y = jnp.zeros(x_ref.shape, F32)
    for k in range(2):
        sel = jnp.where(srow == route_ref[:, k:k + 1], 1.0, 0.0).astype(BF16)
        y = y + route_ref[:, 4 + k:5 + k] * _dot(sel, ysl)
    x_new = x_ref[...] + mod_ref[0, 5:6, :] * y
    if final:
        x_new = _rms(x_new) * gfin_ref[...]
    xo_ref[...] = x_new


def _combine(tables, ys, x, route, mod_l, seq, g_final, final):
    t_tok, d = x.shape
    slab = ys.shape[1:]
    tiles_per_seq = seq // TM
    return pl.pallas_call(
        functools.partial(_combine_kernel, final),
        grid_spec=pltpu.PrefetchScalarGridSpec(
            num_scalar_prefetch=3,
            grid=(t_tok // TM,),
            in_specs=[
                pl.BlockSpec(memory_space=pl.ANY),
                pl.BlockSpec((TM, d), lambda i, *_: (i, 0)),
                pl.BlockSpec((TM, LANES), lambda i, *_: (i, 0)),
                pl.BlockSpec((1,) + mod_l.shape[1:], lambda i, *_: (i // tiles_per_seq, 0, 0)),
                pl.BlockSpec((1, d), lambda i, *_: (0, 0)),
            ],
            out_specs=pl.BlockSpec((TM, d), lambda i, *_: (i, 0)),
            scratch_shapes=[pltpu.VMEM((LOCAL_PAIRS,) + slab, U32), pltpu.SemaphoreType.DMA],
        ),
        out_shape=jax.ShapeDtypeStruct((t_tok, d), F32),
        compiler_params=pltpu.CompilerParams(
            dimension_semantics=("arbitrary",), vmem_limit_bytes=VMEM_LIMIT),
        name="combine_final" if final else "combine",
    )(*tables, ys, x, route, mod_l, g_final)


def _moe(x, xsl, route, cnt, mod_l, seq, layer, w_gate, w_up, w_down, g_final, final):
    t_tok = x.shape[0]
    n_tiles = t_tok // TM
    n_blocks = -(-(2 * t_tok + n_tiles * N_EXPERTS) // EXPERT_BLK) + N_EXPERTS
    n_pairs = n_blocks * BLK_PAIRS

    counts = cnt[:, 0, ROUTE_COL0:ROUTE_COL0 + N_EXPERTS].astype(I32)
    run_len = (counts + 1) // 2
    run_loc = jnp.cumsum(run_len, axis=1) - run_len
    seg_len = jnp.sum(run_len, axis=0)
    seg_pad = (seg_len + BLK_PAIRS - 1) // BLK_PAIRS * BLK_PAIRS
    seg_end = jnp.cumsum(seg_pad)
    seg_start = seg_end - seg_pad
    run_glob = seg_start[None, :] + jnp.cumsum(run_len, axis=0) - run_len
    run_tables = tuple(a.reshape(-1).astype(I32) for a in (run_len, run_loc, run_glob))
    tail_tables = ((seg_pad - seg_len).astype(I32), (seg_start + seg_len).astype(I32))
    n_valid = (seg_end[-1:] // BLK_PAIRS).astype(I32)
    blk_pair0 = jnp.arange(n_blocks, dtype=I32) * BLK_PAIRS
    blk_expert = jnp.sum((seg_end[None, :] <= blk_pair0[:, None]).astype(I32), axis=1)
    experts = jnp.arange(N_EXPERTS, dtype=I32)
    last_used = jnp.max(jnp.where(seg_len > 0, experts, 0))
    blk_expert = jnp.minimum(blk_expert, last_used).astype(I32)

    xs = _dispatch(run_tables + tail_tables + (n_valid,), xsl, n_pairs)
    ys = _experts(blk_expert, n_valid, xs, layer, w_gate, w_up, w_down)
    return _combine(run_tables, ys, x, route, mod_l, seq, g_final, final)


def kernel(x, c, w_ada, b_ada, norm_mix_g, norm_ffn_g, w_in_even, sgu_norm_g, w_spatial, b_spatial, conv_w, w_out_even, w_pool, pool_scale, w_group_router, b_group_router, w_expert_router, b_expert_router, moe_w_gate, moe_w_up, moe_w_down, final_norm_g):
    bsz, seq, d = x.shape
    depth = w_ada.shape[0]
    t_tok = bsz * seq
    assert seq % TM == 0 and d % LANES == 0 and w_spatial.shape[-1] == CHUNK

    mod = _modulation(c, w_ada, b_ada).reshape(depth, bsz, 6, d)
    ltri = jnp.tril(jnp.ones((TM, TM), BF16), -1)
    utri = jnp.triu(jnp.ones((LANES, LANES), F32), 1)
    g_final = final_norm_g.reshape(1, d)

    xf = x.reshape(t_tok, d)
    for l in range(depth):
        i = l // 2
        rw = jnp.concatenate([w_group_router[l], w_expert_router[l]], axis=1)
        rw = jnp.pad(rw, ((0, 0), (0, LANES - rw.shape[1])))
        rw_hi = rw.astype(BF16)
        rw_lo = (rw - rw_hi.astype(F32)).astype(BF16)
        rb = jnp.concatenate([b_group_router[l], b_expert_router[l]])
        rb = jnp.pad(rb, (0, LANES - rb.shape[0])).reshape(1, LANES)
        route_w = [rw_hi, rw_lo, rb, ltri, utri]
        gmix = norm_mix_g[l].reshape(1, d)
        gffn = norm_ffn_g[l].reshape(1, d)
        if l % 2 == 0:
            aw = sgu_norm_g.shape[1]
            weights = [gmix, gffn, w_in_even[i].astype(BF16), sgu_norm_g[i].reshape(aw, 1),
                       w_spatial[i], b_spatial[i].reshape(A_HEADS, 1, CHUNK), conv_w[i],
                       w_out_even[i].astype(BF16)] + route_w
            scratch = [pltpu.VMEM((TM + SUBLANES, conv_w.shape[-1]), F32)]
            xf, xsl, route, cnt = _mixer_call(_even_kernel, xf, mod[l], seq, weights, scratch)
        else:
            weights = [gmix, gffn, w_pool[i].astype(BF16), pool_scale[i].reshape(1, d)] + route_w
            scratch = [pltpu.VMEM((TM + max(POOL_WINDOWS), d), F32)]
            xf, xsl, route, cnt = _mixer_call(_odd_kernel, xf, mod[l], seq, weights, scratch)
        xf = _moe(xf, xsl, route, cnt, mod[l], seq, l, moe_w_gate, moe_w_up, moe_w_down,
                  g_final, l == depth - 1)
    return xf.reshape(bsz, seq, d)
```

```python
import functools

import jax
import jax.numpy as jnp
from jax import lax
from jax.experimental import pallas as pl
from jax.experimental.pallas import tpu as pltpu

F32 = jnp.float32
BF16 = jnp.bfloat16
U32 = jnp.uint32
I32 = jnp.int32

EPS = 1e-6
LANES = 128
SUBLANES = 8
CHUNK = 128
A_HEADS = 8
N_GROUPS = 4
EXPERTS_PER_GROUP = 8
N_EXPERTS = N_GROUPS * EXPERTS_PER_GROUP
POOL_WINDOWS = (2, 4, 8, 16)
CONV_WIDTH = 3
ROUTE_COL0 = N_GROUPS

TM = 512
LOCAL_ROWS = 2 * TM + LANES
LOCAL_PAIRS = LOCAL_ROWS // 2
EXPERT_BLK = 256
BLK_PAIRS = EXPERT_BLK // 2
RUN_BITS = (LOCAL_PAIRS - 1).bit_length()
TAIL_BITS = (BLK_PAIRS - 1).bit_length()
MOD_TN = 1536
VMEM_LIMIT = 56 * 1024 * 1024


def _rms(x):
    return x * lax.rsqrt(jnp.mean(x * x, axis=-1, keepdims=True) + EPS)


def _dot(a, b):
    return jnp.dot(a, b, preferred_element_type=F32)


def _store_pairs(ref, rows_bf16):
    words = pltpu.bitcast(rows_bf16, U32)
    n_pairs, d = words.shape
    slab = d // LANES
    for c in range(slab):
        ref[pl.ds(c, n_pairs, stride=slab), :] = words[:, c * LANES:(c + 1) * LANES]


def _load_pairs(ref, d):
    slab = d // LANES
    n_pairs = ref.shape[0] // slab
    words = jnp.concatenate(
        [ref[pl.ds(c, n_pairs, stride=slab), :] for c in range(slab)], axis=-1)
    return pltpu.bitcast(words, BF16)


def _pair_rows(first_pair, n_pairs, slab):
    return pl.ds(pl.multiple_of(first_pair * slab, slab), n_pairs * slab)


def _mod_kernel(c_ref, w_ref, b_ref, o_ref):
    c = c_ref[...]
    ca = c * jax.nn.sigmoid(c)
    o_ref[0] = jnp.dot(ca, w_ref[0], precision=lax.Precision.HIGHEST,
                       preferred_element_type=F32) + b_ref[0]


def _modulation(c, w_ada, b_ada):
    depth, d, n = w_ada.shape
    bsz = c.shape[0]
    return pl.pallas_call(
        _mod_kernel,
        grid=(depth, n // MOD_TN),
        in_specs=[
            pl.BlockSpec((bsz, d), lambda l, j: (0, 0)),
            pl.BlockSpec((1, d, MOD_TN), lambda l, j: (l, 0, j)),
            pl.BlockSpec((1, 1, MOD_TN), lambda l, j: (l, 0, j)),
        ],
        out_specs=pl.BlockSpec((1, bsz, MOD_TN), lambda l, j: (l, 0, j)),
        out_shape=jax.ShapeDtypeStruct((depth, bsz, n), F32),
        compiler_params=pltpu.CompilerParams(
            dimension_semantics=("arbitrary", "arbitrary"),
            vmem_limit_bytes=VMEM_LIMIT),
        name="modulation",
    )(c, w_ada, b_ada.reshape(depth, 1, n))


def _route_and_sort(x_new, mod_ref, gffn_ref, rw_hi_ref, rw_lo_ref, rb_ref, ltri_ref, utri_ref,
                    xsl_ref, route_ref, cnt_ref):
    tm = x_new.shape[0]
    sh_f = mod_ref[0, 3:4, :]
    sc_f = mod_ref[0, 4:5, :]
    h2 = _rms(x_new) * gffn_ref[...] * (1.0 + sc_f) + sh_f

    hh = h2.astype(BF16)
    hl = (h2 - hh.astype(F32)).astype(BF16)
    w_hi = rw_hi_ref[...]
    logits = _dot(hh, w_hi) + _dot(hl, w_hi) + _dot(hh, rw_lo_ref[...]) + rb_ref[...]

    lane = lax.broadcasted_iota(I32, logits.shape, 1).astype(F32)
    neg = jnp.float32(-jnp.inf)
    big = jnp.float32(LANES)

    gl = jnp.where(lane < N_GROUPS, logits, neg)
    gmax = jnp.max(gl, axis=-1, keepdims=True)
    g_sel = jnp.min(jnp.where(gl == gmax, lane, big), axis=-1, keepdims=True)
    g_w = 1.0 / jnp.sum(jnp.exp(gl - gmax), axis=-1, keepdims=True)

    lo = ROUTE_COL0 + EXPERTS_PER_GROUP * g_sel
    el = jnp.where((lane >= lo) & (lane < lo + EXPERTS_PER_GROUP), logits, neg)
    m1 = jnp.max(el, axis=-1, keepdims=True)
    i1 = jnp.min(jnp.where(el == m1, lane, big), axis=-1, keepdims=True)
    el2 = jnp.where(lane == i1, neg, el)
    m2 = jnp.max(el2, axis=-1, keepdims=True)
    i2 = jnp.min(jnp.where(el2 == m2, lane, big), axis=-1, keepdims=True)
    t = jnp.exp(m2 - m1)
    gate1 = g_w / (1.0 + t)
    gate2 = g_w * t / (1.0 + t)

    is1 = lane == i1
    is2 = lane == i2
    onehot = jnp.where(is1 | is2, 1.0, 0.0)
    before = _dot(ltri_ref[...], onehot.astype(BF16))
    cnt = jnp.sum(onehot, axis=0, keepdims=True)
    pairs = jnp.floor((cnt + 1.0) * 0.5)
    pair_start = jnp.dot(jnp.broadcast_to(pairs, (SUBLANES, LANES)), utri_ref[...],
                         precision=lax.Precision.HIGHEST, preferred_element_type=F32)[0:1]
    pos = before + 2.0 * pair_start
    pos1 = jnp.sum(jnp.where(is1, pos, 0.0), axis=-1, keepdims=True)
    pos2 = jnp.sum(jnp.where(is2, pos, 0.0), axis=-1, keepdims=True)
    cnt_ref[0] = jnp.broadcast_to(cnt, (SUBLANES, LANES))

    route = jnp.where(lane == 0, pos1, 0.0)
    route = jnp.where(lane == 1, pos2, route)
    route = jnp.where(lane == 4, gate1, route)
    route = jnp.where(lane == 5, gate2, route)
    route_ref[...] = route

    pos_t = route.T
    srow = lax.broadcasted_iota(I32, (LOCAL_ROWS, tm), 0).astype(F32)
    perm = jnp.where((srow == pos_t[0:1, :]) | (srow == pos_t[1:2, :]), 1.0, 0.0).astype(BF16)
    _store_pairs(xsl_ref, _dot(perm, hh).astype(BF16))


def _even_kernel(tiles_per_seq,
                 x_ref, mod_ref, gmix_ref, gffn_ref, win_ref, gv_ref, ws_ref, bs_ref,
                 cw_ref, wout_ref, rw_hi_ref, rw_lo_ref, rb_ref, ltri_ref, utri_ref,
                 xo_ref, xsl_ref, route_ref, cnt_ref,
                 zc_scr):
    i = pl.program_id(0)
    tm = x_ref.shape[0]
    aw = gv_ref.shape[0]
    hd = aw // A_HEADS
    n_chunks = tm // CHUNK

    @pl.when(i == 0)
    def _():
        zc_scr[...] = jnp.zeros_like(zc_scr)

    x = x_ref[...]
    sh_m = mod_ref[0, 0:1, :]
    sc_m = mod_ref[0, 1:2, :]
    g_m = mod_ref[0, 2:3, :]
    h = _rms(x) * gmix_ref[...] * (1.0 + sc_m) + sh_m
    z = _dot(h.astype(BF16), win_ref[...])
    u = z[:, 0:aw]
    v = z[:, aw:2 * aw]
    b_gate = z[:, 2 * aw:3 * aw]
    c_gate = z[:, 3 * aw:4 * aw]
    x_in = z[:, 4 * aw:5 * aw]

    v_t = v.T
    row = lax.broadcasted_iota(I32, (CHUNK, CHUNK), 0)
    col = lax.broadcasted_iota(I32, (CHUNK, CHUNK), 1)
    causal = col <= row
    head_rows = []
    for hh in range(A_HEADS):
        vh = v_t[hh * hd:(hh + 1) * hd, :]
        msv = jnp.mean(vh * vh, axis=0, keepdims=True)
        vn = (vh * lax.rsqrt(msv + EPS) * gv_ref[hh * hd:(hh + 1) * hd, :]).astype(BF16)
        lhs = jnp.concatenate(
            [vn[:, c * CHUNK:(c + 1) * CHUNK] for c in range(n_chunks)], axis=0)
        w_m = jnp.where(causal, ws_ref[hh], 0.0).astype(BF16)
        sv_h = lax.dot_general(lhs, w_m, (((1,), (1,)), ((), ())),
                               preferred_element_type=F32)
        sv_h = sv_h + bs_ref[hh]
        head_rows.append(jnp.concatenate(
            [sv_h[c * hd:(c + 1) * hd, :] for c in range(n_chunks)], axis=1))
    sv = jnp.concatenate(head_rows, axis=0).T
    y_a = u * sv

    zc = c_gate * x_in
    first = (i % tiles_per_seq) == 0
    halo = zc_scr[tm:tm + SUBLANES, :]
    zc_scr[0:SUBLANES, :] = jnp.where(first, 0.0, halo)
    zc_scr[SUBLANES:SUBLANES + tm, :] = zc
    conv = cw_ref[2:3, :] * zc
    for k in range(CONV_WIDTH - 1):
        shift = CONV_WIDTH - 1 - k
        conv = conv + cw_ref[k:k + 1, :] * zc_scr[SUBLANES - shift:SUBLANES - shift + tm, :]
    y_b = b_gate * conv

    y = _dot(y_a.astype(BF16), wout_ref[0:aw, :]) + _dot(y_b.astype(BF16), wout_ref[aw:, :])
    x_new = x + g_m * y
    xo_ref[...] = x_new
    _route_and_sort(x_new, mod_ref, gffn_ref, rw_hi_ref, rw_lo_ref, rb_ref, ltri_ref, utri_ref,
                    xsl_ref, route_ref, cnt_ref)


def _odd_kernel(tiles_per_seq,
                x_ref, mod_ref, gmix_ref, gffn_ref, wpool_ref, pscale_ref,
                rw_hi_ref, rw_lo_ref, rb_ref, ltri_ref, utri_ref,
                xo_ref, xsl_ref, route_ref, cnt_ref,
                h_scr):
    i = pl.program_id(0)
    tm, d = x_ref.shape
    halo_rows = max(POOL_WINDOWS)
    gd = d // len(POOL_WINDOWS)

    @pl.when(i == 0)
    def _():
        h_scr[...] = jnp.zeros_like(h_scr)

    x = x_ref[...]
    sh_m = mod_ref[0, 0:1, :]
    sc_m = mod_ref[0, 1:2, :]
    g_m = mod_ref[0, 2:3, :]
    h = _rms(x) * gmix_ref[...] * (1.0 + sc_m) + sh_m

    tile_in_seq = i % tiles_per_seq
    first = tile_in_seq == 0
    halo = h_scr[tm:tm + halo_rows, :]
    h_scr[0:halo_rows, :] = jnp.where(first, 0.0, halo)
    h_scr[halo_rows:halo_rows + tm, :] = h

    pos = (tile_in_seq * tm + lax.broadcasted_iota(I32, (tm, 1), 0)).astype(F32)
    outs = []
    for g, win in enumerate(POOL_WINDOWS):
        cs = slice(g * gd, (g + 1) * gd)
        acc = h[:, cs]
        for j in range(1, win):
            acc = acc + h_scr[halo_rows - j:halo_rows - j + tm, cs]
        count = jnp.minimum(pos + 1.0, jnp.float32(win))
        pooled = acc / count - h[:, cs]
        outs.append(_dot(pooled.astype(BF16), wpool_ref[g]))
    y = jnp.concatenate(outs, axis=-1) * pscale_ref[...]
    x_new = x + g_m * y
    xo_ref[...] = x_new
    _route_and_sort(x_new, mod_ref, gffn_ref, rw_hi_ref, rw_lo_ref, rb_ref, ltri_ref, utri_ref,
                    xsl_ref, route_ref, cnt_ref)


def _mixer_call(kernel_fn, x, mod_l, seq, weights, scratch):
    t_tok, d = x.shape
    n_tiles = t_tok // TM
    tiles_per_seq = seq // TM
    n_slabs = d // LANES

    def const_spec(a):
        return pl.BlockSpec(a.shape, lambda i, nd=a.ndim: (0,) * nd)

    in_specs = [
        pl.BlockSpec((TM, d), lambda i: (i, 0)),
        pl.BlockSpec((1,) + mod_l.shape[1:], lambda i: (i // tiles_per_seq, 0, 0)),
    ] + [const_spec(w) for w in weights]
    out_shape = (
        jax.ShapeDtypeStruct((t_tok, d), F32),
        jax.ShapeDtypeStruct((n_tiles * LOCAL_PAIRS * n_slabs, LANES), U32),
        jax.ShapeDtypeStruct((t_tok, LANES), F32),
        jax.ShapeDtypeStruct((n_tiles, SUBLANES, LANES), F32),
    )
    out_specs = (
        pl.BlockSpec((TM, d), lambda i: (i, 0)),
        pl.BlockSpec((LOCAL_PAIRS * n_slabs, LANES), lambda i: (i, 0)),
        pl.BlockSpec((TM, LANES), lambda i: (i, 0)),
        pl.BlockSpec((1, SUBLANES, LANES), lambda i: (i, 0, 0)),
    )
    return pl.pallas_call(
        functools.partial(kernel_fn, tiles_per_seq),
        grid=(n_tiles,),
        in_specs=in_specs,
        out_specs=out_specs,
        out_shape=out_shape,
        scratch_shapes=scratch,
        compiler_params=pltpu.CompilerParams(
            dimension_semantics=("arbitrary",), vmem_limit_bytes=VMEM_LIMIT),
        name=kernel_fn.__name__.strip("_"),
    )(x, mod_l, *weights)


def _for_each_piece(length, n_bits, fn):
    for bit in range(n_bits):
        size = 1 << bit
        offset = (length >> (bit + 1)) << (bit + 1)

        @pl.when(((length >> bit) & 1) == 1)
        def _(offset=offset, size=size):
            fn(offset, size)


def _dispatch_kernel(len_ref, loc_ref, glob_ref, tail_len_ref, tail_ref, n_valid_ref,
                     xsl_ref, xs_hbm, zero_scr, sem):
    i = pl.program_id(0)
    slab = xsl_ref.shape[0] // LOCAL_PAIRS
    n_blocks = xs_hbm.shape[0] // (BLK_PAIRS * slab)

    def run_copy(loc, glob):
        def make(offset, size):
            return pltpu.make_async_copy(xsl_ref.at[_pair_rows(loc + offset, size, slab)],
                                         xs_hbm.at[_pair_rows(glob + offset, size, slab)], sem)
        return make

    def for_each_run(action):
        def body(e, carry):
            k = i * N_EXPERTS + e
            make = run_copy(loc_ref[k], glob_ref[k])
            _for_each_piece(len_ref[k], RUN_BITS, lambda o, s: action(make(o, s)))
            return carry
        lax.fori_loop(0, N_EXPERTS, body, 0)

    for_each_run(lambda cp: cp.start())
    for_each_run(lambda cp: cp.wait())

    @pl.when(i == pl.num_programs(0) - 1)
    def _():
        zero_scr[...] = jnp.zeros_like(zero_scr)

        def for_each_tail(action):
            def body(e, carry):
                def make(offset, size):
                    return pltpu.make_async_copy(
                        zero_scr.at[_pair_rows(0, size, slab)],
                        xs_hbm.at[_pair_rows(tail_ref[e] + offset, size, slab)], sem)
                _for_each_piece(tail_len_ref[e], TAIL_BITS, lambda o, s: action(make(o, s)))
                return carry
            lax.fori_loop(0, N_EXPERTS, body, 0)

        for_each_tail(lambda cp: cp.start())
        for_each_tail(lambda cp: cp.wait())

        def for_each_spare(action):
            def body(b, carry):
                action(pltpu.make_async_copy(
                    zero_scr, xs_hbm.at[_pair_rows(b * BLK_PAIRS, BLK_PAIRS, slab)], sem))
                return carry
            lax.fori_loop(n_valid_ref[0], n_blocks, body, 0)

        for_each_spare(lambda cp: cp.start())
        for_each_spare(lambda cp: cp.wait())


def _dispatch(tables, xsl, n_pairs, slab):
    n_tiles = xsl.shape[0] // (LOCAL_PAIRS * slab)
    return pl.pallas_call(
        _dispatch_kernel,
        grid_spec=pltpu.PrefetchScalarGridSpec(
            num_scalar_prefetch=6,
            grid=(n_tiles,),
            in_specs=[pl.BlockSpec((LOCAL_PAIRS * slab, LANES), lambda i, *_: (i, 0))],
            out_specs=pl.BlockSpec(memory_space=pl.ANY),
            scratch_shapes=[pltpu.VMEM((BLK_PAIRS * slab, LANES), U32),
                            pltpu.SemaphoreType.DMA],
        ),
        out_shape=jax.ShapeDtypeStruct((n_pairs * slab, LANES), U32),
        compiler_params=pltpu.CompilerParams(
            dimension_semantics=("arbitrary",), vmem_limit_bytes=VMEM_LIMIT),
        name="dispatch",
    )(*tables, xsl)


def _expert_kernel(blk_expert_ref, n_valid_ref, xs_ref, wg_ref, wu_ref, wd_ref, ys_ref,
                   wg_scr, wu_scr, wd_scr):
    j = pl.program_id(0)
    prev = blk_expert_ref[jnp.maximum(j - 1, 0)]

    @pl.when((j == 0) | (blk_expert_ref[j] != prev))
    def _():
        wg_scr[...] = wg_ref[0, 0].astype(BF16)
        wu_scr[...] = wu_ref[0, 0].astype(BF16)
        wd_scr[...] = wd_ref[0, 0].astype(BF16)

    @pl.when(j < n_valid_ref[0])
    def _():
        xb = _load_pairs(xs_ref, wg_scr.shape[0])
        a = _dot(xb, wg_scr[...])
        b = _dot(xb, wu_scr[...])
        hm = (a * jax.nn.sigmoid(a) * b).astype(BF16)
        _store_pairs(ys_ref, _dot(hm, wd_scr[...]).astype(BF16))

    @pl.when(j >= n_valid_ref[0])
    def _():
        ys_ref[...] = jnp.zeros_like(ys_ref)


def _experts(blk_expert, n_valid, xs, layer, w_gate, w_up, w_down):
    _, _, d, de = w_gate.shape
    blk_rows = BLK_PAIRS * (d // LANES)
    n_blocks = xs.shape[0] // blk_rows

    def row_map(j, be, nv):
        return (jnp.minimum(j, nv[0] - 1), 0)

    def w_map(j, be, nv):
        return (layer, be[j], 0, 0)

    return pl.pallas_call(
        _expert_kernel,
        grid_spec=pltpu.PrefetchScalarGridSpec(
            num_scalar_prefetch=2,
            grid=(n_blocks,),
            in_specs=[
                pl.BlockSpec((blk_rows, LANES), row_map),
                pl.BlockSpec((1, 1, d, de), w_map),
                pl.BlockSpec((1, 1, d, de), w_map),
                pl.BlockSpec((1, 1, de, d), w_map),
            ],
            out_specs=pl.BlockSpec((blk_rows, LANES), lambda j, be, nv: (j, 0)),
            scratch_shapes=[pltpu.VMEM((d, de), BF16), pltpu.VMEM((d, de), BF16),
                            pltpu.VMEM((de, d), BF16)],
        ),
        out_shape=jax.ShapeDtypeStruct(xs.shape, U32),
        compiler_params=pltpu.CompilerParams(
            dimension_semantics=("arbitrary",), vmem_limit_bytes=VMEM_LIMIT),
        name="experts",
    )(blk_expert, n_valid, xs, w_gate, w_up, w_down)


def _combine_kernel(final, len_ref, loc_ref, glob_ref, ys_hbm, x_ref, route_ref, mod_ref,
                    gfin_ref, xo_ref, ysl_scr, sem):
    i = pl.program_id(0)
    tm, d = x_ref.shape
    slab = d // LANES

    @pl.when(i == 0)
    def _():
        ysl_scr[...] = jnp.zeros_like(ysl_scr)

    def for_each_run(action):
        def body(e, carry):
            k = i * N_EXPERTS + e
            loc = loc_ref[k]
            glob = glob_ref[k]

            def make(offset, size):
                return pltpu.make_async_copy(ys_hbm.at[_pair_rows(glob + offset, size, slab)],
                                             ysl_scr.at[_pair_rows(loc + offset, size, slab)], sem)
            _for_each_piece(len_ref[k], RUN_BITS, lambda o, s: action(make(o, s)))
            return carry
        lax.fori_loop(0, N_EXPERTS, body, 0)

    for_each_run(lambda cp: cp.start())
    for_each_run(lambda cp: cp.wait())

    ysl = _load_pairs(ysl_scr, d)
    srow = lax.broadcasted_iota(I32, (tm, LOCAL_ROWS), 1).astype(F32)
    y = jnp.zeros(x_ref.shape, F32)
    for k in range(2):
        sel = jnp.where(srow == route_ref[:, k:k + 1], 1.0, 0.0).astype(BF16)
        y = y + route_ref[:, 4 + k:5 + k] * _dot(sel, ysl)
    x_new = x_ref[...] + mod_ref[0, 5:6, :] * y
    if final:
        x_new = _rms(x_new) * gfin_ref[...]
    xo_ref[...] = x_new


def _combine(tables, ys, x, route, mod_l, seq, g_final, final):
    t_tok, d = x.shape
    tiles_per_seq = seq // TM
    return pl.pallas_call(
        functools.partial(_combine_kernel, final),
        grid_spec=pltpu.PrefetchScalarGridSpec(
            num_scalar_prefetch=3,
            grid=(t_tok // TM,),
            in_specs=[
                pl.BlockSpec(memory_space=pl.ANY),
                pl.BlockSpec((TM, d), lambda i, *_: (i, 0)),
                pl.BlockSpec((TM, LANES), lambda i, *_: (i, 0)),
                pl.BlockSpec((1,) + mod_l.shape[1:], lambda i, *_: (i // tiles_per_seq, 0, 0)),
                pl.BlockSpec((1, d), lambda i, *_: (0, 0)),
            ],
            out_specs=pl.BlockSpec((TM, d), lambda i, *_: (i, 0)),
            scratch_shapes=[pltpu.VMEM((LOCAL_PAIRS * (d // LANES), LANES), U32),
                            pltpu.SemaphoreType.DMA],
        ),
        out_shape=jax.ShapeDtypeStruct((t_tok, d), F32),
        compiler_params=pltpu.CompilerParams(
            dimension_semantics=("arbitrary",), vmem_limit_bytes=VMEM_LIMIT),
        name="combine_final" if final else "combine",
    )(*tables, ys, x, route, mod_l, g_final)


def _moe(x, xsl, route, cnt, mod_l, seq, layer, w_gate, w_up, w_down, g_final, final):
    t_tok = x.shape[0]
    n_tiles = t_tok // TM
    n_blocks = -(-(2 * t_tok + n_tiles * N_EXPERTS) // EXPERT_BLK) + N_EXPERTS
    n_pairs = n_blocks * BLK_PAIRS

    counts = cnt[:, 0, ROUTE_COL0:ROUTE_COL0 + N_EXPERTS].astype(I32)
    run_len = (counts + 1) // 2
    run_loc = jnp.cumsum(run_len, axis=1) - run_len
    seg_len = jnp.sum(run_len, axis=0)
    seg_pad = (seg_len + BLK_PAIRS - 1) // BLK_PAIRS * BLK_PAIRS
    seg_end = jnp.cumsum(seg_pad)
    seg_start = seg_end - seg_pad
    run_glob = seg_start[None, :] + jnp.cumsum(run_len, axis=0) - run_len
    run_tables = tuple(a.reshape(-1).astype(I32) for a in (run_len, run_loc, run_glob))
    tail_tables = ((seg_pad - seg_len).astype(I32), (seg_start + seg_len).astype(I32))
    n_valid = (seg_end[-1:] // BLK_PAIRS).astype(I32)
    blk_pair0 = jnp.arange(n_blocks, dtype=I32) * BLK_PAIRS
    blk_expert = jnp.sum((seg_end[None, :] <= blk_pair0[:, None]).astype(I32), axis=1)
    experts = jnp.arange(N_EXPERTS, dtype=I32)
    last_used = jnp.max(jnp.where(seg_len > 0, experts, 0))
    blk_expert = jnp.minimum(blk_expert, last_used).astype(I32)

    xs = _dispatch(run_tables + tail_tables + (n_valid,), xsl, n_pairs, x.shape[1] // LANES)
    ys = _experts(blk_expert, n_valid, xs, layer, w_gate, w_up, w_down)
    return _combine(run_tables, ys, x, route, mod_l, seq, g_final, final)


def kernel(x, c, w_ada, b_ada, norm_mix_g, norm_ffn_g, w_in_even, sgu_norm_g, w_spatial, b_spatial, conv_w, w_out_even, w_pool, pool_scale, w_group_router, b_group_router, w_expert_router, b_expert_router, moe_w_gate, moe_w_up, moe_w_down, final_norm_g):
    bsz, seq, d = x.shape
    depth = w_ada.shape[0]
    t_tok = bsz * seq
    assert seq % TM == 0 and d % LANES == 0 and w_spatial.shape[-1] == CHUNK

    mod = _modulation(c, w_ada, b_ada).reshape(depth, bsz, 6, d)
    ltri = jnp.tril(jnp.ones((TM, TM), BF16), -1)
    utri = jnp.triu(jnp.ones((LANES, LANES), F32), 1)
    g_final = final_norm_g.reshape(1, d)

    xf = x.reshape(t_tok, d)
    for l in range(depth):
        i = l // 2
        rw = jnp.concatenate([w_group_router[l], w_expert_router[l]], axis=1)
        rw = jnp.pad(rw, ((0, 0), (0, LANES - rw.shape[1])))
        rw_hi = rw.astype(BF16)
        rw_lo = (rw - rw_hi.astype(F32)).astype(BF16)
        rb = jnp.concatenate([b_group_router[l], b_expert_router[l]])
        rb = jnp.pad(rb, (0, LANES - rb.shape[0])).reshape(1, LANES)
        route_w = [rw_hi, rw_lo, rb, ltri, utri]
        gmix = norm_mix_g[l].reshape(1, d)
        gffn = norm_ffn_g[l].reshape(1, d)
        if l % 2 == 0:
            aw = sgu_norm_g.shape[1]
            weights = [gmix, gffn, w_in_even[i].astype(BF16), sgu_norm_g[i].reshape(aw, 1),
                       w_spatial[i], b_spatial[i].reshape(A_HEADS, 1, CHUNK), conv_w[i],
                       w_out_even[i].astype(BF16)] + route_w
            scratch = [pltpu.VMEM((TM + SUBLANES, conv_w.shape[-1]), F32)]
            xf, xsl, route, cnt = _mixer_call(_even_kernel, xf, mod[l], seq, weights, scratch)
        else:
            weights = [gmix, gffn, w_pool[i].astype(BF16), pool_scale[i].reshape(1, d)] + route_w
            scratch = [pltpu.VMEM((TM + max(POOL_WINDOWS), d), F32)]
            xf, xsl, route, cnt = _mixer_call(_odd_kernel, xf, mod[l], seq, weights, scratch)
        xf = _moe(xf, xsl, route, cnt, mod[l], seq, l, moe_w_gate, moe_w_up, moe_w_down,
                  g_final, l == depth - 1)
    return xf.reshape(bsz, seq, d)
```

```python
import functools

import jax
import jax.numpy as jnp
from jax import lax
from jax.experimental import pallas as pl
from jax.experimental.pallas import tpu as pltpu

F32 = jnp.float32
BF16 = jnp.bfloat16
U32 = jnp.uint32
I32 = jnp.int32

EPS = 1e-6
LANES = 128
SUBLANES = 8
CHUNK = 128
A_HEADS = 8
N_GROUPS = 4
EXPERTS_PER_GROUP = 8
N_EXPERTS = N_GROUPS * EXPERTS_PER_GROUP
POOL_WINDOWS = (2, 4, 8, 16)
CONV_WIDTH = 3
ROUTE_COL0 = N_GROUPS

TM = 512
LOCAL_ROWS = 2 * TM + LANES
LOCAL_PAIRS = LOCAL_ROWS // 2
EXPERT_BLK = 256
BLK_PAIRS = EXPERT_BLK // 2
RUN_BITS = (LOCAL_PAIRS - 1).bit_length()
TAIL_BITS = (BLK_PAIRS - 1).bit_length()
MOD_TN = 1536
VMEM_LIMIT = 56 * 1024 * 1024


def _rms(x):
    return x * lax.rsqrt(jnp.mean(x * x, axis=-1, keepdims=True) + EPS)


def _dot(a, b):
    return jnp.dot(a, b, preferred_element_type=F32)


def _store_pairs(ref, rows_bf16):
    words = pltpu.bitcast(rows_bf16, U32)
    n_pairs, d = words.shape
    slab = d // LANES
    for c in range(slab):
        ref[pl.ds(c, n_pairs, stride=slab), :] = words[:, c * LANES:(c + 1) * LANES]


def _load_pairs(ref, d):
    slab = d // LANES
    n_pairs = ref.shape[0] // slab
    words = jnp.concatenate(
        [ref[pl.ds(c, n_pairs, stride=slab), :] for c in range(slab)], axis=-1)
    return pltpu.bitcast(words, BF16)


def _pair_rows(first_pair, n_pairs, slab):
    return pl.ds(pl.multiple_of(first_pair * slab, slab), n_pairs * slab)


def _mod_kernel(c_ref, w_ref, b_ref, o_ref):
    c = c_ref[...]
    ca = c * jax.nn.sigmoid(c)
    o_ref[0] = jnp.dot(ca, w_ref[0], precision=lax.Precision.HIGHEST,
                       preferred_element_type=F32) + b_ref[0]


def _modulation(c, w_ada, b_ada):
    depth, d, n = w_ada.shape
    bsz = c.shape[0]
    return pl.pallas_call(
        _mod_kernel,
        grid=(depth, n // MOD_TN),
        in_specs=[
            pl.BlockSpec((bsz, d), lambda l, j: (0, 0)),
            pl.BlockSpec((1, d, MOD_TN), lambda l, j: (l, 0, j)),
            pl.BlockSpec((1, 1, MOD_TN), lambda l, j: (l, 0, j)),
        ],
        out_specs=pl.BlockSpec((1, bsz, MOD_TN), lambda l, j: (l, 0, j)),
        out_shape=jax.ShapeDtypeStruct((depth, bsz, n), F32),
        compiler_params=pltpu.CompilerParams(
            dimension_semantics=("arbitrary", "arbitrary"),
            vmem_limit_bytes=VMEM_LIMIT),
        name="modulation",
    )(c, w_ada, b_ada.reshape(depth, 1, n))


def _route_and_sort(x_new, mod_ref, gffn_ref, rw_hi_ref, rw_lo_ref, rb_ref, ltri_ref, utri_ref,
                    xsl_ref, route_ref, cnt_ref):
    tm = x_new.shape[0]
    sh_f = mod_ref[0, 3:4, :]
    sc_f = mod_ref[0, 4:5, :]
    h2 = _rms(x_new) * gffn_ref[...] * (1.0 + sc_f) + sh_f

    hh = h2.astype(BF16)
    hl = (h2 - hh.astype(F32)).astype(BF16)
    w_hi = rw_hi_ref[...]
    logits = _dot(hh, w_hi) + _dot(hl, w_hi) + _dot(hh, rw_lo_ref[...]) + rb_ref[...]

    lane = lax.broadcasted_iota(I32, logits.shape, 1).astype(F32)
    neg = jnp.float32(-jnp.inf)
    big = jnp.float32(LANES)

    gl = jnp.where(lane < N_GROUPS, logits, neg)
    gmax = jnp.max(gl, axis=-1, keepdims=True)
    g_sel = jnp.min(jnp.where(gl == gmax, lane, big), axis=-1, keepdims=True)
    g_w = 1.0 / jnp.sum(jnp.exp(gl - gmax), axis=-1, keepdims=True)

    lo = ROUTE_COL0 + EXPERTS_PER_GROUP * g_sel
    el = jnp.where((lane >= lo) & (lane < lo + EXPERTS_PER_GROUP), logits, neg)
    m1 = jnp.max(el, axis=-1, keepdims=True)
    i1 = jnp.min(jnp.where(el == m1, lane, big), axis=-1, keepdims=True)
    el2 = jnp.where(lane == i1, neg, el)
    m2 = jnp.max(el2, axis=-1, keepdims=True)
    i2 = jnp.min(jnp.where(el2 == m2, lane, big), axis=-1, keepdims=True)
    t = jnp.exp(m2 - m1)
    gate1 = g_w / (1.0 + t)
    gate2 = g_w * t / (1.0 + t)

    is1 = lane == i1
    is2 = lane == i2
    onehot = jnp.where(is1 | is2, 1.0, 0.0)
    before = _dot(ltri_ref[...], onehot.astype(BF16))
    cnt = jnp.sum(onehot, axis=0, keepdims=True)
    pairs = jnp.floor((cnt + 1.0) * 0.5)
    pair_start = jnp.dot(jnp.broadcast_to(pairs, (SUBLANES, LANES)), utri_ref[...],
                         precision=lax.Precision.HIGHEST, preferred_element_type=F32)[0:1]
    pos = before + 2.0 * pair_start
    pos1 = jnp.sum(jnp.where(is1, pos, 0.0), axis=-1, keepdims=True)
    pos2 = jnp.sum(jnp.where(is2, pos, 0.0), axis=-1, keepdims=True)
    cnt_ref[0] = jnp.broadcast_to(cnt, (SUBLANES, LANES))

    route = jnp.where(lane == 0, pos1, 0.0)
    route = jnp.where(lane == 1, pos2, route)
    route = jnp.where(lane == 4, gate1, route)
    route = jnp.where(lane == 5, gate2, route)
    route_ref[...] = route

    pos_t = route.T
    srow = lax.broadcasted_iota(I32, (LOCAL_ROWS, tm), 0).astype(F32)
    perm = jnp.where((srow == pos_t[0:1, :]) | (srow == pos_t[1:2, :]), 1.0, 0.0).astype(BF16)
    _store_pairs(xsl_ref, _dot(perm, hh).astype(BF16))


def _even_kernel(tiles_per_seq,
                 x_ref, mod_ref, gmix_ref, gffn_ref, win_ref, gv_ref, ws_ref, bs_ref,
                 cw_ref, wout_ref, rw_hi_ref, rw_lo_ref, rb_ref, ltri_ref, utri_ref,
                 xo_ref, xsl_ref, route_ref, cnt_ref,
                 zc_scr):
    i = pl.program_id(0)
    tm = x_ref.shape[0]
    aw = gv_ref.shape[0]
    hd = aw // A_HEADS
    n_chunks = tm // CHUNK

    @pl.when(i == 0)
    def _():
        zc_scr[...] = jnp.zeros_like(zc_scr)

    x = x_ref[...]
    sh_m = mod_ref[0, 0:1, :]
    sc_m = mod_ref[0, 1:2, :]
    g_m = mod_ref[0, 2:3, :]
    h = _rms(x) * gmix_ref[...] * (1.0 + sc_m) + sh_m
    z = _dot(h.astype(BF16), win_ref[...])
    u = z[:, 0:aw]
    v = z[:, aw:2 * aw]
    b_gate = z[:, 2 * aw:3 * aw]
    c_gate = z[:, 3 * aw:4 * aw]
    x_in = z[:, 4 * aw:5 * aw]

    v_t = v.T
    row = lax.broadcasted_iota(I32, (CHUNK, CHUNK), 0)
    col = lax.broadcasted_iota(I32, (CHUNK, CHUNK), 1)
    causal = col <= row
    head_rows = []
    for hh in range(A_HEADS):
        vh = v_t[hh * hd:(hh + 1) * hd, :]
        msv = jnp.mean(vh * vh, axis=0, keepdims=True)
        vn = (vh * lax.rsqrt(msv + EPS) * gv_ref[hh * hd:(hh + 1) * hd, :]).astype(BF16)
        lhs = jnp.concatenate(
            [vn[:, c * CHUNK:(c + 1) * CHUNK] for c in range(n_chunks)], axis=0)
        w_m = jnp.where(causal, ws_ref[hh], 0.0).astype(BF16)
        sv_h = lax.dot_general(lhs, w_m, (((1,), (1,)), ((), ())),
                               preferred_element_type=F32)
        sv_h = sv_h + bs_ref[hh]
        head_rows.append(jnp.concatenate(
            [sv_h[c * hd:(c + 1) * hd, :] for c in range(n_chunks)], axis=1))
    sv = jnp.concatenate(head_rows, axis=0).T
    y_a = u * sv

    zc = c_gate * x_in
    first = (i % tiles_per_seq) == 0
    halo = zc_scr[tm:tm + SUBLANES, :]
    zc_scr[0:SUBLANES, :] = jnp.where(first, 0.0, halo)
    zc_scr[SUBLANES:SUBLANES + tm, :] = zc
    conv = cw_ref[2:3, :] * zc
    for k in range(CONV_WIDTH - 1):
        shift = CONV_WIDTH - 1 - k
        conv = conv + cw_ref[k:k + 1, :] * zc_scr[SUBLANES - shift:SUBLANES - shift + tm, :]
    y_b = b_gate * conv

    y = _dot(y_a.astype(BF16), wout_ref[0:aw, :]) + _dot(y_b.astype(BF16), wout_ref[aw:, :])
    x_new = x + g_m * y
    xo_ref[...] = x_new
    _route_and_sort(x_new, mod_ref, gffn_ref, rw_hi_ref, rw_lo_ref, rb_ref, ltri_ref, utri_ref,
                    xsl_ref, route_ref, cnt_ref)


def _odd_kernel(tiles_per_seq,
                x_ref, mod_ref, gmix_ref, gffn_ref, wpool_ref, pscale_ref,
                rw_hi_ref, rw_lo_ref, rb_ref, ltri_ref, utri_ref,
                xo_ref, xsl_ref, route_ref, cnt_ref,
                h_scr):
    i = pl.program_id(0)
    tm, d = x_ref.shape
    halo_rows = max(POOL_WINDOWS)
    gd = d // len(POOL_WINDOWS)

    @pl.when(i == 0)
    def _():
        h_scr[...] = jnp.zeros_like(h_scr)

    x = x_ref[...]
    sh_m = mod_ref[0, 0:1, :]
    sc_m = mod_ref[0, 1:2, :]
    g_m = mod_ref[0, 2:3, :]
    h = _rms(x) * gmix_ref[...] * (1.0 + sc_m) + sh_m

    tile_in_seq = i % tiles_per_seq
    first = tile_in_seq == 0
    halo = h_scr[tm:tm + halo_rows, :]
    h_scr[0:halo_rows, :] = jnp.where(first, 0.0, halo)
    h_scr[halo_rows:halo_rows + tm, :] = h

    pos = (tile_in_seq * tm + lax.broadcasted_iota(I32, (tm, 1), 0)).astype(F32)
    outs = []
    for g, win in enumerate(POOL_WINDOWS):
        cs = slice(g * gd, (g + 1) * gd)
        acc = h[:, cs]
        for j in range(1, win):
            acc = acc + h_scr[halo_rows - j:halo_rows - j + tm, cs]
        count = jnp.minimum(pos + 1.0, jnp.float32(win))
        pooled = acc / count - h[:, cs]
        outs.append(_dot(pooled.astype(BF16), wpool_ref[g]))
    y = jnp.concatenate(outs, axis=-1) * pscale_ref[...]
    x_new = x + g_m * y
    xo_ref[...] = x_new
    _route_and_sort(x_new, mod_ref, gffn_ref, rw_hi_ref, rw_lo_ref, rb_ref, ltri_ref, utri_ref,
                    xsl_ref, route_ref, cnt_ref)


def _mixer_call(kernel_fn, x, mod_l, seq, weights, scratch):
    t_tok, d = x.shape
    n_tiles = t_tok // TM
    tiles_per_seq = seq // TM
    n_slabs = d // LANES

    def const_spec(a):
        return pl.BlockSpec(a.shape, lambda i, nd=a.ndim: (0,) * nd)

    in_specs = [
        pl.BlockSpec((TM, d), lambda i: (i, 0)),
        pl.BlockSpec((1,) + mod_l.shape[1:], lambda i: (i // tiles_per_seq, 0, 0)),
    ] + [const_spec(w) for w in weights]
    out_shape = (
        jax.ShapeDtypeStruct((t_tok, d), F32),
        jax.ShapeDtypeStruct((n_tiles * LOCAL_PAIRS * n_slabs, LANES), U32),
        jax.ShapeDtypeStruct((t_tok, LANES), F32),
        jax.ShapeDtypeStruct((n_tiles, SUBLANES, LANES), F32),
    )
    out_specs = (
        pl.BlockSpec((TM, d), lambda i: (i, 0)),
        pl.BlockSpec((LOCAL_PAIRS * n_slabs, LANES), lambda i: (i, 0)),
        pl.BlockSpec((TM, LANES), lambda i: (i, 0)),
        pl.BlockSpec((1, SUBLANES, LANES), lambda i: (i, 0, 0)),
    )
    return pl.pallas_call(
        functools.partial(kernel_fn, tiles_per_seq),
        grid=(n_tiles,),
        in_specs=in_specs,
        out_specs=out_specs,
        out_shape=out_shape,
        scratch_shapes=scratch,
        compiler_params=pltpu.CompilerParams(
            dimension_semantics=("arbitrary",), vmem_limit_bytes=VMEM_LIMIT),
        name=kernel_fn.__name__.strip("_"),
    )(x, mod_l, *weights)


def _for_each_piece(length, n_bits, fn):
    for bit in range(n_bits):
        size = 1 << bit
        offset = (length >> (bit + 1)) << (bit + 1)

        @pl.when(((length >> bit) & 1) == 1)
        def _(offset=offset, size=size):
            fn(offset, size)


def _dispatch_kernel(len_ref, loc_ref, glob_ref, tot_ref, tail_len_ref, tail_ref, n_valid_ref,
                     xsl_ref, xs_hbm, zero_scr, sem):
    i = pl.program_id(0)
    slab = xsl_ref.shape[0] // LOCAL_PAIRS
    n_blocks = xs_hbm.shape[0] // (BLK_PAIRS * slab)

    def run_copy(loc, glob, size):
        return pltpu.make_async_copy(xsl_ref.at[_pair_rows(loc, size, slab)],
                                     xs_hbm.at[_pair_rows(glob, size, slab)], sem)

    def start_run(e, carry):
        k = i * N_EXPERTS + e
        loc = loc_ref[k]
        glob = glob_ref[k]
        _for_each_piece(len_ref[k], RUN_BITS, lambda o, s: run_copy(loc + o, glob + o, s).start())
        return carry

    lax.fori_loop(0, N_EXPERTS, start_run, 0)
    _for_each_piece(tot_ref[i], RUN_BITS, lambda o, s: run_copy(0, 0, s).wait())

    @pl.when(i == pl.num_programs(0) - 1)
    def _():
        zero_scr[...] = jnp.zeros_like(zero_scr)

        def for_each_tail(action):
            def body(e, carry):
                def make(offset, size):
                    return pltpu.make_async_copy(
                        zero_scr.at[_pair_rows(0, size, slab)],
                        xs_hbm.at[_pair_rows(tail_ref[e] + offset, size, slab)], sem)
                _for_each_piece(tail_len_ref[e], TAIL_BITS, lambda o, s: action(make(o, s)))
                return carry
            lax.fori_loop(0, N_EXPERTS, body, 0)

        for_each_tail(lambda cp: cp.start())
        for_each_tail(lambda cp: cp.wait())

        def for_each_spare(action):
            def body(b, carry):
                action(pltpu.make_async_copy(
                    zero_scr, xs_hbm.at[_pair_rows(b * BLK_PAIRS, BLK_PAIRS, slab)], sem))
                return carry
            lax.fori_loop(n_valid_ref[0], n_blocks, body, 0)

        for_each_spare(lambda cp: cp.start())
        for_each_spare(lambda cp: cp.wait())


def _dispatch(tables, xsl, n_pairs, slab):
    n_tiles = xsl.shape[0] // (LOCAL_PAIRS * slab)
    return pl.pallas_call(
        _dispatch_kernel,
        grid_spec=pltpu.PrefetchScalarGridSpec(
            num_scalar_prefetch=7,
            grid=(n_tiles,),
            in_specs=[pl.BlockSpec((LOCAL_PAIRS * slab, LANES), lambda i, *_: (i, 0))],
            out_specs=pl.BlockSpec(memory_space=pl.ANY),
            scratch_shapes=[pltpu.VMEM((BLK_PAIRS * slab, LANES), U32),
                            pltpu.SemaphoreType.DMA],
        ),
        out_shape=jax.ShapeDtypeStruct((n_pairs * slab, LANES), U32),
        compiler_params=pltpu.CompilerParams(
            dimension_semantics=("arbitrary",), vmem_limit_bytes=VMEM_LIMIT),
        name="dispatch",
    )(*tables, xsl)


def _expert_kernel(blk_expert_ref, n_valid_ref, xs_ref, wg_ref, wu_ref, wd_ref, ys_ref,
                   wg_scr, wu_scr, wd_scr):
    j = pl.program_id(0)
    prev = blk_expert_ref[jnp.maximum(j - 1, 0)]

    @pl.when((j == 0) | (blk_expert_ref[j] != prev))
    def _():
        wg_scr[...] = wg_ref[0, 0].astype(BF16)
        wu_scr[...] = wu_ref[0, 0].astype(BF16)
        wd_scr[...] = wd_ref[0, 0].astype(BF16)

    @pl.when(j < n_valid_ref[0])
    def _():
        xb = _load_pairs(xs_ref, wg_scr.shape[0])
        a = _dot(xb, wg_scr[...])
        b = _dot(xb, wu_scr[...])
        hm = (a * jax.nn.sigmoid(a) * b).astype(BF16)
        _store_pairs(ys_ref, _dot(hm, wd_scr[...]).astype(BF16))

    @pl.when(j >= n_valid_ref[0])
    def _():
        ys_ref[...] = jnp.zeros_like(ys_ref)


def _experts(blk_expert, n_valid, xs, layer, w_gate, w_up, w_down):
    _, _, d, de = w_gate.shape
    blk_rows = BLK_PAIRS * (d // LANES)
    n_blocks = xs.shape[0] // blk_rows

    def row_map(j, be, nv):
        return (jnp.maximum(jnp.minimum(j, nv[0] - 1), 0), 0)

    def w_map(j, be, nv):
        return (layer, be[j], 0, 0)

    return pl.pallas_call(
        _expert_kernel,
        grid_spec=pltpu.PrefetchScalarGridSpec(
            num_scalar_prefetch=2,
            grid=(n_blocks,),
            in_specs=[
                pl.BlockSpec((blk_rows, LANES), row_map),
                pl.BlockSpec((1, 1, d, de), w_map),
                pl.BlockSpec((1, 1, d, de), w_map),
                pl.BlockSpec((1, 1, de, d), w_map),
            ],
            out_specs=pl.BlockSpec((blk_rows, LANES), lambda j, be, nv: (j, 0)),
            scratch_shapes=[pltpu.VMEM((d, de), BF16), pltpu.VMEM((d, de), BF16),
                            pltpu.VMEM((de, d), BF16)],
        ),
        out_shape=jax.ShapeDtypeStruct(xs.shape, U32),
        compiler_params=pltpu.CompilerParams(
            dimension_semantics=("arbitrary",), vmem_limit_bytes=VMEM_LIMIT),
        name="experts",
    )(blk_expert, n_valid, xs, w_gate, w_up, w_down)


def _combine_kernel(final, len_ref, loc_ref, glob_ref, tot_ref, ys_hbm, x_ref, route_ref,
                    mod_ref, gfin_ref, xo_ref, ysl_scr, sem):
    i = pl.program_id(0)
    tm, d = x_ref.shape
    slab = d // LANES

    def run_copy(tile, glob, loc, size):
        slot = tile % 2
        return pltpu.make_async_copy(ys_hbm.at[_pair_rows(glob, size, slab)],
                                     ysl_scr.at[slot, _pair_rows(loc, size, slab)], sem.at[slot])

    def fetch(tile):
        def body(e, carry):
            k = tile * N_EXPERTS + e
            loc = loc_ref[k]
            glob = glob_ref[k]
            _for_each_piece(len_ref[k], RUN_BITS,
                            lambda o, s: run_copy(tile, glob + o, loc + o, s).start())
            return carry
        lax.fori_loop(0, N_EXPERTS, body, 0)

    @pl.when(i == 0)
    def _():
        ysl_scr[...] = jnp.zeros_like(ysl_scr)
        fetch(i)

    @pl.when(i + 1 < pl.num_programs(0))
    def _():
        fetch(i + 1)

    _for_each_piece(tot_ref[i], RUN_BITS, lambda o, s: run_copy(i, 0, 0, s).wait())

    ysl = _load_pairs(ysl_scr.at[i % 2], d)
    srow = lax.broadcasted_iota(I32, (tm, LOCAL_ROWS), 1).astype(F32)
    y = jnp.zeros(x_ref.shape, F32)
    for k in range(2):
        sel = jnp.where(srow == route_ref[:, k:k + 1], 1.0, 0.0).astype(BF16)
        y = y + route_ref[:, 4 + k:5 + k] * _dot(sel, ysl)
    x_new = x_ref[...] + mod_ref[0, 5:6, :] * y
    if final:
        x_new = _rms(x_new) * gfin_ref[...]
    xo_ref[...] = x_new


def _combine(tables, ys, x, route, mod_l, seq, g_final, final):
    t_tok, d = x.shape
    tiles_per_seq = seq // TM
    return pl.pallas_call(
        functools.partial(_combine_kernel, final),
        grid_spec=pltpu.PrefetchScalarGridSpec(
            num_scalar_prefetch=4,
            grid=(t_tok // TM,),
            in_specs=[
                pl.BlockSpec(memory_space=pl.ANY),
                pl.BlockSpec((TM, d), lambda i, *_: (i, 0)),
                pl.BlockSpec((TM, LANES), lambda i, *_: (i, 0)),
                pl.BlockSpec((1,) + mod_l.shape[1:], lambda i, *_: (i // tiles_per_seq, 0, 0)),
                pl.BlockSpec((1, d), lambda i, *_: (0, 0)),
            ],
            out_specs=pl.BlockSpec((TM, d), lambda i, *_: (i, 0)),
            scratch_shapes=[pltpu.VMEM((2, LOCAL_PAIRS * (d // LANES), LANES), U32),
                            pltpu.SemaphoreType.DMA((2,))],
        ),
        out_shape=jax.ShapeDtypeStruct((t_tok, d), F32),
        compiler_params=pltpu.CompilerParams(
            dimension_semantics=("arbitrary",), vmem_limit_bytes=VMEM_LIMIT),
        name="combine_final" if final else "combine",
    )(*tables, ys, x, route, mod_l, g_final)


def _moe(x, xsl, route, cnt, mod_l, seq, layer, w_gate, w_up, w_down, g_final, final):
    t_tok = x.shape[0]
    n_tiles = t_tok // TM
    n_blocks = -(-(2 * t_tok + n_tiles * N_EXPERTS) // EXPERT_BLK) + N_EXPERTS
    n_pairs = n_blocks * BLK_PAIRS

    counts = cnt[:, 0, ROUTE_COL0:ROUTE_COL0 + N_EXPERTS].astype(I32)
    run_len = (counts + 1) // 2
    run_loc = jnp.cumsum(run_len, axis=1) - run_len
    seg_len = jnp.sum(run_len, axis=0)
    seg_pad = (seg_len + BLK_PAIRS - 1) // BLK_PAIRS * BLK_PAIRS
    seg_end = jnp.cumsum(seg_pad)
    seg_start = seg_end - seg_pad
    run_glob = seg_start[None, :] + jnp.cumsum(run_len, axis=0) - run_len
    run_tables = tuple(a.reshape(-1).astype(I32)
                       for a in (run_len, run_loc, run_glob, jnp.sum(run_len, axis=1)))
    tail_tables = ((seg_pad - seg_len).astype(I32), (seg_start + seg_len).astype(I32))
    n_valid = (seg_end[-1:] // BLK_PAIRS).astype(I32)
    blk_pair0 = jnp.arange(n_blocks, dtype=I32) * BLK_PAIRS
    blk_expert = jnp.sum((seg_end[None, :] <= blk_pair0[:, None]).astype(I32), axis=1)
    experts = jnp.arange(N_EXPERTS, dtype=I32)
    last_used = jnp.max(jnp.where(seg_len > 0, experts, 0))
    blk_expert = jnp.minimum(blk_expert, last_used).astype(I32)

    xs = _dispatch(run_tables + tail_tables + (n_valid,), xsl, n_pairs, x.shape[1] // LANES)
    ys = _experts(blk_expert, n_valid, xs, layer, w_gate, w_up, w_down)
    return _combine(run_tables, ys, x, route, mod_l, seq, g_final, final)


def kernel(x, c, w_ada, b_ada, norm_mix_g, norm_ffn_g, w_in_even, sgu_norm_g, w_spatial, b_spatial, conv_w, w_out_even, w_pool, pool_scale, w_group_router, b_group_router, w_expert_router, b_expert_router, moe_w_gate, moe_w_up, moe_w_down, final_norm_g):
    bsz, seq, d = x.shape
    depth = w_ada.shape[0]
    t_tok = bsz * seq
    assert seq % TM == 0 and d % LANES == 0 and w_spatial.shape[-1] == CHUNK

    mod = _modulation(c, w_ada, b_ada).reshape(depth, bsz, 6, d)
    ltri = jnp.tril(jnp.ones((TM, TM), BF16), -1)
    utri = jnp.triu(jnp.ones((LANES, LANES), F32), 1)
    g_final = final_norm_g.reshape(1, d)

    xf = x.reshape(t_tok, d)
    for l in range(depth):
        i = l // 2
        rw = jnp.concatenate([w_group_router[l], w_expert_router[l]], axis=1)
        rw = jnp.pad(rw, ((0, 0), (0, LANES - rw.shape[1])))
        rw_hi = rw.astype(BF16)
        rw_lo = (rw - rw_hi.astype(F32)).astype(BF16)
        rb = jnp.concatenate([b_group_router[l], b_expert_router[l]])
        rb = jnp.pad(rb, (0, LANES - rb.shape[0])).reshape(1, LANES)
        route_w = [rw_hi, rw_lo, rb, ltri, utri]
        gmix = norm_mix_g[l].reshape(1, d)
        gffn = norm_ffn_g[l].reshape(1, d)
        if l % 2 == 0:
            aw = sgu_norm_g.shape[1]
            weights = [gmix, gffn, w_in_even[i].astype(BF16), sgu_norm_g[i].reshape(aw, 1),
                       w_spatial[i], b_spatial[i].reshape(A_HEADS, 1, CHUNK), conv_w[i],
                       w_out_even[i].astype(BF16)] + route_w
            scratch = [pltpu.VMEM((TM + SUBLANES, conv_w.shape[-1]), F32)]
            xf, xsl, route, cnt = _mixer_call(_even_kernel, xf, mod[l], seq, weights, scratch)
        else:
            weights = [gmix, gffn, w_pool[i].astype(BF16), pool_scale[i].reshape(1, d)] + route_w
            scratch = [pltpu.VMEM((TM + max(POOL_WINDOWS), d), F32)]
            xf, xsl, route, cnt = _mixer_call(_odd_kernel, xf, mod[l], seq, weights, scratch)
        xf = _moe(xf, xsl, route, cnt, mod[l], seq, l, moe_w_gate, moe_w_up, moe_w_down,
                  g_final, l == depth - 1)
    return xf.reshape(bsz, seq, d)
```

```python
import functools

import jax
import jax.numpy as jnp
from jax import lax
from jax.experimental import pallas as pl
from jax.experimental.pallas import tpu as pltpu

F32 = jnp.float32
BF16 = jnp.bfloat16
U32 = jnp.uint32
I32 = jnp.int32

EPS = 1e-6
LANES = 128
SUBLANES = 8
CHUNK = 128
A_HEADS = 8
N_GROUPS = 4
EXPERTS_PER_GROUP = 8
N_EXPERTS = N_GROUPS * EXPERTS_PER_GROUP
POOL_WINDOWS = (2, 4, 8, 16)
CONV_WIDTH = 3
ROUTE_COL0 = N_GROUPS

TM = 512
LOCAL_ROWS = 2 * TM + LANES
LOCAL_PAIRS = LOCAL_ROWS // 2
EXPERT_BLK = 256
BLK_PAIRS = EXPERT_BLK // 2
RUN_BITS = (LOCAL_PAIRS - 1).bit_length()
TAIL_BITS = (BLK_PAIRS - 1).bit_length()
MOD_TN = 1536
VMEM_LIMIT = 56 * 1024 * 1024


def _rms(x):
    return x * lax.rsqrt(jnp.mean(x * x, axis=-1, keepdims=True) + EPS)


def _dot(a, b):
    return jnp.dot(a, b, preferred_element_type=F32)


def _store_pairs(ref, rows_bf16):
    words = pltpu.bitcast(rows_bf16, U32)
    n_pairs, d = words.shape
    slab = d // LANES
    for c in range(slab):
        ref[pl.ds(c, n_pairs, stride=slab), :] = words[:, c * LANES:(c + 1) * LANES]


def _load_pairs(ref, d):
    slab = d // LANES
    n_pairs = ref.shape[0] // slab
    words = jnp.concatenate(
        [ref[pl.ds(c, n_pairs, stride=slab), :] for c in range(slab)], axis=-1)
    return pltpu.bitcast(words, BF16)


def _pair_rows(first_pair, n_pairs, slab):
    return pl.ds(pl.multiple_of(first_pair * slab, slab), n_pairs * slab)


def _mod_kernel(c_ref, w_ref, b_ref, o_ref):
    c = c_ref[...]
    ca = c * jax.nn.sigmoid(c)
    o_ref[0] = jnp.dot(ca, w_ref[0], precision=lax.Precision.HIGHEST,
                       preferred_element_type=F32) + b_ref[0]


def _modulation(c, w_ada, b_ada):
    depth, d, n = w_ada.shape
    bsz = c.shape[0]
    return pl.pallas_call(
        _mod_kernel,
        grid=(depth, n // MOD_TN),
        in_specs=[
            pl.BlockSpec((bsz, d), lambda l, j: (0, 0)),
            pl.BlockSpec((1, d, MOD_TN), lambda l, j: (l, 0, j)),
            pl.BlockSpec((1, 1, MOD_TN), lambda l, j: (l, 0, j)),
        ],
        out_specs=pl.BlockSpec((1, bsz, MOD_TN), lambda l, j: (l, 0, j)),
        out_shape=jax.ShapeDtypeStruct((depth, bsz, n), F32),
        compiler_params=pltpu.CompilerParams(
            dimension_semantics=("arbitrary", "arbitrary"),
            vmem_limit_bytes=VMEM_LIMIT),
        name="modulation",
    )(c, w_ada, b_ada.reshape(depth, 1, n))


def _route_and_sort(x_new, mod_ref, gffn_ref, rw_ref, rb_ref, ltri_ref, utri_ref,
                    xsl_ref, route_ref, cnt_ref):
    tm = x_new.shape[0]
    sh_f = mod_ref[0, 3:4, :]
    sc_f = mod_ref[0, 4:5, :]
    h2 = _rms(x_new) * gffn_ref[...] * (1.0 + sc_f) + sh_f

    hh = h2.astype(BF16)
    hl = (h2 - hh.astype(F32)).astype(BF16)
    both = _dot(hh, rw_ref[...])
    logits = (both[:, 0:LANES] + both[:, LANES:2 * LANES] + _dot(hl, rw_ref[:, 0:LANES])
              + rb_ref[...])

    lane = lax.broadcasted_iota(I32, logits.shape, 1).astype(F32)
    neg = jnp.float32(-jnp.inf)
    big = jnp.float32(LANES)

    gl = jnp.where(lane < N_GROUPS, logits, neg)
    gmax = jnp.max(gl, axis=-1, keepdims=True)
    g_sel = jnp.min(jnp.where(gl == gmax, lane, big), axis=-1, keepdims=True)
    g_w = 1.0 / jnp.sum(jnp.exp(gl - gmax), axis=-1, keepdims=True)

    lo = ROUTE_COL0 + EXPERTS_PER_GROUP * g_sel
    el = jnp.where((lane >= lo) & (lane < lo + EXPERTS_PER_GROUP), logits, neg)
    m1 = jnp.max(el, axis=-1, keepdims=True)
    i1 = jnp.min(jnp.where(el == m1, lane, big), axis=-1, keepdims=True)
    el2 = jnp.where(lane == i1, neg, el)
    m2 = jnp.max(el2, axis=-1, keepdims=True)
    i2 = jnp.min(jnp.where(el2 == m2, lane, big), axis=-1, keepdims=True)
    t = jnp.exp(m2 - m1)
    gate1 = g_w / (1.0 + t)
    gate2 = g_w * t / (1.0 + t)

    is1 = lane == i1
    is2 = lane == i2
    onehot = jnp.where(is1 | is2, 1.0, 0.0)
    before = _dot(ltri_ref[...], onehot.astype(BF16))
    cnt = jnp.sum(onehot, axis=0, keepdims=True)
    pairs = jnp.floor((cnt + 1.0) * 0.5)
    pair_start = jnp.dot(jnp.broadcast_to(pairs, (SUBLANES, LANES)), utri_ref[...],
                         precision=lax.Precision.HIGHEST, preferred_element_type=F32)[0:1]
    pos = before + 2.0 * pair_start
    pos1 = jnp.sum(jnp.where(is1, pos, 0.0), axis=-1, keepdims=True)
    pos2 = jnp.sum(jnp.where(is2, pos, 0.0), axis=-1, keepdims=True)
    cnt_ref[0] = jnp.broadcast_to(cnt, (SUBLANES, LANES))

    route = jnp.where(lane == 0, pos1, 0.0)
    route = jnp.where(lane == 1, pos2, route)
    route = jnp.where(lane == 4, gate1, route)
    route = jnp.where(lane == 5, gate2, route)
    route_ref[...] = route

    pos_t = route.T
    srow = lax.broadcasted_iota(I32, (LOCAL_ROWS, tm), 0).astype(F32)
    perm = jnp.where((srow == pos_t[0:1, :]) | (srow == pos_t[1:2, :]), 1.0, 0.0).astype(BF16)
    _store_pairs(xsl_ref, _dot(perm, hh).astype(BF16))


def _even_kernel(tiles_per_seq,
                 x_ref, mod_ref, gmix_ref, gffn_ref, win_ref, gv_ref, ws_ref, bs_ref,
                 cw_ref, wout_ref, rw_ref, rb_ref, ltri_ref, utri_ref,
                 xo_ref, xsl_ref, route_ref, cnt_ref,
                 zc_scr):
    i = pl.program_id(0)
    tm = x_ref.shape[0]
    aw = gv_ref.shape[0]
    hd = aw // A_HEADS
    n_chunks = tm // CHUNK

    @pl.when(i == 0)
    def _():
        zc_scr[...] = jnp.zeros_like(zc_scr)

    x = x_ref[...]
    sh_m = mod_ref[0, 0:1, :]
    sc_m = mod_ref[0, 1:2, :]
    g_m = mod_ref[0, 2:3, :]
    h = _rms(x) * gmix_ref[...] * (1.0 + sc_m) + sh_m
    z = _dot(h.astype(BF16), win_ref[...])
    u = z[:, 0:aw]
    v = z[:, aw:2 * aw]
    b_gate = z[:, 2 * aw:3 * aw]
    c_gate = z[:, 3 * aw:4 * aw]
    x_in = z[:, 4 * aw:5 * aw]

    v_t = v.T
    row = lax.broadcasted_iota(I32, (CHUNK, CHUNK), 0)
    col = lax.broadcasted_iota(I32, (CHUNK, CHUNK), 1)
    causal = col <= row
    head_rows = []
    for hh in range(A_HEADS):
        vh = v_t[hh * hd:(hh + 1) * hd, :]
        msv = jnp.mean(vh * vh, axis=0, keepdims=True)
        vn = (vh * lax.rsqrt(msv + EPS) * gv_ref[hh * hd:(hh + 1) * hd, :]).astype(BF16)
        lhs = jnp.concatenate(
            [vn[:, c * CHUNK:(c + 1) * CHUNK] for c in range(n_chunks)], axis=0)
        w_m = jnp.where(causal, ws_ref[hh], 0.0).astype(BF16)
        sv_h = lax.dot_general(lhs, w_m, (((1,), (1,)), ((), ())),
                               preferred_element_type=F32)
        sv_h = sv_h + bs_ref[hh]
        head_rows.append(jnp.concatenate(
            [sv_h[c * hd:(c + 1) * hd, :] for c in range(n_chunks)], axis=1))
    sv = jnp.concatenate(head_rows, axis=0).T
    y_a = u * sv

    zc = c_gate * x_in
    first = (i % tiles_per_seq) == 0
    halo = zc_scr[tm:tm + SUBLANES, :]
    zc_scr[0:SUBLANES, :] = jnp.where(first, 0.0, halo)
    zc_scr[SUBLANES:SUBLANES + tm, :] = zc
    conv = cw_ref[2:3, :] * zc
    for k in range(CONV_WIDTH - 1):
        shift = CONV_WIDTH - 1 - k
        conv = conv + cw_ref[k:k + 1, :] * zc_scr[SUBLANES - shift:SUBLANES - shift + tm, :]
    y_b = b_gate * conv

    y = _dot(y_a.astype(BF16), wout_ref[0:aw, :]) + _dot(y_b.astype(BF16), wout_ref[aw:, :])
    x_new = x + g_m * y
    xo_ref[...] = x_new
    _route_and_sort(x_new, mod_ref, gffn_ref, rw_ref, rb_ref, ltri_ref, utri_ref,
                    xsl_ref, route_ref, cnt_ref)


def _odd_kernel(tiles_per_seq,
                x_ref, mod_ref, gmix_ref, gffn_ref, wpool_ref, pscale_ref,
                rw_ref, rb_ref, ltri_ref, utri_ref,
                xo_ref, xsl_ref, route_ref, cnt_ref,
                h_scr):
    i = pl.program_id(0)
    tm, d = x_ref.shape
    halo_rows = max(POOL_WINDOWS)
    gd = d // len(POOL_WINDOWS)

    @pl.when(i == 0)
    def _():
        h_scr[...] = jnp.zeros_like(h_scr)

    x = x_ref[...]
    sh_m = mod_ref[0, 0:1, :]
    sc_m = mod_ref[0, 1:2, :]
    g_m = mod_ref[0, 2:3, :]
    h = _rms(x) * gmix_ref[...] * (1.0 + sc_m) + sh_m

    tile_in_seq = i % tiles_per_seq
    first = tile_in_seq == 0
    halo = h_scr[tm:tm + halo_rows, :]
    h_scr[0:halo_rows, :] = jnp.where(first, 0.0, halo)
    h_scr[halo_rows:halo_rows + tm, :] = h

    pos = (tile_in_seq * tm + lax.broadcasted_iota(I32, (tm, 1), 0)).astype(F32)
    outs = []
    for g, win in enumerate(POOL_WINDOWS):
        cs = slice(g * gd, (g + 1) * gd)
        acc = h[:, cs]
        for j in range(1, win):
            acc = acc + h_scr[halo_rows - j:halo_rows - j + tm, cs]
        count = jnp.minimum(pos + 1.0, jnp.float32(win))
        pooled = acc / count - h[:, cs]
        outs.append(_dot(pooled.astype(BF16), wpool_ref[g]))
    y = jnp.concatenate(outs, axis=-1) * pscale_ref[...]
    x_new = x + g_m * y
    xo_ref[...] = x_new
    _route_and_sort(x_new, mod_ref, gffn_ref, rw_ref, rb_ref, ltri_ref, utri_ref,
                    xsl_ref, route_ref, cnt_ref)


def _mixer_call(kernel_fn, x, mod_l, seq, weights, scratch):
    t_tok, d = x.shape
    n_tiles = t_tok // TM
    tiles_per_seq = seq // TM
    n_slabs = d // LANES

    def const_spec(a):
        return pl.BlockSpec(a.shape, lambda i, nd=a.ndim: (0,) * nd)

    in_specs = [
        pl.BlockSpec((TM, d), lambda i: (i, 0)),
        pl.BlockSpec((1,) + mod_l.shape[1:], lambda i: (i // tiles_per_seq, 0, 0)),
    ] + [const_spec(w) for w in weights]
    out_shape = (
        jax.ShapeDtypeStruct((t_tok, d), F32),
        jax.ShapeDtypeStruct((n_tiles * LOCAL_PAIRS * n_slabs, LANES), U32),
        jax.ShapeDtypeStruct((t_tok, LANES), F32),
        jax.ShapeDtypeStruct((n_tiles, SUBLANES, LANES), F32),
    )
    out_specs = (
        pl.BlockSpec((TM, d), lambda i: (i, 0)),
        pl.BlockSpec((LOCAL_PAIRS * n_slabs, LANES), lambda i: (i, 0)),
        pl.BlockSpec((TM, LANES), lambda i: (i, 0)),
        pl.BlockSpec((1, SUBLANES, LANES), lambda i: (i, 0, 0)),
    )
    return pl.pallas_call(
        functools.partial(kernel_fn, tiles_per_seq),
        grid=(n_tiles,),
        in_specs=in_specs,
        out_specs=out_specs,
        out_shape=out_shape,
        scratch_shapes=scratch,
        compiler_params=pltpu.CompilerParams(
            dimension_semantics=("arbitrary",), vmem_limit_bytes=VMEM_LIMIT),
        name=kernel_fn.__name__.strip("_"),
    )(x, mod_l, *weights)


def _for_each_piece(length, n_bits, fn):
    for bit in range(n_bits):
        size = 1 << bit
        offset = (length >> (bit + 1)) << (bit + 1)

        @pl.when(((length >> bit) & 1) == 1)
        def _(offset=offset, size=size):
            fn(offset, size)


def _dispatch_kernel(len_ref, loc_ref, glob_ref, tot_ref, tail_len_ref, tail_ref, n_valid_ref,
                     xsl_ref, xs_hbm, zero_scr, sem):
    i = pl.program_id(0)
    slab = xsl_ref.shape[0] // LOCAL_PAIRS
    n_blocks = xs_hbm.shape[0] // (BLK_PAIRS * slab)

    def run_copy(loc, glob, size):
        return pltpu.make_async_copy(xsl_ref.at[_pair_rows(loc, size, slab)],
                                     xs_hbm.at[_pair_rows(glob, size, slab)], sem)

    def start_run(e, carry):
        k = i * N_EXPERTS + e
        loc = loc_ref[k]
        glob = glob_ref[k]
        _for_each_piece(len_ref[k], RUN_BITS, lambda o, s: run_copy(loc + o, glob + o, s).start())
        return carry

    lax.fori_loop(0, N_EXPERTS, start_run, 0)
    _for_each_piece(tot_ref[i], RUN_BITS, lambda o, s: run_copy(0, 0, s).wait())

    @pl.when(i == pl.num_programs(0) - 1)
    def _():
        zero_scr[...] = jnp.zeros_like(zero_scr)

        def for_each_tail(action):
            def body(e, carry):
                def make(offset, size):
                    return pltpu.make_async_copy(
                        zero_scr.at[_pair_rows(0, size, slab)],
                        xs_hbm.at[_pair_rows(tail_ref[e] + offset, size, slab)], sem)
                _for_each_piece(tail_len_ref[e], TAIL_BITS, lambda o, s: action(make(o, s)))
                return carry
            lax.fori_loop(0, N_EXPERTS, body, 0)

        for_each_tail(lambda cp: cp.start())
        for_each_tail(lambda cp: cp.wait())

        def for_each_spare(action):
            def body(b, carry):
                action(pltpu.make_async_copy(
                    zero_scr, xs_hbm.at[_pair_rows(b * BLK_PAIRS, BLK_PAIRS, slab)], sem))
                return carry
            lax.fori_loop(n_valid_ref[0], n_blocks, body, 0)

        for_each_spare(lambda cp: cp.start())
        for_each_spare(lambda cp: cp.wait())


def _dispatch(tables, xsl, n_pairs, slab):
    n_tiles = xsl.shape[0] // (LOCAL_PAIRS * slab)
    return pl.pallas_call(
        _dispatch_kernel,
        grid_spec=pltpu.PrefetchScalarGridSpec(
            num_scalar_prefetch=7,
            grid=(n_tiles,),
            in_specs=[pl.BlockSpec((LOCAL_PAIRS * slab, LANES), lambda i, *_: (i, 0))],
            out_specs=pl.BlockSpec(memory_space=pl.ANY),
            scratch_shapes=[pltpu.VMEM((BLK_PAIRS * slab, LANES), U32),
                            pltpu.SemaphoreType.DMA],
        ),
        out_shape=jax.ShapeDtypeStruct((n_pairs * slab, LANES), U32),
        compiler_params=pltpu.CompilerParams(
            dimension_semantics=("arbitrary",), vmem_limit_bytes=VMEM_LIMIT),
        name="dispatch",
    )(*tables, xsl)


def _expert_kernel(layer, blk_expert_ref, n_valid_ref, next_ref, slot_ref,
                   xs_ref, wg_hbm, wu_hbm, wd_hbm, ys_ref,
                   wg_stage, wu_stage, wd_stage, wg_scr, wu_scr, wd_scr, sem):
    j = pl.program_id(0)
    expert = blk_expert_ref[j]
    prev = blk_expert_ref[jnp.maximum(j - 1, 0)]

    def weight_copies(e, slot):
        return [pltpu.make_async_copy(hbm.at[layer, e], stage.at[slot], sem.at[slot])
                for hbm, stage in ((wg_hbm, wg_stage), (wu_hbm, wu_stage), (wd_hbm, wd_stage))]

    @pl.when(j == 0)
    def _():
        for cp in weight_copies(expert, slot_ref[expert]):
            cp.start()

    @pl.when((j == 0) | (expert != prev))
    def _():
        slot = slot_ref[expert]
        for cp in weight_copies(expert, slot):
            cp.wait()
        nxt = next_ref[expert]

        @pl.when(nxt != expert)
        def _():
            for cp in weight_copies(nxt, 1 - slot):
                cp.start()

        wg_scr[...] = wg_stage[slot].astype(BF16)
        wu_scr[...] = wu_stage[slot].astype(BF16)
        wd_scr[...] = wd_stage[slot].astype(BF16)

    @pl.when(j < n_valid_ref[0])
    def _():
        xb = _load_pairs(xs_ref, wg_scr.shape[0])
        a = _dot(xb, wg_scr[...])
        b = _dot(xb, wu_scr[...])
        hm = (a * jax.nn.sigmoid(a) * b).astype(BF16)
        _store_pairs(ys_ref, _dot(hm, wd_scr[...]).astype(BF16))

    @pl.when(j >= n_valid_ref[0])
    def _():
        ys_ref[...] = jnp.zeros_like(ys_ref)


def _experts(tables, xs, layer, w_gate, w_up, w_down):
    _, _, d, de = w_gate.shape
    blk_rows = BLK_PAIRS * (d // LANES)
    n_blocks = xs.shape[0] // blk_rows

    def row_map(j, be, nv, *_):
        return (jnp.maximum(jnp.minimum(j, nv[0] - 1), 0), 0)

    return pl.pallas_call(
        functools.partial(_expert_kernel, layer),
        grid_spec=pltpu.PrefetchScalarGridSpec(
            num_scalar_prefetch=4,
            grid=(n_blocks,),
            in_specs=[
                pl.BlockSpec((blk_rows, LANES), row_map),
                pl.BlockSpec(memory_space=pl.ANY),
                pl.BlockSpec(memory_space=pl.ANY),
                pl.BlockSpec(memory_space=pl.ANY),
            ],
            out_specs=pl.BlockSpec((blk_rows, LANES), lambda j, *_: (j, 0)),
            scratch_shapes=[
                pltpu.VMEM((2, d, de), F32), pltpu.VMEM((2, d, de), F32),
                pltpu.VMEM((2, de, d), F32),
                pltpu.VMEM((d, de), BF16), pltpu.VMEM((d, de), BF16), pltpu.VMEM((de, d), BF16),
                pltpu.SemaphoreType.DMA((2,)),
            ],
        ),
        out_shape=jax.ShapeDtypeStruct(xs.shape, U32),
        compiler_params=pltpu.CompilerParams(
            dimension_semantics=("arbitrary",), vmem_limit_bytes=VMEM_LIMIT),
        name="experts",
    )(*tables, xs, w_gate, w_up, w_down)


def _combine_kernel(final, len_ref, loc_ref, glob_ref, tot_ref, ys_hbm, x_ref, route_ref,
                    mod_ref, gfin_ref, xo_ref, ysl_scr, sem):
    i = pl.program_id(0)
    tm, d = x_ref.shape
    slab = d // LANES

    def run_copy(tile, glob, loc, size):
        slot = tile % 2
        return pltpu.make_async_copy(ys_hbm.at[_pair_rows(glob, size, slab)],
                                     ysl_scr.at[slot, _pair_rows(loc, size, slab)], sem.at[slot])

    def fetch(tile):
        def body(e, carry):
            k = tile * N_EXPERTS + e
            loc = loc_ref[k]
            glob = glob_ref[k]
            _for_each_piece(len_ref[k], RUN_BITS,
                            lambda o, s: run_copy(tile, glob + o, loc + o, s).start())
            return carry
        lax.fori_loop(0, N_EXPERTS, body, 0)

    @pl.when(i == 0)
    def _():
        ysl_scr[...] = jnp.zeros_like(ysl_scr)
        fetch(i)

    @pl.when(i + 1 < pl.num_programs(0))
    def _():
        fetch(i + 1)

    _for_each_piece(tot_ref[i], RUN_BITS, lambda o, s: run_copy(i, 0, 0, s).wait())

    ysl = _load_pairs(ysl_scr.at[i % 2], d)
    srow = lax.broadcasted_iota(I32, (tm, LOCAL_ROWS), 1).astype(F32)
    y = jnp.zeros(x_ref.shape, F32)
    for k in range(2):
        sel = jnp.where(srow == route_ref[:, k:k + 1], 1.0, 0.0).astype(BF16)
        y = y + route_ref[:, 4 + k:5 + k] * _dot(sel, ysl)
    x_new = x_ref[...] + mod_ref[0, 5:6, :] * y
    if final:
        x_new = _rms(x_new) * gfin_ref[...]
    xo_ref[...] = x_new


def _combine(tables, ys, x, route, mod_l, seq, g_final, final):
    t_tok, d = x.shape
    tiles_per_seq = seq // TM
    return pl.pallas_call(
        functools.partial(_combine_kernel, final),
        grid_spec=pltpu.PrefetchScalarGridSpec(
            num_scalar_prefetch=4,
            grid=(t_tok // TM,),
            in_specs=[
                pl.BlockSpec(memory_space=pl.ANY),
                pl.BlockSpec((TM, d), lambda i, *_: (i, 0)),
                pl.BlockSpec((TM, LANES), lambda i, *_: (i, 0)),
                pl.BlockSpec((1,) + mod_l.shape[1:], lambda i, *_: (i // tiles_per_seq, 0, 0)),
                pl.BlockSpec((1, d), lambda i, *_: (0, 0)),
            ],
            out_specs=pl.BlockSpec((TM, d), lambda i, *_: (i, 0)),
            scratch_shapes=[pltpu.VMEM((2, LOCAL_PAIRS * (d // LANES), LANES), U32),
                            pltpu.SemaphoreType.DMA((2,))],
        ),
        out_shape=jax.ShapeDtypeStruct((t_tok, d), F32),
        compiler_params=pltpu.CompilerParams(
            dimension_semantics=("arbitrary",), vmem_limit_bytes=VMEM_LIMIT),
        name="combine_final" if final else "combine",
    )(*tables, ys, x, route, mod_l, g_final)


def _moe(x, xsl, route, cnt, mod_l, seq, layer, w_gate, w_up, w_down, g_final, final):
    t_tok = x.shape[0]
    n_tiles = t_tok // TM
    n_blocks = -(-(2 * t_tok + n_tiles * N_EXPERTS) // EXPERT_BLK) + N_EXPERTS
    n_pairs = n_blocks * BLK_PAIRS

    counts = cnt[:, 0, ROUTE_COL0:ROUTE_COL0 + N_EXPERTS].astype(I32)
    run_len = (counts + 1) // 2
    run_loc = jnp.cumsum(run_len, axis=1) - run_len
    seg_len = jnp.sum(run_len, axis=0)
    seg_pad = (seg_len + BLK_PAIRS - 1) // BLK_PAIRS * BLK_PAIRS
    seg_end = jnp.cumsum(seg_pad)
    seg_start = seg_end - seg_pad
    run_glob = seg_start[None, :] + jnp.cumsum(run_len, axis=0) - run_len
    run_tables = tuple(a.reshape(-1).astype(I32)
                       for a in (run_len, run_loc, run_glob, jnp.sum(run_len, axis=1)))
    tail_tables = ((seg_pad - seg_len).astype(I32), (seg_start + seg_len).astype(I32))
    n_valid = (seg_end[-1:] // BLK_PAIRS).astype(I32)
    blk_pair0 = jnp.arange(n_blocks, dtype=I32) * BLK_PAIRS
    blk_expert = jnp.sum((seg_end[None, :] <= blk_pair0[:, None]).astype(I32), axis=1)
    experts = jnp.arange(N_EXPERTS, dtype=I32)
    used = seg_len > 0
    last_used = jnp.max(jnp.where(used, experts, 0))
    blk_expert = jnp.minimum(blk_expert, last_used).astype(I32)
    later = jnp.where((experts[None, :] > experts[:, None]) & used[None, :],
                      experts[None, :], N_EXPERTS)
    next_used = jnp.min(later, axis=1)
    next_used = jnp.where(next_used == N_EXPERTS, experts, next_used).astype(I32)
    slot_of = ((jnp.cumsum(used.astype(I32)) - 1) % 2).astype(I32)

    xs = _dispatch(run_tables + tail_tables + (n_valid,), xsl, n_pairs, x.shape[1] // LANES)
    ys = _experts((blk_expert, n_valid, next_used, slot_of), xs, layer, w_gate, w_up, w_down)
    return _combine(run_tables, ys, x, route, mod_l, seq, g_final, final)


def kernel(x, c, w_ada, b_ada, norm_mix_g, norm_ffn_g, w_in_even, sgu_norm_g, w_spatial, b_spatial, conv_w, w_out_even, w_pool, pool_scale, w_group_router, b_group_router, w_expert_router, b_expert_router, moe_w_gate, moe_w_up, moe_w_down, final_norm_g):
    bsz, seq, d = x.shape
    depth = w_ada.shape[0]
    t_tok = bsz * seq
    assert seq % TM == 0 and d % LANES == 0 and w_spatial.shape[-1] == CHUNK

    mod = _modulation(c, w_ada, b_ada).reshape(depth, bsz, 6, d)
    ltri = jnp.tril(jnp.ones((TM, TM), BF16), -1)
    utri = jnp.triu(jnp.ones((LANES, LANES), F32), 1)
    g_final = final_norm_g.reshape(1, d)

    xf = x.reshape(t_tok, d)
    for l in range(depth):
        i = l // 2
        rw = jnp.concatenate([w_group_router[l], w_expert_router[l]], axis=1)
        rw = jnp.pad(rw, ((0, 0), (0, LANES - rw.shape[1])))
        rw_hi = rw.astype(BF16)
        rw_lo = (rw - rw_hi.astype(F32)).astype(BF16)
        rb = jnp.concatenate([b_group_router[l], b_expert_router[l]])
        rb = jnp.pad(rb, (0, LANES - rb.shape[0])).reshape(1, LANES)
        route_w = [jnp.concatenate([rw_hi, rw_lo], axis=1), rb, ltri, utri]
        gmix = norm_mix_g[l].reshape(1, d)
        gffn = norm_ffn_g[l].reshape(1, d)
        if l % 2 == 0:
            aw = sgu_norm_g.shape[1]
            weights = [gmix, gffn, w_in_even[i].astype(BF16), sgu_norm_g[i].reshape(aw, 1),
                       w_spatial[i], b_spatial[i].reshape(A_HEADS, 1, CHUNK), conv_w[i],
                       w_out_even[i].astype(BF16)] + route_w
            scratch = [pltpu.VMEM((TM + SUBLANES, conv_w.shape[-1]), F32)]
            xf, xsl, route, cnt = _mixer_call(_even_kernel, xf, mod[l], seq, weights, scratch)
        else:
            weights = [gmix, gffn, w_pool[i].astype(BF16), pool_scale[i].reshape(1, d)] + route_w
            scratch = [pltpu.VMEM((TM + max(POOL_WINDOWS), d), F32)]
            xf, xsl, route, cnt = _mixer_call(_odd_kernel, xf, mod[l], seq, weights, scratch)
        xf = _moe(xf, xsl, route, cnt, mod[l], seq, l, moe_w_gate, moe_w_up, moe_w_down,
                  g_final, l == depth - 1)
    return xf.reshape(bsz, seq, d)
```

```python
import functools

import jax
import jax.numpy as jnp
from jax import lax
from jax.experimental import pallas as pl
from jax.experimental.pallas import tpu as pltpu

F32 = jnp.float32
BF16 = jnp.bfloat16
U32 = jnp.uint32
I32 = jnp.int32

EPS = 1e-6
LANES = 128
SUBLANES = 8
CHUNK = 128
A_HEADS = 8
N_GROUPS = 4
EXPERTS_PER_GROUP = 8
N_EXPERTS = N_GROUPS * EXPERTS_PER_GROUP
POOL_WINDOWS = (2, 4, 8, 16)
CONV_WIDTH = 3
ROUTE_COL0 = N_GROUPS

TM = 512
SORT_TM = 256
SECTIONS = TM // SORT_TM
LOCAL_ROWS = 2 * SORT_TM + LANES
LOCAL_PAIRS = LOCAL_ROWS // 2
TILE_PAIRS = SECTIONS * LOCAL_PAIRS
TILE_RUNS = SECTIONS * N_EXPERTS
EXPERT_BLK = 256
BLK_PAIRS = EXPERT_BLK // 2
RUN_BITS = (LOCAL_PAIRS - 1).bit_length()
TOT_BITS = TILE_PAIRS.bit_length()
TAIL_BITS = (BLK_PAIRS - 1).bit_length()
MOD_TN = 1536
VMEM_LIMIT = 56 * 1024 * 1024


def _rms(x):
    return x * lax.rsqrt(jnp.mean(x * x, axis=-1, keepdims=True) + EPS)


def _dot(a, b):
    return jnp.dot(a, b, preferred_element_type=F32)


def _store_pairs(ref, base, rows_bf16):
    words = pltpu.bitcast(rows_bf16, U32)
    n_pairs, d = words.shape
    slab = d // LANES
    for c in range(slab):
        ref[pl.ds(base + c, n_pairs, stride=slab), :] = words[:, c * LANES:(c + 1) * LANES]


def _load_pairs(ref, base, n_pairs, d):
    slab = d // LANES
    words = jnp.concatenate(
        [ref[pl.ds(base + c, n_pairs, stride=slab), :] for c in range(slab)], axis=-1)
    return pltpu.bitcast(words, BF16)


def _pair_rows(first_pair, n_pairs, slab):
    return pl.ds(pl.multiple_of(first_pair * slab, slab), n_pairs * slab)


def _mod_kernel(c_ref, w_ref, b_ref, o_ref):
    c = c_ref[...]
    ca = c * jax.nn.sigmoid(c)
    o_ref[0] = jnp.dot(ca, w_ref[0], precision=lax.Precision.HIGHEST,
                       preferred_element_type=F32) + b_ref[0]


def _modulation(c, w_ada, b_ada):
    depth, d, n = w_ada.shape
    bsz = c.shape[0]
    return pl.pallas_call(
        _mod_kernel,
        grid=(depth, n // MOD_TN),
        in_specs=[
            pl.BlockSpec((bsz, d), lambda l, j: (0, 0)),
            pl.BlockSpec((1, d, MOD_TN), lambda l, j: (l, 0, j)),
            pl.BlockSpec((1, 1, MOD_TN), lambda l, j: (l, 0, j)),
        ],
        out_specs=pl.BlockSpec((1, bsz, MOD_TN), lambda l, j: (l, 0, j)),
        out_shape=jax.ShapeDtypeStruct((depth, bsz, n), F32),
        compiler_params=pltpu.CompilerParams(
            dimension_semantics=("arbitrary", "arbitrary"),
            vmem_limit_bytes=VMEM_LIMIT),
        name="modulation",
    )(c, w_ada, b_ada.reshape(depth, 1, n))


def _route_and_sort(x_new, mod_ref, gffn_ref, rw_ref, rb_ref, ltri_ref, utri_ref,
                    xsl_ref, route_ref, cnt_ref):
    tm = x_new.shape[0]
    sh_f = mod_ref[0, 3:4, :]
    sc_f = mod_ref[0, 4:5, :]
    h2 = _rms(x_new) * gffn_ref[...] * (1.0 + sc_f) + sh_f

    hh = h2.astype(BF16)
    hl = (h2 - hh.astype(F32)).astype(BF16)
    both = _dot(hh, rw_ref[...])
    logits = (both[:, 0:LANES] + both[:, LANES:2 * LANES] + _dot(hl, rw_ref[:, 0:LANES])
              + rb_ref[...])

    lane = lax.broadcasted_iota(I32, logits.shape, 1).astype(F32)
    neg = jnp.float32(-jnp.inf)
    big = jnp.float32(LANES)

    gl = jnp.where(lane < N_GROUPS, logits, neg)
    gmax = jnp.max(gl, axis=-1, keepdims=True)
    g_sel = jnp.min(jnp.where(gl == gmax, lane, big), axis=-1, keepdims=True)
    g_w = 1.0 / jnp.sum(jnp.exp(gl - gmax), axis=-1, keepdims=True)

    lo = ROUTE_COL0 + EXPERTS_PER_GROUP * g_sel
    el = jnp.where((lane >= lo) & (lane < lo + EXPERTS_PER_GROUP), logits, neg)
    m1 = jnp.max(el, axis=-1, keepdims=True)
    i1 = jnp.min(jnp.where(el == m1, lane, big), axis=-1, keepdims=True)
    el2 = jnp.where(lane == i1, neg, el)
    m2 = jnp.max(el2, axis=-1, keepdims=True)
    i2 = jnp.min(jnp.where(el2 == m2, lane, big), axis=-1, keepdims=True)
    t = jnp.exp(m2 - m1)
    gate1 = g_w / (1.0 + t)
    gate2 = g_w * t / (1.0 + t)

    sections = [slice(s * SORT_TM, (s + 1) * SORT_TM) for s in range(tm // SORT_TM)]
    is1 = lane == i1
    is2 = lane == i2
    onehot = jnp.where(is1 | is2, 1.0, 0.0)
    before = _dot(ltri_ref[...], onehot.astype(BF16))
    cnts = [jnp.sum(onehot[rows], axis=0, keepdims=True) for rows in sections]
    cnt_rows = jnp.concatenate(
        cnts + [jnp.zeros((SUBLANES - len(sections), LANES), F32)], axis=0)
    pairs = jnp.floor((cnt_rows + 1.0) * 0.5)
    pair_start = jnp.dot(pairs, utri_ref[...], precision=lax.Precision.HIGHEST,
                         preferred_element_type=F32)
    pos = jnp.concatenate([before[rows] + 2.0 * pair_start[s:s + 1]
                           for s, rows in enumerate(sections)], axis=0)
    pos1 = jnp.sum(jnp.where(is1, pos, 0.0), axis=-1, keepdims=True)
    pos2 = jnp.sum(jnp.where(is2, pos, 0.0), axis=-1, keepdims=True)
    for s, cnt in enumerate(cnts):
        cnt_ref[s] = jnp.broadcast_to(cnt, (SUBLANES, LANES))

    route = jnp.where(lane == 0, pos1, 0.0)
    route = jnp.where(lane == 1, pos2, route)
    route = jnp.where(lane == 4, gate1, route)
    route = jnp.where(lane == 5, gate2, route)
    route_ref[...] = route

    pos_t = route.T
    srow = lax.broadcasted_iota(I32, (LOCAL_ROWS, SORT_TM), 0).astype(F32)
    section_rows = LOCAL_PAIRS * (x_new.shape[1] // LANES)
    for s, cols in enumerate(sections):
        perm = jnp.where((srow == pos_t[0:1, cols]) | (srow == pos_t[1:2, cols]), 1.0, 0.0)
        _store_pairs(xsl_ref, s * section_rows, _dot(perm.astype(BF16), hh[cols]).astype(BF16))


def _even_kernel(tiles_per_seq,
                 x_ref, mod_ref, gmix_ref, gffn_ref, win_ref, gv_ref, ws_ref, bs_ref,
                 cw_ref, wout_ref, rw_ref, rb_ref, ltri_ref, utri_ref,
                 xo_ref, xsl_ref, route_ref, cnt_ref,
                 zc_scr):
    i = pl.program_id(0)
    tm = x_ref.shape[0]
    aw = gv_ref.shape[0]
    hd = aw // A_HEADS
    n_chunks = tm // CHUNK

    @pl.when(i == 0)
    def _():
        zc_scr[...] = jnp.zeros_like(zc_scr)

    x = x_ref[...]
    sh_m = mod_ref[0, 0:1, :]
    sc_m = mod_ref[0, 1:2, :]
    g_m = mod_ref[0, 2:3, :]
    h = _rms(x) * gmix_ref[...] * (1.0 + sc_m) + sh_m
    z = _dot(h.astype(BF16), win_ref[...])
    u = z[:, 0:aw]
    v = z[:, aw:2 * aw]
    b_gate = z[:, 2 * aw:3 * aw]
    c_gate = z[:, 3 * aw:4 * aw]
    x_in = z[:, 4 * aw:5 * aw]

    v_t = v.T
    row = lax.broadcasted_iota(I32, (CHUNK, CHUNK), 0)
    col = lax.broadcasted_iota(I32, (CHUNK, CHUNK), 1)
    causal = col <= row
    head_rows = []
    for hh in range(A_HEADS):
        vh = v_t[hh * hd:(hh + 1) * hd, :]
        msv = jnp.mean(vh * vh, axis=0, keepdims=True)
        vn = (vh * lax.rsqrt(msv + EPS) * gv_ref[hh * hd:(hh + 1) * hd, :]).astype(BF16)
        lhs = jnp.concatenate(
            [vn[:, c * CHUNK:(c + 1) * CHUNK] for c in range(n_chunks)], axis=0)
        w_m = jnp.where(causal, ws_ref[hh], 0.0).astype(BF16)
        sv_h = lax.dot_general(lhs, w_m, (((1,), (1,)), ((), ())),
                               preferred_element_type=F32)
        sv_h = sv_h + bs_ref[hh]
        head_rows.append(jnp.concatenate(
            [sv_h[c * hd:(c + 1) * hd, :] for c in range(n_chunks)], axis=1))
    sv = jnp.concatenate(head_rows, axis=0).T
    y_a = u * sv

    zc = c_gate * x_in
    first = (i % tiles_per_seq) == 0
    halo = zc_scr[tm:tm + SUBLANES, :]
    zc_scr[0:SUBLANES, :] = jnp.where(first, 0.0, halo)
    zc_scr[SUBLANES:SUBLANES + tm, :] = zc
    conv = cw_ref[2:3, :] * zc
    for k in range(CONV_WIDTH - 1):
        shift = CONV_WIDTH - 1 - k
        conv = conv + cw_ref[k:k + 1, :] * zc_scr[SUBLANES - shift:SUBLANES - shift + tm, :]
    y_b = b_gate * conv

    y = _dot(y_a.astype(BF16), wout_ref[0:aw, :]) + _dot(y_b.astype(BF16), wout_ref[aw:, :])
    x_new = x + g_m * y
    xo_ref[...] = x_new
    _route_and_sort(x_new, mod_ref, gffn_ref, rw_ref, rb_ref, ltri_ref, utri_ref,
                    xsl_ref, route_ref, cnt_ref)


def _odd_kernel(tiles_per_seq,
                x_ref, mod_ref, gmix_ref, gffn_ref, wpool_ref, pscale_ref,
                rw_ref, rb_ref, ltri_ref, utri_ref,
                xo_ref, xsl_ref, route_ref, cnt_ref,
                h_scr):
    i = pl.program_id(0)
    tm, d = x_ref.shape
    halo_rows = max(POOL_WINDOWS)
    gd = d // len(POOL_WINDOWS)

    @pl.when(i == 0)
    def _():
        h_scr[...] = jnp.zeros_like(h_scr)

    x = x_ref[...]
    sh_m = mod_ref[0, 0:1, :]
    sc_m = mod_ref[0, 1:2, :]
    g_m = mod_ref[0, 2:3, :]
    h = _rms(x) * gmix_ref[...] * (1.0 + sc_m) + sh_m

    tile_in_seq = i % tiles_per_seq
    first = tile_in_seq == 0
    halo = h_scr[tm:tm + halo_rows, :]
    h_scr[0:halo_rows, :] = jnp.where(first, 0.0, halo)
    h_scr[halo_rows:halo_rows + tm, :] = h

    pos = (tile_in_seq * tm + lax.broadcasted_iota(I32, (tm, 1), 0)).astype(F32)
    outs = []
    for g, win in enumerate(POOL_WINDOWS):
        cs = slice(g * gd, (g + 1) * gd)
        acc = h[:, cs]
        for j in range(1, win):
            acc = acc + h_scr[halo_rows - j:halo_rows - j + tm, cs]
        count = jnp.minimum(pos + 1.0, jnp.float32(win))
        pooled = acc / count - h[:, cs]
        outs.append(_dot(pooled.astype(BF16), wpool_ref[g]))
    y = jnp.concatenate(outs, axis=-1) * pscale_ref[...]
    x_new = x + g_m * y
    xo_ref[...] = x_new
    _route_and_sort(x_new, mod_ref, gffn_ref, rw_ref, rb_ref, ltri_ref, utri_ref,
                    xsl_ref, route_ref, cnt_ref)


def _mixer_call(kernel_fn, x, mod_l, seq, weights, scratch):
    t_tok, d = x.shape
    n_tiles = t_tok // TM
    tiles_per_seq = seq // TM
    n_slabs = d // LANES

    def const_spec(a):
        return pl.BlockSpec(a.shape, lambda i, nd=a.ndim: (0,) * nd)

    in_specs = [
        pl.BlockSpec((TM, d), lambda i: (i, 0)),
        pl.BlockSpec((1,) + mod_l.shape[1:], lambda i: (i // tiles_per_seq, 0, 0)),
    ] + [const_spec(w) for w in weights]
    out_shape = (
        jax.ShapeDtypeStruct((t_tok, d), F32),
        jax.ShapeDtypeStruct((n_tiles * TILE_PAIRS * n_slabs, LANES), U32),
        jax.ShapeDtypeStruct((t_tok, LANES), F32),
        jax.ShapeDtypeStruct((n_tiles * SECTIONS, SUBLANES, LANES), F32),
    )
    out_specs = (
        pl.BlockSpec((TM, d), lambda i: (i, 0)),
        pl.BlockSpec((TILE_PAIRS * n_slabs, LANES), lambda i: (i, 0)),
        pl.BlockSpec((TM, LANES), lambda i: (i, 0)),
        pl.BlockSpec((SECTIONS, SUBLANES, LANES), lambda i: (i, 0, 0)),
    )
    return pl.pallas_call(
        functools.partial(kernel_fn, tiles_per_seq),
        grid=(n_tiles,),
        in_specs=in_specs,
        out_specs=out_specs,
        out_shape=out_shape,
        scratch_shapes=scratch,
        compiler_params=pltpu.CompilerParams(
            dimension_semantics=("arbitrary",), vmem_limit_bytes=VMEM_LIMIT),
        name=kernel_fn.__name__.strip("_"),
    )(x, mod_l, *weights)


def _for_each_piece(length, n_bits, fn):
    for bit in range(n_bits):
        size = 1 << bit
        offset = (length >> (bit + 1)) << (bit + 1)

        @pl.when(((length >> bit) & 1) == 1)
        def _(offset=offset, size=size):
            fn(offset, size)


def _dispatch_kernel(len_ref, loc_ref, glob_ref, tot_ref, tail_len_ref, tail_ref, n_valid_ref,
                     xsl_ref, xs_hbm, zero_scr, sem):
    i = pl.program_id(0)
    slab = xsl_ref.shape[0] // TILE_PAIRS
    n_blocks = xs_hbm.shape[0] // (BLK_PAIRS * slab)

    def run_copy(loc, glob, size):
        return pltpu.make_async_copy(xsl_ref.at[_pair_rows(loc, size, slab)],
                                     xs_hbm.at[_pair_rows(glob, size, slab)], sem)

    def start_run(r, carry):
        k = i * TILE_RUNS + r
        loc = loc_ref[k]
        glob = glob_ref[k]
        _for_each_piece(len_ref[k], RUN_BITS, lambda o, s: run_copy(loc + o, glob + o, s).start())
        return carry

    lax.fori_loop(0, TILE_RUNS, start_run, 0)
    _for_each_piece(tot_ref[i], TOT_BITS, lambda o, s: run_copy(0, 0, s).wait())

    @pl.when(i == pl.num_programs(0) - 1)
    def _():
        zero_scr[...] = jnp.zeros_like(zero_scr)

        def for_each_tail(action):
            def body(e, carry):
                def make(offset, size):
                    return pltpu.make_async_copy(
                        zero_scr.at[_pair_rows(0, size, slab)],
                        xs_hbm.at[_pair_rows(tail_ref[e] + offset, size, slab)], sem)
                _for_each_piece(tail_len_ref[e], TAIL_BITS, lambda o, s: action(make(o, s)))
                return carry
            lax.fori_loop(0, N_EXPERTS, body, 0)

        for_each_tail(lambda cp: cp.start())
        for_each_tail(lambda cp: cp.wait())

        def for_each_spare(action):
            def body(b, carry):
                action(pltpu.make_async_copy(
                    zero_scr, xs_hbm.at[_pair_rows(b * BLK_PAIRS, BLK_PAIRS, slab)], sem))
                return carry
            lax.fori_loop(n_valid_ref[0], n_blocks, body, 0)

        for_each_spare(lambda cp: cp.start())
        for_each_spare(lambda cp: cp.wait())


def _dispatch(tables, xsl, n_pairs, slab):
    n_tiles = xsl.shape[0] // (TILE_PAIRS * slab)
    return pl.pallas_call(
        _dispatch_kernel,
        grid_spec=pltpu.PrefetchScalarGridSpec(
            num_scalar_prefetch=7,
            grid=(n_tiles,),
            in_specs=[pl.BlockSpec((TILE_PAIRS * slab, LANES), lambda i, *_: (i, 0))],
            out_specs=pl.BlockSpec(memory_space=pl.ANY),
            scratch_shapes=[pltpu.VMEM((BLK_PAIRS * slab, LANES), U32),
                            pltpu.SemaphoreType.DMA],
        ),
        out_shape=jax.ShapeDtypeStruct((n_pairs * slab, LANES), U32),
        compiler_params=pltpu.CompilerParams(
            dimension_semantics=("arbitrary",), vmem_limit_bytes=VMEM_LIMIT),
        name="dispatch",
    )(*tables, xsl)


def _expert_kernel(layer, blk_expert_ref, n_valid_ref, next_ref, slot_ref,
                   xs_ref, wg_hbm, wu_hbm, wd_hbm, ys_ref,
                   wg_stage, wu_stage, wd_stage, wg_scr, wu_scr, wd_scr, sem):
    j = pl.program_id(0)
    expert = blk_expert_ref[j]
    prev = blk_expert_ref[jnp.maximum(j - 1, 0)]

    def weight_copies(e, slot):
        return [pltpu.make_async_copy(hbm.at[layer, e], stage.at[slot], sem.at[slot])
                for hbm, stage in ((wg_hbm, wg_stage), (wu_hbm, wu_stage), (wd_hbm, wd_stage))]

    @pl.when(j == 0)
    def _():
        for cp in weight_copies(expert, slot_ref[expert]):
            cp.start()

    @pl.when((j == 0) | (expert != prev))
    def _():
        slot = slot_ref[expert]
        for cp in weight_copies(expert, slot):
            cp.wait()
        nxt = next_ref[expert]

        @pl.when(nxt != expert)
        def _():
            for cp in weight_copies(nxt, 1 - slot):
                cp.start()

        wg_scr[...] = wg_stage[slot].astype(BF16)
        wu_scr[...] = wu_stage[slot].astype(BF16)
        wd_scr[...] = wd_stage[slot].astype(BF16)

    @pl.when(j < n_valid_ref[0])
    def _():
        xb = _load_pairs(xs_ref, 0, BLK_PAIRS, wg_scr.shape[0])
        a = _dot(xb, wg_scr[...])
        b = _dot(xb, wu_scr[...])
        hm = (a * jax.nn.sigmoid(a) * b).astype(BF16)
        _store_pairs(ys_ref, 0, _dot(hm, wd_scr[...]).astype(BF16))

    @pl.when(j >= n_valid_ref[0])
    def _():
        ys_ref[...] = jnp.zeros_like(ys_ref)


def _experts(tables, xs, layer, w_gate, w_up, w_down):
    _, _, d, de = w_gate.shape
    blk_rows = BLK_PAIRS * (d // LANES)
    n_blocks = xs.shape[0] // blk_rows

    def row_map(j, be, nv, *_):
        return (jnp.maximum(jnp.minimum(j, nv[0] - 1), 0), 0)

    return pl.pallas_call(
        functools.partial(_expert_kernel, layer),
        grid_spec=pltpu.PrefetchScalarGridSpec(
            num_scalar_prefetch=4,
            grid=(n_blocks,),
            in_specs=[
                pl.BlockSpec((blk_rows, LANES), row_map),
                pl.BlockSpec(memory_space=pl.ANY),
                pl.BlockSpec(memory_space=pl.ANY),
                pl.BlockSpec(memory_space=pl.ANY),
            ],
            out_specs=pl.BlockSpec((blk_rows, LANES), lambda j, *_: (j, 0)),
            scratch_shapes=[
                pltpu.VMEM((2, d, de), F32), pltpu.VMEM((2, d, de), F32),
                pltpu.VMEM((2, de, d), F32),
                pltpu.VMEM((d, de), BF16), pltpu.VMEM((d, de), BF16), pltpu.VMEM((de, d), BF16),
                pltpu.SemaphoreType.DMA((2,)),
            ],
        ),
        out_shape=jax.ShapeDtypeStruct(xs.shape, U32),
        compiler_params=pltpu.CompilerParams(
            dimension_semantics=("arbitrary",), vmem_limit_bytes=VMEM_LIMIT),
        name="experts",
    )(*tables, xs, w_gate, w_up, w_down)


def _combine_kernel(final, len_ref, loc_ref, glob_ref, tot_ref, ys_hbm, x_ref, route_ref,
                    mod_ref, gfin_ref, xo_ref, ysl_scr, sem):
    i = pl.program_id(0)
    tm, d = x_ref.shape
    slab = d // LANES

    def run_copy(tile, glob, loc, size):
        slot = tile % 2
        return pltpu.make_async_copy(ys_hbm.at[_pair_rows(glob, size, slab)],
                                     ysl_scr.at[slot, _pair_rows(loc, size, slab)], sem.at[slot])

    def fetch(tile):
        def body(r, carry):
            k = tile * TILE_RUNS + r
            loc = loc_ref[k]
            glob = glob_ref[k]
            _for_each_piece(len_ref[k], RUN_BITS,
                            lambda o, s: run_copy(tile, glob + o, loc + o, s).start())
            return carry
        lax.fori_loop(0, TILE_RUNS, body, 0)

    @pl.when(i == 0)
    def _():
        ysl_scr[...] = jnp.zeros_like(ysl_scr)
        fetch(i)

    @pl.when(i + 1 < pl.num_programs(0))
    def _():
        fetch(i + 1)

    _for_each_piece(tot_ref[i], TOT_BITS, lambda o, s: run_copy(i, 0, 0, s).wait())

    ysl_ref = ysl_scr.at[i % 2]
    srow = lax.broadcasted_iota(I32, (SORT_TM, LOCAL_ROWS), 1).astype(F32)
    y_sections = []
    for s in range(tm // SORT_TM):
        rows = slice(s * SORT_TM, (s + 1) * SORT_TM)
        ysl = _load_pairs(ysl_ref, s * LOCAL_PAIRS * slab, LOCAL_PAIRS, d)
        y = jnp.zeros((SORT_TM, d), F32)
        for k in range(2):
            sel = jnp.where(srow == route_ref[rows, k:k + 1], 1.0, 0.0).astype(BF16)
            y = y + route_ref[rows, 4 + k:5 + k] * _dot(sel, ysl)
        y_sections.append(y)
    x_new = x_ref[...] + mod_ref[0, 5:6, :] * jnp.concatenate(y_sections, axis=0)
    if final:
        x_new = _rms(x_new) * gfin_ref[...]
    xo_ref[...] = x_new


def _combine(tables, ys, x, route, mod_l, seq, g_final, final):
    t_tok, d = x.shape
    tiles_per_seq = seq // TM
    return pl.pallas_call(
        functools.partial(_combine_kernel, final),
        grid_spec=pltpu.PrefetchScalarGridSpec(
            num_scalar_prefetch=4,
            grid=(t_tok // TM,),
            in_specs=[
                pl.BlockSpec(memory_space=pl.ANY),
                pl.BlockSpec((TM, d), lambda i, *_: (i, 0)),
                pl.BlockSpec((TM, LANES), lambda i, *_: (i, 0)),
                pl.BlockSpec((1,) + mod_l.shape[1:], lambda i, *_: (i // tiles_per_seq, 0, 0)),
                pl.BlockSpec((1, d), lambda i, *_: (0, 0)),
            ],
            out_specs=pl.BlockSpec((TM, d), lambda i, *_: (i, 0)),
            scratch_shapes=[pltpu.VMEM((2, TILE_PAIRS * (d // LANES), LANES), U32),
                            pltpu.SemaphoreType.DMA((2,))],
        ),
        out_shape=jax.ShapeDtypeStruct((t_tok, d), F32),
        compiler_params=pltpu.CompilerParams(
            dimension_semantics=("arbitrary",), vmem_limit_bytes=VMEM_LIMIT),
        name="combine_final" if final else "combine",
    )(*tables, ys, x, route, mod_l, g_final)


def _moe(x, xsl, route, cnt, mod_l, seq, layer, w_gate, w_up, w_down, g_final, final):
    t_tok = x.shape[0]
    n_tiles = t_tok // TM
    n_blocks = -(-(2 * t_tok + n_tiles * TILE_RUNS) // EXPERT_BLK) + N_EXPERTS
    n_pairs = n_blocks * BLK_PAIRS

    counts = cnt[:, 0, ROUTE_COL0:ROUTE_COL0 + N_EXPERTS].astype(I32)
    run_len = (counts + 1) // 2
    section_base = (jnp.arange(counts.shape[0], dtype=I32) % SECTIONS) * LOCAL_PAIRS
    run_loc = jnp.cumsum(run_len, axis=1) - run_len + section_base[:, None]
    seg_len = jnp.sum(run_len, axis=0)
    seg_pad = (seg_len + BLK_PAIRS - 1) // BLK_PAIRS * BLK_PAIRS
    seg_end = jnp.cumsum(seg_pad)
    seg_start = seg_end - seg_pad
    run_glob = seg_start[None, :] + jnp.cumsum(run_len, axis=0) - run_len
    tile_tot = jnp.sum(run_len.reshape(n_tiles, TILE_RUNS), axis=1)
    run_tables = tuple(a.reshape(-1).astype(I32) for a in (run_len, run_loc, run_glob, tile_tot))
    tail_tables = ((seg_pad - seg_len).astype(I32), (seg_start + seg_len).astype(I32))
    n_valid = (seg_end[-1:] // BLK_PAIRS).astype(I32)
    blk_pair0 = jnp.arange(n_blocks, dtype=I32) * BLK_PAIRS
    blk_expert = jnp.sum((seg_end[None, :] <= blk_pair0[:, None]).astype(I32), axis=1)
    experts = jnp.arange(N_EXPERTS, dtype=I32)
    used = seg_len > 0
    last_used = jnp.max(jnp.where(used, experts, 0))
    blk_expert = jnp.minimum(blk_expert, last_used).astype(I32)
    later = jnp.where((experts[None, :] > experts[:, None]) & used[None, :],
                      experts[None, :], N_EXPERTS)
    next_used = jnp.min(later, axis=1)
    next_used = jnp.where(next_used == N_EXPERTS, experts, next_used).astype(I32)
    slot_of = ((jnp.cumsum(used.astype(I32)) - 1) % 2).astype(I32)

    xs = _dispatch(run_tables + tail_tables + (n_valid,), xsl, n_pairs, x.shape[1] // LANES)
    ys = _experts((blk_expert, n_valid, next_used, slot_of), xs, layer, w_gate, w_up, w_down)
    return _combine(run_tables, ys, x, route, mod_l, seq, g_final, final)


def kernel(x, c, w_ada, b_ada, norm_mix_g, norm_ffn_g, w_in_even, sgu_norm_g, w_spatial, b_spatial, conv_w, w_out_even, w_pool, pool_scale, w_group_router, b_group_router, w_expert_router, b_expert_router, moe_w_gate, moe_w_up, moe_w_down, final_norm_g):
    bsz, seq, d = x.shape
    depth = w_ada.shape[0]
    t_tok = bsz * seq
    assert seq % TM == 0 and d % LANES == 0 and w_spatial.shape[-1] == CHUNK

    mod = _modulation(c, w_ada, b_ada).reshape(depth, bsz, 6, d)
    tok = jnp.arange(TM)
    ltri = ((tok[:, None] > tok[None, :])
            & (tok[:, None] // SORT_TM == tok[None, :] // SORT_TM)).astype(BF16)
    utri = jnp.triu(jnp.ones((LANES, LANES), F32), 1)
    g_final = final_norm_g.reshape(1, d)

    xf = x.reshape(t_tok, d)
    for l in range(depth):
        i = l // 2
        rw = jnp.concatenate([w_group_router[l], w_expert_router[l]], axis=1)
        rw = jnp.pad(rw, ((0, 0), (0, LANES - rw.shape[1])))
        rw_hi = rw.astype(BF16)
        rw_lo = (rw - rw_hi.astype(F32)).astype(BF16)
        rb = jnp.concatenate([b_group_router[l], b_expert_router[l]])
        rb = jnp.pad(rb, (0, LANES - rb.shape[0])).reshape(1, LANES)
        route_w = [jnp.concatenate([rw_hi, rw_lo], axis=1), rb, ltri, utri]
        gmix = norm_mix_g[l].reshape(1, d)
        gffn = norm_ffn_g[l].reshape(1, d)
        if l % 2 == 0:
            aw = sgu_norm_g.shape[1]
            weights = [gmix, gffn, w_in_even[i].astype(BF16), sgu_norm_g[i].reshape(aw, 1),
                       w_spatial[i], b_spatial[i].reshape(A_HEADS, 1, CHUNK), conv_w[i],
                       w_out_even[i].astype(BF16)] + route_w
            scratch = [pltpu.VMEM((TM + SUBLANES, conv_w.shape[-1]), F32)]
            xf, xsl, route, cnt = _mixer_call(_even_kernel, xf, mod[l], seq, weights, scratch)
        else:
            weights = [gmix, gffn, w_pool[i].astype(BF16), pool_scale[i].reshape(1, d)] + route_w
            scratch = [pltpu.VMEM((TM + max(POOL_WINDOWS), d), F32)]
            xf, xsl, route, cnt = _mixer_call(_odd_kernel, xf, mod[l], seq, weights, scratch)
        xf = _moe(xf, xsl, route, cnt, mod[l], seq, l, moe_w_gate, moe_w_up, moe_w_down,
                  g_final, l == depth - 1)
    return xf.reshape(bsz, seq, d)
```

```python
import functools

import jax
import jax.numpy as jnp
from jax import lax
from jax.experimental import pallas as pl
from jax.experimental.pallas import tpu as pltpu

F32 = jnp.float32
BF16 = jnp.bfloat16
U32 = jnp.uint32
I32 = jnp.int32

EPS = 1e-6
LANES = 128
SUBLANES = 8
CHUNK = 128
A_HEADS = 8
N_GROUPS = 4
EXPERTS_PER_GROUP = 8
N_EXPERTS = N_GROUPS * EXPERTS_PER_GROUP
POOL_WINDOWS = (2, 4, 8, 16)
CONV_WIDTH = 3
ROUTE_COL0 = N_GROUPS

TM = 512
SORT_TM = 256
SECTIONS = TM // SORT_TM
LOCAL_ROWS = 2 * SORT_TM + LANES
LOCAL_PAIRS = LOCAL_ROWS // 2
TILE_PAIRS = SECTIONS * LOCAL_PAIRS
TILE_RUNS = SECTIONS * N_EXPERTS
EXPERT_BLK = 256
BLK_PAIRS = EXPERT_BLK // 2
TAIL_BITS = (BLK_PAIRS - 1).bit_length()
MOD_TN = 1536
VMEM_LIMIT = 56 * 1024 * 1024


def _rms(x):
    return x * lax.rsqrt(jnp.mean(x * x, axis=-1, keepdims=True) + EPS)


def _dot(a, b):
    return jnp.dot(a, b, preferred_element_type=F32)


def _store_pairs(ref, base, rows_bf16):
    words = pltpu.bitcast(rows_bf16, U32)
    n_pairs, d = words.shape
    slab = d // LANES
    for c in range(slab):
        ref[pl.ds(base + c, n_pairs, stride=slab), :] = words[:, c * LANES:(c + 1) * LANES]


def _load_pairs(ref, base, n_pairs, d):
    slab = d // LANES
    words = jnp.concatenate(
        [ref[pl.ds(base + c, n_pairs, stride=slab), :] for c in range(slab)], axis=-1)
    return pltpu.bitcast(words, BF16)


def _mod_kernel(c_ref, w_ref, b_ref, o_ref):
    c = c_ref[...]
    ca = c * jax.nn.sigmoid(c)
    o_ref[0] = jnp.dot(ca, w_ref[0], precision=lax.Precision.HIGHEST,
                       preferred_element_type=F32) + b_ref[0]


def _modulation(c, w_ada, b_ada):
    depth, d, n = w_ada.shape
    bsz = c.shape[0]
    return pl.pallas_call(
        _mod_kernel,
        grid=(depth, n // MOD_TN),
        in_specs=[
            pl.BlockSpec((bsz, d), lambda l, j: (0, 0)),
            pl.BlockSpec((1, d, MOD_TN), lambda l, j: (l, 0, j)),
            pl.BlockSpec((1, 1, MOD_TN), lambda l, j: (l, 0, j)),
        ],
        out_specs=pl.BlockSpec((1, bsz, MOD_TN), lambda l, j: (l, 0, j)),
        out_shape=jax.ShapeDtypeStruct((depth, bsz, n), F32),
        compiler_params=pltpu.CompilerParams(
            dimension_semantics=("arbitrary", "arbitrary"),
            vmem_limit_bytes=VMEM_LIMIT),
        name="modulation",
    )(c, w_ada, b_ada.reshape(depth, 1, n))


def _route_and_sort(x_new, mod_ref, gffn_ref, rw_ref, rb_ref, ltri_ref, utri_ref,
                    xsl_ref, route_ref, cnt_ref):
    tm = x_new.shape[0]
    sh_f = mod_ref[0, 3:4, :]
    sc_f = mod_ref[0, 4:5, :]
    h2 = _rms(x_new) * gffn_ref[...] * (1.0 + sc_f) + sh_f

    hh = h2.astype(BF16)
    hl = (h2 - hh.astype(F32)).astype(BF16)
    both = _dot(hh, rw_ref[...])
    logits = (both[:, 0:LANES] + both[:, LANES:2 * LANES] + _dot(hl, rw_ref[:, 0:LANES])
              + rb_ref[...])

    lane = lax.broadcasted_iota(I32, logits.shape, 1).astype(F32)
    neg = jnp.float32(-jnp.inf)
    big = jnp.float32(LANES)

    gl = jnp.where(lane < N_GROUPS, logits, neg)
    gmax = jnp.max(gl, axis=-1, keepdims=True)
    g_sel = jnp.min(jnp.where(gl == gmax, lane, big), axis=-1, keepdims=True)
    g_w = 1.0 / jnp.sum(jnp.exp(gl - gmax), axis=-1, keepdims=True)

    lo = ROUTE_COL0 + EXPERTS_PER_GROUP * g_sel
    el = jnp.where((lane >= lo) & (lane < lo + EXPERTS_PER_GROUP), logits, neg)
    m1 = jnp.max(el, axis=-1, keepdims=True)
    i1 = jnp.min(jnp.where(el == m1, lane, big), axis=-1, keepdims=True)
    el2 = jnp.where(lane == i1, neg, el)
    m2 = jnp.max(el2, axis=-1, keepdims=True)
    i2 = jnp.min(jnp.where(el2 == m2, lane, big), axis=-1, keepdims=True)
    t = jnp.exp(m2 - m1)
    gate1 = g_w / (1.0 + t)
    gate2 = g_w * t / (1.0 + t)

    sections = [slice(s * SORT_TM, (s + 1) * SORT_TM) for s in range(tm // SORT_TM)]
    is1 = lane == i1
    is2 = lane == i2
    onehot = jnp.where(is1 | is2, 1.0, 0.0)
    before = _dot(ltri_ref[...], onehot.astype(BF16))
    cnts = [jnp.sum(onehot[rows], axis=0, keepdims=True) for rows in sections]
    cnt_rows = jnp.concatenate(
        cnts + [jnp.zeros((SUBLANES - len(sections), LANES), F32)], axis=0)
    pairs = jnp.floor((cnt_rows + 1.0) * 0.5)
    pair_start = jnp.dot(pairs, utri_ref[...], precision=lax.Precision.HIGHEST,
                         preferred_element_type=F32)
    pos = jnp.concatenate([before[rows] + 2.0 * pair_start[s:s + 1]
                           for s, rows in enumerate(sections)], axis=0)
    pos1 = jnp.sum(jnp.where(is1, pos, 0.0), axis=-1, keepdims=True)
    pos2 = jnp.sum(jnp.where(is2, pos, 0.0), axis=-1, keepdims=True)
    for s, cnt in enumerate(cnts):
        cnt_ref[s] = jnp.broadcast_to(cnt, (SUBLANES, LANES))

    route = jnp.where(lane == 0, pos1, 0.0)
    route = jnp.where(lane == 1, pos2, route)
    route = jnp.where(lane == 4, gate1, route)
    route = jnp.where(lane == 5, gate2, route)
    route_ref[...] = route

    pos_t = route.T
    srow = lax.broadcasted_iota(I32, (LOCAL_ROWS, SORT_TM), 0).astype(F32)
    section_rows = LOCAL_PAIRS * (x_new.shape[1] // LANES)
    for s, cols in enumerate(sections):
        perm = jnp.where((srow == pos_t[0:1, cols]) | (srow == pos_t[1:2, cols]), 1.0, 0.0)
        _store_pairs(xsl_ref, s * section_rows, _dot(perm.astype(BF16), hh[cols]).astype(BF16))


def _even_kernel(tiles_per_seq,
                 x_ref, mod_ref, gmix_ref, gffn_ref, win_ref, gv_ref, ws_ref, bs_ref,
                 cw_ref, wout_ref, rw_ref, rb_ref, ltri_ref, utri_ref,
                 xo_ref, xsl_ref, route_ref, cnt_ref,
                 zc_scr):
    i = pl.program_id(0)
    tm = x_ref.shape[0]
    aw = gv_ref.shape[0]
    hd = aw // A_HEADS
    n_chunks = tm // CHUNK

    @pl.when(i == 0)
    def _():
        zc_scr[...] = jnp.zeros_like(zc_scr)

    x = x_ref[...]
    sh_m = mod_ref[0, 0:1, :]
    sc_m = mod_ref[0, 1:2, :]
    g_m = mod_ref[0, 2:3, :]
    h = _rms(x) * gmix_ref[...] * (1.0 + sc_m) + sh_m
    z = _dot(h.astype(BF16), win_ref[...])
    u = z[:, 0:aw]
    v = z[:, aw:2 * aw]
    b_gate = z[:, 2 * aw:3 * aw]
    c_gate = z[:, 3 * aw:4 * aw]
    x_in = z[:, 4 * aw:5 * aw]

    v_t = v.T
    row = lax.broadcasted_iota(I32, (CHUNK, CHUNK), 0)
    col = lax.broadcasted_iota(I32, (CHUNK, CHUNK), 1)
    causal = col <= row
    head_rows = []
    for hh in range(A_HEADS):
        vh = v_t[hh * hd:(hh + 1) * hd, :]
        msv = jnp.mean(vh * vh, axis=0, keepdims=True)
        vn = (vh * lax.rsqrt(msv + EPS) * gv_ref[hh * hd:(hh + 1) * hd, :]).astype(BF16)
        lhs = jnp.concatenate(
            [vn[:, c * CHUNK:(c + 1) * CHUNK] for c in range(n_chunks)], axis=0)
        w_m = jnp.where(causal, ws_ref[hh], 0.0).astype(BF16)
        sv_h = lax.dot_general(lhs, w_m, (((1,), (1,)), ((), ())),
                               preferred_element_type=F32)
        sv_h = sv_h + bs_ref[hh]
        head_rows.append(jnp.concatenate(
            [sv_h[c * hd:(c + 1) * hd, :] for c in range(n_chunks)], axis=1))
    sv = jnp.concatenate(head_rows, axis=0).T
    y_a = u * sv

    zc = c_gate * x_in
    first = (i % tiles_per_seq) == 0
    halo = zc_scr[tm:tm + SUBLANES, :]
    zc_scr[0:SUBLANES, :] = jnp.where(first, 0.0, halo)
    zc_scr[SUBLANES:SUBLANES + tm, :] = zc
    conv = cw_ref[2:3, :] * zc
    for k in range(CONV_WIDTH - 1):
        shift = CONV_WIDTH - 1 - k
        conv = conv + cw_ref[k:k + 1, :] * zc_scr[SUBLANES - shift:SUBLANES - shift + tm, :]
    y_b = b_gate * conv

    y = _dot(y_a.astype(BF16), wout_ref[0:aw, :]) + _dot(y_b.astype(BF16), wout_ref[aw:, :])
    x_new = x + g_m * y
    xo_ref[...] = x_new
    _route_and_sort(x_new, mod_ref, gffn_ref, rw_ref, rb_ref, ltri_ref, utri_ref,
                    xsl_ref, route_ref, cnt_ref)


def _odd_kernel(tiles_per_seq,
                x_ref, mod_ref, gmix_ref, gffn_ref, wpool_ref, pscale_ref,
                rw_ref, rb_ref, ltri_ref, utri_ref,
                xo_ref, xsl_ref, route_ref, cnt_ref,
                h_scr):
    i = pl.program_id(0)
    tm, d = x_ref.shape
    halo_rows = max(POOL_WINDOWS)
    gd = d // len(POOL_WINDOWS)

    @pl.when(i == 0)
    def _():
        h_scr[...] = jnp.zeros_like(h_scr)

    x = x_ref[...]
    sh_m = mod_ref[0, 0:1, :]
    sc_m = mod_ref[0, 1:2, :]
    g_m = mod_ref[0, 2:3, :]
    h = _rms(x) * gmix_ref[...] * (1.0 + sc_m) + sh_m

    tile_in_seq = i % tiles_per_seq
    first = tile_in_seq == 0
    halo = h_scr[tm:tm + halo_rows, :]
    h_scr[0:halo_rows, :] = jnp.where(first, 0.0, halo)
    h_scr[halo_rows:halo_rows + tm, :] = h

    pos = (tile_in_seq * tm + lax.broadcasted_iota(I32, (tm, 1), 0)).astype(F32)
    outs = []
    for g, win in enumerate(POOL_WINDOWS):
        cs = slice(g * gd, (g + 1) * gd)
        acc = h[:, cs]
        for j in range(1, win):
            acc = acc + h_scr[halo_rows - j:halo_rows - j + tm, cs]
        count = jnp.minimum(pos + 1.0, jnp.float32(win))
        pooled = acc / count - h[:, cs]
        outs.append(_dot(pooled.astype(BF16), wpool_ref[g]))
    y = jnp.concatenate(outs, axis=-1) * pscale_ref[...]
    x_new = x + g_m * y
    xo_ref[...] = x_new
    _route_and_sort(x_new, mod_ref, gffn_ref, rw_ref, rb_ref, ltri_ref, utri_ref,
                    xsl_ref, route_ref, cnt_ref)


def _mixer_call(kernel_fn, x, mod_l, seq, weights, scratch):
    t_tok, d = x.shape
    n_tiles = t_tok // TM
    tiles_per_seq = seq // TM
    n_slabs = d // LANES

    def const_spec(a):
        return pl.BlockSpec(a.shape, lambda i, nd=a.ndim: (0,) * nd)

    in_specs = [
        pl.BlockSpec((TM, d), lambda i: (i, 0)),
        pl.BlockSpec((1,) + mod_l.shape[1:], lambda i: (i // tiles_per_seq, 0, 0)),
    ] + [const_spec(w) for w in weights]
    out_shape = (
        jax.ShapeDtypeStruct((t_tok, d), F32),
        jax.ShapeDtypeStruct((n_tiles * TILE_PAIRS * n_slabs, LANES), U32),
        jax.ShapeDtypeStruct((t_tok, LANES), F32),
        jax.ShapeDtypeStruct((n_tiles * SECTIONS, SUBLANES, LANES), F32),
    )
    out_specs = (
        pl.BlockSpec((TM, d), lambda i: (i, 0)),
        pl.BlockSpec((TILE_PAIRS * n_slabs, LANES), lambda i: (i, 0)),
        pl.BlockSpec((TM, LANES), lambda i: (i, 0)),
        pl.BlockSpec((SECTIONS, SUBLANES, LANES), lambda i: (i, 0, 0)),
    )
    return pl.pallas_call(
        functools.partial(kernel_fn, tiles_per_seq),
        grid=(n_tiles,),
        in_specs=in_specs,
        out_specs=out_specs,
        out_shape=out_shape,
        scratch_shapes=scratch,
        compiler_params=pltpu.CompilerParams(
            dimension_semantics=("arbitrary",), vmem_limit_bytes=VMEM_LIMIT),
        name=kernel_fn.__name__.strip("_"),
    )(x, mod_l, *weights)


def _for_each_piece(n_rows, unit, n_bits, fn):
    offset = 0
    for bit in reversed(range(n_bits)):
        rows = unit << bit
        part = n_rows & rows

        @pl.when(part != 0)
        def _(offset=offset, rows=rows):
            fn(offset, rows)

        offset = offset + part


def _rows(first_row, n_rows, unit):
    return pl.ds(pl.multiple_of(first_row, unit), n_rows)


def _dispatch_kernel(len_ref, loc_ref, glob_ref, tot_ref, tail_len_ref, tail_ref, n_valid_ref,
                     xsl_ref, xs_hbm, zero_scr, sem):
    i = pl.program_id(0)
    slab = xsl_ref.shape[0] // TILE_PAIRS
    blk_rows = BLK_PAIRS * slab
    n_blocks = xs_hbm.shape[0] // blk_rows

    def run_copy(loc, glob, rows):
        return pltpu.make_async_copy(xsl_ref.at[_rows(loc, rows, slab)],
                                     xs_hbm.at[_rows(glob, rows, slab)], sem)

    def start_run(r, carry):
        k = i * TILE_RUNS + r
        n = len_ref[k]

        @pl.when(n > 0)
        def _():
            run_copy(loc_ref[k], glob_ref[k], n).start()
        return carry

    lax.fori_loop(0, TILE_RUNS, start_run, 0, unroll=4)

    @pl.when(tot_ref[i] > 0)
    def _():
        run_copy(0, 0, tot_ref[i]).wait()

    @pl.when(i == pl.num_programs(0) - 1)
    def _():
        zero_scr[...] = jnp.zeros_like(zero_scr)

        def for_each_tail(action):
            def body(e, carry):
                tail = tail_ref[e]
                _for_each_piece(
                    tail_len_ref[e], slab, TAIL_BITS,
                    lambda o, n: action(pltpu.make_async_copy(
                        zero_scr.at[_rows(0, n, slab)], xs_hbm.at[_rows(tail + o, n, slab)], sem)))
                return carry
            lax.fori_loop(0, N_EXPERTS, body, 0)

        for_each_tail(lambda cp: cp.start())
        for_each_tail(lambda cp: cp.wait())

        def for_each_spare(action):
            def body(b, carry):
                action(pltpu.make_async_copy(
                    zero_scr, xs_hbm.at[_rows(b * blk_rows, blk_rows, slab)], sem))
                return carry
            lax.fori_loop(n_valid_ref[0], n_blocks, body, 0)

        for_each_spare(lambda cp: cp.start())
        for_each_spare(lambda cp: cp.wait())


def _dispatch(tables, xsl, n_pairs, slab):
    n_tiles = xsl.shape[0] // (TILE_PAIRS * slab)
    return pl.pallas_call(
        _dispatch_kernel,
        grid_spec=pltpu.PrefetchScalarGridSpec(
            num_scalar_prefetch=7,
            grid=(n_tiles,),
            in_specs=[pl.BlockSpec((TILE_PAIRS * slab, LANES), lambda i, *_: (i, 0))],
            out_specs=pl.BlockSpec(memory_space=pl.ANY),
            scratch_shapes=[pltpu.VMEM((BLK_PAIRS * slab, LANES), U32),
                            pltpu.SemaphoreType.DMA],
        ),
        out_shape=jax.ShapeDtypeStruct((n_pairs * slab, LANES), U32),
        compiler_params=pltpu.CompilerParams(
            dimension_semantics=("arbitrary",), vmem_limit_bytes=VMEM_LIMIT),
        name="dispatch",
    )(*tables, xsl)


def _expert_kernel(layer, blk_expert_ref, n_valid_ref, next_ref, slot_ref,
                   xs_ref, wg_hbm, wu_hbm, wd_hbm, ys_ref,
                   wg_stage, wu_stage, wd_stage, wg_scr, wu_scr, wd_scr, sem):
    j = pl.program_id(0)
    expert = blk_expert_ref[j]
    prev = blk_expert_ref[jnp.maximum(j - 1, 0)]

    def weight_copies(e, slot):
        return [pltpu.make_async_copy(hbm.at[layer, e], stage.at[slot], sem.at[slot])
                for hbm, stage in ((wg_hbm, wg_stage), (wu_hbm, wu_stage), (wd_hbm, wd_stage))]

    @pl.when(j == 0)
    def _():
        for cp in weight_copies(expert, slot_ref[expert]):
            cp.start()

    @pl.when((j == 0) | (expert != prev))
    def _():
        slot = slot_ref[expert]
        for cp in weight_copies(expert, slot):
            cp.wait()
        nxt = next_ref[expert]

        @pl.when(nxt != expert)
        def _():
            for cp in weight_copies(nxt, 1 - slot):
                cp.start()

        wg_scr[...] = wg_stage[slot].astype(BF16)
        wu_scr[...] = wu_stage[slot].astype(BF16)
        wd_scr[...] = wd_stage[slot].astype(BF16)

    @pl.when(j < n_valid_ref[0])
    def _():
        xb = _load_pairs(xs_ref, 0, BLK_PAIRS, wg_scr.shape[0])
        a = _dot(xb, wg_scr[...])
        b = _dot(xb, wu_scr[...])
        hm = (a * jax.nn.sigmoid(a) * b).astype(BF16)
        _store_pairs(ys_ref, 0, _dot(hm, wd_scr[...]).astype(BF16))

    @pl.when(j >= n_valid_ref[0])
    def _():
        ys_ref[...] = jnp.zeros_like(ys_ref)


def _experts(tables, xs, layer, w_gate, w_up, w_down):
    _, _, d, de = w_gate.shape
    blk_rows = BLK_PAIRS * (d // LANES)
    n_blocks = xs.shape[0] // blk_rows

    def row_map(j, be, nv, *_):
        return (jnp.maximum(jnp.minimum(j, nv[0] - 1), 0), 0)

    return pl.pallas_call(
        functools.partial(_expert_kernel, layer),
        grid_spec=pltpu.PrefetchScalarGridSpec(
            num_scalar_prefetch=4,
            grid=(n_blocks,),
            in_specs=[
                pl.BlockSpec((blk_rows, LANES), row_map),
                pl.BlockSpec(memory_space=pl.ANY),
                pl.BlockSpec(memory_space=pl.ANY),
                pl.BlockSpec(memory_space=pl.ANY),
            ],
            out_specs=pl.BlockSpec((blk_rows, LANES), lambda j, *_: (j, 0)),
            scratch_shapes=[
                pltpu.VMEM((2, d, de), F32), pltpu.VMEM((2, d, de), F32),
                pltpu.VMEM((2, de, d), F32),
                pltpu.VMEM((d, de), BF16), pltpu.VMEM((d, de), BF16), pltpu.VMEM((de, d), BF16),
                pltpu.SemaphoreType.DMA((2,)),
            ],
        ),
        out_shape=jax.ShapeDtypeStruct(xs.shape, U32),
        compiler_params=pltpu.CompilerParams(
            dimension_semantics=("arbitrary",), vmem_limit_bytes=VMEM_LIMIT),
        name="experts",
    )(*tables, xs, w_gate, w_up, w_down)


def _combine_kernel(final, len_ref, loc_ref, glob_ref, tot_ref, ys_hbm, x_ref, route_ref,
                    mod_ref, gfin_ref, xo_ref, ysl_scr, sem):
    i = pl.program_id(0)
    tm, d = x_ref.shape
    slab = d // LANES

    def run_copy(tile, glob, loc, rows):
        slot = tile % 2
        return pltpu.make_async_copy(ys_hbm.at[_rows(glob, rows, slab)],
                                     ysl_scr.at[slot, _rows(loc, rows, slab)], sem.at[slot])

    def fetch(tile):
        def body(r, carry):
            k = tile * TILE_RUNS + r
            n = len_ref[k]

            @pl.when(n > 0)
            def _():
                run_copy(tile, glob_ref[k], loc_ref[k], n).start()
            return carry
        lax.fori_loop(0, TILE_RUNS, body, 0, unroll=4)

    @pl.when(i == 0)
    def _():
        ysl_scr[...] = jnp.zeros_like(ysl_scr)
        fetch(i)

    @pl.when(i + 1 < pl.num_programs(0))
    def _():
        fetch(i + 1)

    @pl.when(tot_ref[i] > 0)
    def _():
        run_copy(i, 0, 0, tot_ref[i]).wait()

    ysl_ref = ysl_scr.at[i % 2]
    srow = lax.broadcasted_iota(I32, (SORT_TM, LOCAL_ROWS), 1).astype(F32)
    y_sections = []
    for s in range(tm // SORT_TM):
        rows = slice(s * SORT_TM, (s + 1) * SORT_TM)
        ysl = _load_pairs(ysl_ref, s * LOCAL_PAIRS * slab, LOCAL_PAIRS, d)
        y = jnp.zeros((SORT_TM, d), F32)
        for k in range(2):
            sel = jnp.where(srow == route_ref[rows, k:k + 1], 1.0, 0.0).astype(BF16)
            y = y + route_ref[rows, 4 + k:5 + k] * _dot(sel, ysl)
        y_sections.append(y)
    x_new = x_ref[...] + mod_ref[0, 5:6, :] * jnp.concatenate(y_sections, axis=0)
    if final:
        x_new = _rms(x_new) * gfin_ref[...]
    xo_ref[...] = x_new


def _combine(tables, ys, x, route, mod_l, seq, g_final, final):
    t_tok, d = x.shape
    tiles_per_seq = seq // TM
    return pl.pallas_call(
        functools.partial(_combine_kernel, final),
        grid_spec=pltpu.PrefetchScalarGridSpec(
            num_scalar_prefetch=4,
            grid=(t_tok // TM,),
            in_specs=[
                pl.BlockSpec(memory_space=pl.ANY),
                pl.BlockSpec((TM, d), lambda i, *_: (i, 0)),
                pl.BlockSpec((TM, LANES), lambda i, *_: (i, 0)),
                pl.BlockSpec((1,) + mod_l.shape[1:], lambda i, *_: (i // tiles_per_seq, 0, 0)),
                pl.BlockSpec((1, d), lambda i, *_: (0, 0)),
            ],
            out_specs=pl.BlockSpec((TM, d), lambda i, *_: (i, 0)),
            scratch_shapes=[pltpu.VMEM((2, TILE_PAIRS * (d // LANES), LANES), U32),
                            pltpu.SemaphoreType.DMA((2,))],
        ),
        out_shape=jax.ShapeDtypeStruct((t_tok, d), F32),
        compiler_params=pltpu.CompilerParams(
            dimension_semantics=("arbitrary",), vmem_limit_bytes=VMEM_LIMIT),
        name="combine_final" if final else "combine",
    )(*tables, ys, x, route, mod_l, g_final)


def _moe(x, xsl, route, cnt, mod_l, seq, layer, w_gate, w_up, w_down, g_final, final):
    t_tok = x.shape[0]
    n_tiles = t_tok // TM
    n_blocks = -(-(2 * t_tok + n_tiles * TILE_RUNS) // EXPERT_BLK) + N_EXPERTS
    n_pairs = n_blocks * BLK_PAIRS

    counts = cnt[:, 0, ROUTE_COL0:ROUTE_COL0 + N_EXPERTS].astype(I32)
    run_len = (counts + 1) // 2
    section_base = (jnp.arange(counts.shape[0], dtype=I32) % SECTIONS) * LOCAL_PAIRS
    run_loc = jnp.cumsum(run_len, axis=1) - run_len + section_base[:, None]
    seg_len = jnp.sum(run_len, axis=0)
    seg_pad = (seg_len + BLK_PAIRS - 1) // BLK_PAIRS * BLK_PAIRS
    seg_end = jnp.cumsum(seg_pad)
    seg_start = seg_end - seg_pad
    run_glob = seg_start[None, :] + jnp.cumsum(run_len, axis=0) - run_len
    tile_tot = jnp.sum(run_len.reshape(n_tiles, TILE_RUNS), axis=1)
    slab = x.shape[1] // LANES
    run_tables = tuple((a.reshape(-1) * slab).astype(I32)
                       for a in (run_len, run_loc, run_glob, tile_tot))
    tail_tables = (((seg_pad - seg_len) * slab).astype(I32),
                   ((seg_start + seg_len) * slab).astype(I32))
    n_valid = (seg_end[-1:] // BLK_PAIRS).astype(I32)
    blk_pair0 = jnp.arange(n_blocks, dtype=I32) * BLK_PAIRS
    blk_expert = jnp.sum((seg_end[None, :] <= blk_pair0[:, None]).astype(I32), axis=1)
    experts = jnp.arange(N_EXPERTS, dtype=I32)
    used = seg_len > 0
    last_used = jnp.max(jnp.where(used, experts, 0))
    blk_expert = jnp.minimum(blk_expert, last_used).astype(I32)
    later = jnp.where((experts[None, :] > experts[:, None]) & used[None, :],
                      experts[None, :], N_EXPERTS)
    next_used = jnp.min(later, axis=1)
    next_used = jnp.where(next_used == N_EXPERTS, experts, next_used).astype(I32)
    slot_of = ((jnp.cumsum(used.astype(I32)) - 1) % 2).astype(I32)

    xs = _dispatch(run_tables + tail_tables + (n_valid,), xsl, n_pairs, slab)
    ys = _experts((blk_expert, n_valid, next_used, slot_of), xs, layer, w_gate, w_up, w_down)
    return _combine(run_tables, ys, x, route, mod_l, seq, g_final, final)


def kernel(x, c, w_ada, b_ada, norm_mix_g, norm_ffn_g, w_in_even, sgu_norm_g, w_spatial, b_spatial, conv_w, w_out_even, w_pool, pool_scale, w_group_router, b_group_router, w_expert_router, b_expert_router, moe_w_gate, moe_w_up, moe_w_down, final_norm_g):
    bsz, seq, d = x.shape
    depth = w_ada.shape[0]
    t_tok = bsz * seq
    assert seq % TM == 0 and d % LANES == 0 and w_spatial.shape[-1] == CHUNK

    mod = _modulation(c, w_ada, b_ada).reshape(depth, bsz, 6, d)
    tok = jnp.arange(TM)
    ltri = ((tok[:, None] > tok[None, :])
            & (tok[:, None] // SORT_TM == tok[None, :] // SORT_TM)).astype(BF16)
    utri = jnp.triu(jnp.ones((LANES, LANES), F32), 1)
    g_final = final_norm_g.reshape(1, d)

    xf = x.reshape(t_tok, d)
    for l in range(depth):
        i = l // 2
        rw = jnp.concatenate([w_group_router[l], w_expert_router[l]], axis=1)
        rw = jnp.pad(rw, ((0, 0), (0, LANES - rw.shape[1])))
        rw_hi = rw.astype(BF16)
        rw_lo = (rw - rw_hi.astype(F32)).astype(BF16)
        rb = jnp.concatenate([b_group_router[l], b_expert_router[l]])
        rb = jnp.pad(rb, (0, LANES - rb.shape[0])).reshape(1, LANES)
        route_w = [jnp.concatenate([rw_hi, rw_lo], axis=1), rb, ltri, utri]
        gmix = norm_mix_g[l].reshape(1, d)
        gffn = norm_ffn_g[l].reshape(1, d)
        if l % 2 == 0:
            aw = sgu_norm_g.shape[1]
            weights = [gmix, gffn, w_in_even[i].astype(BF16), sgu_norm_g[i].reshape(aw, 1),
                       w_spatial[i], b_spatial[i].reshape(A_HEADS, 1, CHUNK), conv_w[i],
                       w_out_even[i].astype(BF16)] + route_w
            scratch = [pltpu.VMEM((TM + SUBLANES, conv_w.shape[-1]), F32)]
            xf, xsl, route, cnt = _mixer_call(_even_kernel, xf, mod[l], seq, weights, scratch)
        else:
            weights = [gmix, gffn, w_pool[i].astype(BF16), pool_scale[i].reshape(1, d)] + route_w
            scratch = [pltpu.VMEM((TM + max(POOL_WINDOWS), d), F32)]
            xf, xsl, route, cnt = _mixer_call(_odd_kernel, xf, mod[l], seq, weights, scratch)
        xf = _moe(xf, xsl, route, cnt, mod[l], seq, l, moe_w_gate, moe_w_up, moe_w_down,
                  g_final, l == depth - 1)
    return xf.reshape(bsz, seq, d)
```

```python
import functools

import jax
import jax.numpy as jnp
from jax import lax
from jax.experimental import pallas as pl
from jax.experimental.pallas import tpu as pltpu

F32 = jnp.float32
BF16 = jnp.bfloat16
U32 = jnp.uint32
I32 = jnp.int32

EPS = 1e-6
LANES = 128
SUBLANES = 8
CHUNK = 128
A_HEADS = 8
N_GROUPS = 4
EXPERTS_PER_GROUP = 8
N_EXPERTS = N_GROUPS * EXPERTS_PER_GROUP
POOL_WINDOWS = (2, 4, 8, 16)
CONV_WIDTH = 3
ROUTE_COL0 = N_GROUPS

TM = 512
SORT_TM = 256
SECTIONS = TM // SORT_TM
LOCAL_ROWS = 2 * SORT_TM + LANES
LOCAL_PAIRS = LOCAL_ROWS // 2
TILE_PAIRS = SECTIONS * LOCAL_PAIRS
TILE_RUNS = SECTIONS * N_EXPERTS
EXPERT_BLK = 512
BLK_PAIRS = EXPERT_BLK // 2
TAIL_BITS = (BLK_PAIRS - 1).bit_length()
MOD_TN = 1536
VMEM_LIMIT = 56 * 1024 * 1024


def _rms(x):
    return x * lax.rsqrt(jnp.mean(x * x, axis=-1, keepdims=True) + EPS)


def _dot(a, b):
    return jnp.dot(a, b, preferred_element_type=F32)


def _store_pairs(ref, base, rows_bf16):
    words = pltpu.bitcast(rows_bf16, U32)
    n_pairs, d = words.shape
    slab = d // LANES
    for c in range(slab):
        ref[pl.ds(base + c, n_pairs, stride=slab), :] = words[:, c * LANES:(c + 1) * LANES]


def _load_pairs(ref, base, n_pairs, d):
    slab = d // LANES
    words = jnp.concatenate(
        [ref[pl.ds(base + c, n_pairs, stride=slab), :] for c in range(slab)], axis=-1)
    return pltpu.bitcast(words, BF16)


def _mod_kernel(c_ref, w_ref, b_ref, o_ref):
    c = c_ref[...]
    ca = c * jax.nn.sigmoid(c)
    o_ref[0] = jnp.dot(ca, w_ref[0], precision=lax.Precision.HIGHEST,
                       preferred_element_type=F32) + b_ref[0]


def _modulation(c, w_ada, b_ada):
    depth, d, n = w_ada.shape
    bsz = c.shape[0]
    return pl.pallas_call(
        _mod_kernel,
        grid=(depth, n // MOD_TN),
        in_specs=[
            pl.BlockSpec((bsz, d), lambda l, j: (0, 0)),
            pl.BlockSpec((1, d, MOD_TN), lambda l, j: (l, 0, j)),
            pl.BlockSpec((1, 1, MOD_TN), lambda l, j: (l, 0, j)),
        ],
        out_specs=pl.BlockSpec((1, bsz, MOD_TN), lambda l, j: (l, 0, j)),
        out_shape=jax.ShapeDtypeStruct((depth, bsz, n), F32),
        compiler_params=pltpu.CompilerParams(
            dimension_semantics=("arbitrary", "arbitrary"),
            vmem_limit_bytes=VMEM_LIMIT),
        name="modulation",
    )(c, w_ada, b_ada.reshape(depth, 1, n))


def _route_and_sort(x_new, mod_ref, gffn_ref, rw_ref, rb_ref, ltri_ref, utri_ref,
                    xsl_ref, route_ref, cnt_ref):
    tm = x_new.shape[0]
    sh_f = mod_ref[0, 3:4, :]
    sc_f = mod_ref[0, 4:5, :]
    h2 = _rms(x_new) * gffn_ref[...] * (1.0 + sc_f) + sh_f

    hh = h2.astype(BF16)
    hl = (h2 - hh.astype(F32)).astype(BF16)
    both = _dot(hh, rw_ref[...])
    logits = (both[:, 0:LANES] + both[:, LANES:2 * LANES] + _dot(hl, rw_ref[:, 0:LANES])
              + rb_ref[...])

    lane = lax.broadcasted_iota(I32, logits.shape, 1).astype(F32)
    neg = jnp.float32(-jnp.inf)
    big = jnp.float32(LANES)

    gl = jnp.where(lane < N_GROUPS, logits, neg)
    gmax = jnp.max(gl, axis=-1, keepdims=True)
    g_sel = jnp.min(jnp.where(gl == gmax, lane, big), axis=-1, keepdims=True)
    g_w = 1.0 / jnp.sum(jnp.exp(gl - gmax), axis=-1, keepdims=True)

    lo = ROUTE_COL0 + EXPERTS_PER_GROUP * g_sel
    el = jnp.where((lane >= lo) & (lane < lo + EXPERTS_PER_GROUP), logits, neg)
    m1 = jnp.max(el, axis=-1, keepdims=True)
    i1 = jnp.min(jnp.where(el == m1, lane, big), axis=-1, keepdims=True)
    el2 = jnp.where(lane == i1, neg, el)
    m2 = jnp.max(el2, axis=-1, keepdims=True)
    i2 = jnp.min(jnp.where(el2 == m2, lane, big), axis=-1, keepdims=True)
    t = jnp.exp(m2 - m1)
    gate1 = g_w / (1.0 + t)
    gate2 = g_w * t / (1.0 + t)

    sections = [slice(s * SORT_TM, (s + 1) * SORT_TM) for s in range(tm // SORT_TM)]
    is1 = lane == i1
    is2 = lane == i2
    onehot = jnp.where(is1 | is2, 1.0, 0.0)
    before = _dot(ltri_ref[...], onehot.astype(BF16))
    cnts = [jnp.sum(onehot[rows], axis=0, keepdims=True) for rows in sections]
    cnt_rows = jnp.concatenate(
        cnts + [jnp.zeros((SUBLANES - len(sections), LANES), F32)], axis=0)
    pairs = jnp.floor((cnt_rows + 1.0) * 0.5)
    pair_start = jnp.dot(pairs, utri_ref[...], precision=lax.Precision.HIGHEST,
                         preferred_element_type=F32)
    pos = jnp.concatenate([before[rows] + 2.0 * pair_start[s:s + 1]
                           for s, rows in enumerate(sections)], axis=0)
    pos1 = jnp.sum(jnp.where(is1, pos, 0.0), axis=-1, keepdims=True)
    pos2 = jnp.sum(jnp.where(is2, pos, 0.0), axis=-1, keepdims=True)
    for s, cnt in enumerate(cnts):
        cnt_ref[s] = jnp.broadcast_to(cnt, (SUBLANES, LANES))

    route = jnp.where(lane == 0, pos1, 0.0)
    route = jnp.where(lane == 1, pos2, route)
    route = jnp.where(lane == 4, gate1, route)
    route = jnp.where(lane == 5, gate2, route)
    route_ref[...] = route

    pos_t = route.T
    srow = lax.broadcasted_iota(I32, (LOCAL_ROWS, SORT_TM), 0).astype(F32)
    section_rows = LOCAL_PAIRS * (x_new.shape[1] // LANES)
    for s, cols in enumerate(sections):
        perm = jnp.where((srow == pos_t[0:1, cols]) | (srow == pos_t[1:2, cols]), 1.0, 0.0)
        _store_pairs(xsl_ref, s * section_rows, _dot(perm.astype(BF16), hh[cols]).astype(BF16))


def _even_kernel(tiles_per_seq,
                 x_ref, mod_ref, gmix_ref, gffn_ref, win_ref, gv_ref, ws_ref, bs_ref,
                 cw_ref, wout_ref, rw_ref, rb_ref, ltri_ref, utri_ref,
                 xo_ref, xsl_ref, route_ref, cnt_ref,
                 zc_scr):
    i = pl.program_id(0)
    tm = x_ref.shape[0]
    aw = gv_ref.shape[0]
    hd = aw // A_HEADS
    n_chunks = tm // CHUNK

    @pl.when(i == 0)
    def _():
        zc_scr[...] = jnp.zeros_like(zc_scr)

    x = x_ref[...]
    sh_m = mod_ref[0, 0:1, :]
    sc_m = mod_ref[0, 1:2, :]
    g_m = mod_ref[0, 2:3, :]
    h = _rms(x) * gmix_ref[...] * (1.0 + sc_m) + sh_m
    z = _dot(h.astype(BF16), win_ref[...])
    u = z[:, 0:aw]
    v = z[:, aw:2 * aw]
    b_gate = z[:, 2 * aw:3 * aw]
    c_gate = z[:, 3 * aw:4 * aw]
    x_in = z[:, 4 * aw:5 * aw]

    v_t = v.T
    row = lax.broadcasted_iota(I32, (CHUNK, CHUNK), 0)
    col = lax.broadcasted_iota(I32, (CHUNK, CHUNK), 1)
    causal = col <= row
    head_rows = []
    for hh in range(A_HEADS):
        vh = v_t[hh * hd:(hh + 1) * hd, :]
        msv = jnp.mean(vh * vh, axis=0, keepdims=True)
        vn = (vh * lax.rsqrt(msv + EPS) * gv_ref[hh * hd:(hh + 1) * hd, :]).astype(BF16)
        lhs = jnp.concatenate(
            [vn[:, c * CHUNK:(c + 1) * CHUNK] for c in range(n_chunks)], axis=0)
        w_m = jnp.where(causal, ws_ref[hh], 0.0).astype(BF16)
        sv_h = lax.dot_general(lhs, w_m, (((1,), (1,)), ((), ())),
                               preferred_element_type=F32)
        sv_h = sv_h + bs_ref[hh]
        head_rows.append(jnp.concatenate(
            [sv_h[c * hd:(c + 1) * hd, :] for c in range(n_chunks)], axis=1))
    sv = jnp.concatenate(head_rows, axis=0).T
    y_a = u * sv

    zc = c_gate * x_in
    first = (i % tiles_per_seq) == 0
    halo = zc_scr[tm:tm + SUBLANES, :]
    zc_scr[0:SUBLANES, :] = jnp.where(first, 0.0, halo)
    zc_scr[SUBLANES:SUBLANES + tm, :] = zc
    conv = cw_ref[2:3, :] * zc
    for k in range(CONV_WIDTH - 1):
        shift = CONV_WIDTH - 1 - k
        conv = conv + cw_ref[k:k + 1, :] * zc_scr[SUBLANES - shift:SUBLANES - shift + tm, :]
    y_b = b_gate * conv

    y = _dot(y_a.astype(BF16), wout_ref[0:aw, :]) + _dot(y_b.astype(BF16), wout_ref[aw:, :])
    x_new = x + g_m * y
    xo_ref[...] = x_new
    _route_and_sort(x_new, mod_ref, gffn_ref, rw_ref, rb_ref, ltri_ref, utri_ref,
                    xsl_ref, route_ref, cnt_ref)


def _odd_kernel(tiles_per_seq,
                x_ref, mod_ref, gmix_ref, gffn_ref, wpool_ref, pscale_ref,
                rw_ref, rb_ref, ltri_ref, utri_ref,
                xo_ref, xsl_ref, route_ref, cnt_ref,
                *level_scrs):
    i = pl.program_id(0)
    tm, d = x_ref.shape
    halo_rows = max(POOL_WINDOWS)
    gd = d // len(POOL_WINDOWS)

    @pl.when(i == 0)
    def _():
        for scr in level_scrs:
            scr[...] = jnp.zeros_like(scr)

    x = x_ref[...]
    sh_m = mod_ref[0, 0:1, :]
    sc_m = mod_ref[0, 1:2, :]
    g_m = mod_ref[0, 2:3, :]
    h = _rms(x) * gmix_ref[...] * (1.0 + sc_m) + sh_m

    tile_in_seq = i % tiles_per_seq
    first = tile_in_seq == 0

    sums = h
    window_sums = []
    for g, scr in enumerate(level_scrs):
        tail = scr[tm:tm + halo_rows, :]
        scr[0:halo_rows, :] = jnp.where(first, 0.0, tail)
        scr[halo_rows:halo_rows + tm, :] = sums
        lag = POOL_WINDOWS[g] // 2
        sums = sums + scr[halo_rows - lag:halo_rows - lag + tm, :]
        window_sums.append(sums[:, 0:gd])
        if g + 1 < len(level_scrs):
            sums = sums[:, gd:]

    pos = (tile_in_seq * tm + lax.broadcasted_iota(I32, (tm, 1), 0)).astype(F32)
    outs = []
    for g, win in enumerate(POOL_WINDOWS):
        cs = slice(g * gd, (g + 1) * gd)
        count = jnp.minimum(pos + 1.0, jnp.float32(win))
        pooled = window_sums[g] / count - h[:, cs]
        outs.append(_dot(pooled.astype(BF16), wpool_ref[g]))
    y = jnp.concatenate(outs, axis=-1) * pscale_ref[...]
    x_new = x + g_m * y
    xo_ref[...] = x_new
    _route_and_sort(x_new, mod_ref, gffn_ref, rw_ref, rb_ref, ltri_ref, utri_ref,
                    xsl_ref, route_ref, cnt_ref)


def _mixer_call(kernel_fn, x, mod_l, seq, weights, scratch):
    t_tok, d = x.shape
    n_tiles = t_tok // TM
    tiles_per_seq = seq // TM
    n_slabs = d // LANES

    def const_spec(a):
        return pl.BlockSpec(a.shape, lambda i, nd=a.ndim: (0,) * nd)

    in_specs = [
        pl.BlockSpec((TM, d), lambda i: (i, 0)),
        pl.BlockSpec((1,) + mod_l.shape[1:], lambda i: (i // tiles_per_seq, 0, 0)),
    ] + [const_spec(w) for w in weights]
    out_shape = (
        jax.ShapeDtypeStruct((t_tok, d), F32),
        jax.ShapeDtypeStruct((n_tiles * TILE_PAIRS * n_slabs, LANES), U32),
        jax.ShapeDtypeStruct((t_tok, LANES), F32),
        jax.ShapeDtypeStruct((n_tiles * SECTIONS, SUBLANES, LANES), F32),
    )
    out_specs = (
        pl.BlockSpec((TM, d), lambda i: (i, 0)),
        pl.BlockSpec((TILE_PAIRS * n_slabs, LANES), lambda i: (i, 0)),
        pl.BlockSpec((TM, LANES), lambda i: (i, 0)),
        pl.BlockSpec((SECTIONS, SUBLANES, LANES), lambda i: (i, 0, 0)),
    )
    return pl.pallas_call(
        functools.partial(kernel_fn, tiles_per_seq),
        grid=(n_tiles,),
        in_specs=in_specs,
        out_specs=out_specs,
        out_shape=out_shape,
        scratch_shapes=scratch,
        compiler_params=pltpu.CompilerParams(
            dimension_semantics=("arbitrary",), vmem_limit_bytes=VMEM_LIMIT),
        name=kernel_fn.__name__.strip("_"),
    )(x, mod_l, *weights)


def _for_each_piece(n_rows, unit, n_bits, fn):
    offset = 0
    for bit in reversed(range(n_bits)):
        rows = unit << bit
        part = n_rows & rows

        @pl.when(part != 0)
        def _(offset=offset, rows=rows):
            fn(offset, rows)

        offset = offset + part


def _rows(first_row, n_rows, unit):
    return pl.ds(pl.multiple_of(first_row, unit), n_rows)


def _dispatch_kernel(len_ref, loc_ref, glob_ref, tot_ref, tail_len_ref, tail_ref, n_valid_ref,
                     xsl_ref, xs_hbm, zero_scr, sem):
    i = pl.program_id(0)
    slab = xsl_ref.shape[0] // TILE_PAIRS
    blk_rows = BLK_PAIRS * slab
    n_blocks = xs_hbm.shape[0] // blk_rows

    def run_copy(loc, glob, rows):
        return pltpu.make_async_copy(xsl_ref.at[_rows(loc, rows, slab)],
                                     xs_hbm.at[_rows(glob, rows, slab)], sem)

    def start_run(r, carry):
        k = i * TILE_RUNS + r
        n = len_ref[k]

        @pl.when(n > 0)
        def _():
            run_copy(loc_ref[k], glob_ref[k], n).start()
        return carry

    lax.fori_loop(0, TILE_RUNS, start_run, 0, unroll=4)

    @pl.when(tot_ref[i] > 0)
    def _():
        run_copy(0, 0, tot_ref[i]).wait()

    @pl.when(i == pl.num_programs(0) - 1)
    def _():
        zero_scr[...] = jnp.zeros_like(zero_scr)

        def for_each_tail(action):
            def body(e, carry):
                tail = tail_ref[e]
                _for_each_piece(
                    tail_len_ref[e], slab, TAIL_BITS,
                    lambda o, n: action(pltpu.make_async_copy(
                        zero_scr.at[_rows(0, n, slab)], xs_hbm.at[_rows(tail + o, n, slab)], sem)))
                return carry
            lax.fori_loop(0, N_EXPERTS, body, 0)

        for_each_tail(lambda cp: cp.start())
        for_each_tail(lambda cp: cp.wait())

        def for_each_spare(action):
            def body(b, carry):
                action(pltpu.make_async_copy(
                    zero_scr, xs_hbm.at[_rows(b * blk_rows, blk_rows, slab)], sem))
                return carry
            lax.fori_loop(n_valid_ref[0], n_blocks, body, 0)

        for_each_spare(lambda cp: cp.start())
        for_each_spare(lambda cp: cp.wait())


def _dispatch(tables, xsl, n_pairs, slab):
    n_tiles = xsl.shape[0] // (TILE_PAIRS * slab)
    return pl.pallas_call(
        _dispatch_kernel,
        grid_spec=pltpu.PrefetchScalarGridSpec(
            num_scalar_prefetch=7,
            grid=(n_tiles,),
            in_specs=[pl.BlockSpec((TILE_PAIRS * slab, LANES), lambda i, *_: (i, 0))],
            out_specs=pl.BlockSpec(memory_space=pl.ANY),
            scratch_shapes=[pltpu.VMEM((BLK_PAIRS * slab, LANES), U32),
                            pltpu.SemaphoreType.DMA],
        ),
        out_shape=jax.ShapeDtypeStruct((n_pairs * slab, LANES), U32),
        compiler_params=pltpu.CompilerParams(
            dimension_semantics=("arbitrary",), vmem_limit_bytes=VMEM_LIMIT),
        name="dispatch",
    )(*tables, xsl)


def _expert_kernel(layer, blk_expert_ref, n_valid_ref, next_ref, slot_ref,
                   xs_ref, wg_hbm, wu_hbm, wd_hbm, ys_ref,
                   wg_stage, wu_stage, wd_stage, wg_scr, wu_scr, wd_scr, sem):
    j = pl.program_id(0)
    expert = blk_expert_ref[j]
    prev = blk_expert_ref[jnp.maximum(j - 1, 0)]

    def weight_copies(e, slot):
        return [pltpu.make_async_copy(hbm.at[layer, e], stage.at[slot], sem.at[slot])
                for hbm, stage in ((wg_hbm, wg_stage), (wu_hbm, wu_stage), (wd_hbm, wd_stage))]

    @pl.when(j == 0)
    def _():
        for cp in weight_copies(expert, slot_ref[expert]):
            cp.start()

    @pl.when((j == 0) | (expert != prev))
    def _():
        slot = slot_ref[expert]
        for cp in weight_copies(expert, slot):
            cp.wait()
        nxt = next_ref[expert]

        @pl.when(nxt != expert)
        def _():
            for cp in weight_copies(nxt, 1 - slot):
                cp.start()

        wg_scr[...] = wg_stage[slot].astype(BF16)
        wu_scr[...] = wu_stage[slot].astype(BF16)
        wd_scr[...] = wd_stage[slot].astype(BF16)

    @pl.when(j < n_valid_ref[0])
    def _():
        xb = _load_pairs(xs_ref, 0, BLK_PAIRS, wg_scr.shape[0])
        a = _dot(xb, wg_scr[...])
        b = _dot(xb, wu_scr[...])
        hm = (a * jax.nn.sigmoid(a) * b).astype(BF16)
        _store_pairs(ys_ref, 0, _dot(hm, wd_scr[...]).astype(BF16))

    @pl.when(j >= n_valid_ref[0])
    def _():
        ys_ref[...] = jnp.zeros_like(ys_ref)


def _experts(tables, xs, layer, w_gate, w_up, w_down):
    _, _, d, de = w_gate.shape
    blk_rows = BLK_PAIRS * (d // LANES)
    n_blocks = xs.shape[0] // blk_rows

    def row_map(j, be, nv, *_):
        return (jnp.maximum(jnp.minimum(j, nv[0] - 1), 0), 0)

    return pl.pallas_call(
        functools.partial(_expert_kernel, layer),
        grid_spec=pltpu.PrefetchScalarGridSpec(
            num_scalar_prefetch=4,
            grid=(n_blocks,),
            in_specs=[
                pl.BlockSpec((blk_rows, LANES), row_map),
                pl.BlockSpec(memory_space=pl.ANY),
                pl.BlockSpec(memory_space=pl.ANY),
                pl.BlockSpec(memory_space=pl.ANY),
            ],
            out_specs=pl.BlockSpec((blk_rows, LANES), lambda j, *_: (j, 0)),
            scratch_shapes=[
                pltpu.VMEM((2, d, de), F32), pltpu.VMEM((2, d, de), F32),
                pltpu.VMEM((2, de, d), F32),
                pltpu.VMEM((d, de), BF16), pltpu.VMEM((d, de), BF16), pltpu.VMEM((de, d), BF16),
                pltpu.SemaphoreType.DMA((2,)),
            ],
        ),
        out_shape=jax.ShapeDtypeStruct(xs.shape, U32),
        compiler_params=pltpu.CompilerParams(
            dimension_semantics=("arbitrary",), vmem_limit_bytes=VMEM_LIMIT),
        name="experts",
    )(*tables, xs, w_gate, w_up, w_down)


def _combine_kernel(final, len_ref, loc_ref, glob_ref, tot_ref, ys_hbm, x_ref, route_ref,
                    mod_ref, gfin_ref, xo_ref, ysl_scr, sem):
    i = pl.program_id(0)
    tm, d = x_ref.shape
    slab = d // LANES

    def run_copy(tile, glob, loc, rows):
        slot = tile % 2
        return pltpu.make_async_copy(ys_hbm.at[_rows(glob, rows, slab)],
                                     ysl_scr.at[slot, _rows(loc, rows, slab)], sem.at[slot])

    def fetch(tile):
        def body(r, carry):
            k = tile * TILE_RUNS + r
            n = len_ref[k]

            @pl.when(n > 0)
            def _():
                run_copy(tile, glob_ref[k], loc_ref[k], n).start()
            return carry
        lax.fori_loop(0, TILE_RUNS, body, 0, unroll=4)

    @pl.when(i == 0)
    def _():
        ysl_scr[...] = jnp.zeros_like(ysl_scr)
        fetch(i)

    @pl.when(i + 1 < pl.num_programs(0))
    def _():
        fetch(i + 1)

    @pl.when(tot_ref[i] > 0)
    def _():
        run_copy(i, 0, 0, tot_ref[i]).wait()

    ysl_ref = ysl_scr.at[i % 2]
    srow = lax.broadcasted_iota(I32, (SORT_TM, LOCAL_ROWS), 1).astype(F32)
    y_sections = []
    for s in range(tm // SORT_TM):
        rows = slice(s * SORT_TM, (s + 1) * SORT_TM)
        ysl = _load_pairs(ysl_ref, s * LOCAL_PAIRS * slab, LOCAL_PAIRS, d)
        y = jnp.zeros((SORT_TM, d), F32)
        for k in range(2):
            sel = jnp.where(srow == route_ref[rows, k:k + 1], 1.0, 0.0).astype(BF16)
            y = y + route_ref[rows, 4 + k:5 + k] * _dot(sel, ysl)
        y_sections.append(y)
    x_new = x_ref[...] + mod_ref[0, 5:6, :] * jnp.concatenate(y_sections, axis=0)
    if final:
        x_new = _rms(x_new) * gfin_ref[...]
    xo_ref[...] = x_new


def _combine(tables, ys, x, route, mod_l, seq, g_final, final):
    t_tok, d = x.shape
    tiles_per_seq = seq // TM
    return pl.pallas_call(
        functools.partial(_combine_kernel, final),
        grid_spec=pltpu.PrefetchScalarGridSpec(
            num_scalar_prefetch=4,
            grid=(t_tok // TM,),
            in_specs=[
                pl.BlockSpec(memory_space=pl.ANY),
                pl.BlockSpec((TM, d), lambda i, *_: (i, 0)),
                pl.BlockSpec((TM, LANES), lambda i, *_: (i, 0)),
                pl.BlockSpec((1,) + mod_l.shape[1:], lambda i, *_: (i // tiles_per_seq, 0, 0)),
                pl.BlockSpec((1, d), lambda i, *_: (0, 0)),
            ],
            out_specs=pl.BlockSpec((TM, d), lambda i, *_: (i, 0)),
            scratch_shapes=[pltpu.VMEM((2, TILE_PAIRS * (d // LANES), LANES), U32),
                            pltpu.SemaphoreType.DMA((2,))],
        ),
        out_shape=jax.ShapeDtypeStruct((t_tok, d), F32),
        compiler_params=pltpu.CompilerParams(
            dimension_semantics=("arbitrary",), vmem_limit_bytes=VMEM_LIMIT),
        name="combine_final" if final else "combine",
    )(*tables, ys, x, route, mod_l, g_final)


def _moe(x, xsl, route, cnt, mod_l, seq, layer, w_gate, w_up, w_down, g_final, final):
    t_tok = x.shape[0]
    n_tiles = t_tok // TM
    n_blocks = -(-(2 * t_tok + n_tiles * TILE_RUNS) // EXPERT_BLK) + N_EXPERTS
    n_pairs = n_blocks * BLK_PAIRS

    counts = cnt[:, 0, ROUTE_COL0:ROUTE_COL0 + N_EXPERTS].astype(I32)
    run_len = (counts + 1) // 2
    section_base = (jnp.arange(counts.shape[0], dtype=I32) % SECTIONS) * LOCAL_PAIRS
    run_loc = jnp.cumsum(run_len, axis=1) - run_len + section_base[:, None]
    seg_len = jnp.sum(run_len, axis=0)
    seg_pad = (seg_len + BLK_PAIRS - 1) // BLK_PAIRS * BLK_PAIRS
    seg_end = jnp.cumsum(seg_pad)
    seg_start = seg_end - seg_pad
    run_glob = seg_start[None, :] + jnp.cumsum(run_len, axis=0) - run_len
    tile_tot = jnp.sum(run_len.reshape(n_tiles, TILE_RUNS), axis=1)
    slab = x.shape[1] // LANES
    run_tables = tuple((a.reshape(-1) * slab).astype(I32)
                       for a in (run_len, run_loc, run_glob, tile_tot))
    tail_tables = (((seg_pad - seg_len) * slab).astype(I32),
                   ((seg_start + seg_len) * slab).astype(I32))
    n_valid = (seg_end[-1:] // BLK_PAIRS).astype(I32)
    blk_pair0 = jnp.arange(n_blocks, dtype=I32) * BLK_PAIRS
    blk_expert = jnp.sum((seg_end[None, :] <= blk_pair0[:, None]).astype(I32), axis=1)
    experts = jnp.arange(N_EXPERTS, dtype=I32)
    used = seg_len > 0
    last_used = jnp.max(jnp.where(used, experts, 0))
    blk_expert = jnp.minimum(blk_expert, last_used).astype(I32)
    later = jnp.where((experts[None, :] > experts[:, None]) & used[None, :],
                      experts[None, :], N_EXPERTS)
    next_used = jnp.min(later, axis=1)
    next_used = jnp.where(next_used == N_EXPERTS, experts, next_used).astype(I32)
    slot_of = ((jnp.cumsum(used.astype(I32)) - 1) % 2).astype(I32)

    xs = _dispatch(run_tables + tail_tables + (n_valid,), xsl, n_pairs, slab)
    ys = _experts((blk_expert, n_valid, next_used, slot_of), xs, layer, w_gate, w_up, w_down)
    return _combine(run_tables, ys, x, route, mod_l, seq, g_final, final)


def kernel(x, c, w_ada, b_ada, norm_mix_g, norm_ffn_g, w_in_even, sgu_norm_g, w_spatial, b_spatial, conv_w, w_out_even, w_pool, pool_scale, w_group_router, b_group_router, w_expert_router, b_expert_router, moe_w_gate, moe_w_up, moe_w_down, final_norm_g):
    bsz, seq, d = x.shape
    depth = w_ada.shape[0]
    t_tok = bsz * seq
    assert seq % TM == 0 and d % LANES == 0 and w_spatial.shape[-1] == CHUNK
    assert all(w == 2 ** (g + 1) for g, w in enumerate(POOL_WINDOWS))

    mod = _modulation(c, w_ada, b_ada).reshape(depth, bsz, 6, d)
    tok = jnp.arange(TM)
    ltri = ((tok[:, None] > tok[None, :])
            & (tok[:, None] // SORT_TM == tok[None, :] // SORT_TM)).astype(BF16)
    utri = jnp.triu(jnp.ones((LANES, LANES), F32), 1)
    g_final = final_norm_g.reshape(1, d)

    xf = x.reshape(t_tok, d)
    for l in range(depth):
        i = l // 2
        rw = jnp.concatenate([w_group_router[l], w_expert_router[l]], axis=1)
        rw = jnp.pad(rw, ((0, 0), (0, LANES - rw.shape[1])))
        rw_hi = rw.astype(BF16)
        rw_lo = (rw - rw_hi.astype(F32)).astype(BF16)
        rb = jnp.concatenate([b_group_router[l], b_expert_router[l]])
        rb = jnp.pad(rb, (0, LANES - rb.shape[0])).reshape(1, LANES)
        route_w = [jnp.concatenate([rw_hi, rw_lo], axis=1), rb, ltri, utri]
        gmix = norm_mix_g[l].reshape(1, d)
        gffn = norm_ffn_g[l].reshape(1, d)
        if l % 2 == 0:
            aw = sgu_norm_g.shape[1]
            weights = [gmix, gffn, w_in_even[i].astype(BF16), sgu_norm_g[i].reshape(aw, 1),
                       w_spatial[i], b_spatial[i].reshape(A_HEADS, 1, CHUNK), conv_w[i],
                       w_out_even[i].astype(BF16)] + route_w
            scratch = [pltpu.VMEM((TM + SUBLANES, conv_w.shape[-1]), F32)]
            xf, xsl, route, cnt = _mixer_call(_even_kernel, xf, mod[l], seq, weights, scratch)
        else:
            weights = [gmix, gffn, w_pool[i].astype(BF16), pool_scale[i].reshape(1, d)] + route_w
            gd = d // len(POOL_WINDOWS)
            scratch = [pltpu.VMEM((TM + max(POOL_WINDOWS), d - g * gd), F32)
                       for g in range(len(POOL_WINDOWS))]
            xf, xsl, route, cnt = _mixer_call(_odd_kernel, xf, mod[l], seq, weights, scratch)
        xf = _moe(xf, xsl, route, cnt, mod[l], seq, l, moe_w_gate, moe_w_up, moe_w_down,
                  g_final, l == depth - 1)
    return xf.reshape(bsz, seq, d)
```

```python
import functools

import jax
import jax.numpy as jnp
from jax import lax
from jax.experimental import pallas as pl
from jax.experimental.pallas import tpu as pltpu

F32 = jnp.float32
BF16 = jnp.bfloat16
U32 = jnp.uint32
I32 = jnp.int32

EPS = 1e-6
LANES = 128
SUBLANES = 8
CHUNK = 128
A_HEADS = 8
N_GROUPS = 4
EXPERTS_PER_GROUP = 8
N_EXPERTS = N_GROUPS * EXPERTS_PER_GROUP
POOL_WINDOWS = (2, 4, 8, 16)
CONV_WIDTH = 3
ROUTE_COL0 = N_GROUPS

TM = 512
SORT_TM = 256
SECTIONS = TM // SORT_TM
LOCAL_ROWS = 2 * SORT_TM + LANES
LOCAL_PAIRS = LOCAL_ROWS // 2
TILE_PAIRS = SECTIONS * LOCAL_PAIRS
TILE_RUNS = SECTIONS * N_EXPERTS
EXPERT_BLK = 512
BLK_PAIRS = EXPERT_BLK // 2
MOD_TN = 1536
VMEM_LIMIT = 56 * 1024 * 1024


def _rms(x):
    return x * lax.rsqrt(jnp.mean(x * x, axis=-1, keepdims=True) + EPS)


def _dot(a, b):
    return jnp.dot(a, b, preferred_element_type=F32)


def _store_pairs(ref, base, rows_bf16):
    words = pltpu.bitcast(rows_bf16, U32)
    n_pairs, d = words.shape
    slab = d // LANES
    for c in range(slab):
        ref[pl.ds(base + c, n_pairs, stride=slab), :] = words[:, c * LANES:(c + 1) * LANES]


def _load_pairs(ref, base, n_pairs, d):
    slab = d // LANES
    words = jnp.concatenate(
        [ref[pl.ds(base + c, n_pairs, stride=slab), :] for c in range(slab)], axis=-1)
    return pltpu.bitcast(words, BF16)


def _mod_kernel(c_ref, w_ref, b_ref, o_ref):
    c = c_ref[...]
    ca = c * jax.nn.sigmoid(c)
    o_ref[0] = jnp.dot(ca, w_ref[0], precision=lax.Precision.HIGHEST,
                       preferred_element_type=F32) + b_ref[0]


def _modulation(c, w_ada, b_ada):
    depth, d, n = w_ada.shape
    bsz = c.shape[0]
    return pl.pallas_call(
        _mod_kernel,
        grid=(depth, n // MOD_TN),
        in_specs=[
            pl.BlockSpec((bsz, d), lambda l, j: (0, 0)),
            pl.BlockSpec((1, d, MOD_TN), lambda l, j: (l, 0, j)),
            pl.BlockSpec((1, 1, MOD_TN), lambda l, j: (l, 0, j)),
        ],
        out_specs=pl.BlockSpec((1, bsz, MOD_TN), lambda l, j: (l, 0, j)),
        out_shape=jax.ShapeDtypeStruct((depth, bsz, n), F32),
        compiler_params=pltpu.CompilerParams(
            dimension_semantics=("arbitrary", "arbitrary"),
            vmem_limit_bytes=VMEM_LIMIT),
        name="modulation",
    )(c, w_ada, b_ada.reshape(depth, 1, n))


def _route_and_sort(x_new, mod_ref, gffn_ref, rw_ref, rb_ref, ltri_ref, utri_ref,
                    xsl_ref, route_ref, cnt_ref):
    tm = x_new.shape[0]
    sh_f = mod_ref[0, 3:4, :]
    sc_f = mod_ref[0, 4:5, :]
    h2 = _rms(x_new) * gffn_ref[...] * (1.0 + sc_f) + sh_f

    hh = h2.astype(BF16)
    hl = (h2 - hh.astype(F32)).astype(BF16)
    both = _dot(hh, rw_ref[...])
    logits = (both[:, 0:LANES] + both[:, LANES:2 * LANES] + _dot(hl, rw_ref[:, 0:LANES])
              + rb_ref[...])

    lane = lax.broadcasted_iota(I32, logits.shape, 1).astype(F32)
    neg = jnp.float32(-jnp.inf)
    big = jnp.float32(LANES)

    gl = jnp.where(lane < N_GROUPS, logits, neg)
    gmax = jnp.max(gl, axis=-1, keepdims=True)
    g_sel = jnp.min(jnp.where(gl == gmax, lane, big), axis=-1, keepdims=True)
    g_w = 1.0 / jnp.sum(jnp.exp(gl - gmax), axis=-1, keepdims=True)

    lo = ROUTE_COL0 + EXPERTS_PER_GROUP * g_sel
    el = jnp.where((lane >= lo) & (lane < lo + EXPERTS_PER_GROUP), logits, neg)
    m1 = jnp.max(el, axis=-1, keepdims=True)
    i1 = jnp.min(jnp.where(el == m1, lane, big), axis=-1, keepdims=True)
    el2 = jnp.where(lane == i1, neg, el)
    m2 = jnp.max(el2, axis=-1, keepdims=True)
    i2 = jnp.min(jnp.where(el2 == m2, lane, big), axis=-1, keepdims=True)
    t = jnp.exp(m2 - m1)
    gate1 = g_w / (1.0 + t)
    gate2 = g_w * t / (1.0 + t)

    sections = [slice(s * SORT_TM, (s + 1) * SORT_TM) for s in range(tm // SORT_TM)]
    is1 = lane == i1
    is2 = lane == i2
    onehot = jnp.where(is1 | is2, 1.0, 0.0)
    before = _dot(ltri_ref[...], onehot.astype(BF16))
    cnts = [jnp.sum(onehot[rows], axis=0, keepdims=True) for rows in sections]
    cnt_rows = jnp.concatenate(
        cnts + [jnp.zeros((SUBLANES - len(sections), LANES), F32)], axis=0)
    pairs = jnp.floor((cnt_rows + 1.0) * 0.5)
    pair_start = jnp.dot(pairs, utri_ref[...], precision=lax.Precision.HIGHEST,
                         preferred_element_type=F32)
    pos = jnp.concatenate([before[rows] + 2.0 * pair_start[s:s + 1]
                           for s, rows in enumerate(sections)], axis=0)
    pos1 = jnp.sum(jnp.where(is1, pos, 0.0), axis=-1, keepdims=True)
    pos2 = jnp.sum(jnp.where(is2, pos, 0.0), axis=-1, keepdims=True)
    for s, cnt in enumerate(cnts):
        cnt_ref[s] = jnp.broadcast_to(cnt, (SUBLANES, LANES))

    route = jnp.where(lane == 0, pos1, 0.0)
    route = jnp.where(lane == 1, pos2, route)
    route = jnp.where(lane == 4, gate1, route)
    route = jnp.where(lane == 5, gate2, route)
    route_ref[...] = route

    pos_t = route.T
    srow = lax.broadcasted_iota(I32, (LOCAL_ROWS, SORT_TM), 0).astype(F32)
    section_rows = LOCAL_PAIRS * (x_new.shape[1] // LANES)
    for s, cols in enumerate(sections):
        perm = jnp.where((srow == pos_t[0:1, cols]) | (srow == pos_t[1:2, cols]), 1.0, 0.0)
        _store_pairs(xsl_ref, s * section_rows, _dot(perm.astype(BF16), hh[cols]).astype(BF16))


def _even_kernel(tiles_per_seq,
                 x_ref, mod_ref, gmix_ref, gffn_ref, win_ref, gv_ref, ws_ref, bs_ref,
                 cw_ref, wout_ref, rw_ref, rb_ref, ltri_ref, utri_ref,
                 xo_ref, xsl_ref, route_ref, cnt_ref,
                 zc_scr):
    i = pl.program_id(0)
    tm = x_ref.shape[0]
    aw = gv_ref.shape[0]
    hd = aw // A_HEADS
    n_chunks = tm // CHUNK

    @pl.when(i == 0)
    def _():
        zc_scr[...] = jnp.zeros_like(zc_scr)

    x = x_ref[...]
    sh_m = mod_ref[0, 0:1, :]
    sc_m = mod_ref[0, 1:2, :]
    g_m = mod_ref[0, 2:3, :]
    h = _rms(x) * gmix_ref[...] * (1.0 + sc_m) + sh_m
    z = _dot(h.astype(BF16), win_ref[...])
    u = z[:, 0:aw]
    v = z[:, aw:2 * aw]
    b_gate = z[:, 2 * aw:3 * aw]
    c_gate = z[:, 3 * aw:4 * aw]
    x_in = z[:, 4 * aw:5 * aw]

    v_t = v.T
    row = lax.broadcasted_iota(I32, (CHUNK, CHUNK), 0)
    col = lax.broadcasted_iota(I32, (CHUNK, CHUNK), 1)
    causal = col <= row
    head_rows = []
    for hh in range(A_HEADS):
        vh = v_t[hh * hd:(hh + 1) * hd, :]
        msv = jnp.mean(vh * vh, axis=0, keepdims=True)
        vn = (vh * lax.rsqrt(msv + EPS) * gv_ref[hh * hd:(hh + 1) * hd, :]).astype(BF16)
        lhs = jnp.concatenate(
            [vn[:, c * CHUNK:(c + 1) * CHUNK] for c in range(n_chunks)], axis=0)
        w_m = jnp.where(causal, ws_ref[hh], 0.0).astype(BF16)
        sv_h = lax.dot_general(lhs, w_m, (((1,), (1,)), ((), ())),
                               preferred_element_type=F32)
        sv_h = sv_h + bs_ref[hh]
        head_rows.append(jnp.concatenate(
            [sv_h[c * hd:(c + 1) * hd, :] for c in range(n_chunks)], axis=1))
    sv = jnp.concatenate(head_rows, axis=0).T
    y_a = u * sv

    zc = c_gate * x_in
    first = (i % tiles_per_seq) == 0
    halo = zc_scr[tm:tm + SUBLANES, :]
    zc_scr[0:SUBLANES, :] = jnp.where(first, 0.0, halo)
    zc_scr[SUBLANES:SUBLANES + tm, :] = zc
    conv = cw_ref[2:3, :] * zc
    for k in range(CONV_WIDTH - 1):
        shift = CONV_WIDTH - 1 - k
        conv = conv + cw_ref[k:k + 1, :] * zc_scr[SUBLANES - shift:SUBLANES - shift + tm, :]
    y_b = b_gate * conv

    y = _dot(y_a.astype(BF16), wout_ref[0:aw, :]) + _dot(y_b.astype(BF16), wout_ref[aw:, :])
    x_new = x + g_m * y
    xo_ref[...] = x_new
    _route_and_sort(x_new, mod_ref, gffn_ref, rw_ref, rb_ref, ltri_ref, utri_ref,
                    xsl_ref, route_ref, cnt_ref)


def _odd_kernel(tiles_per_seq,
                x_ref, mod_ref, gmix_ref, gffn_ref, wpool_ref, pscale_ref,
                rw_ref, rb_ref, ltri_ref, utri_ref,
                xo_ref, xsl_ref, route_ref, cnt_ref,
                *level_scrs):
    i = pl.program_id(0)
    tm, d = x_ref.shape
    halo_rows = max(POOL_WINDOWS)
    gd = d // len(POOL_WINDOWS)

    @pl.when(i == 0)
    def _():
        for scr in level_scrs:
            scr[...] = jnp.zeros_like(scr)

    x = x_ref[...]
    sh_m = mod_ref[0, 0:1, :]
    sc_m = mod_ref[0, 1:2, :]
    g_m = mod_ref[0, 2:3, :]
    h = _rms(x) * gmix_ref[...] * (1.0 + sc_m) + sh_m

    tile_in_seq = i % tiles_per_seq
    first = tile_in_seq == 0

    sums = h
    window_sums = []
    for g, scr in enumerate(level_scrs):
        tail = scr[tm:tm + halo_rows, :]
        scr[0:halo_rows, :] = jnp.where(first, 0.0, tail)
        scr[halo_rows:halo_rows + tm, :] = sums
        lag = POOL_WINDOWS[g] // 2
        sums = sums + scr[halo_rows - lag:halo_rows - lag + tm, :]
        window_sums.append(sums[:, 0:gd])
        if g + 1 < len(level_scrs):
            sums = sums[:, gd:]

    pos = (tile_in_seq * tm + lax.broadcasted_iota(I32, (tm, 1), 0)).astype(F32)
    outs = []
    for g, win in enumerate(POOL_WINDOWS):
        cs = slice(g * gd, (g + 1) * gd)
        count = jnp.minimum(pos + 1.0, jnp.float32(win))
        pooled = window_sums[g] / count - h[:, cs]
        outs.append(_dot(pooled.astype(BF16), wpool_ref[g]))
    y = jnp.concatenate(outs, axis=-1) * pscale_ref[...]
    x_new = x + g_m * y
    xo_ref[...] = x_new
    _route_and_sort(x_new, mod_ref, gffn_ref, rw_ref, rb_ref, ltri_ref, utri_ref,
                    xsl_ref, route_ref, cnt_ref)


def _mixer_call(kernel_fn, x, mod_l, seq, weights, scratch):
    t_tok, d = x.shape
    n_tiles = t_tok // TM
    tiles_per_seq = seq // TM
    n_slabs = d // LANES

    def const_spec(a):
        return pl.BlockSpec(a.shape, lambda i, nd=a.ndim: (0,) * nd)

    in_specs = [
        pl.BlockSpec((TM, d), lambda i: (i, 0)),
        pl.BlockSpec((1,) + mod_l.shape[1:], lambda i: (i // tiles_per_seq, 0, 0)),
    ] + [const_spec(w) for w in weights]
    out_shape = (
        jax.ShapeDtypeStruct((t_tok, d), F32),
        jax.ShapeDtypeStruct((n_tiles * TILE_PAIRS * n_slabs, LANES), U32),
        jax.ShapeDtypeStruct((t_tok, LANES), F32),
        jax.ShapeDtypeStruct((n_tiles * SECTIONS, SUBLANES, LANES), F32),
    )
    out_specs = (
        pl.BlockSpec((TM, d), lambda i: (i, 0)),
        pl.BlockSpec((TILE_PAIRS * n_slabs, LANES), lambda i: (i, 0)),
        pl.BlockSpec((TM, LANES), lambda i: (i, 0)),
        pl.BlockSpec((SECTIONS, SUBLANES, LANES), lambda i: (i, 0, 0)),
    )
    return pl.pallas_call(
        functools.partial(kernel_fn, tiles_per_seq),
        grid=(n_tiles,),
        in_specs=in_specs,
        out_specs=out_specs,
        out_shape=out_shape,
        scratch_shapes=scratch,
        compiler_params=pltpu.CompilerParams(
            dimension_semantics=("arbitrary",), vmem_limit_bytes=VMEM_LIMIT),
        name=kernel_fn.__name__.strip("_"),
    )(x, mod_l, *weights)


def _rows(first_row, n_rows, unit):
    return pl.ds(pl.multiple_of(first_row, unit), n_rows)


def _expert_kernel(layer, blk_expert_ref, n_valid_ref, next_ref, slot_ref,
                   blk_first_ref, blk_last_ref, blk_fill_ref, run_len_ref, run_src_ref, run_dst_ref,
                   xsl_hbm, wg_hbm, wu_hbm, wd_hbm, ys_ref,
                   x_scr, zero_scr, wg_stage, wu_stage, wd_stage, wg_scr, wu_scr, wd_scr,
                   x_sem, sem):
    j = pl.program_id(0)
    n_valid = n_valid_ref[0]
    expert = blk_expert_ref[j]
    prev = blk_expert_ref[jnp.maximum(j - 1, 0)]
    d = wg_scr.shape[0]
    slab = d // LANES
    blk_rows = BLK_PAIRS * slab

    def fetch(blk):
        slot = blk % 2
        row0 = blk * blk_rows

        def body(q, carry):
            dst = run_dst_ref[q]
            lo = jnp.maximum(dst, row0)
            hi = jnp.minimum(dst + run_len_ref[q], row0 + blk_rows)

            @pl.when(hi > lo)
            def _():
                pltpu.make_async_copy(
                    xsl_hbm.at[_rows(run_src_ref[q] + (lo - dst), hi - lo, slab)],
                    x_scr.at[slot, _rows(lo - row0, hi - lo, slab)], x_sem.at[slot]).start()
            return carry

        lax.fori_loop(blk_first_ref[blk], blk_last_ref[blk], body, 0)
        fill = blk_fill_ref[blk]

        @pl.when(fill > 0)
        def _():
            pltpu.make_async_copy(zero_scr.at[_rows(0, fill, slab)],
                                  x_scr.at[slot, _rows(blk_rows - fill, fill, slab)],
                                  x_sem.at[slot]).start()

    @pl.when(j == 0)
    def _():
        zero_scr[...] = jnp.zeros_like(zero_scr)
        fetch(j)

    @pl.when(j + 1 < n_valid)
    def _():
        fetch(j + 1)

    def weight_copies(e, slot):
        return [pltpu.make_async_copy(hbm.at[layer, e], stage.at[slot], sem.at[slot])
                for hbm, stage in ((wg_hbm, wg_stage), (wu_hbm, wu_stage), (wd_hbm, wd_stage))]

    @pl.when(j == 0)
    def _():
        for cp in weight_copies(expert, slot_ref[expert]):
            cp.start()

    @pl.when((j == 0) | (expert != prev))
    def _():
        slot = slot_ref[expert]
        for cp in weight_copies(expert, slot):
            cp.wait()
        nxt = next_ref[expert]

        @pl.when(nxt != expert)
        def _():
            for cp in weight_copies(nxt, 1 - slot):
                cp.start()

        wg_scr[...] = wg_stage[slot].astype(BF16)
        wu_scr[...] = wu_stage[slot].astype(BF16)
        wd_scr[...] = wd_stage[slot].astype(BF16)

    @pl.when(j < n_valid)
    def _():
        slot = j % 2
        pltpu.make_async_copy(zero_scr, x_scr.at[slot], x_sem.at[slot]).wait()
        xb = _load_pairs(x_scr.at[slot], 0, BLK_PAIRS, d)
        a = _dot(xb, wg_scr[...])
        b = _dot(xb, wu_scr[...])
        hm = (a * jax.nn.sigmoid(a) * b).astype(BF16)
        _store_pairs(ys_ref, 0, _dot(hm, wd_scr[...]).astype(BF16))

    @pl.when(j >= n_valid)
    def _():
        ys_ref[...] = jnp.zeros_like(ys_ref)


def _experts(tables, xsl, n_blocks, layer, w_gate, w_up, w_down):
    _, _, d, de = w_gate.shape
    blk_rows = BLK_PAIRS * (d // LANES)
    return pl.pallas_call(
        functools.partial(_expert_kernel, layer),
        grid_spec=pltpu.PrefetchScalarGridSpec(
            num_scalar_prefetch=len(tables),
            grid=(n_blocks,),
            in_specs=[pl.BlockSpec(memory_space=pl.ANY)] * 4,
            out_specs=pl.BlockSpec((blk_rows, LANES), lambda j, *_: (j, 0)),
            scratch_shapes=[
                pltpu.VMEM((2, blk_rows, LANES), U32), pltpu.VMEM((blk_rows, LANES), U32),
                pltpu.VMEM((2, d, de), F32), pltpu.VMEM((2, d, de), F32),
                pltpu.VMEM((2, de, d), F32),
                pltpu.VMEM((d, de), BF16), pltpu.VMEM((d, de), BF16), pltpu.VMEM((de, d), BF16),
                pltpu.SemaphoreType.DMA((2,)), pltpu.SemaphoreType.DMA((2,)),
            ],
        ),
        out_shape=jax.ShapeDtypeStruct((n_blocks * blk_rows, LANES), U32),
        compiler_params=pltpu.CompilerParams(
            dimension_semantics=("arbitrary",), vmem_limit_bytes=VMEM_LIMIT),
        name="experts",
    )(*tables, xsl, w_gate, w_up, w_down)


def _combine_kernel(final, len_ref, loc_ref, glob_ref, tot_ref, ys_hbm, x_ref, route_ref,
                    mod_ref, gfin_ref, xo_ref, ysl_scr, sem):
    i = pl.program_id(0)
    tm, d = x_ref.shape
    slab = d // LANES

    def run_copy(tile, glob, loc, rows):
        slot = tile % 2
        return pltpu.make_async_copy(ys_hbm.at[_rows(glob, rows, slab)],
                                     ysl_scr.at[slot, _rows(loc, rows, slab)], sem.at[slot])

    def fetch(tile):
        def body(r, carry):
            k = tile * TILE_RUNS + r
            n = len_ref[k]

            @pl.when(n > 0)
            def _():
                run_copy(tile, glob_ref[k], loc_ref[k], n).start()
            return carry
        lax.fori_loop(0, TILE_RUNS, body, 0, unroll=4)

    @pl.when(i == 0)
    def _():
        ysl_scr[...] = jnp.zeros_like(ysl_scr)
        fetch(i)

    @pl.when(i + 1 < pl.num_programs(0))
    def _():
        fetch(i + 1)

    @pl.when(tot_ref[i] > 0)
    def _():
        run_copy(i, 0, 0, tot_ref[i]).wait()

    ysl_ref = ysl_scr.at[i % 2]
    srow = lax.broadcasted_iota(I32, (SORT_TM, LOCAL_ROWS), 1).astype(F32)
    y_sections = []
    for s in range(tm // SORT_TM):
        rows = slice(s * SORT_TM, (s + 1) * SORT_TM)
        ysl = _load_pairs(ysl_ref, s * LOCAL_PAIRS * slab, LOCAL_PAIRS, d)
        y = jnp.zeros((SORT_TM, d), F32)
        for k in range(2):
            sel = jnp.where(srow == route_ref[rows, k:k + 1], 1.0, 0.0).astype(BF16)
            y = y + route_ref[rows, 4 + k:5 + k] * _dot(sel, ysl)
        y_sections.append(y)
    x_new = x_ref[...] + mod_ref[0, 5:6, :] * jnp.concatenate(y_sections, axis=0)
    if final:
        x_new = _rms(x_new) * gfin_ref[...]
    xo_ref[...] = x_new


def _combine(tables, ys, x, route, mod_l, seq, g_final, final):
    t_tok, d = x.shape
    tiles_per_seq = seq // TM
    return pl.pallas_call(
        functools.partial(_combine_kernel, final),
        grid_spec=pltpu.PrefetchScalarGridSpec(
            num_scalar_prefetch=4,
            grid=(t_tok // TM,),
            in_specs=[
                pl.BlockSpec(memory_space=pl.ANY),
                pl.BlockSpec((TM, d), lambda i, *_: (i, 0)),
                pl.BlockSpec((TM, LANES), lambda i, *_: (i, 0)),
                pl.BlockSpec((1,) + mod_l.shape[1:], lambda i, *_: (i // tiles_per_seq, 0, 0)),
                pl.BlockSpec((1, d), lambda i, *_: (0, 0)),
            ],
            out_specs=pl.BlockSpec((TM, d), lambda i, *_: (i, 0)),
            scratch_shapes=[pltpu.VMEM((2, TILE_PAIRS * (d // LANES), LANES), U32),
                            pltpu.SemaphoreType.DMA((2,))],
        ),
        out_shape=jax.ShapeDtypeStruct((t_tok, d), F32),
        compiler_params=pltpu.CompilerParams(
            dimension_semantics=("arbitrary",), vmem_limit_bytes=VMEM_LIMIT),
        name="combine_final" if final else "combine",
    )(*tables, ys, x, route, mod_l, g_final)


def _moe(x, xsl, route, cnt, mod_l, seq, layer, w_gate, w_up, w_down, g_final, final):
    t_tok = x.shape[0]
    n_tiles = t_tok // TM
    n_blocks = -(-(2 * t_tok + n_tiles * TILE_RUNS) // EXPERT_BLK) + N_EXPERTS

    counts = cnt[:, 0, ROUTE_COL0:ROUTE_COL0 + N_EXPERTS].astype(I32)
    run_len = (counts + 1) // 2
    section_base = (jnp.arange(counts.shape[0], dtype=I32) % SECTIONS) * LOCAL_PAIRS
    run_loc = jnp.cumsum(run_len, axis=1) - run_len + section_base[:, None]
    seg_len = jnp.sum(run_len, axis=0)
    seg_pad = (seg_len + BLK_PAIRS - 1) // BLK_PAIRS * BLK_PAIRS
    seg_end = jnp.cumsum(seg_pad)
    seg_start = seg_end - seg_pad
    run_glob = seg_start[None, :] + jnp.cumsum(run_len, axis=0) - run_len
    tile_tot = jnp.sum(run_len.reshape(n_tiles, TILE_RUNS), axis=1)
    slab = x.shape[1] // LANES
    run_tables = tuple((a.reshape(-1) * slab).astype(I32)
                       for a in (run_len, run_loc, run_glob, tile_tot))
    n_valid = (seg_end[-1:] // BLK_PAIRS).astype(I32)
    blk_pair0 = jnp.arange(n_blocks, dtype=I32) * BLK_PAIRS
    blk_expert = jnp.sum((seg_end[None, :] <= blk_pair0[:, None]).astype(I32), axis=1)
    section_pair0 = jnp.arange(counts.shape[0], dtype=I32) * LOCAL_PAIRS
    run_src = jnp.cumsum(run_len, axis=1) - run_len + section_pair0[:, None]
    em_len, em_src, em_dst = (a.T.reshape(-1) for a in (run_len, run_src, run_glob))
    blk_first = jnp.sum(((em_dst + em_len)[None, :] <= blk_pair0[:, None]).astype(I32), axis=1)
    blk_last = jnp.sum((em_dst[None, :] < (blk_pair0 + BLK_PAIRS)[:, None]).astype(I32), axis=1)
    seg_stop = jnp.sum(jnp.where(
        jnp.arange(N_EXPERTS, dtype=I32)[None, :] == blk_expert[:, None],
        (seg_start + seg_len)[None, :], 0), axis=1)
    blk_fill = BLK_PAIRS - jnp.clip(seg_stop - blk_pair0, 0, BLK_PAIRS)
    pull_tables = tuple((a * slab).astype(I32) for a in (blk_fill, em_len, em_src, em_dst))
    experts = jnp.arange(N_EXPERTS, dtype=I32)
    used = seg_len > 0
    last_used = jnp.max(jnp.where(used, experts, 0))
    blk_expert = jnp.minimum(blk_expert, last_used).astype(I32)
    later = jnp.where((experts[None, :] > experts[:, None]) & used[None, :],
                      experts[None, :], N_EXPERTS)
    next_used = jnp.min(later, axis=1)
    next_used = jnp.where(next_used == N_EXPERTS, experts, next_used).astype(I32)
    slot_of = ((jnp.cumsum(used.astype(I32)) - 1) % 2).astype(I32)

    expert_tables = ((blk_expert, n_valid, next_used, slot_of,
                      blk_first.astype(I32), blk_last.astype(I32)) + pull_tables)
    ys = _experts(expert_tables, xsl, n_blocks, layer, w_gate, w_up, w_down)
    return _combine(run_tables, ys, x, route, mod_l, seq, g_final, final)


def kernel(x, c, w_ada, b_ada, norm_mix_g, norm_ffn_g, w_in_even, sgu_norm_g, w_spatial, b_spatial, conv_w, w_out_even, w_pool, pool_scale, w_group_router, b_group_router, w_expert_router, b_expert_router, moe_w_gate, moe_w_up, moe_w_down, final_norm_g):
    bsz, seq, d = x.shape
    depth = w_ada.shape[0]
    t_tok = bsz * seq
    assert seq % TM == 0 and d % LANES == 0 and w_spatial.shape[-1] == CHUNK
    assert all(w == 2 ** (g + 1) for g, w in enumerate(POOL_WINDOWS))

    mod = _modulation(c, w_ada, b_ada).reshape(depth, bsz, 6, d)
    tok = jnp.arange(TM)
    ltri = ((tok[:, None] > tok[None, :])
            & (tok[:, None] // SORT_TM == tok[None, :] // SORT_TM)).astype(BF16)
    utri = jnp.triu(jnp.ones((LANES, LANES), F32), 1)
    g_final = final_norm_g.reshape(1, d)

    xf = x.reshape(t_tok, d)
    for l in range(depth):
        i = l // 2
        rw = jnp.concatenate([w_group_router[l], w_expert_router[l]], axis=1)
        rw = jnp.pad(rw, ((0, 0), (0, LANES - rw.shape[1])))
        rw_hi = rw.astype(BF16)
        rw_lo = (rw - rw_hi.astype(F32)).astype(BF16)
        rb = jnp.concatenate([b_group_router[l], b_expert_router[l]])
        rb = jnp.pad(rb, (0, LANES - rb.shape[0])).reshape(1, LANES)
        route_w = [jnp.concatenate([rw_hi, rw_lo], axis=1), rb, ltri, utri]
        gmix = norm_mix_g[l].reshape(1, d)
        gffn = norm_ffn_g[l].reshape(1, d)
        if l % 2 == 0:
            aw = sgu_norm_g.shape[1]
            weights = [gmix, gffn, w_in_even[i].astype(BF16), sgu_norm_g[i].reshape(aw, 1),
                       w_spatial[i], b_spatial[i].reshape(A_HEADS, 1, CHUNK), conv_w[i],
                       w_out_even[i].astype(BF16)] + route_w
            scratch = [pltpu.VMEM((TM + SUBLANES, conv_w.shape[-1]), F32)]
            xf, xsl, route, cnt = _mixer_call(_even_kernel, xf, mod[l], seq, weights, scratch)
        else:
            weights = [gmix, gffn, w_pool[i].astype(BF16), pool_scale[i].reshape(1, d)] + route_w
            gd = d // len(POOL_WINDOWS)
            scratch = [pltpu.VMEM((TM + max(POOL_WINDOWS), d - g * gd), F32)
                       for g in range(len(POOL_WINDOWS))]
            xf, xsl, route, cnt = _mixer_call(_odd_kernel, xf, mod[l], seq, weights, scratch)
        xf = _moe(xf, xsl, route, cnt, mod[l], seq, l, moe_w_gate, moe_w_up, moe_w_down,
                  g_final, l == depth - 1)
    return xf.reshape(bsz, seq, d)
```

```python
import functools

import jax
import jax.numpy as jnp
from jax import lax
from jax.experimental import pallas as pl
from jax.experimental.pallas import tpu as pltpu

F32 = jnp.float32
BF16 = jnp.bfloat16
U32 = jnp.uint32
I32 = jnp.int32

EPS = 1e-6
LANES = 128
SUBLANES = 8
CHUNK = 128
A_HEADS = 8
N_GROUPS = 4
EXPERTS_PER_GROUP = 8
N_EXPERTS = N_GROUPS * EXPERTS_PER_GROUP
POOL_WINDOWS = (2, 4, 8, 16)
CONV_WIDTH = 3
ROUTE_COL0 = N_GROUPS

TM = 512
SORT_TM = 256
SECTIONS = TM // SORT_TM
LOCAL_ROWS = 2 * SORT_TM + LANES
LOCAL_PAIRS = LOCAL_ROWS // 2
TILE_PAIRS = SECTIONS * LOCAL_PAIRS
TILE_RUNS = SECTIONS * N_EXPERTS
EXPERT_BLK = 512
BLK_PAIRS = EXPERT_BLK // 2
MOD_TN = 1536
VMEM_LIMIT = 56 * 1024 * 1024


def _rms(x):
    return x * lax.rsqrt(jnp.mean(x * x, axis=-1, keepdims=True) + EPS)


def _dot(a, b):
    return jnp.dot(a, b, preferred_element_type=F32)


def _store_pairs(ref, base, rows_bf16):
    words = pltpu.bitcast(rows_bf16, U32)
    n_pairs, d = words.shape
    slab = d // LANES
    for c in range(slab):
        ref[pl.ds(base + c, n_pairs, stride=slab), :] = words[:, c * LANES:(c + 1) * LANES]


def _load_pairs(ref, base, n_pairs, d):
    slab = d // LANES
    words = jnp.concatenate(
        [ref[pl.ds(base + c, n_pairs, stride=slab), :] for c in range(slab)], axis=-1)
    return pltpu.bitcast(words, BF16)


def _mod_kernel(c_ref, w_ref, b_ref, o_ref):
    c = c_ref[...]
    ca = c * jax.nn.sigmoid(c)
    o_ref[0] = jnp.dot(ca, w_ref[0], precision=lax.Precision.HIGHEST,
                       preferred_element_type=F32) + b_ref[0]


def _modulation(c, w_ada, b_ada):
    depth, d, n = w_ada.shape
    bsz = c.shape[0]
    return pl.pallas_call(
        _mod_kernel,
        grid=(depth, n // MOD_TN),
        in_specs=[
            pl.BlockSpec((bsz, d), lambda l, j: (0, 0)),
            pl.BlockSpec((1, d, MOD_TN), lambda l, j: (l, 0, j)),
            pl.BlockSpec((1, 1, MOD_TN), lambda l, j: (l, 0, j)),
        ],
        out_specs=pl.BlockSpec((1, bsz, MOD_TN), lambda l, j: (l, 0, j)),
        out_shape=jax.ShapeDtypeStruct((depth, bsz, n), F32),
        compiler_params=pltpu.CompilerParams(
            dimension_semantics=("arbitrary", "arbitrary"),
            vmem_limit_bytes=VMEM_LIMIT),
        name="modulation",
    )(c, w_ada, b_ada.reshape(depth, 1, n))


def _route_and_sort(x_new, mod_ref, gffn_ref, rw_ref, rb_ref, ltri_ref, utri_ref,
                    xsl_ref, route_ref, cnt_ref):
    tm = x_new.shape[0]
    sh_f = mod_ref[0, 3:4, :]
    sc_f = mod_ref[0, 4:5, :]
    h2 = _rms(x_new) * gffn_ref[...] * (1.0 + sc_f) + sh_f

    hh = h2.astype(BF16)
    hl = (h2 - hh.astype(F32)).astype(BF16)
    both = _dot(hh, rw_ref[...])
    logits = (both[:, 0:LANES] + both[:, LANES:2 * LANES] + _dot(hl, rw_ref[:, 0:LANES])
              + rb_ref[...])

    lane = lax.broadcasted_iota(I32, logits.shape, 1).astype(F32)
    neg = jnp.float32(-jnp.inf)
    big = jnp.float32(LANES)

    gl = jnp.where(lane < N_GROUPS, logits, neg)
    gmax = jnp.max(gl, axis=-1, keepdims=True)
    g_sel = jnp.min(jnp.where(gl == gmax, lane, big), axis=-1, keepdims=True)
    g_w = 1.0 / jnp.sum(jnp.exp(gl - gmax), axis=-1, keepdims=True)

    lo = ROUTE_COL0 + EXPERTS_PER_GROUP * g_sel
    el = jnp.where((lane >= lo) & (lane < lo + EXPERTS_PER_GROUP), logits, neg)
    m1 = jnp.max(el, axis=-1, keepdims=True)
    i1 = jnp.min(jnp.where(el == m1, lane, big), axis=-1, keepdims=True)
    el2 = jnp.where(lane == i1, neg, el)
    m2 = jnp.max(el2, axis=-1, keepdims=True)
    i2 = jnp.min(jnp.where(el2 == m2, lane, big), axis=-1, keepdims=True)
    t = jnp.exp(m2 - m1)
    gate1 = g_w / (1.0 + t)
    gate2 = g_w * t / (1.0 + t)

    sections = [slice(s * SORT_TM, (s + 1) * SORT_TM) for s in range(tm // SORT_TM)]
    is1 = lane == i1
    is2 = lane == i2
    onehot = jnp.where(is1 | is2, 1.0, 0.0)
    before = _dot(ltri_ref[...], onehot.astype(BF16))
    cnts = [jnp.sum(onehot[rows], axis=0, keepdims=True) for rows in sections]
    cnt_rows = jnp.concatenate(
        cnts + [jnp.zeros((SUBLANES - len(sections), LANES), F32)], axis=0)
    pairs = jnp.floor((cnt_rows + 1.0) * 0.5)
    pair_start = jnp.dot(pairs, utri_ref[...], precision=lax.Precision.HIGHEST,
                         preferred_element_type=F32)
    pos = jnp.concatenate([before[rows] + 2.0 * pair_start[s:s + 1]
                           for s, rows in enumerate(sections)], axis=0)
    pos1 = jnp.sum(jnp.where(is1, pos, 0.0), axis=-1, keepdims=True)
    pos2 = jnp.sum(jnp.where(is2, pos, 0.0), axis=-1, keepdims=True)
    for s, cnt in enumerate(cnts):
        cnt_ref[s] = jnp.broadcast_to(cnt, (SUBLANES, LANES))

    route = jnp.where(lane == 0, pos1, 0.0)
    route = jnp.where(lane == 1, pos2, route)
    route = jnp.where(lane == 4, gate1, route)
    route = jnp.where(lane == 5, gate2, route)
    route_ref[...] = route

    pos_t = route.T
    srow = lax.broadcasted_iota(I32, (LOCAL_ROWS, SORT_TM), 0).astype(F32)
    section_rows = LOCAL_PAIRS * (x_new.shape[1] // LANES)
    for s, cols in enumerate(sections):
        perm = jnp.where((srow == pos_t[0:1, cols]) | (srow == pos_t[1:2, cols]), 1.0, 0.0)
        _store_pairs(xsl_ref, s * section_rows, _dot(perm.astype(BF16), hh[cols]).astype(BF16))


def _even_kernel(tiles_per_seq,
                 x_ref, mod_ref, gmix_ref, gffn_ref, win_ref, gv_ref, ws_ref, bs_ref,
                 cw_ref, wout_ref, rw_ref, rb_ref, ltri_ref, utri_ref,
                 xo_ref, xsl_ref, route_ref, cnt_ref,
                 zc_scr):
    i = pl.program_id(0)
    tm = x_ref.shape[0]
    aw = gv_ref.shape[0]
    hd = aw // A_HEADS
    n_chunks = tm // CHUNK

    @pl.when(i == 0)
    def _():
        zc_scr[...] = jnp.zeros_like(zc_scr)

    x = x_ref[...]
    sh_m = mod_ref[0, 0:1, :]
    sc_m = mod_ref[0, 1:2, :]
    g_m = mod_ref[0, 2:3, :]
    h = _rms(x) * gmix_ref[...] * (1.0 + sc_m) + sh_m
    z = _dot(h.astype(BF16), win_ref[...])
    u = z[:, 0:aw]
    v = z[:, aw:2 * aw]
    b_gate = z[:, 2 * aw:3 * aw]
    c_gate = z[:, 3 * aw:4 * aw]
    x_in = z[:, 4 * aw:5 * aw]

    v_t = v.T
    row = lax.broadcasted_iota(I32, (CHUNK, CHUNK), 0)
    col = lax.broadcasted_iota(I32, (CHUNK, CHUNK), 1)
    causal = col <= row
    head_rows = []
    for hh in range(A_HEADS):
        vh = v_t[hh * hd:(hh + 1) * hd, :]
        msv = jnp.mean(vh * vh, axis=0, keepdims=True)
        vn = (vh * lax.rsqrt(msv + EPS) * gv_ref[hh * hd:(hh + 1) * hd, :]).astype(BF16)
        lhs = jnp.concatenate(
            [vn[:, c * CHUNK:(c + 1) * CHUNK] for c in range(n_chunks)], axis=0)
        w_m = jnp.where(causal, ws_ref[hh], 0.0).astype(BF16)
        sv_h = lax.dot_general(lhs, w_m, (((1,), (1,)), ((), ())),
                               preferred_element_type=F32)
        sv_h = sv_h + bs_ref[hh]
        head_rows.append(jnp.concatenate(
            [sv_h[c * hd:(c + 1) * hd, :] for c in range(n_chunks)], axis=1))
    sv = jnp.concatenate(head_rows, axis=0).T
    y_a = u * sv

    zc = c_gate * x_in
    first = (i % tiles_per_seq) == 0
    halo = zc_scr[tm:tm + SUBLANES, :]
    zc_scr[0:SUBLANES, :] = jnp.where(first, 0.0, halo)
    zc_scr[SUBLANES:SUBLANES + tm, :] = zc
    conv = cw_ref[2:3, :] * zc
    for k in range(CONV_WIDTH - 1):
        shift = CONV_WIDTH - 1 - k
        conv = conv + cw_ref[k:k + 1, :] * zc_scr[SUBLANES - shift:SUBLANES - shift + tm, :]
    y_b = b_gate * conv

    y = _dot(y_a.astype(BF16), wout_ref[0:aw, :]) + _dot(y_b.astype(BF16), wout_ref[aw:, :])
    x_new = x + g_m * y
    xo_ref[...] = x_new
    _route_and_sort(x_new, mod_ref, gffn_ref, rw_ref, rb_ref, ltri_ref, utri_ref,
                    xsl_ref, route_ref, cnt_ref)


def _odd_kernel(tiles_per_seq,
                x_ref, mod_ref, gmix_ref, gffn_ref, wpool_ref, pscale_ref,
                rw_ref, rb_ref, ltri_ref, utri_ref,
                xo_ref, xsl_ref, route_ref, cnt_ref,
                *level_scrs):
    i = pl.program_id(0)
    tm, d = x_ref.shape
    halo_rows = max(POOL_WINDOWS)
    gd = d // len(POOL_WINDOWS)

    @pl.when(i == 0)
    def _():
        for scr in level_scrs:
            scr[...] = jnp.zeros_like(scr)

    x = x_ref[...]
    sh_m = mod_ref[0, 0:1, :]
    sc_m = mod_ref[0, 1:2, :]
    g_m = mod_ref[0, 2:3, :]
    h = _rms(x) * gmix_ref[...] * (1.0 + sc_m) + sh_m

    tile_in_seq = i % tiles_per_seq
    first = tile_in_seq == 0

    sums = h
    window_sums = []
    for g, scr in enumerate(level_scrs):
        tail = scr[tm:tm + halo_rows, :]
        scr[0:halo_rows, :] = jnp.where(first, 0.0, tail)
        scr[halo_rows:halo_rows + tm, :] = sums
        lag = POOL_WINDOWS[g] // 2
        sums = sums + scr[halo_rows - lag:halo_rows - lag + tm, :]
        window_sums.append(sums[:, 0:gd])
        if g + 1 < len(level_scrs):
            sums = sums[:, gd:]

    pos = (tile_in_seq * tm + lax.broadcasted_iota(I32, (tm, 1), 0)).astype(F32)
    outs = []
    for g, win in enumerate(POOL_WINDOWS):
        cs = slice(g * gd, (g + 1) * gd)
        count = jnp.minimum(pos + 1.0, jnp.float32(win))
        pooled = window_sums[g] / count - h[:, cs]
        outs.append(_dot(pooled.astype(BF16), wpool_ref[g]))
    y = jnp.concatenate(outs, axis=-1) * pscale_ref[...]
    x_new = x + g_m * y
    xo_ref[...] = x_new
    _route_and_sort(x_new, mod_ref, gffn_ref, rw_ref, rb_ref, ltri_ref, utri_ref,
                    xsl_ref, route_ref, cnt_ref)


def _mixer_call(kernel_fn, x, mod_l, seq, weights, scratch):
    t_tok, d = x.shape
    n_tiles = t_tok // TM
    tiles_per_seq = seq // TM
    n_slabs = d // LANES

    def const_spec(a):
        return pl.BlockSpec(a.shape, lambda i, nd=a.ndim: (0,) * nd)

    in_specs = [
        pl.BlockSpec((TM, d), lambda i: (i, 0)),
        pl.BlockSpec((1,) + mod_l.shape[1:], lambda i: (i // tiles_per_seq, 0, 0)),
    ] + [const_spec(w) for w in weights]
    out_shape = (
        jax.ShapeDtypeStruct((t_tok, d), F32),
        jax.ShapeDtypeStruct((n_tiles * TILE_PAIRS * n_slabs, LANES), U32),
        jax.ShapeDtypeStruct((t_tok, LANES), F32),
        jax.ShapeDtypeStruct((n_tiles * SECTIONS, SUBLANES, LANES), F32),
    )
    out_specs = (
        pl.BlockSpec((TM, d), lambda i: (i, 0)),
        pl.BlockSpec((TILE_PAIRS * n_slabs, LANES), lambda i: (i, 0)),
        pl.BlockSpec((TM, LANES), lambda i: (i, 0)),
        pl.BlockSpec((SECTIONS, SUBLANES, LANES), lambda i: (i, 0, 0)),
    )
    return pl.pallas_call(
        functools.partial(kernel_fn, tiles_per_seq),
        grid=(n_tiles,),
        in_specs=in_specs,
        out_specs=out_specs,
        out_shape=out_shape,
        scratch_shapes=scratch,
        compiler_params=pltpu.CompilerParams(
            dimension_semantics=("arbitrary",), vmem_limit_bytes=VMEM_LIMIT),
        name=kernel_fn.__name__.strip("_"),
    )(x, mod_l, *weights)


def _rows(first_row, n_rows, unit):
    return pl.ds(pl.multiple_of(first_row, unit), n_rows)


def _expert_kernel(layer, blk_expert_ref, n_valid_ref, next_ref, slot_ref,
                   blk_first_ref, blk_last_ref, blk_fill_ref, run_len_ref, run_src_ref, run_dst_ref,
                   xsl_hbm, wg_hbm, wu_hbm, wd_hbm, ys_ref,
                   x_scr, zero_scr, wg_stage, wu_stage, wd_stage, wg_scr, wu_scr, wd_scr,
                   x_sem, sem):
    j = pl.program_id(0)
    n_valid = n_valid_ref[0]
    expert = blk_expert_ref[j]
    prev = blk_expert_ref[jnp.maximum(j - 1, 0)]
    d = wg_scr.shape[0]
    slab = d // LANES
    blk_rows = BLK_PAIRS * slab

    def fetch(blk):
        slot = blk % 2
        row0 = blk * blk_rows

        def copy_run(q):
            dst = run_dst_ref[q]
            lo = jnp.maximum(dst, row0)
            hi = jnp.minimum(dst + run_len_ref[q], row0 + blk_rows)

            @pl.when(hi > lo)
            def _():
                pltpu.make_async_copy(
                    xsl_hbm.at[_rows(run_src_ref[q] + (lo - dst), hi - lo, slab)],
                    x_scr.at[slot, _rows(lo - row0, hi - lo, slab)], x_sem.at[slot]).start()

        first = blk_first_ref[blk]

        def body(t, carry):
            copy_run(first + 2 * t)
            copy_run(first + 2 * t + 1)
            return carry

        lax.fori_loop(0, (blk_last_ref[blk] - first + 1) // 2, body, 0)
        fill = blk_fill_ref[blk]

        @pl.when(fill > 0)
        def _():
            pltpu.make_async_copy(zero_scr.at[_rows(0, fill, slab)],
                                  x_scr.at[slot, _rows(blk_rows - fill, fill, slab)],
                                  x_sem.at[slot]).start()

    @pl.when(j == 0)
    def _():
        zero_scr[...] = jnp.zeros_like(zero_scr)
        fetch(j)

    @pl.when(j + 1 < n_valid)
    def _():
        fetch(j + 1)

    def weight_copies(e, slot):
        return [pltpu.make_async_copy(hbm.at[layer, e], stage.at[slot], sem.at[slot])
                for hbm, stage in ((wg_hbm, wg_stage), (wu_hbm, wu_stage), (wd_hbm, wd_stage))]

    @pl.when(j == 0)
    def _():
        for cp in weight_copies(expert, slot_ref[expert]):
            cp.start()

    @pl.when((j == 0) | (expert != prev))
    def _():
        slot = slot_ref[expert]
        for cp in weight_copies(expert, slot):
            cp.wait()
        nxt = next_ref[expert]

        @pl.when(nxt != expert)
        def _():
            for cp in weight_copies(nxt, 1 - slot):
                cp.start()

        wg_scr[...] = wg_stage[slot].astype(BF16)
        wu_scr[...] = wu_stage[slot].astype(BF16)
        wd_scr[...] = wd_stage[slot].astype(BF16)

    @pl.when(j < n_valid)
    def _():
        slot = j % 2
        pltpu.make_async_copy(zero_scr, x_scr.at[slot], x_sem.at[slot]).wait()
        xb = _load_pairs(x_scr.at[slot], 0, BLK_PAIRS, d)
        a = _dot(xb, wg_scr[...])
        b = _dot(xb, wu_scr[...])
        hm = (a * jax.nn.sigmoid(a) * b).astype(BF16)
        _store_pairs(ys_ref, 0, _dot(hm, wd_scr[...]).astype(BF16))

    @pl.when(j >= n_valid)
    def _():
        ys_ref[...] = jnp.zeros_like(ys_ref)


def _experts(tables, xsl, n_blocks, layer, w_gate, w_up, w_down):
    _, _, d, de = w_gate.shape
    blk_rows = BLK_PAIRS * (d // LANES)
    return pl.pallas_call(
        functools.partial(_expert_kernel, layer),
        grid_spec=pltpu.PrefetchScalarGridSpec(
            num_scalar_prefetch=len(tables),
            grid=(n_blocks,),
            in_specs=[pl.BlockSpec(memory_space=pl.ANY)] * 4,
            out_specs=pl.BlockSpec((blk_rows, LANES), lambda j, *_: (j, 0)),
            scratch_shapes=[
                pltpu.VMEM((2, blk_rows, LANES), U32), pltpu.VMEM((blk_rows, LANES), U32),
                pltpu.VMEM((2, d, de), F32), pltpu.VMEM((2, d, de), F32),
                pltpu.VMEM((2, de, d), F32),
                pltpu.VMEM((d, de), BF16), pltpu.VMEM((d, de), BF16), pltpu.VMEM((de, d), BF16),
                pltpu.SemaphoreType.DMA((2,)), pltpu.SemaphoreType.DMA((2,)),
            ],
        ),
        out_shape=jax.ShapeDtypeStruct((n_blocks * blk_rows, LANES), U32),
        compiler_params=pltpu.CompilerParams(
            dimension_semantics=("arbitrary",), vmem_limit_bytes=VMEM_LIMIT),
        name="experts",
    )(*tables, xsl, w_gate, w_up, w_down)


def _combine_kernel(final, len_ref, loc_ref, glob_ref, tot_ref, ys_hbm, x_ref, route_ref,
                    mod_ref, gfin_ref, xo_ref, ysl_scr, sem):
    i = pl.program_id(0)
    tm, d = x_ref.shape
    slab = d // LANES

    def run_copy(tile, glob, loc, rows):
        slot = tile % 2
        return pltpu.make_async_copy(ys_hbm.at[_rows(glob, rows, slab)],
                                     ysl_scr.at[slot, _rows(loc, rows, slab)], sem.at[slot])

    def fetch(tile):
        def body(r, carry):
            k = tile * TILE_RUNS + r
            n = len_ref[k]

            @pl.when(n > 0)
            def _():
                run_copy(tile, glob_ref[k], loc_ref[k], n).start()
            return carry
        lax.fori_loop(0, TILE_RUNS, body, 0, unroll=4)

    @pl.when(i == 0)
    def _():
        ysl_scr[...] = jnp.zeros_like(ysl_scr)
        fetch(i)

    @pl.when(i + 1 < pl.num_programs(0))
    def _():
        fetch(i + 1)

    @pl.when(tot_ref[i] > 0)
    def _():
        run_copy(i, 0, 0, tot_ref[i]).wait()

    ysl_ref = ysl_scr.at[i % 2]
    srow = lax.broadcasted_iota(I32, (SORT_TM, LOCAL_ROWS), 1).astype(F32)
    y_sections = []
    for s in range(tm // SORT_TM):
        rows = slice(s * SORT_TM, (s + 1) * SORT_TM)
        ysl = _load_pairs(ysl_ref, s * LOCAL_PAIRS * slab, LOCAL_PAIRS, d)
        y = jnp.zeros((SORT_TM, d), F32)
        for k in range(2):
            sel = jnp.where(srow == route_ref[rows, k:k + 1], 1.0, 0.0).astype(BF16)
            y = y + route_ref[rows, 4 + k:5 + k] * _dot(sel, ysl)
        y_sections.append(y)
    x_new = x_ref[...] + mod_ref[0, 5:6, :] * jnp.concatenate(y_sections, axis=0)
    if final:
        x_new = _rms(x_new) * gfin_ref[...]
    xo_ref[...] = x_new


def _combine(tables, ys, x, route, mod_l, seq, g_final, final):
    t_tok, d = x.shape
    tiles_per_seq = seq // TM
    return pl.pallas_call(
        functools.partial(_combine_kernel, final),
        grid_spec=pltpu.PrefetchScalarGridSpec(
            num_scalar_prefetch=4,
            grid=(t_tok // TM,),
            in_specs=[
                pl.BlockSpec(memory_space=pl.ANY),
                pl.BlockSpec((TM, d), lambda i, *_: (i, 0)),
                pl.BlockSpec((TM, LANES), lambda i, *_: (i, 0)),
                pl.BlockSpec((1,) + mod_l.shape[1:], lambda i, *_: (i // tiles_per_seq, 0, 0)),
                pl.BlockSpec((1, d), lambda i, *_: (0, 0)),
            ],
            out_specs=pl.BlockSpec((TM, d), lambda i, *_: (i, 0)),
            scratch_shapes=[pltpu.VMEM((2, TILE_PAIRS * (d // LANES), LANES), U32),
                            pltpu.SemaphoreType.DMA((2,))],
        ),
        out_shape=jax.ShapeDtypeStruct((t_tok, d), F32),
        compiler_params=pltpu.CompilerParams(
            dimension_semantics=("arbitrary",), vmem_limit_bytes=VMEM_LIMIT),
        name="combine_final" if final else "combine",
    )(*tables, ys, x, route, mod_l, g_final)


def _moe(x, xsl, route, cnt, mod_l, seq, layer, w_gate, w_up, w_down, g_final, final):
    t_tok = x.shape[0]
    n_tiles = t_tok // TM
    n_blocks = -(-(2 * t_tok + n_tiles * TILE_RUNS) // EXPERT_BLK) + N_EXPERTS

    counts = cnt[:, 0, ROUTE_COL0:ROUTE_COL0 + N_EXPERTS].astype(I32)
    run_len = (counts + 1) // 2
    section_base = (jnp.arange(counts.shape[0], dtype=I32) % SECTIONS) * LOCAL_PAIRS
    run_loc = jnp.cumsum(run_len, axis=1) - run_len + section_base[:, None]
    seg_len = jnp.sum(run_len, axis=0)
    seg_pad = (seg_len + BLK_PAIRS - 1) // BLK_PAIRS * BLK_PAIRS
    seg_end = jnp.cumsum(seg_pad)
    seg_start = seg_end - seg_pad
    run_glob = seg_start[None, :] + jnp.cumsum(run_len, axis=0) - run_len
    tile_tot = jnp.sum(run_len.reshape(n_tiles, TILE_RUNS), axis=1)
    slab = x.shape[1] // LANES
    run_tables = tuple((a.reshape(-1) * slab).astype(I32)
                       for a in (run_len, run_loc, run_glob, tile_tot))
    n_valid = (seg_end[-1:] // BLK_PAIRS).astype(I32)
    blk_pair0 = jnp.arange(n_blocks, dtype=I32) * BLK_PAIRS
    blk_expert = jnp.sum((seg_end[None, :] <= blk_pair0[:, None]).astype(I32), axis=1)
    section_pair0 = jnp.arange(counts.shape[0], dtype=I32) * LOCAL_PAIRS
    run_src = jnp.cumsum(run_len, axis=1) - run_len + section_pair0[:, None]
    em_len, em_src, em_dst = (
        jnp.concatenate([a.T.reshape(-1), jnp.full((1,), end, I32)])
        for a, end in ((run_len, 0), (run_src, 0), (run_glob, n_blocks * BLK_PAIRS)))
    blk_first = jnp.sum(((em_dst + em_len)[None, :] <= blk_pair0[:, None]).astype(I32), axis=1)
    blk_last = jnp.sum((em_dst[None, :] < (blk_pair0 + BLK_PAIRS)[:, None]).astype(I32), axis=1)
    seg_stop = jnp.sum(jnp.where(
        jnp.arange(N_EXPERTS, dtype=I32)[None, :] == blk_expert[:, None],
        (seg_start + seg_len)[None, :], 0), axis=1)
    blk_fill = BLK_PAIRS - jnp.clip(seg_stop - blk_pair0, 0, BLK_PAIRS)
    pull_tables = tuple((a * slab).astype(I32) for a in (blk_fill, em_len, em_src, em_dst))
    experts = jnp.arange(N_EXPERTS, dtype=I32)
    used = seg_len > 0
    last_used = jnp.max(jnp.where(used, experts, 0))
    blk_expert = jnp.minimum(blk_expert, last_used).astype(I32)
    later = jnp.where((experts[None, :] > experts[:, None]) & used[None, :],
                      experts[None, :], N_EXPERTS)
    next_used = jnp.min(later, axis=1)
    next_used = jnp.where(next_used == N_EXPERTS, experts, next_used).astype(I32)
    slot_of = ((jnp.cumsum(used.astype(I32)) - 1) % 2).astype(I32)

    expert_tables = ((blk_expert, n_valid, next_used, slot_of,
                      blk_first.astype(I32), blk_last.astype(I32)) + pull_tables)
    ys = _experts(expert_tables, xsl, n_blocks, layer, w_gate, w_up, w_down)
    return _combine(run_tables, ys, x, route, mod_l, seq, g_final, final)


def kernel(x, c, w_ada, b_ada, norm_mix_g, norm_ffn_g, w_in_even, sgu_norm_g, w_spatial, b_spatial, conv_w, w_out_even, w_pool, pool_scale, w_group_router, b_group_router, w_expert_router, b_expert_router, moe_w_gate, moe_w_up, moe_w_down, final_norm_g):
    bsz, seq, d = x.shape
    depth = w_ada.shape[0]
    t_tok = bsz * seq
    assert seq % TM == 0 and d % LANES == 0 and w_spatial.shape[-1] == CHUNK
    assert all(w == 2 ** (g + 1) for g, w in enumerate(POOL_WINDOWS))

    mod = _modulation(c, w_ada, b_ada).reshape(depth, bsz, 6, d)
    tok = jnp.arange(TM)
    ltri = ((tok[:, None] > tok[None, :])
            & (tok[:, None] // SORT_TM == tok[None, :] // SORT_TM)).astype(BF16)
    utri = jnp.triu(jnp.ones((LANES, LANES), F32), 1)
    g_final = final_norm_g.reshape(1, d)

    xf = x.reshape(t_tok, d)
    for l in range(depth):
        i = l // 2
        rw = jnp.concatenate([w_group_router[l], w_expert_router[l]], axis=1)
        rw = jnp.pad(rw, ((0, 0), (0, LANES - rw.shape[1])))
        rw_hi = rw.astype(BF16)
        rw_lo = (rw - rw_hi.astype(F32)).astype(BF16)
        rb = jnp.concatenate([b_group_router[l], b_expert_router[l]])
        rb = jnp.pad(rb, (0, LANES - rb.shape[0])).reshape(1, LANES)
        route_w = [jnp.concatenate([rw_hi, rw_lo], axis=1), rb, ltri, utri]
        gmix = norm_mix_g[l].reshape(1, d)
        gffn = norm_ffn_g[l].reshape(1, d)
        if l % 2 == 0:
            aw = sgu_norm_g.shape[1]
            weights = [gmix, gffn, w_in_even[i].astype(BF16), sgu_norm_g[i].reshape(aw, 1),
                       w_spatial[i], b_spatial[i].reshape(A_HEADS, 1, CHUNK), conv_w[i],
                       w_out_even[i].astype(BF16)] + route_w
            scratch = [pltpu.VMEM((TM + SUBLANES, conv_w.shape[-1]), F32)]
            xf, xsl, route, cnt = _mixer_call(_even_kernel, xf, mod[l], seq, weights, scratch)
        else:
            weights = [gmix, gffn, w_pool[i].astype(BF16), pool_scale[i].reshape(1, d)] + route_w
            gd = d // len(POOL_WINDOWS)
            scratch = [pltpu.VMEM((TM + max(POOL_WINDOWS), d - g * gd), F32)
                       for g in range(len(POOL_WINDOWS))]
            xf, xsl, route, cnt = _mixer_call(_odd_kernel, xf, mod[l], seq, weights, scratch)
        xf = _moe(xf, xsl, route, cnt, mod[l], seq, l, moe_w_gate, moe_w_up, moe_w_down,
                  g_final, l == depth - 1)
    return xf.reshape(bsz, seq, d)
```

```python
import functools

import jax
import jax.numpy as jnp
from jax import lax
from jax.experimental import pallas as pl
from jax.experimental.pallas import tpu as pltpu

F32 = jnp.float32
BF16 = jnp.bfloat16
U32 = jnp.uint32
I32 = jnp.int32

EPS = 1e-6
LANES = 128
SUBLANES = 8
CHUNK = 128
A_HEADS = 8
N_GROUPS = 4
EXPERTS_PER_GROUP = 8
N_EXPERTS = N_GROUPS * EXPERTS_PER_GROUP
POOL_WINDOWS = (2, 4, 8, 16)
CONV_WIDTH = 3
ROUTE_COL0 = N_GROUPS

TM = 512
SORT_TM = 256
SECTIONS = TM // SORT_TM
LOCAL_ROWS = 2 * SORT_TM + LANES
LOCAL_PAIRS = LOCAL_ROWS // 2
TILE_PAIRS = SECTIONS * LOCAL_PAIRS
TILE_RUNS = SECTIONS * N_EXPERTS
EXPERT_BLK = 512
BLK_PAIRS = EXPERT_BLK // 2
MOD_TN = 1536
VMEM_LIMIT = 56 * 1024 * 1024


def _rms(x):
    return x * lax.rsqrt(jnp.mean(x * x, axis=-1, keepdims=True) + EPS)


def _dot(a, b):
    return jnp.dot(a, b, preferred_element_type=F32)


def _store_pairs(ref, base, rows_bf16):
    words = pltpu.bitcast(rows_bf16, U32)
    n_pairs, d = words.shape
    slab = d // LANES
    for c in range(slab):
        ref[pl.ds(base + c, n_pairs, stride=slab), :] = words[:, c * LANES:(c + 1) * LANES]


def _load_pairs(ref, base, n_pairs, d):
    slab = d // LANES
    words = jnp.concatenate(
        [ref[pl.ds(base + c, n_pairs, stride=slab), :] for c in range(slab)], axis=-1)
    return pltpu.bitcast(words, BF16)


def _mod_kernel(c_ref, w_ref, b_ref, o_ref):
    c = c_ref[...]
    ca = c * jax.nn.sigmoid(c)
    o_ref[0] = jnp.dot(ca, w_ref[0], precision=lax.Precision.HIGHEST,
                       preferred_element_type=F32) + b_ref[0]


def _modulation(c, w_ada, b_ada):
    depth, d, n = w_ada.shape
    bsz = c.shape[0]
    return pl.pallas_call(
        _mod_kernel,
        grid=(depth, n // MOD_TN),
        in_specs=[
            pl.BlockSpec((bsz, d), lambda l, j: (0, 0)),
            pl.BlockSpec((1, d, MOD_TN), lambda l, j: (l, 0, j)),
            pl.BlockSpec((1, 1, MOD_TN), lambda l, j: (l, 0, j)),
        ],
        out_specs=pl.BlockSpec((1, bsz, MOD_TN), lambda l, j: (l, 0, j)),
        out_shape=jax.ShapeDtypeStruct((depth, bsz, n), F32),
        compiler_params=pltpu.CompilerParams(
            dimension_semantics=("arbitrary", "arbitrary"),
            vmem_limit_bytes=VMEM_LIMIT),
        name="modulation",
    )(c, w_ada, b_ada.reshape(depth, 1, n))


def _route_and_sort(x_new, mod_ref, gffn_ref, rw_ref, rb_ref, ltri_ref, utri_ref,
                    xsl_ref, route_ref, cnt_ref):
    tm = x_new.shape[0]
    sh_f = mod_ref[0, 3:4, :]
    sc_f = mod_ref[0, 4:5, :]
    h2 = _rms(x_new) * gffn_ref[...] * (1.0 + sc_f) + sh_f

    hh = h2.astype(BF16)
    hl = (h2 - hh.astype(F32)).astype(BF16)
    both = _dot(hh, rw_ref[...])
    logits = (both[:, 0:LANES] + both[:, LANES:2 * LANES] + _dot(hl, rw_ref[:, 0:LANES])
              + rb_ref[...])

    lane = lax.broadcasted_iota(I32, logits.shape, 1).astype(F32)
    neg = jnp.float32(-jnp.inf)
    big = jnp.float32(LANES)

    gl = jnp.where(lane < N_GROUPS, logits, neg)
    gmax = jnp.max(gl, axis=-1, keepdims=True)
    g_sel = jnp.min(jnp.where(gl == gmax, lane, big), axis=-1, keepdims=True)
    g_w = 1.0 / jnp.sum(jnp.exp(gl - gmax), axis=-1, keepdims=True)

    lo = ROUTE_COL0 + EXPERTS_PER_GROUP * g_sel
    el = jnp.where((lane >= lo) & (lane < lo + EXPERTS_PER_GROUP), logits, neg)
    m1 = jnp.max(el, axis=-1, keepdims=True)
    i1 = jnp.min(jnp.where(el == m1, lane, big), axis=-1, keepdims=True)
    el2 = jnp.where(lane == i1, neg, el)
    m2 = jnp.max(el2, axis=-1, keepdims=True)
    i2 = jnp.min(jnp.where(el2 == m2, lane, big), axis=-1, keepdims=True)
    t = jnp.exp(m2 - m1)
    gate1 = g_w / (1.0 + t)
    gate2 = g_w * t / (1.0 + t)

    sections = [slice(s * SORT_TM, (s + 1) * SORT_TM) for s in range(tm // SORT_TM)]
    is1 = lane == i1
    is2 = lane == i2
    onehot = jnp.where(is1 | is2, 1.0, 0.0)
    before = _dot(ltri_ref[...], onehot.astype(BF16))
    cnts = [jnp.sum(onehot[rows], axis=0, keepdims=True) for rows in sections]
    cnt_rows = jnp.concatenate(
        cnts + [jnp.zeros((SUBLANES - len(sections), LANES), F32)], axis=0)
    pairs = jnp.floor((cnt_rows + 1.0) * 0.5)
    pair_start = jnp.dot(pairs, utri_ref[...], precision=lax.Precision.HIGHEST,
                         preferred_element_type=F32)
    pos = jnp.concatenate([before[rows] + 2.0 * pair_start[s:s + 1]
                           for s, rows in enumerate(sections)], axis=0)
    pos1 = jnp.sum(jnp.where(is1, pos, 0.0), axis=-1, keepdims=True)
    pos2 = jnp.sum(jnp.where(is2, pos, 0.0), axis=-1, keepdims=True)
    for s, cnt in enumerate(cnts):
        cnt_ref[s] = jnp.broadcast_to(cnt, (SUBLANES, LANES))

    route = jnp.where(lane == 0, pos1, 0.0)
    route = jnp.where(lane == 1, pos2, route)
    route = jnp.where(lane == 4, gate1, route)
    route = jnp.where(lane == 5, gate2, route)
    route_ref[...] = route

    pos_t = route.T
    srow = lax.broadcasted_iota(I32, (LOCAL_ROWS, SORT_TM), 0).astype(F32)
    section_rows = LOCAL_PAIRS * (x_new.shape[1] // LANES)
    for s, cols in enumerate(sections):
        perm = jnp.where((srow == pos_t[0:1, cols]) | (srow == pos_t[1:2, cols]), 1.0, 0.0)
        _store_pairs(xsl_ref, s * section_rows, _dot(perm.astype(BF16), hh[cols]).astype(BF16))


def _rows(first_row, n_rows, unit):
    return pl.ds(pl.multiple_of(first_row, unit), n_rows)


def _unpermute(tables, ys_hbm, route_ref, ysl_scr, sem, shape):
    len_ref, loc_ref, glob_ref, tot_ref = tables
    i = pl.program_id(0)
    tm, d = shape
    slab = d // LANES

    def run_copy(tile, glob, loc, rows):
        slot = tile % 2
        return pltpu.make_async_copy(ys_hbm.at[_rows(glob, rows, slab)],
                                     ysl_scr.at[slot, _rows(loc, rows, slab)], sem.at[slot])

    def fetch(tile):
        def body(r, carry):
            k = tile * TILE_RUNS + r
            n = len_ref[k]

            @pl.when(n > 0)
            def _():
                run_copy(tile, glob_ref[k], loc_ref[k], n).start()
            return carry
        lax.fori_loop(0, TILE_RUNS, body, 0, unroll=4)

    @pl.when(i == 0)
    def _():
        ysl_scr[...] = jnp.zeros_like(ysl_scr)
        fetch(i)

    @pl.when(i + 1 < pl.num_programs(0))
    def _():
        fetch(i + 1)

    @pl.when(tot_ref[i] > 0)
    def _():
        run_copy(i, 0, 0, tot_ref[i]).wait()

    ysl_ref = ysl_scr.at[i % 2]
    srow = lax.broadcasted_iota(I32, (SORT_TM, LOCAL_ROWS), 1).astype(F32)
    y_sections = []
    for s in range(tm // SORT_TM):
        rows = slice(s * SORT_TM, (s + 1) * SORT_TM)
        ysl = _load_pairs(ysl_ref, s * LOCAL_PAIRS * slab, LOCAL_PAIRS, d)
        y = jnp.zeros((SORT_TM, d), F32)
        for k in range(2):
            sel = jnp.where(srow == route_ref[rows, k:k + 1], 1.0, 0.0).astype(BF16)
            y = y + route_ref[rows, 4 + k:5 + k] * _dot(sel, ysl)
        y_sections.append(y)
    return jnp.concatenate(y_sections, axis=0)


def _split_prev(fused, refs):
    if not fused:
        return None, refs
    tables, (ys_hbm, route_ref, mod_ref), (ysl_scr, sem) = refs[:4], refs[4:7], refs[-2:]
    return (tables, ys_hbm, route_ref, mod_ref, ysl_scr, sem), refs[7:-2]


def _mixer_input(prev, x_ref):
    if prev is None:
        return x_ref[...]
    tables, ys_hbm, route_ref, mod_ref, ysl_scr, sem = prev
    y = _unpermute(tables, ys_hbm, route_ref, ysl_scr, sem, x_ref.shape)
    return x_ref[...] + mod_ref[0, 5:6, :] * y


def _even_kernel(tiles_per_seq, fused, *refs):
    prev, refs = _split_prev(fused, refs)
    (x_ref, mod_ref, gmix_ref, gffn_ref, win_ref, gv_ref, ws_ref, bs_ref,
     cw_ref, wout_ref, rw_ref, rb_ref, ltri_ref, utri_ref,
     xo_ref, xsl_ref, route_ref, cnt_ref,
     zc_scr) = refs
    i = pl.program_id(0)
    tm = x_ref.shape[0]
    aw = gv_ref.shape[0]
    hd = aw // A_HEADS
    n_chunks = tm // CHUNK

    @pl.when(i == 0)
    def _():
        zc_scr[...] = jnp.zeros_like(zc_scr)

    x = _mixer_input(prev, x_ref)
    sh_m = mod_ref[0, 0:1, :]
    sc_m = mod_ref[0, 1:2, :]
    g_m = mod_ref[0, 2:3, :]
    h = _rms(x) * gmix_ref[...] * (1.0 + sc_m) + sh_m
    z = _dot(h.astype(BF16), win_ref[...])
    u = z[:, 0:aw]
    v = z[:, aw:2 * aw]
    b_gate = z[:, 2 * aw:3 * aw]
    c_gate = z[:, 3 * aw:4 * aw]
    x_in = z[:, 4 * aw:5 * aw]

    v_t = v.T
    row = lax.broadcasted_iota(I32, (CHUNK, CHUNK), 0)
    col = lax.broadcasted_iota(I32, (CHUNK, CHUNK), 1)
    causal = col <= row
    head_rows = []
    for hh in range(A_HEADS):
        vh = v_t[hh * hd:(hh + 1) * hd, :]
        msv = jnp.mean(vh * vh, axis=0, keepdims=True)
        vn = (vh * lax.rsqrt(msv + EPS) * gv_ref[hh * hd:(hh + 1) * hd, :]).astype(BF16)
        lhs = jnp.concatenate(
            [vn[:, c * CHUNK:(c + 1) * CHUNK] for c in range(n_chunks)], axis=0)
        w_m = jnp.where(causal, ws_ref[hh], 0.0).astype(BF16)
        sv_h = lax.dot_general(lhs, w_m, (((1,), (1,)), ((), ())),
                               preferred_element_type=F32)
        sv_h = sv_h + bs_ref[hh]
        head_rows.append(jnp.concatenate(
            [sv_h[c * hd:(c + 1) * hd, :] for c in range(n_chunks)], axis=1))
    sv = jnp.concatenate(head_rows, axis=0).T
    y_a = u * sv

    zc = c_gate * x_in
    first = (i % tiles_per_seq) == 0
    halo = zc_scr[tm:tm + SUBLANES, :]
    zc_scr[0:SUBLANES, :] = jnp.where(first, 0.0, halo)
    zc_scr[SUBLANES:SUBLANES + tm, :] = zc
    conv = cw_ref[2:3, :] * zc
    for k in range(CONV_WIDTH - 1):
        shift = CONV_WIDTH - 1 - k
        conv = conv + cw_ref[k:k + 1, :] * zc_scr[SUBLANES - shift:SUBLANES - shift + tm, :]
    y_b = b_gate * conv

    y = _dot(y_a.astype(BF16), wout_ref[0:aw, :]) + _dot(y_b.astype(BF16), wout_ref[aw:, :])
    x_new = x + g_m * y
    xo_ref[...] = x_new
    _route_and_sort(x_new, mod_ref, gffn_ref, rw_ref, rb_ref, ltri_ref, utri_ref,
                    xsl_ref, route_ref, cnt_ref)


def _odd_kernel(tiles_per_seq, fused, *refs):
    prev, refs = _split_prev(fused, refs)
    (x_ref, mod_ref, gmix_ref, gffn_ref, wpool_ref, pscale_ref,
     rw_ref, rb_ref, ltri_ref, utri_ref,
     xo_ref, xsl_ref, route_ref, cnt_ref) = refs[:14]
    level_scrs = refs[14:]
    i = pl.program_id(0)
    tm, d = x_ref.shape
    halo_rows = max(POOL_WINDOWS)
    gd = d // len(POOL_WINDOWS)

    @pl.when(i == 0)
    def _():
        for scr in level_scrs:
            scr[...] = jnp.zeros_like(scr)

    x = _mixer_input(prev, x_ref)
    sh_m = mod_ref[0, 0:1, :]
    sc_m = mod_ref[0, 1:2, :]
    g_m = mod_ref[0, 2:3, :]
    h = _rms(x) * gmix_ref[...] * (1.0 + sc_m) + sh_m

    tile_in_seq = i % tiles_per_seq
    first = tile_in_seq == 0

    sums = h
    window_sums = []
    for g, scr in enumerate(level_scrs):
        tail = scr[tm:tm + halo_rows, :]
        scr[0:halo_rows, :] = jnp.where(first, 0.0, tail)
        scr[halo_rows:halo_rows + tm, :] = sums
        lag = POOL_WINDOWS[g] // 2
        sums = sums + scr[halo_rows - lag:halo_rows - lag + tm, :]
        window_sums.append(sums[:, 0:gd])
        if g + 1 < len(level_scrs):
            sums = sums[:, gd:]

    pos = (tile_in_seq * tm + lax.broadcasted_iota(I32, (tm, 1), 0)).astype(F32)
    outs = []
    for g, win in enumerate(POOL_WINDOWS):
        cs = slice(g * gd, (g + 1) * gd)
        count = jnp.minimum(pos + 1.0, jnp.float32(win))
        pooled = window_sums[g] / count - h[:, cs]
        outs.append(_dot(pooled.astype(BF16), wpool_ref[g]))
    y = jnp.concatenate(outs, axis=-1) * pscale_ref[...]
    x_new = x + g_m * y
    xo_ref[...] = x_new
    _route_and_sort(x_new, mod_ref, gffn_ref, rw_ref, rb_ref, ltri_ref, utri_ref,
                    xsl_ref, route_ref, cnt_ref)


def _mixer_call(kernel_fn, x, mod_l, seq, weights, scratch, prev):
    t_tok, d = x.shape
    n_tiles = t_tok // TM
    tiles_per_seq = seq // TM
    n_slabs = d // LANES

    def const_spec(a):
        return pl.BlockSpec(a.shape, lambda i, *_, nd=a.ndim: (0,) * nd)

    def mod_spec(m):
        return pl.BlockSpec((1,) + m.shape[1:], lambda i, *_: (i // tiles_per_seq, 0, 0))

    tables, prev_inputs, prev_specs, prev_scratch = (), (), [], []
    if prev is not None:
        tables, ys, route_prev, mod_prev = prev
        prev_inputs = (ys, route_prev, mod_prev)
        prev_specs = [pl.BlockSpec(memory_space=pl.ANY),
                      pl.BlockSpec((TM, LANES), lambda i, *_: (i, 0)),
                      mod_spec(mod_prev)]
        prev_scratch = [pltpu.VMEM((2, TILE_PAIRS * n_slabs, LANES), U32),
                        pltpu.SemaphoreType.DMA((2,))]
    in_specs = prev_specs + [
        pl.BlockSpec((TM, d), lambda i, *_: (i, 0)),
        mod_spec(mod_l),
    ] + [const_spec(w) for w in weights]
    out_shape = (
        jax.ShapeDtypeStruct((t_tok, d), F32),
        jax.ShapeDtypeStruct((n_tiles * TILE_PAIRS * n_slabs, LANES), U32),
        jax.ShapeDtypeStruct((t_tok, LANES), F32),
        jax.ShapeDtypeStruct((n_tiles * SECTIONS, SUBLANES, LANES), F32),
    )
    out_specs = (
        pl.BlockSpec((TM, d), lambda i, *_: (i, 0)),
        pl.BlockSpec((TILE_PAIRS * n_slabs, LANES), lambda i, *_: (i, 0)),
        pl.BlockSpec((TM, LANES), lambda i, *_: (i, 0)),
        pl.BlockSpec((SECTIONS, SUBLANES, LANES), lambda i, *_: (i, 0, 0)),
    )
    return pl.pallas_call(
        functools.partial(kernel_fn, tiles_per_seq, prev is not None),
        grid_spec=pltpu.PrefetchScalarGridSpec(
            num_scalar_prefetch=len(tables),
            grid=(n_tiles,),
            in_specs=in_specs,
            out_specs=out_specs,
            scratch_shapes=scratch + prev_scratch,
        ),
        out_shape=out_shape,
        compiler_params=pltpu.CompilerParams(
            dimension_semantics=("arbitrary",), vmem_limit_bytes=VMEM_LIMIT),
        name=kernel_fn.__name__.strip("_"),
    )(*tables, *prev_inputs, x, mod_l, *weights)


def _expert_kernel(layer, blk_expert_ref, n_valid_ref, next_ref, slot_ref,
                   blk_first_ref, blk_last_ref, blk_fill_ref, run_len_ref, run_src_ref, run_dst_ref,
                   xsl_hbm, wg_hbm, wu_hbm, wd_hbm, ys_ref,
                   x_scr, zero_scr, wg_stage, wu_stage, wd_stage, wg_scr, wu_scr, wd_scr,
                   x_sem, sem):
    j = pl.program_id(0)
    n_valid = n_valid_ref[0]
    expert = blk_expert_ref[j]
    prev = blk_expert_ref[jnp.maximum(j - 1, 0)]
    d = wg_scr.shape[0]
    slab = d // LANES
    blk_rows = BLK_PAIRS * slab

    def fetch(blk):
        slot = blk % 2
        row0 = blk * blk_rows

        def copy_run(q):
            dst = run_dst_ref[q]
            lo = jnp.maximum(dst, row0)
            hi = jnp.minimum(dst + run_len_ref[q], row0 + blk_rows)

            @pl.when(hi > lo)
            def _():
                pltpu.make_async_copy(
                    xsl_hbm.at[_rows(run_src_ref[q] + (lo - dst), hi - lo, slab)],
                    x_scr.at[slot, _rows(lo - row0, hi - lo, slab)], x_sem.at[slot]).start()

        first = blk_first_ref[blk]

        def body(t, carry):
            copy_run(first + 2 * t)
            copy_run(first + 2 * t + 1)
            return carry

        lax.fori_loop(0, (blk_last_ref[blk] - first + 1) // 2, body, 0)
        fill = blk_fill_ref[blk]

        @pl.when(fill > 0)
        def _():
            pltpu.make_async_copy(zero_scr.at[_rows(0, fill, slab)],
                                  x_scr.at[slot, _rows(blk_rows - fill, fill, slab)],
                                  x_sem.at[slot]).start()

    @pl.when(j == 0)
    def _():
        zero_scr[...] = jnp.zeros_like(zero_scr)
        fetch(j)

    @pl.when(j + 1 < n_valid)
    def _():
        fetch(j + 1)

    def weight_copies(e, slot):
        return [pltpu.make_async_copy(hbm.at[layer, e], stage.at[slot], sem.at[slot])
                for hbm, stage in ((wg_hbm, wg_stage), (wu_hbm, wu_stage), (wd_hbm, wd_stage))]

    @pl.when(j == 0)
    def _():
        for cp in weight_copies(expert, slot_ref[expert]):
            cp.start()

    @pl.when((j == 0) | (expert != prev))
    def _():
        slot = slot_ref[expert]
        for cp in weight_copies(expert, slot):
            cp.wait()
        nxt = next_ref[expert]

        @pl.when(nxt != expert)
        def _():
            for cp in weight_copies(nxt, 1 - slot):
                cp.start()

        wg_scr[...] = wg_stage[slot].astype(BF16)
        wu_scr[...] = wu_stage[slot].astype(BF16)
        wd_scr[...] = wd_stage[slot].astype(BF16)

    @pl.when(j < n_valid)
    def _():
        slot = j % 2
        pltpu.make_async_copy(zero_scr, x_scr.at[slot], x_sem.at[slot]).wait()
        xb = _load_pairs(x_scr.at[slot], 0, BLK_PAIRS, d)
        a = _dot(xb, wg_scr[...])
        b = _dot(xb, wu_scr[...])
        hm = (a * jax.nn.sigmoid(a) * b).astype(BF16)
        _store_pairs(ys_ref, 0, _dot(hm, wd_scr[...]).astype(BF16))

    @pl.when(j >= n_valid)
    def _():
        ys_ref[...] = jnp.zeros_like(ys_ref)


def _experts(tables, xsl, n_blocks, layer, w_gate, w_up, w_down):
    _, _, d, de = w_gate.shape
    blk_rows = BLK_PAIRS * (d // LANES)
    return pl.pallas_call(
        functools.partial(_expert_kernel, layer),
        grid_spec=pltpu.PrefetchScalarGridSpec(
            num_scalar_prefetch=len(tables),
            grid=(n_blocks,),
            in_specs=[pl.BlockSpec(memory_space=pl.ANY)] * 4,
            out_specs=pl.BlockSpec((blk_rows, LANES), lambda j, *_: (j, 0)),
            scratch_shapes=[
                pltpu.VMEM((2, blk_rows, LANES), U32), pltpu.VMEM((blk_rows, LANES), U32),
                pltpu.VMEM((2, d, de), F32), pltpu.VMEM((2, d, de), F32),
                pltpu.VMEM((2, de, d), F32),
                pltpu.VMEM((d, de), BF16), pltpu.VMEM((d, de), BF16), pltpu.VMEM((de, d), BF16),
                pltpu.SemaphoreType.DMA((2,)), pltpu.SemaphoreType.DMA((2,)),
            ],
        ),
        out_shape=jax.ShapeDtypeStruct((n_blocks * blk_rows, LANES), U32),
        compiler_params=pltpu.CompilerParams(
            dimension_semantics=("arbitrary",), vmem_limit_bytes=VMEM_LIMIT),
        name="experts",
    )(*tables, xsl, w_gate, w_up, w_down)


def _final_kernel(len_ref, loc_ref, glob_ref, tot_ref, ys_hbm, x_ref, route_ref,
                  mod_ref, gfin_ref, xo_ref, ysl_scr, sem):
    prev = ((len_ref, loc_ref, glob_ref, tot_ref), ys_hbm, route_ref, mod_ref, ysl_scr, sem)
    xo_ref[...] = _rms(_mixer_input(prev, x_ref)) * gfin_ref[...]


def _final(tables, ys, x, route, mod_l, seq, g_final):
    t_tok, d = x.shape
    tiles_per_seq = seq // TM
    return pl.pallas_call(
        _final_kernel,
        grid_spec=pltpu.PrefetchScalarGridSpec(
            num_scalar_prefetch=4,
            grid=(t_tok // TM,),
            in_specs=[
                pl.BlockSpec(memory_space=pl.ANY),
                pl.BlockSpec((TM, d), lambda i, *_: (i, 0)),
                pl.BlockSpec((TM, LANES), lambda i, *_: (i, 0)),
                pl.BlockSpec((1,) + mod_l.shape[1:], lambda i, *_: (i // tiles_per_seq, 0, 0)),
                pl.BlockSpec((1, d), lambda i, *_: (0, 0)),
            ],
            out_specs=pl.BlockSpec((TM, d), lambda i, *_: (i, 0)),
            scratch_shapes=[pltpu.VMEM((2, TILE_PAIRS * (d // LANES), LANES), U32),
                            pltpu.SemaphoreType.DMA((2,))],
        ),
        out_shape=jax.ShapeDtypeStruct((t_tok, d), F32),
        compiler_params=pltpu.CompilerParams(
            dimension_semantics=("arbitrary",), vmem_limit_bytes=VMEM_LIMIT),
        name="final",
    )(*tables, ys, x, route, mod_l, g_final)


def _moe(x, xsl, cnt, layer, w_gate, w_up, w_down):
    t_tok = x.shape[0]
    n_tiles = t_tok // TM
    n_blocks = -(-(2 * t_tok + n_tiles * TILE_RUNS) // EXPERT_BLK) + N_EXPERTS

    counts = cnt[:, 0, ROUTE_COL0:ROUTE_COL0 + N_EXPERTS].astype(I32)
    run_len = (counts + 1) // 2
    section_base = (jnp.arange(counts.shape[0], dtype=I32) % SECTIONS) * LOCAL_PAIRS
    run_loc = jnp.cumsum(run_len, axis=1) - run_len + section_base[:, None]
    seg_len = jnp.sum(run_len, axis=0)
    seg_pad = (seg_len + BLK_PAIRS - 1) // BLK_PAIRS * BLK_PAIRS
    seg_end = jnp.cumsum(seg_pad)
    seg_start = seg_end - seg_pad
    run_glob = seg_start[None, :] + jnp.cumsum(run_len, axis=0) - run_len
    tile_tot = jnp.sum(run_len.reshape(n_tiles, TILE_RUNS), axis=1)
    slab = x.shape[1] // LANES
    run_tables = tuple((a.reshape(-1) * slab).astype(I32)
                       for a in (run_len, run_loc, run_glob, tile_tot))
    n_valid = (seg_end[-1:] // BLK_PAIRS).astype(I32)
    blk_pair0 = jnp.arange(n_blocks, dtype=I32) * BLK_PAIRS
    blk_expert = jnp.sum((seg_end[None, :] <= blk_pair0[:, None]).astype(I32), axis=1)
    section_pair0 = jnp.arange(counts.shape[0], dtype=I32) * LOCAL_PAIRS
    run_src = jnp.cumsum(run_len, axis=1) - run_len + section_pair0[:, None]
    em_len, em_src, em_dst = (
        jnp.concatenate([a.T.reshape(-1), jnp.full((1,), end, I32)])
        for a, end in ((run_len, 0), (run_src, 0), (run_glob, n_blocks * BLK_PAIRS)))
    blk_first = jnp.sum(((em_dst + em_len)[None, :] <= blk_pair0[:, None]).astype(I32), axis=1)
    blk_last = jnp.sum((em_dst[None, :] < (blk_pair0 + BLK_PAIRS)[:, None]).astype(I32), axis=1)
    seg_stop = jnp.sum(jnp.where(
        jnp.arange(N_EXPERTS, dtype=I32)[None, :] == blk_expert[:, None],
        (seg_start + seg_len)[None, :], 0), axis=1)
    blk_fill = BLK_PAIRS - jnp.clip(seg_stop - blk_pair0, 0, BLK_PAIRS)
    pull_tables = tuple((a * slab).astype(I32) for a in (blk_fill, em_len, em_src, em_dst))
    experts = jnp.arange(N_EXPERTS, dtype=I32)
    used = seg_len > 0
    last_used = jnp.max(jnp.where(used, experts, 0))
    blk_expert = jnp.minimum(blk_expert, last_used).astype(I32)
    later = jnp.where((experts[None, :] > experts[:, None]) & used[None, :],
                      experts[None, :], N_EXPERTS)
    next_used = jnp.min(later, axis=1)
    next_used = jnp.where(next_used == N_EXPERTS, experts, next_used).astype(I32)
    slot_of = ((jnp.cumsum(used.astype(I32)) - 1) % 2).astype(I32)

    expert_tables = ((blk_expert, n_valid, next_used, slot_of,
                      blk_first.astype(I32), blk_last.astype(I32)) + pull_tables)
    return run_tables, _experts(expert_tables, xsl, n_blocks, layer, w_gate, w_up, w_down)


def kernel(x, c, w_ada, b_ada, norm_mix_g, norm_ffn_g, w_in_even, sgu_norm_g, w_spatial, b_spatial, conv_w, w_out_even, w_pool, pool_scale, w_group_router, b_group_router, w_expert_router, b_expert_router, moe_w_gate, moe_w_up, moe_w_down, final_norm_g):
    bsz, seq, d = x.shape
    depth = w_ada.shape[0]
    t_tok = bsz * seq
    assert seq % TM == 0 and d % LANES == 0 and w_spatial.shape[-1] == CHUNK
    assert all(w == 2 ** (g + 1) for g, w in enumerate(POOL_WINDOWS))

    mod = _modulation(c, w_ada, b_ada).reshape(depth, bsz, 6, d)
    tok = jnp.arange(TM)
    ltri = ((tok[:, None] > tok[None, :])
            & (tok[:, None] // SORT_TM == tok[None, :] // SORT_TM)).astype(BF16)
    utri = jnp.triu(jnp.ones((LANES, LANES), F32), 1)
    g_final = final_norm_g.reshape(1, d)

    xf = x.reshape(t_tok, d)
    prev = None
    for l in range(depth):
        i = l // 2
        rw = jnp.concatenate([w_group_router[l], w_expert_router[l]], axis=1)
        rw = jnp.pad(rw, ((0, 0), (0, LANES - rw.shape[1])))
        rw_hi = rw.astype(BF16)
        rw_lo = (rw - rw_hi.astype(F32)).astype(BF16)
        rb = jnp.concatenate([b_group_router[l], b_expert_router[l]])
        rb = jnp.pad(rb, (0, LANES - rb.shape[0])).reshape(1, LANES)
        route_w = [jnp.concatenate([rw_hi, rw_lo], axis=1), rb, ltri, utri]
        gmix = norm_mix_g[l].reshape(1, d)
        gffn = norm_ffn_g[l].reshape(1, d)
        if l % 2 == 0:
            aw = sgu_norm_g.shape[1]
            weights = [gmix, gffn, w_in_even[i].astype(BF16), sgu_norm_g[i].reshape(aw, 1),
                       w_spatial[i], b_spatial[i].reshape(A_HEADS, 1, CHUNK), conv_w[i],
                       w_out_even[i].astype(BF16)] + route_w
            scratch = [pltpu.VMEM((TM + SUBLANES, conv_w.shape[-1]), F32)]
            xf, xsl, route, cnt = _mixer_call(_even_kernel, xf, mod[l], seq, weights, scratch, prev)
        else:
            weights = [gmix, gffn, w_pool[i].astype(BF16), pool_scale[i].reshape(1, d)] + route_w
            gd = d // len(POOL_WINDOWS)
            scratch = [pltpu.VMEM((TM + max(POOL_WINDOWS), d - g * gd), F32)
                       for g in range(len(POOL_WINDOWS))]
            xf, xsl, route, cnt = _mixer_call(_odd_kernel, xf, mod[l], seq, weights, scratch, prev)
        run_tables, ys = _moe(xf, xsl, cnt, l, moe_w_gate, moe_w_up, moe_w_down)
        prev = (run_tables, ys, route, mod[l])
    run_tables, ys, route, mod_l = prev
    return _final(run_tables, ys, xf, route, mod_l, seq, g_final).reshape(bsz, seq, d)
```

```python
import functools

import jax
import jax.numpy as jnp
from jax import lax
from jax.experimental import pallas as pl
from jax.experimental.pallas import tpu as pltpu

F32 = jnp.float32
BF16 = jnp.bfloat16
U32 = jnp.uint32
I32 = jnp.int32

EPS = 1e-6
LANES = 128
SUBLANES = 8
CHUNK = 128
A_HEADS = 8
N_GROUPS = 4
EXPERTS_PER_GROUP = 8
N_EXPERTS = N_GROUPS * EXPERTS_PER_GROUP
POOL_WINDOWS = (2, 4, 8, 16)
CONV_WIDTH = 3
ROUTE_COL0 = N_GROUPS
ROUTE_ROWS = 48

TM = 512
SORT_TM = 256
SECTIONS = TM // SORT_TM
LOCAL_ROWS = 2 * SORT_TM + LANES
LOCAL_PAIRS = LOCAL_ROWS // 2
TILE_PAIRS = SECTIONS * LOCAL_PAIRS
TILE_RUNS = SECTIONS * N_EXPERTS
EXPERT_BLK = 512
BLK_PAIRS = EXPERT_BLK // 2
MOD_TN = 1536
VMEM_LIMIT = 56 * 1024 * 1024


def _rms(x):
    return x * lax.rsqrt(jnp.mean(x * x, axis=-1, keepdims=True) + EPS)


def _dot(a, b):
    return jnp.dot(a, b, preferred_element_type=F32)


def _store_pairs(ref, base, rows_bf16):
    words = pltpu.bitcast(rows_bf16, U32)
    n_pairs, d = words.shape
    slab = d // LANES
    for c in range(slab):
        ref[pl.ds(base + c, n_pairs, stride=slab), :] = words[:, c * LANES:(c + 1) * LANES]


def _load_pairs(ref, base, n_pairs, d):
    slab = d // LANES
    words = jnp.concatenate(
        [ref[pl.ds(base + c, n_pairs, stride=slab), :] for c in range(slab)], axis=-1)
    return pltpu.bitcast(words, BF16)


def _mod_kernel(c_ref, w_ref, b_ref, o_ref):
    c = c_ref[...]
    ca = c * jax.nn.sigmoid(c)
    o_ref[0] = jnp.dot(ca, w_ref[0], precision=lax.Precision.HIGHEST,
                       preferred_element_type=F32) + b_ref[0]


def _modulation(c, w_ada, b_ada):
    depth, d, n = w_ada.shape
    bsz = c.shape[0]
    return pl.pallas_call(
        _mod_kernel,
        grid=(depth, n // MOD_TN),
        in_specs=[
            pl.BlockSpec((bsz, d), lambda l, j: (0, 0)),
            pl.BlockSpec((1, d, MOD_TN), lambda l, j: (l, 0, j)),
            pl.BlockSpec((1, 1, MOD_TN), lambda l, j: (l, 0, j)),
        ],
        out_specs=pl.BlockSpec((1, bsz, MOD_TN), lambda l, j: (l, 0, j)),
        out_shape=jax.ShapeDtypeStruct((depth, bsz, n), F32),
        compiler_params=pltpu.CompilerParams(
            dimension_semantics=("arbitrary", "arbitrary"),
            vmem_limit_bytes=VMEM_LIMIT),
        name="modulation",
    )(c, w_ada, b_ada.reshape(depth, 1, n))


def _route_and_sort(x_new, mod_ref, gffn_ref, rwt_ref, rbc_ref, earlier_ref, lower_ref,
                    xsl_ref, route_ref, cnt_ref):
    tm = x_new.shape[0]
    sh_f = mod_ref[0, 3:4, :]
    sc_f = mod_ref[0, 4:5, :]
    h2 = _rms(x_new) * gffn_ref[...] * (1.0 + sc_f) + sh_f

    rr = ROUTE_ROWS
    nt = (((1,), (1,)), ((), ()))
    hh = h2.astype(BF16)
    hl = (h2 - hh.astype(F32)).astype(BF16)
    both = lax.dot_general(rwt_ref[...], hh, nt, preferred_element_type=F32)
    logits = (both[0:rr] + both[rr:2 * rr]
              + lax.dot_general(rwt_ref[0:rr, :], hl, nt, preferred_element_type=F32)
              + rbc_ref[...])

    row = lax.broadcasted_iota(I32, logits.shape, 0).astype(F32)
    neg = jnp.float32(-jnp.inf)
    big = jnp.float32(LANES)

    gl = jnp.where(row < N_GROUPS, logits, neg)
    gmax = jnp.max(gl, axis=0, keepdims=True)
    g_sel = jnp.min(jnp.where(gl == gmax, row, big), axis=0, keepdims=True)
    g_w = 1.0 / jnp.sum(jnp.exp(gl - gmax), axis=0, keepdims=True)

    lo = ROUTE_COL0 + EXPERTS_PER_GROUP * g_sel
    el = jnp.where((row >= lo) & (row < lo + EXPERTS_PER_GROUP), logits, neg)
    m1 = jnp.max(el, axis=0, keepdims=True)
    i1 = jnp.min(jnp.where(el == m1, row, big), axis=0, keepdims=True)
    el2 = jnp.where(row == i1, neg, el)
    m2 = jnp.max(el2, axis=0, keepdims=True)
    i2 = jnp.min(jnp.where(el2 == m2, row, big), axis=0, keepdims=True)
    t = jnp.exp(m2 - m1)
    gate1 = g_w / (1.0 + t)
    gate2 = g_w * t / (1.0 + t)

    sections = [slice(s * SORT_TM, (s + 1) * SORT_TM) for s in range(tm // SORT_TM)]
    is1 = row == i1
    is2 = row == i2
    onehot = jnp.where(is1 | is2, 1.0, 0.0)
    before = _dot(onehot.astype(BF16), earlier_ref[...])
    lane = lax.broadcasted_iota(I32, (rr, LANES), 1)
    cnts = [jnp.sum(onehot[:, cols], axis=1, keepdims=True) for cols in sections]
    pairs = jnp.zeros((rr, LANES), F32)
    for s, cnt in enumerate(cnts):
        cnt_ref[s] = jnp.broadcast_to(cnt, (rr, LANES))
        pairs = jnp.where(lane == s, jnp.floor((cnt + 1.0) * 0.5), pairs)
    pairs = jnp.concatenate([pairs, jnp.zeros((LANES - rr, LANES), F32)], axis=0)
    pair_start = jnp.dot(lower_ref[...], pairs, precision=lax.Precision.HIGHEST,
                         preferred_element_type=F32)
    pos = jnp.concatenate([before[:, cols] + 2.0 * pair_start[0:rr, s:s + 1]
                           for s, cols in enumerate(sections)], axis=1)
    pos1 = jnp.sum(jnp.where(is1, pos, 0.0), axis=0, keepdims=True)
    pos2 = jnp.sum(jnp.where(is2, pos, 0.0), axis=0, keepdims=True)

    r8 = lax.broadcasted_iota(I32, (SUBLANES, tm), 0)
    route_t = jnp.where(r8 == 0, pos1, 0.0)
    route_t = jnp.where(r8 == 1, pos2, route_t)
    route_t = jnp.where(r8 == 4, gate1, route_t)
    route_t = jnp.where(r8 == 5, gate2, route_t)
    route_t = jnp.concatenate([route_t, jnp.zeros((LANES - SUBLANES, tm), F32)], axis=0)
    route_ref[...] = route_t.T

    srow = lax.broadcasted_iota(I32, (LOCAL_ROWS, SORT_TM), 0).astype(F32)
    section_rows = LOCAL_PAIRS * (x_new.shape[1] // LANES)
    for s, cols in enumerate(sections):
        perm = jnp.where((srow == pos1[:, cols]) | (srow == pos2[:, cols]), 1.0, 0.0)
        _store_pairs(xsl_ref, s * section_rows, _dot(perm.astype(BF16), hh[cols]).astype(BF16))


def _rows(first_row, n_rows, unit):
    return pl.ds(pl.multiple_of(first_row, unit), n_rows)


def _unpermute(tables, ys_hbm, route_ref, ysl_scr, sem, shape):
    len_ref, loc_ref, glob_ref, tot_ref = tables
    i = pl.program_id(0)
    tm, d = shape
    slab = d // LANES

    def run_copy(tile, glob, loc, rows):
        slot = tile % 2
        return pltpu.make_async_copy(ys_hbm.at[_rows(glob, rows, slab)],
                                     ysl_scr.at[slot, _rows(loc, rows, slab)], sem.at[slot])

    def fetch(tile):
        def body(r, carry):
            k = tile * TILE_RUNS + r
            n = len_ref[k]

            @pl.when(n > 0)
            def _():
                run_copy(tile, glob_ref[k], loc_ref[k], n).start()
            return carry
        lax.fori_loop(0, TILE_RUNS, body, 0, unroll=4)

    @pl.when(i == 0)
    def _():
        ysl_scr[...] = jnp.zeros_like(ysl_scr)
        fetch(i)

    @pl.when(i + 1 < pl.num_programs(0))
    def _():
        fetch(i + 1)

    @pl.when(tot_ref[i] > 0)
    def _():
        run_copy(i, 0, 0, tot_ref[i]).wait()

    ysl_ref = ysl_scr.at[i % 2]
    srow = lax.broadcasted_iota(I32, (SORT_TM, LOCAL_ROWS), 1).astype(F32)
    y_sections = []
    for s in range(tm // SORT_TM):
        rows = slice(s * SORT_TM, (s + 1) * SORT_TM)
        ysl = _load_pairs(ysl_ref, s * LOCAL_PAIRS * slab, LOCAL_PAIRS, d)
        y = jnp.zeros((SORT_TM, d), F32)
        for k in range(2):
            sel = jnp.where(srow == route_ref[rows, k:k + 1], 1.0, 0.0).astype(BF16)
            y = y + route_ref[rows, 4 + k:5 + k] * _dot(sel, ysl)
        y_sections.append(y)
    return jnp.concatenate(y_sections, axis=0)


def _split_prev(fused, refs):
    if not fused:
        return None, refs
    tables, (ys_hbm, route_ref, mod_ref), (ysl_scr, sem) = refs[:4], refs[4:7], refs[-2:]
    return (tables, ys_hbm, route_ref, mod_ref, ysl_scr, sem), refs[7:-2]


def _mixer_input(prev, x_ref):
    if prev is None:
        return x_ref[...]
    tables, ys_hbm, route_ref, mod_ref, ysl_scr, sem = prev
    y = _unpermute(tables, ys_hbm, route_ref, ysl_scr, sem, x_ref.shape)
    return x_ref[...] + mod_ref[0, 5:6, :] * y


def _even_kernel(tiles_per_seq, fused, *refs):
    prev, refs = _split_prev(fused, refs)
    (x_ref, mod_ref, gmix_ref, gffn_ref, win_ref, gv_ref, ws_ref, bs_ref,
     cw_ref, wout_ref, rwt_ref, rbc_ref, earlier_ref, lower_ref,
     xo_ref, xsl_ref, route_ref, cnt_ref,
     zc_scr) = refs
    i = pl.program_id(0)
    tm = x_ref.shape[0]
    aw = gv_ref.shape[0]
    hd = aw // A_HEADS
    n_chunks = tm // CHUNK

    @pl.when(i == 0)
    def _():
        zc_scr[...] = jnp.zeros_like(zc_scr)

    x = _mixer_input(prev, x_ref)
    sh_m = mod_ref[0, 0:1, :]
    sc_m = mod_ref[0, 1:2, :]
    g_m = mod_ref[0, 2:3, :]
    h = _rms(x) * gmix_ref[...] * (1.0 + sc_m) + sh_m
    z = _dot(h.astype(BF16), win_ref[...])
    u = z[:, 0:aw]
    v = z[:, aw:2 * aw]
    b_gate = z[:, 2 * aw:3 * aw]
    c_gate = z[:, 3 * aw:4 * aw]
    x_in = z[:, 4 * aw:5 * aw]

    v_t = v.T
    row = lax.broadcasted_iota(I32, (CHUNK, CHUNK), 0)
    col = lax.broadcasted_iota(I32, (CHUNK, CHUNK), 1)
    causal = col <= row
    head_rows = []
    for hh in range(A_HEADS):
        vh = v_t[hh * hd:(hh + 1) * hd, :]
        msv = jnp.mean(vh * vh, axis=0, keepdims=True)
        vn = (vh * lax.rsqrt(msv + EPS) * gv_ref[hh * hd:(hh + 1) * hd, :]).astype(BF16)
        lhs = jnp.concatenate(
            [vn[:, c * CHUNK:(c + 1) * CHUNK] for c in range(n_chunks)], axis=0)
        w_m = jnp.where(causal, ws_ref[hh], 0.0).astype(BF16)
        sv_h = lax.dot_general(lhs, w_m, (((1,), (1,)), ((), ())),
                               preferred_element_type=F32)
        sv_h = sv_h + bs_ref[hh]
        head_rows.append(jnp.concatenate(
            [sv_h[c * hd:(c + 1) * hd, :] for c in range(n_chunks)], axis=1))
    sv = jnp.concatenate(head_rows, axis=0).T
    y_a = u * sv

    zc = c_gate * x_in
    first = (i % tiles_per_seq) == 0
    halo = zc_scr[tm:tm + SUBLANES, :]
    zc_scr[0:SUBLANES, :] = jnp.where(first, 0.0, halo)
    zc_scr[SUBLANES:SUBLANES + tm, :] = zc
    conv = cw_ref[2:3, :] * zc
    for k in range(CONV_WIDTH - 1):
        shift = CONV_WIDTH - 1 - k
        conv = conv + cw_ref[k:k + 1, :] * zc_scr[SUBLANES - shift:SUBLANES - shift + tm, :]
    y_b = b_gate * conv

    y = _dot(y_a.astype(BF16), wout_ref[0:aw, :]) + _dot(y_b.astype(BF16), wout_ref[aw:, :])
    x_new = x + g_m * y
    xo_ref[...] = x_new
    _route_and_sort(x_new, mod_ref, gffn_ref, rwt_ref, rbc_ref, earlier_ref, lower_ref,
                    xsl_ref, route_ref, cnt_ref)


def _odd_kernel(tiles_per_seq, fused, *refs):
    prev, refs = _split_prev(fused, refs)
    (x_ref, mod_ref, gmix_ref, gffn_ref, wpool_ref, pscale_ref,
     rwt_ref, rbc_ref, earlier_ref, lower_ref,
     xo_ref, xsl_ref, route_ref, cnt_ref) = refs[:14]
    level_scrs = refs[14:]
    i = pl.program_id(0)
    tm, d = x_ref.shape
    halo_rows = max(POOL_WINDOWS)
    gd = d // len(POOL_WINDOWS)

    @pl.when(i == 0)
    def _():
        for scr in level_scrs:
            scr[...] = jnp.zeros_like(scr)

    x = _mixer_input(prev, x_ref)
    sh_m = mod_ref[0, 0:1, :]
    sc_m = mod_ref[0, 1:2, :]
    g_m = mod_ref[0, 2:3, :]
    h = _rms(x) * gmix_ref[...] * (1.0 + sc_m) + sh_m

    tile_in_seq = i % tiles_per_seq
    first = tile_in_seq == 0

    sums = h
    window_sums = []
    for g, scr in enumerate(level_scrs):
        tail = scr[tm:tm + halo_rows, :]
        scr[0:halo_rows, :] = jnp.where(first, 0.0, tail)
        scr[halo_rows:halo_rows + tm, :] = sums
        lag = POOL_WINDOWS[g] // 2
        sums = sums + scr[halo_rows - lag:halo_rows - lag + tm, :]
        window_sums.append(sums[:, 0:gd])
        if g + 1 < len(level_scrs):
            sums = sums[:, gd:]

    pos = (tile_in_seq * tm + lax.broadcasted_iota(I32, (tm, 1), 0)).astype(F32)
    outs = []
    for g, win in enumerate(POOL_WINDOWS):
        cs = slice(g * gd, (g + 1) * gd)
        count = jnp.minimum(pos + 1.0, jnp.float32(win))
        pooled = window_sums[g] / count - h[:, cs]
        outs.append(_dot(pooled.astype(BF16), wpool_ref[g]))
    y = jnp.concatenate(outs, axis=-1) * pscale_ref[...]
    x_new = x + g_m * y
    xo_ref[...] = x_new
    _route_and_sort(x_new, mod_ref, gffn_ref, rwt_ref, rbc_ref, earlier_ref, lower_ref,
                    xsl_ref, route_ref, cnt_ref)


def _mixer_call(kernel_fn, x, mod_l, seq, weights, scratch, prev):
    t_tok, d = x.shape
    n_tiles = t_tok // TM
    tiles_per_seq = seq // TM
    n_slabs = d // LANES

    def const_spec(a):
        return pl.BlockSpec(a.shape, lambda i, *_, nd=a.ndim: (0,) * nd)

    def mod_spec(m):
        return pl.BlockSpec((1,) + m.shape[1:], lambda i, *_: (i // tiles_per_seq, 0, 0))

    tables, prev_inputs, prev_specs, prev_scratch = (), (), [], []
    if prev is not None:
        tables, ys, route_prev, mod_prev = prev
        prev_inputs = (ys, route_prev, mod_prev)
        prev_specs = [pl.BlockSpec(memory_space=pl.ANY),
                      pl.BlockSpec((TM, LANES), lambda i, *_: (i, 0)),
                      mod_spec(mod_prev)]
        prev_scratch = [pltpu.VMEM((2, TILE_PAIRS * n_slabs, LANES), U32),
                        pltpu.SemaphoreType.DMA((2,))]
    in_specs = prev_specs + [
        pl.BlockSpec((TM, d), lambda i, *_: (i, 0)),
        mod_spec(mod_l),
    ] + [const_spec(w) for w in weights]
    out_shape = (
        jax.ShapeDtypeStruct((t_tok, d), F32),
        jax.ShapeDtypeStruct((n_tiles * TILE_PAIRS * n_slabs, LANES), U32),
        jax.ShapeDtypeStruct((t_tok, LANES), F32),
        jax.ShapeDtypeStruct((n_tiles * SECTIONS, ROUTE_ROWS, LANES), F32),
    )
    out_specs = (
        pl.BlockSpec((TM, d), lambda i, *_: (i, 0)),
        pl.BlockSpec((TILE_PAIRS * n_slabs, LANES), lambda i, *_: (i, 0)),
        pl.BlockSpec((TM, LANES), lambda i, *_: (i, 0)),
        pl.BlockSpec((SECTIONS, ROUTE_ROWS, LANES), lambda i, *_: (i, 0, 0)),
    )
    return pl.pallas_call(
        functools.partial(kernel_fn, tiles_per_seq, prev is not None),
        grid_spec=pltpu.PrefetchScalarGridSpec(
            num_scalar_prefetch=len(tables),
            grid=(n_tiles,),
            in_specs=in_specs,
            out_specs=out_specs,
            scratch_shapes=scratch + prev_scratch,
        ),
        out_shape=out_shape,
        compiler_params=pltpu.CompilerParams(
            dimension_semantics=("arbitrary",), vmem_limit_bytes=VMEM_LIMIT),
        name=kernel_fn.__name__.strip("_"),
    )(*tables, *prev_inputs, x, mod_l, *weights)


def _expert_kernel(layer, blk_expert_ref, n_valid_ref, next_ref, slot_ref,
                   blk_first_ref, blk_last_ref, blk_fill_ref, run_len_ref, run_src_ref, run_dst_ref,
                   xsl_hbm, wg_hbm, wu_hbm, wd_hbm, ys_ref,
                   x_scr, zero_scr, wg_stage, wu_stage, wd_stage, wg_scr, wu_scr, wd_scr,
                   x_sem, sem):
    j = pl.program_id(0)
    n_valid = n_valid_ref[0]
    expert = blk_expert_ref[j]
    prev = blk_expert_ref[jnp.maximum(j - 1, 0)]
    d = wg_scr.shape[0]
    slab = d // LANES
    blk_rows = BLK_PAIRS * slab

    def fetch(blk):
        slot = blk % 2
        row0 = blk * blk_rows

        def copy_run(q):
            dst = run_dst_ref[q]
            lo = jnp.maximum(dst, row0)
            hi = jnp.minimum(dst + run_len_ref[q], row0 + blk_rows)

            @pl.when(hi > lo)
            def _():
                pltpu.make_async_copy(
                    xsl_hbm.at[_rows(run_src_ref[q] + (lo - dst), hi - lo, slab)],
                    x_scr.at[slot, _rows(lo - row0, hi - lo, slab)], x_sem.at[slot]).start()

        first = blk_first_ref[blk]

        def body(t, carry):
            copy_run(first + 2 * t)
            copy_run(first + 2 * t + 1)
            return carry

        lax.fori_loop(0, (blk_last_ref[blk] - first + 1) // 2, body, 0)
        fill = blk_fill_ref[blk]

        @pl.when(fill > 0)
        def _():
            pltpu.make_async_copy(zero_scr.at[_rows(0, fill, slab)],
                                  x_scr.at[slot, _rows(blk_rows - fill, fill, slab)],
                                  x_sem.at[slot]).start()

    @pl.when(j == 0)
    def _():
        zero_scr[...] = jnp.zeros_like(zero_scr)
        fetch(j)

    @pl.when(j + 1 < n_valid)
    def _():
        fetch(j + 1)

    def weight_copies(e, slot):
        return [pltpu.make_async_copy(hbm.at[layer, e], stage.at[slot], sem.at[slot])
                for hbm, stage in ((wg_hbm, wg_stage), (wu_hbm, wu_stage), (wd_hbm, wd_stage))]

    @pl.when(j == 0)
    def _():
        for cp in weight_copies(expert, slot_ref[expert]):
            cp.start()

    @pl.when((j == 0) | (expert != prev))
    def _():
        slot = slot_ref[expert]
        for cp in weight_copies(expert, slot):
            cp.wait()
        nxt = next_ref[expert]

        @pl.when(nxt != expert)
        def _():
            for cp in weight_copies(nxt, 1 - slot):
                cp.start()

        wg_scr[...] = wg_stage[slot].astype(BF16)
        wu_scr[...] = wu_stage[slot].astype(BF16)
        wd_scr[...] = wd_stage[slot].astype(BF16)

    @pl.when(j < n_valid)
    def _():
        slot = j % 2
        pltpu.make_async_copy(zero_scr, x_scr.at[slot], x_sem.at[slot]).wait()
        xb = _load_pairs(x_scr.at[slot], 0, BLK_PAIRS, d)
        a = _dot(xb, wg_scr[...])
        b = _dot(xb, wu_scr[...])
        hm = (a * jax.nn.sigmoid(a) * b).astype(BF16)
        _store_pairs(ys_ref, 0, _dot(hm, wd_scr[...]).astype(BF16))

    @pl.when(j >= n_valid)
    def _():
        ys_ref[...] = jnp.zeros_like(ys_ref)


def _experts(tables, xsl, n_blocks, layer, w_gate, w_up, w_down):
    _, _, d, de = w_gate.shape
    blk_rows = BLK_PAIRS * (d // LANES)
    return pl.pallas_call(
        functools.partial(_expert_kernel, layer),
        grid_spec=pltpu.PrefetchScalarGridSpec(
            num_scalar_prefetch=len(tables),
            grid=(n_blocks,),
            in_specs=[pl.BlockSpec(memory_space=pl.ANY)] * 4,
            out_specs=pl.BlockSpec((blk_rows, LANES), lambda j, *_: (j, 0)),
            scratch_shapes=[
                pltpu.VMEM((2, blk_rows, LANES), U32), pltpu.VMEM((blk_rows, LANES), U32),
                pltpu.VMEM((2, d, de), F32), pltpu.VMEM((2, d, de), F32),
                pltpu.VMEM((2, de, d), F32),
                pltpu.VMEM((d, de), BF16), pltpu.VMEM((d, de), BF16), pltpu.VMEM((de, d), BF16),
                pltpu.SemaphoreType.DMA((2,)), pltpu.SemaphoreType.DMA((2,)),
            ],
        ),
        out_shape=jax.ShapeDtypeStruct((n_blocks * blk_rows, LANES), U32),
        compiler_params=pltpu.CompilerParams(
            dimension_semantics=("arbitrary",), vmem_limit_bytes=VMEM_LIMIT),
        name="experts",
    )(*tables, xsl, w_gate, w_up, w_down)


def _final_kernel(len_ref, loc_ref, glob_ref, tot_ref, ys_hbm, x_ref, route_ref,
                  mod_ref, gfin_ref, xo_ref, ysl_scr, sem):
    prev = ((len_ref, loc_ref, glob_ref, tot_ref), ys_hbm, route_ref, mod_ref, ysl_scr, sem)
    xo_ref[...] = _rms(_mixer_input(prev, x_ref)) * gfin_ref[...]


def _final(tables, ys, x, route, mod_l, seq, g_final):
    t_tok, d = x.shape
    tiles_per_seq = seq // TM
    return pl.pallas_call(
        _final_kernel,
        grid_spec=pltpu.PrefetchScalarGridSpec(
            num_scalar_prefetch=4,
            grid=(t_tok // TM,),
            in_specs=[
                pl.BlockSpec(memory_space=pl.ANY),
                pl.BlockSpec((TM, d), lambda i, *_: (i, 0)),
                pl.BlockSpec((TM, LANES), lambda i, *_: (i, 0)),
                pl.BlockSpec((1,) + mod_l.shape[1:], lambda i, *_: (i // tiles_per_seq, 0, 0)),
                pl.BlockSpec((1, d), lambda i, *_: (0, 0)),
            ],
            out_specs=pl.BlockSpec((TM, d), lambda i, *_: (i, 0)),
            scratch_shapes=[pltpu.VMEM((2, TILE_PAIRS * (d // LANES), LANES), U32),
                            pltpu.SemaphoreType.DMA((2,))],
        ),
        out_shape=jax.ShapeDtypeStruct((t_tok, d), F32),
        compiler_params=pltpu.CompilerParams(
            dimension_semantics=("arbitrary",), vmem_limit_bytes=VMEM_LIMIT),
        name="final",
    )(*tables, ys, x, route, mod_l, g_final)


def _moe(x, xsl, cnt, layer, w_gate, w_up, w_down):
    t_tok = x.shape[0]
    n_tiles = t_tok // TM
    n_blocks = -(-(2 * t_tok + n_tiles * TILE_RUNS) // EXPERT_BLK) + N_EXPERTS

    counts = cnt[:, ROUTE_COL0:ROUTE_COL0 + N_EXPERTS, 0].astype(I32)
    run_len = (counts + 1) // 2
    section_base = (jnp.arange(counts.shape[0], dtype=I32) % SECTIONS) * LOCAL_PAIRS
    run_loc = jnp.cumsum(run_len, axis=1) - run_len + section_base[:, None]
    seg_len = jnp.sum(run_len, axis=0)
    seg_pad = (seg_len + BLK_PAIRS - 1) // BLK_PAIRS * BLK_PAIRS
    seg_end = jnp.cumsum(seg_pad)
    seg_start = seg_end - seg_pad
    run_glob = seg_start[None, :] + jnp.cumsum(run_len, axis=0) - run_len
    tile_tot = jnp.sum(run_len.reshape(n_tiles, TILE_RUNS), axis=1)
    slab = x.shape[1] // LANES
    run_tables = tuple((a.reshape(-1) * slab).astype(I32)
                       for a in (run_len, run_loc, run_glob, tile_tot))
    n_valid = (seg_end[-1:] // BLK_PAIRS).astype(I32)
    blk_pair0 = jnp.arange(n_blocks, dtype=I32) * BLK_PAIRS
    blk_expert = jnp.sum((seg_end[None, :] <= blk_pair0[:, None]).astype(I32), axis=1)
    section_pair0 = jnp.arange(counts.shape[0], dtype=I32) * LOCAL_PAIRS
    run_src = jnp.cumsum(run_len, axis=1) - run_len + section_pair0[:, None]
    em_len, em_src, em_dst = (
        jnp.concatenate([a.T.reshape(-1), jnp.full((1,), end, I32)])
        for a, end in ((run_len, 0), (run_src, 0), (run_glob, n_blocks * BLK_PAIRS)))
    blk_first = jnp.sum(((em_dst + em_len)[None, :] <= blk_pair0[:, None]).astype(I32), axis=1)
    blk_last = jnp.sum((em_dst[None, :] < (blk_pair0 + BLK_PAIRS)[:, None]).astype(I32), axis=1)
    seg_stop = jnp.sum(jnp.where(
        jnp.arange(N_EXPERTS, dtype=I32)[None, :] == blk_expert[:, None],
        (seg_start + seg_len)[None, :], 0), axis=1)
    blk_fill = BLK_PAIRS - jnp.clip(seg_stop - blk_pair0, 0, BLK_PAIRS)
    pull_tables = tuple((a * slab).astype(I32) for a in (blk_fill, em_len, em_src, em_dst))
    experts = jnp.arange(N_EXPERTS, dtype=I32)
    used = seg_len > 0
    last_used = jnp.max(jnp.where(used, experts, 0))
    blk_expert = jnp.minimum(blk_expert, last_used).astype(I32)
    later = jnp.where((experts[None, :] > experts[:, None]) & used[None, :],
                      experts[None, :], N_EXPERTS)
    next_used = jnp.min(later, axis=1)
    next_used = jnp.where(next_used == N_EXPERTS, experts, next_used).astype(I32)
    slot_of = ((jnp.cumsum(used.astype(I32)) - 1) % 2).astype(I32)

    expert_tables = ((blk_expert, n_valid, next_used, slot_of,
                      blk_first.astype(I32), blk_last.astype(I32)) + pull_tables)
    return run_tables, _experts(expert_tables, xsl, n_blocks, layer, w_gate, w_up, w_down)


def kernel(x, c, w_ada, b_ada, norm_mix_g, norm_ffn_g, w_in_even, sgu_norm_g, w_spatial, b_spatial, conv_w, w_out_even, w_pool, pool_scale, w_group_router, b_group_router, w_expert_router, b_expert_router, moe_w_gate, moe_w_up, moe_w_down, final_norm_g):
    bsz, seq, d = x.shape
    depth = w_ada.shape[0]
    t_tok = bsz * seq
    assert seq % TM == 0 and d % LANES == 0 and w_spatial.shape[-1] == CHUNK
    assert all(w == 2 ** (g + 1) for g, w in enumerate(POOL_WINDOWS))

    mod = _modulation(c, w_ada, b_ada).reshape(depth, bsz, 6, d)
    tok = jnp.arange(TM)
    earlier = ((tok[:, None] < tok[None, :])
               & (tok[:, None] // SORT_TM == tok[None, :] // SORT_TM)).astype(BF16)
    lower = jnp.tril(jnp.ones((LANES, LANES), F32), -1)
    g_final = final_norm_g.reshape(1, d)

    xf = x.reshape(t_tok, d)
    prev = None
    for l in range(depth):
        i = l // 2
        rw = jnp.concatenate([w_group_router[l], w_expert_router[l]], axis=1).T
        rw = jnp.pad(rw, ((0, ROUTE_ROWS - rw.shape[0]), (0, 0)))
        rw_hi = rw.astype(BF16)
        rw_lo = (rw - rw_hi.astype(F32)).astype(BF16)
        rb = jnp.concatenate([b_group_router[l], b_expert_router[l]])
        rb = jnp.pad(rb, (0, ROUTE_ROWS - rb.shape[0])).reshape(ROUTE_ROWS, 1)
        route_w = [jnp.concatenate([rw_hi, rw_lo], axis=0), rb, earlier, lower]
        gmix = norm_mix_g[l].reshape(1, d)
        gffn = norm_ffn_g[l].reshape(1, d)
        if l % 2 == 0:
            aw = sgu_norm_g.shape[1]
            weights = [gmix, gffn, w_in_even[i].astype(BF16), sgu_norm_g[i].reshape(aw, 1),
                       w_spatial[i], b_spatial[i].reshape(A_HEADS, 1, CHUNK), conv_w[i],
                       w_out_even[i].astype(BF16)] + route_w
            scratch = [pltpu.VMEM((TM + SUBLANES, conv_w.shape[-1]), F32)]
            xf, xsl, route, cnt = _mixer_call(_even_kernel, xf, mod[l], seq, weights, scratch, prev)
        else:
            weights = [gmix, gffn, w_pool[i].astype(BF16), pool_scale[i].reshape(1, d)] + route_w
            gd = d // len(POOL_WINDOWS)
            scratch = [pltpu.VMEM((TM + max(POOL_WINDOWS), d - g * gd), F32)
                       for g in range(len(POOL_WINDOWS))]
            xf, xsl, route, cnt = _mixer_call(_odd_kernel, xf, mod[l], seq, weights, scratch, prev)
        run_tables, ys = _moe(xf, xsl, cnt, l, moe_w_gate, moe_w_up, moe_w_down)
        prev = (run_tables, ys, route, mod[l])
    run_tables, ys, route, mod_l = prev
    return _final(run_tables, ys, xf, route, mod_l, seq, g_final).reshape(bsz, seq, d)
```

```python
import functools

import jax
import jax.numpy as jnp
from jax import lax
from jax.experimental import pallas as pl
from jax.experimental.pallas import tpu as pltpu

F32 = jnp.float32
BF16 = jnp.bfloat16
U32 = jnp.uint32
I32 = jnp.int32

EPS = 1e-6
LANES = 128
SUBLANES = 8
CHUNK = 128
A_HEADS = 8
N_GROUPS = 4
EXPERTS_PER_GROUP = 8
N_EXPERTS = N_GROUPS * EXPERTS_PER_GROUP
POOL_WINDOWS = (2, 4, 8, 16)
CONV_WIDTH = 3
ROUTE_COL0 = N_GROUPS
ROUTE_ROWS = 48

TM = 512
SORT_TM = 256
SECTIONS = TM // SORT_TM
LOCAL_ROWS = 2 * SORT_TM + LANES
LOCAL_PAIRS = LOCAL_ROWS // 2
TILE_PAIRS = SECTIONS * LOCAL_PAIRS
TILE_RUNS = SECTIONS * N_EXPERTS
EXPERT_BLK = 512
BLK_PAIRS = EXPERT_BLK // 2
BLK_PARTS = 4
MOD_TN = 1536
VMEM_LIMIT = 56 * 1024 * 1024


def _rms(x):
    return x * lax.rsqrt(jnp.mean(x * x, axis=-1, keepdims=True) + EPS)


def _dot(a, b):
    return jnp.dot(a, b, preferred_element_type=F32)


def _store_pairs(ref, base, rows_bf16):
    words = pltpu.bitcast(rows_bf16, U32)
    n_pairs, d = words.shape
    slab = d // LANES
    for c in range(slab):
        ref[pl.ds(base + c, n_pairs, stride=slab), :] = words[:, c * LANES:(c + 1) * LANES]


def _load_pairs(ref, base, n_pairs, d):
    slab = d // LANES
    words = jnp.concatenate(
        [ref[pl.ds(base + c, n_pairs, stride=slab), :] for c in range(slab)], axis=-1)
    return pltpu.bitcast(words, BF16)


def _mod_kernel(c_ref, w_ref, b_ref, o_ref):
    c = c_ref[...]
    ca = c * jax.nn.sigmoid(c)
    o_ref[0] = jnp.dot(ca, w_ref[0], precision=lax.Precision.HIGHEST,
                       preferred_element_type=F32) + b_ref[0]


def _modulation(c, w_ada, b_ada):
    depth, d, n = w_ada.shape
    bsz = c.shape[0]
    return pl.pallas_call(
        _mod_kernel,
        grid=(depth, n // MOD_TN),
        in_specs=[
            pl.BlockSpec((bsz, d), lambda l, j: (0, 0)),
            pl.BlockSpec((1, d, MOD_TN), lambda l, j: (l, 0, j)),
            pl.BlockSpec((1, 1, MOD_TN), lambda l, j: (l, 0, j)),
        ],
        out_specs=pl.BlockSpec((1, bsz, MOD_TN), lambda l, j: (l, 0, j)),
        out_shape=jax.ShapeDtypeStruct((depth, bsz, n), F32),
        compiler_params=pltpu.CompilerParams(
            dimension_semantics=("arbitrary", "arbitrary"),
            vmem_limit_bytes=VMEM_LIMIT),
        name="modulation",
    )(c, w_ada, b_ada.reshape(depth, 1, n))


def _route_and_sort(x_new, mod_ref, gffn_ref, rwt_ref, rbc_ref, earlier_ref, lower_ref,
                    xsl_ref, route_ref, cnt_ref):
    tm = x_new.shape[0]
    sh_f = mod_ref[0, 3:4, :]
    sc_f = mod_ref[0, 4:5, :]
    h2 = _rms(x_new) * gffn_ref[...] * (1.0 + sc_f) + sh_f

    rr = ROUTE_ROWS
    nt = (((1,), (1,)), ((), ()))
    hh = h2.astype(BF16)
    hl = (h2 - hh.astype(F32)).astype(BF16)
    both = lax.dot_general(rwt_ref[...], hh, nt, preferred_element_type=F32)
    logits = (both[0:rr] + both[rr:2 * rr]
              + lax.dot_general(rwt_ref[0:rr, :], hl, nt, preferred_element_type=F32)
              + rbc_ref[...])

    row = lax.broadcasted_iota(I32, logits.shape, 0).astype(F32)
    neg = jnp.float32(-jnp.inf)
    big = jnp.float32(LANES)

    gl = jnp.where(row < N_GROUPS, logits, neg)
    gmax = jnp.max(gl, axis=0, keepdims=True)
    g_sel = jnp.min(jnp.where(gl == gmax, row, big), axis=0, keepdims=True)
    g_w = 1.0 / jnp.sum(jnp.exp(gl - gmax), axis=0, keepdims=True)

    lo = ROUTE_COL0 + EXPERTS_PER_GROUP * g_sel
    el = jnp.where((row >= lo) & (row < lo + EXPERTS_PER_GROUP), logits, neg)
    m1 = jnp.max(el, axis=0, keepdims=True)
    i1 = jnp.min(jnp.where(el == m1, row, big), axis=0, keepdims=True)
    el2 = jnp.where(row == i1, neg, el)
    m2 = jnp.max(el2, axis=0, keepdims=True)
    i2 = jnp.min(jnp.where(el2 == m2, row, big), axis=0, keepdims=True)
    t = jnp.exp(m2 - m1)
    gate1 = g_w / (1.0 + t)
    gate2 = g_w * t / (1.0 + t)

    sections = [slice(s * SORT_TM, (s + 1) * SORT_TM) for s in range(tm // SORT_TM)]
    is1 = row == i1
    is2 = row == i2
    onehot = jnp.where(is1 | is2, 1.0, 0.0)
    before = _dot(onehot.astype(BF16), earlier_ref[...])
    lane = lax.broadcasted_iota(I32, (rr, LANES), 1)
    cnts = [jnp.sum(onehot[:, cols], axis=1, keepdims=True) for cols in sections]
    pairs = jnp.zeros((rr, LANES), F32)
    for s, cnt in enumerate(cnts):
        cnt_ref[s] = jnp.broadcast_to(cnt, (rr, LANES))
        pairs = jnp.where(lane == s, jnp.floor((cnt + 1.0) * 0.5), pairs)
    pairs = jnp.concatenate([pairs, jnp.zeros((LANES - rr, LANES), F32)], axis=0)
    pair_start = jnp.dot(lower_ref[...], pairs, precision=lax.Precision.HIGHEST,
                         preferred_element_type=F32)
    pos = jnp.concatenate([before[:, cols] + 2.0 * pair_start[0:rr, s:s + 1]
                           for s, cols in enumerate(sections)], axis=1)
    pos1 = jnp.sum(jnp.where(is1, pos, 0.0), axis=0, keepdims=True)
    pos2 = jnp.sum(jnp.where(is2, pos, 0.0), axis=0, keepdims=True)

    r8 = lax.broadcasted_iota(I32, (SUBLANES, tm), 0)
    route_t = jnp.where(r8 == 0, pos1, 0.0)
    route_t = jnp.where(r8 == 1, pos2, route_t)
    route_t = jnp.where(r8 == 4, gate1, route_t)
    route_t = jnp.where(r8 == 5, gate2, route_t)
    route_t = jnp.concatenate([route_t, jnp.zeros((LANES - SUBLANES, tm), F32)], axis=0)
    route_ref[...] = route_t.T

    srow = lax.broadcasted_iota(I32, (LOCAL_ROWS, SORT_TM), 0).astype(F32)
    section_rows = LOCAL_PAIRS * (x_new.shape[1] // LANES)
    for s, cols in enumerate(sections):
        perm = jnp.where((srow == pos1[:, cols]) | (srow == pos2[:, cols]), 1.0, 0.0)
        _store_pairs(xsl_ref, s * section_rows, _dot(perm.astype(BF16), hh[cols]).astype(BF16))


def _rows(first_row, n_rows, unit):
    return pl.ds(pl.multiple_of(first_row, unit), n_rows)


def _unpermute(tables, ys_hbm, route_ref, ysl_scr, sem, shape):
    len_ref, loc_ref, glob_ref, tot_ref = tables
    i = pl.program_id(0)
    tm, d = shape
    slab = d // LANES

    def run_copy(tile, glob, loc, rows):
        slot = tile % 2
        return pltpu.make_async_copy(ys_hbm.at[_rows(glob, rows, slab)],
                                     ysl_scr.at[slot, _rows(loc, rows, slab)], sem.at[slot])

    def fetch(tile):
        def body(r, carry):
            k = tile * TILE_RUNS + r
            n = len_ref[k]

            @pl.when(n > 0)
            def _():
                run_copy(tile, glob_ref[k], loc_ref[k], n).start()
            return carry
        lax.fori_loop(0, TILE_RUNS, body, 0, unroll=4)

    @pl.when(i == 0)
    def _():
        ysl_scr[...] = jnp.zeros_like(ysl_scr)
        fetch(i)

    @pl.when(i + 1 < pl.num_programs(0))
    def _():
        fetch(i + 1)

    @pl.when(tot_ref[i] > 0)
    def _():
        run_copy(i, 0, 0, tot_ref[i]).wait()

    ysl_ref = ysl_scr.at[i % 2]
    srow = lax.broadcasted_iota(I32, (SORT_TM, LOCAL_ROWS), 1).astype(F32)
    y_sections = []
    for s in range(tm // SORT_TM):
        rows = slice(s * SORT_TM, (s + 1) * SORT_TM)
        ysl = _load_pairs(ysl_ref, s * LOCAL_PAIRS * slab, LOCAL_PAIRS, d)
        y = jnp.zeros((SORT_TM, d), F32)
        for k in range(2):
            sel = jnp.where(srow == route_ref[rows, k:k + 1], 1.0, 0.0).astype(BF16)
            y = y + route_ref[rows, 4 + k:5 + k] * _dot(sel, ysl)
        y_sections.append(y)
    return jnp.concatenate(y_sections, axis=0)


def _split_prev(fused, refs):
    if not fused:
        return None, refs
    tables, (ys_hbm, route_ref, mod_ref), (ysl_scr, sem) = refs[:4], refs[4:7], refs[-2:]
    return (tables, ys_hbm, route_ref, mod_ref, ysl_scr, sem), refs[7:-2]


def _mixer_input(prev, x_ref):
    if prev is None:
        return x_ref[...]
    tables, ys_hbm, route_ref, mod_ref, ysl_scr, sem = prev
    y = _unpermute(tables, ys_hbm, route_ref, ysl_scr, sem, x_ref.shape)
    return x_ref[...] + mod_ref[0, 5:6, :] * y


def _even_kernel(tiles_per_seq, fused, *refs):
    prev, refs = _split_prev(fused, refs)
    (x_ref, mod_ref, gmix_ref, gffn_ref, win_ref, gv_ref, ws_ref, bs_ref,
     cw_ref, wout_ref, rwt_ref, rbc_ref, earlier_ref, lower_ref,
     xo_ref, xsl_ref, route_ref, cnt_ref,
     zc_scr) = refs
    i = pl.program_id(0)
    tm = x_ref.shape[0]
    aw = gv_ref.shape[0]
    hd = aw // A_HEADS
    n_chunks = tm // CHUNK

    @pl.when(i == 0)
    def _():
        zc_scr[...] = jnp.zeros_like(zc_scr)

    x = _mixer_input(prev, x_ref)
    sh_m = mod_ref[0, 0:1, :]
    sc_m = mod_ref[0, 1:2, :]
    g_m = mod_ref[0, 2:3, :]
    h = _rms(x) * gmix_ref[...] * (1.0 + sc_m) + sh_m
    z = _dot(h.astype(BF16), win_ref[...])
    u = z[:, 0:aw]
    v = z[:, aw:2 * aw]
    b_gate = z[:, 2 * aw:3 * aw]
    c_gate = z[:, 3 * aw:4 * aw]
    x_in = z[:, 4 * aw:5 * aw]

    v_t = v.T
    row = lax.broadcasted_iota(I32, (CHUNK, CHUNK), 0)
    col = lax.broadcasted_iota(I32, (CHUNK, CHUNK), 1)
    causal = col <= row
    head_rows = []
    for hh in range(A_HEADS):
        vh = v_t[hh * hd:(hh + 1) * hd, :]
        msv = jnp.mean(vh * vh, axis=0, keepdims=True)
        vn = (vh * lax.rsqrt(msv + EPS) * gv_ref[hh * hd:(hh + 1) * hd, :]).astype(BF16)
        lhs = jnp.concatenate(
            [vn[:, c * CHUNK:(c + 1) * CHUNK] for c in range(n_chunks)], axis=0)
        w_m = jnp.where(causal, ws_ref[hh], 0.0).astype(BF16)
        sv_h = lax.dot_general(lhs, w_m, (((1,), (1,)), ((), ())),
                               preferred_element_type=F32)
        sv_h = sv_h + bs_ref[hh]
        head_rows.append(jnp.concatenate(
            [sv_h[c * hd:(c + 1) * hd, :] for c in range(n_chunks)], axis=1))
    sv = jnp.concatenate(head_rows, axis=0).T
    y_a = u * sv

    zc = c_gate * x_in
    first = (i % tiles_per_seq) == 0
    halo = zc_scr[tm:tm + SUBLANES, :]
    zc_scr[0:SUBLANES, :] = jnp.where(first, 0.0, halo)
    zc_scr[SUBLANES:SUBLANES + tm, :] = zc
    conv = cw_ref[2:3, :] * zc
    for k in range(CONV_WIDTH - 1):
        shift = CONV_WIDTH - 1 - k
        conv = conv + cw_ref[k:k + 1, :] * zc_scr[SUBLANES - shift:SUBLANES - shift + tm, :]
    y_b = b_gate * conv

    y = _dot(y_a.astype(BF16), wout_ref[0:aw, :]) + _dot(y_b.astype(BF16), wout_ref[aw:, :])
    x_new = x + g_m * y
    xo_ref[...] = x_new
    _route_and_sort(x_new, mod_ref, gffn_ref, rwt_ref, rbc_ref, earlier_ref, lower_ref,
                    xsl_ref, route_ref, cnt_ref)


def _odd_kernel(tiles_per_seq, fused, *refs):
    prev, refs = _split_prev(fused, refs)
    (x_ref, mod_ref, gmix_ref, gffn_ref, wpool_ref, pscale_ref,
     rwt_ref, rbc_ref, earlier_ref, lower_ref,
     xo_ref, xsl_ref, route_ref, cnt_ref) = refs[:14]
    level_scrs = refs[14:]
    i = pl.program_id(0)
    tm, d = x_ref.shape
    halo_rows = max(POOL_WINDOWS)
    gd = d // len(POOL_WINDOWS)

    @pl.when(i == 0)
    def _():
        for scr in level_scrs:
            scr[...] = jnp.zeros_like(scr)

    x = _mixer_input(prev, x_ref)
    sh_m = mod_ref[0, 0:1, :]
    sc_m = mod_ref[0, 1:2, :]
    g_m = mod_ref[0, 2:3, :]
    h = _rms(x) * gmix_ref[...] * (1.0 + sc_m) + sh_m

    tile_in_seq = i % tiles_per_seq
    first = tile_in_seq == 0

    sums = h
    window_sums = []
    for g, scr in enumerate(level_scrs):
        tail = scr[tm:tm + halo_rows, :]
        scr[0:halo_rows, :] = jnp.where(first, 0.0, tail)
        scr[halo_rows:halo_rows + tm, :] = sums
        lag = POOL_WINDOWS[g] // 2
        sums = sums + scr[halo_rows - lag:halo_rows - lag + tm, :]
        window_sums.append(sums[:, 0:gd])
        if g + 1 < len(level_scrs):
            sums = sums[:, gd:]

    pos = (tile_in_seq * tm + lax.broadcasted_iota(I32, (tm, 1), 0)).astype(F32)
    outs = []
    for g, win in enumerate(POOL_WINDOWS):
        cs = slice(g * gd, (g + 1) * gd)
        count = jnp.minimum(pos + 1.0, jnp.float32(win))
        pooled = window_sums[g] / count - h[:, cs]
        outs.append(_dot(pooled.astype(BF16), wpool_ref[g]))
    y = jnp.concatenate(outs, axis=-1) * pscale_ref[...]
    x_new = x + g_m * y
    xo_ref[...] = x_new
    _route_and_sort(x_new, mod_ref, gffn_ref, rwt_ref, rbc_ref, earlier_ref, lower_ref,
                    xsl_ref, route_ref, cnt_ref)


def _mixer_call(kernel_fn, x, mod_l, seq, weights, scratch, prev):
    t_tok, d = x.shape
    n_tiles = t_tok // TM
    tiles_per_seq = seq // TM
    n_slabs = d // LANES

    def const_spec(a):
        return pl.BlockSpec(a.shape, lambda i, *_, nd=a.ndim: (0,) * nd)

    def mod_spec(m):
        return pl.BlockSpec((1,) + m.shape[1:], lambda i, *_: (i // tiles_per_seq, 0, 0))

    tables, prev_inputs, prev_specs, prev_scratch = (), (), [], []
    if prev is not None:
        tables, ys, route_prev, mod_prev = prev
        prev_inputs = (ys, route_prev, mod_prev)
        prev_specs = [pl.BlockSpec(memory_space=pl.ANY),
                      pl.BlockSpec((TM, LANES), lambda i, *_: (i, 0)),
                      mod_spec(mod_prev)]
        prev_scratch = [pltpu.VMEM((2, TILE_PAIRS * n_slabs, LANES), U32),
                        pltpu.SemaphoreType.DMA((2,))]
    in_specs = prev_specs + [
        pl.BlockSpec((TM, d), lambda i, *_: (i, 0)),
        mod_spec(mod_l),
    ] + [const_spec(w) for w in weights]
    out_shape = (
        jax.ShapeDtypeStruct((t_tok, d), F32),
        jax.ShapeDtypeStruct((n_tiles * TILE_PAIRS * n_slabs, LANES), U32),
        jax.ShapeDtypeStruct((t_tok, LANES), F32),
        jax.ShapeDtypeStruct((n_tiles * SECTIONS, ROUTE_ROWS, LANES), F32),
    )
    out_specs = (
        pl.BlockSpec((TM, d), lambda i, *_: (i, 0)),
        pl.BlockSpec((TILE_PAIRS * n_slabs, LANES), lambda i, *_: (i, 0)),
        pl.BlockSpec((TM, LANES), lambda i, *_: (i, 0)),
        pl.BlockSpec((SECTIONS, ROUTE_ROWS, LANES), lambda i, *_: (i, 0, 0)),
    )
    return pl.pallas_call(
        functools.partial(kernel_fn, tiles_per_seq, prev is not None),
        grid_spec=pltpu.PrefetchScalarGridSpec(
            num_scalar_prefetch=len(tables),
            grid=(n_tiles,),
            in_specs=in_specs,
            out_specs=out_specs,
            scratch_shapes=scratch + prev_scratch,
        ),
        out_shape=out_shape,
        compiler_params=pltpu.CompilerParams(
            dimension_semantics=("arbitrary",), vmem_limit_bytes=VMEM_LIMIT),
        name=kernel_fn.__name__.strip("_"),
    )(*tables, *prev_inputs, x, mod_l, *weights)


def _expert_kernel(layer, blk_expert_ref, n_valid_ref, next_ref, slot_ref,
                   blk_first_ref, blk_last_ref, blk_fill_ref, run_len_ref, run_src_ref, run_dst_ref,
                   xsl_hbm, wg_hbm, wu_hbm, wd_hbm, ys_ref,
                   x_scr, zero_scr, wg_stage, wu_stage, wd_stage, wg_scr, wu_scr, wd_scr,
                   x_sem, sem):
    j = pl.program_id(0)
    n_valid = n_valid_ref[0]
    expert = blk_expert_ref[j]
    prev = blk_expert_ref[jnp.maximum(j - 1, 0)]
    d = wg_scr.shape[0]
    slab = d // LANES
    blk_rows = BLK_PAIRS * slab

    def fetch(blk):
        slot = blk % 2
        row0 = blk * blk_rows

        def copy_run(q):
            dst = run_dst_ref[q]
            lo = jnp.maximum(dst, row0)
            hi = jnp.minimum(dst + run_len_ref[q], row0 + blk_rows)

            @pl.when(hi > lo)
            def _():
                pltpu.make_async_copy(
                    xsl_hbm.at[_rows(run_src_ref[q] + (lo - dst), hi - lo, slab)],
                    x_scr.at[slot, _rows(lo - row0, hi - lo, slab)], x_sem.at[slot]).start()

        first = blk_first_ref[blk]

        def body(t, carry):
            copy_run(first + 2 * t)
            copy_run(first + 2 * t + 1)
            return carry

        lax.fori_loop(0, (blk_last_ref[blk] - first + 1) // 2, body, 0)
        fill = blk_fill_ref[blk]

        @pl.when(fill > 0)
        def _():
            pltpu.make_async_copy(zero_scr.at[_rows(0, fill, slab)],
                                  x_scr.at[slot, _rows(blk_rows - fill, fill, slab)],
                                  x_sem.at[slot]).start()

    @pl.when(j == 0)
    def _():
        zero_scr[...] = jnp.zeros_like(zero_scr)
        fetch(j)

    @pl.when(j + 1 < n_valid)
    def _():
        fetch(j + 1)

    def weight_copies(e, slot):
        return [pltpu.make_async_copy(hbm.at[layer, e], stage.at[slot], sem.at[slot])
                for hbm, stage in ((wg_hbm, wg_stage), (wu_hbm, wu_stage), (wd_hbm, wd_stage))]

    @pl.when(j == 0)
    def _():
        for cp in weight_copies(expert, slot_ref[expert]):
            cp.start()

    @pl.when((j == 0) | (expert != prev))
    def _():
        slot = slot_ref[expert]
        for cp in weight_copies(expert, slot):
            cp.wait()
        nxt = next_ref[expert]

        @pl.when(nxt != expert)
        def _():
            for cp in weight_copies(nxt, 1 - slot):
                cp.start()

        wg_scr[...] = wg_stage[slot].astype(BF16)
        wu_scr[...] = wu_stage[slot].astype(BF16)
        wd_scr[...] = wd_stage[slot].astype(BF16)

    slot = j % 2

    @pl.when(j < n_valid)
    def _():
        pltpu.make_async_copy(zero_scr, x_scr.at[slot], x_sem.at[slot]).wait()

    part_pairs = BLK_PAIRS // BLK_PARTS
    part_rows = part_pairs * slab
    n_parts = jnp.where(j < n_valid,
                        (blk_rows - blk_fill_ref[j] + part_rows - 1) // part_rows, 0)
    for p in range(1, BLK_PARTS + 1):
        @pl.when(n_parts == p)
        def _(p=p):
            xb = _load_pairs(x_scr.at[slot], 0, p * part_pairs, d)
            a = _dot(xb, wg_scr[...])
            b = _dot(xb, wu_scr[...])
            hm = (a * jax.nn.sigmoid(a) * b).astype(BF16)
            _store_pairs(ys_ref, 0, _dot(hm, wd_scr[...]).astype(BF16))
            if p < BLK_PARTS:
                ys_ref[p * part_rows:, :] = jnp.zeros((blk_rows - p * part_rows, LANES), U32)

    @pl.when(n_parts == 0)
    def _():
        ys_ref[...] = jnp.zeros_like(ys_ref)


def _experts(tables, xsl, n_blocks, layer, w_gate, w_up, w_down):
    _, _, d, de = w_gate.shape
    blk_rows = BLK_PAIRS * (d // LANES)
    return pl.pallas_call(
        functools.partial(_expert_kernel, layer),
        grid_spec=pltpu.PrefetchScalarGridSpec(
            num_scalar_prefetch=len(tables),
            grid=(n_blocks,),
            in_specs=[pl.BlockSpec(memory_space=pl.ANY)] * 4,
            out_specs=pl.BlockSpec((blk_rows, LANES), lambda j, *_: (j, 0)),
            scratch_shapes=[
                pltpu.VMEM((2, blk_rows, LANES), U32), pltpu.VMEM((blk_rows, LANES), U32),
                pltpu.VMEM((2, d, de), F32), pltpu.VMEM((2, d, de), F32),
                pltpu.VMEM((2, de, d), F32),
                pltpu.VMEM((d, de), BF16), pltpu.VMEM((d, de), BF16), pltpu.VMEM((de, d), BF16),
                pltpu.SemaphoreType.DMA((2,)), pltpu.SemaphoreType.DMA((2,)),
            ],
        ),
        out_shape=jax.ShapeDtypeStruct((n_blocks * blk_rows, LANES), U32),
        compiler_params=pltpu.CompilerParams(
            dimension_semantics=("arbitrary",), vmem_limit_bytes=VMEM_LIMIT),
        name="experts",
    )(*tables, xsl, w_gate, w_up, w_down)


def _final_kernel(len_ref, loc_ref, glob_ref, tot_ref, ys_hbm, x_ref, route_ref,
                  mod_ref, gfin_ref, xo_ref, ysl_scr, sem):
    prev = ((len_ref, loc_ref, glob_ref, tot_ref), ys_hbm, route_ref, mod_ref, ysl_scr, sem)
    xo_ref[...] = _rms(_mixer_input(prev, x_ref)) * gfin_ref[...]


def _final(tables, ys, x, route, mod_l, seq, g_final):
    t_tok, d = x.shape
    tiles_per_seq = seq // TM
    return pl.pallas_call(
        _final_kernel,
        grid_spec=pltpu.PrefetchScalarGridSpec(
            num_scalar_prefetch=4,
            grid=(t_tok // TM,),
            in_specs=[
                pl.BlockSpec(memory_space=pl.ANY),
                pl.BlockSpec((TM, d), lambda i, *_: (i, 0)),
                pl.BlockSpec((TM, LANES), lambda i, *_: (i, 0)),
                pl.BlockSpec((1,) + mod_l.shape[1:], lambda i, *_: (i // tiles_per_seq, 0, 0)),
                pl.BlockSpec((1, d), lambda i, *_: (0, 0)),
            ],
            out_specs=pl.BlockSpec((TM, d), lambda i, *_: (i, 0)),
            scratch_shapes=[pltpu.VMEM((2, TILE_PAIRS * (d // LANES), LANES), U32),
                            pltpu.SemaphoreType.DMA((2,))],
        ),
        out_shape=jax.ShapeDtypeStruct((t_tok, d), F32),
        compiler_params=pltpu.CompilerParams(
            dimension_semantics=("arbitrary",), vmem_limit_bytes=VMEM_LIMIT),
        name="final",
    )(*tables, ys, x, route, mod_l, g_final)


def _moe(x, xsl, cnt, layer, w_gate, w_up, w_down):
    t_tok = x.shape[0]
    n_tiles = t_tok // TM
    n_blocks = -(-(2 * t_tok + n_tiles * TILE_RUNS) // EXPERT_BLK) + N_EXPERTS

    counts = cnt[:, ROUTE_COL0:ROUTE_COL0 + N_EXPERTS, 0].astype(I32)
    run_len = (counts + 1) // 2
    section_base = (jnp.arange(counts.shape[0], dtype=I32) % SECTIONS) * LOCAL_PAIRS
    run_loc = jnp.cumsum(run_len, axis=1) - run_len + section_base[:, None]
    seg_len = jnp.sum(run_len, axis=0)
    seg_pad = (seg_len + BLK_PAIRS - 1) // BLK_PAIRS * BLK_PAIRS
    seg_end = jnp.cumsum(seg_pad)
    seg_start = seg_end - seg_pad
    run_glob = seg_start[None, :] + jnp.cumsum(run_len, axis=0) - run_len
    tile_tot = jnp.sum(run_len.reshape(n_tiles, TILE_RUNS), axis=1)
    slab = x.shape[1] // LANES
    run_tables = tuple((a.reshape(-1) * slab).astype(I32)
                       for a in (run_len, run_loc, run_glob, tile_tot))
    n_valid = (seg_end[-1:] // BLK_PAIRS).astype(I32)
    blk_pair0 = jnp.arange(n_blocks, dtype=I32) * BLK_PAIRS
    blk_expert = jnp.sum((seg_end[None, :] <= blk_pair0[:, None]).astype(I32), axis=1)
    section_pair0 = jnp.arange(counts.shape[0], dtype=I32) * LOCAL_PAIRS
    run_src = jnp.cumsum(run_len, axis=1) - run_len + section_pair0[:, None]
    em_len, em_src, em_dst = (
        jnp.concatenate([a.T.reshape(-1), jnp.full((1,), end, I32)])
        for a, end in ((run_len, 0), (run_src, 0), (run_glob, n_blocks * BLK_PAIRS)))
    blk_first = jnp.sum(((em_dst + em_len)[None, :] <= blk_pair0[:, None]).astype(I32), axis=1)
    blk_last = jnp.sum((em_dst[None, :] < (blk_pair0 + BLK_PAIRS)[:, None]).astype(I32), axis=1)
    seg_stop = jnp.sum(jnp.where(
        jnp.arange(N_EXPERTS, dtype=I32)[None, :] == blk_expert[:, None],
        (seg_start + seg_len)[None, :], 0), axis=1)
    blk_fill = BLK_PAIRS - jnp.clip(seg_stop - blk_pair0, 0, BLK_PAIRS)
    pull_tables = tuple((a * slab).astype(I32) for a in (blk_fill, em_len, em_src, em_dst))
    experts = jnp.arange(N_EXPERTS, dtype=I32)
    used = seg_len > 0
    last_used = jnp.max(jnp.where(used, experts, 0))
    blk_expert = jnp.minimum(blk_expert, last_used).astype(I32)
    later = jnp.where((experts[None, :] > experts[:, None]) & used[None, :],
                      experts[None, :], N_EXPERTS)
    next_used = jnp.min(later, axis=1)
    next_used = jnp.where(next_used == N_EXPERTS, experts, next_used).astype(I32)
    slot_of = ((jnp.cumsum(used.astype(I32)) - 1) % 2).astype(I32)

    expert_tables = ((blk_expert, n_valid, next_used, slot_of,
                      blk_first.astype(I32), blk_last.astype(I32)) + pull_tables)
    return run_tables, _experts(expert_tables, xsl, n_blocks, layer, w_gate, w_up, w_down)


def kernel(x, c, w_ada, b_ada, norm_mix_g, norm_ffn_g, w_in_even, sgu_norm_g, w_spatial, b_spatial, conv_w, w_out_even, w_pool, pool_scale, w_group_router, b_group_router, w_expert_router, b_expert_router, moe_w_gate, moe_w_up, moe_w_down, final_norm_g):
    bsz, seq, d = x.shape
    depth = w_ada.shape[0]
    t_tok = bsz * seq
    assert seq % TM == 0 and d % LANES == 0 and w_spatial.shape[-1] == CHUNK
    assert all(w == 2 ** (g + 1) for g, w in enumerate(POOL_WINDOWS))

    mod = _modulation(c, w_ada, b_ada).reshape(depth, bsz, 6, d)
    tok = jnp.arange(TM)
    earlier = ((tok[:, None] < tok[None, :])
               & (tok[:, None] // SORT_TM == tok[None, :] // SORT_TM)).astype(BF16)
    lower = jnp.tril(jnp.ones((LANES, LANES), F32), -1)
    g_final = final_norm_g.reshape(1, d)

    xf = x.reshape(t_tok, d)
    prev = None
    for l in range(depth):
        i = l // 2
        rw = jnp.concatenate([w_group_router[l], w_expert_router[l]], axis=1).T
        rw = jnp.pad(rw, ((0, ROUTE_ROWS - rw.shape[0]), (0, 0)))
        rw_hi = rw.astype(BF16)
        rw_lo = (rw - rw_hi.astype(F32)).astype(BF16)
        rb = jnp.concatenate([b_group_router[l], b_expert_router[l]])
        rb = jnp.pad(rb, (0, ROUTE_ROWS - rb.shape[0])).reshape(ROUTE_ROWS, 1)
        route_w = [jnp.concatenate([rw_hi, rw_lo], axis=0), rb, earlier, lower]
        gmix = norm_mix_g[l].reshape(1, d)
        gffn = norm_ffn_g[l].reshape(1, d)
        if l % 2 == 0:
            aw = sgu_norm_g.shape[1]
            weights = [gmix, gffn, w_in_even[i].astype(BF16), sgu_norm_g[i].reshape(aw, 1),
                       w_spatial[i], b_spatial[i].reshape(A_HEADS, 1, CHUNK), conv_w[i],
                       w_out_even[i].astype(BF16)] + route_w
            scratch = [pltpu.VMEM((TM + SUBLANES, conv_w.shape[-1]), F32)]
            xf, xsl, route, cnt = _mixer_call(_even_kernel, xf, mod[l], seq, weights, scratch, prev)
        else:
            weights = [gmix, gffn, w_pool[i].astype(BF16), pool_scale[i].reshape(1, d)] + route_w
            gd = d // len(POOL_WINDOWS)
            scratch = [pltpu.VMEM((TM + max(POOL_WINDOWS), d - g * gd), F32)
                       for g in range(len(POOL_WINDOWS))]
            xf, xsl, route, cnt = _mixer_call(_odd_kernel, xf, mod[l], seq, weights, scratch, prev)
        run_tables, ys = _moe(xf, xsl, cnt, l, moe_w_gate, moe_w_up, moe_w_down)
        prev = (run_tables, ys, route, mod[l])
    run_tables, ys, route, mod_l = prev
    return _final(run_tables, ys, xf, route, mod_l, seq, g_final).reshape(bsz, seq, d)
```

```python
import functools

import jax
import jax.numpy as jnp
from jax import lax
from jax.experimental import pallas as pl
from jax.experimental.pallas import tpu as pltpu

F32 = jnp.float32
BF16 = jnp.bfloat16
U32 = jnp.uint32
I32 = jnp.int32

EPS = 1e-6
LANES = 128
SUBLANES = 8
CHUNK = 128
A_HEADS = 8
N_GROUPS = 4
EXPERTS_PER_GROUP = 8
N_EXPERTS = N_GROUPS * EXPERTS_PER_GROUP
POOL_WINDOWS = (2, 4, 8, 16)
CONV_WIDTH = 3
ROUTE_COL0 = N_GROUPS
ROUTE_ROWS = 48

TM = 512
SORT_TM = 256
SECTIONS = TM // SORT_TM
LOCAL_ROWS = 2 * SORT_TM + LANES
LOCAL_PAIRS = LOCAL_ROWS // 2
TILE_PAIRS = SECTIONS * LOCAL_PAIRS
TILE_RUNS = SECTIONS * N_EXPERTS
EXPERT_BLK = 1024
BLK_PAIRS = EXPERT_BLK // 2
BLK_PARTS = 8
MOD_TN = 1536
VMEM_LIMIT = 56 * 1024 * 1024


def _rms(x):
    return x * lax.rsqrt(jnp.mean(x * x, axis=-1, keepdims=True) + EPS)


def _dot(a, b):
    return jnp.dot(a, b, preferred_element_type=F32)


def _store_pairs(ref, base, rows_bf16):
    words = pltpu.bitcast(rows_bf16, U32)
    n_pairs, d = words.shape
    slab = d // LANES
    for c in range(slab):
        ref[pl.ds(base + c, n_pairs, stride=slab), :] = words[:, c * LANES:(c + 1) * LANES]


def _load_pairs(ref, base, n_pairs, d):
    slab = d // LANES
    words = jnp.concatenate(
        [ref[pl.ds(base + c, n_pairs, stride=slab), :] for c in range(slab)], axis=-1)
    return pltpu.bitcast(words, BF16)


def _mod_kernel(c_ref, w_ref, b_ref, o_ref):
    c = c_ref[...]
    ca = c * jax.nn.sigmoid(c)
    o_ref[0] = jnp.dot(ca, w_ref[0], precision=lax.Precision.HIGHEST,
                       preferred_element_type=F32) + b_ref[0]


def _modulation(c, w_ada, b_ada):
    depth, d, n = w_ada.shape
    bsz = c.shape[0]
    return pl.pallas_call(
        _mod_kernel,
        grid=(depth, n // MOD_TN),
        in_specs=[
            pl.BlockSpec((bsz, d), lambda l, j: (0, 0)),
            pl.BlockSpec((1, d, MOD_TN), lambda l, j: (l, 0, j)),
            pl.BlockSpec((1, 1, MOD_TN), lambda l, j: (l, 0, j)),
        ],
        out_specs=pl.BlockSpec((1, bsz, MOD_TN), lambda l, j: (l, 0, j)),
        out_shape=jax.ShapeDtypeStruct((depth, bsz, n), F32),
        compiler_params=pltpu.CompilerParams(
            dimension_semantics=("arbitrary", "arbitrary"),
            vmem_limit_bytes=VMEM_LIMIT),
        name="modulation",
    )(c, w_ada, b_ada.reshape(depth, 1, n))


def _route_and_sort(x_new, mod_ref, gffn_ref, rwt_ref, rbc_ref, earlier_ref, lower_ref,
                    xsl_ref, route_ref, cnt_ref):
    tm = x_new.shape[0]
    sh_f = mod_ref[0, 3:4, :]
    sc_f = mod_ref[0, 4:5, :]
    h2 = _rms(x_new) * gffn_ref[...] * (1.0 + sc_f) + sh_f

    rr = ROUTE_ROWS
    nt = (((1,), (1,)), ((), ()))
    hh = h2.astype(BF16)
    hl = (h2 - hh.astype(F32)).astype(BF16)
    both = lax.dot_general(rwt_ref[...], hh, nt, preferred_element_type=F32)
    logits = (both[0:rr] + both[rr:2 * rr]
              + lax.dot_general(rwt_ref[0:rr, :], hl, nt, preferred_element_type=F32)
              + rbc_ref[...])

    row = lax.broadcasted_iota(I32, logits.shape, 0).astype(F32)
    neg = jnp.float32(-jnp.inf)
    big = jnp.float32(LANES)

    gl = jnp.where(row < N_GROUPS, logits, neg)
    gmax = jnp.max(gl, axis=0, keepdims=True)
    g_sel = jnp.min(jnp.where(gl == gmax, row, big), axis=0, keepdims=True)
    g_w = 1.0 / jnp.sum(jnp.exp(gl - gmax), axis=0, keepdims=True)

    lo = ROUTE_COL0 + EXPERTS_PER_GROUP * g_sel
    el = jnp.where((row >= lo) & (row < lo + EXPERTS_PER_GROUP), logits, neg)
    m1 = jnp.max(el, axis=0, keepdims=True)
    i1 = jnp.min(jnp.where(el == m1, row, big), axis=0, keepdims=True)
    el2 = jnp.where(row == i1, neg, el)
    m2 = jnp.max(el2, axis=0, keepdims=True)
    i2 = jnp.min(jnp.where(el2 == m2, row, big), axis=0, keepdims=True)
    t = jnp.exp(m2 - m1)
    gate1 = g_w / (1.0 + t)
    gate2 = g_w * t / (1.0 + t)

    sections = [slice(s * SORT_TM, (s + 1) * SORT_TM) for s in range(tm // SORT_TM)]
    is1 = row == i1
    is2 = row == i2
    onehot = jnp.where(is1 | is2, 1.0, 0.0)
    before = _dot(onehot.astype(BF16), earlier_ref[...])
    lane = lax.broadcasted_iota(I32, (rr, LANES), 1)
    cnts = [jnp.sum(onehot[:, cols], axis=1, keepdims=True) for cols in sections]
    pairs = jnp.zeros((rr, LANES), F32)
    for s, cnt in enumerate(cnts):
        cnt_ref[s] = jnp.broadcast_to(cnt, (rr, LANES))
        pairs = jnp.where(lane == s, jnp.floor((cnt + 1.0) * 0.5), pairs)
    pairs = jnp.concatenate([pairs, jnp.zeros((LANES - rr, LANES), F32)], axis=0)
    pair_start = jnp.dot(lower_ref[...], pairs, precision=lax.Precision.HIGHEST,
                         preferred_element_type=F32)
    pos = jnp.concatenate([before[:, cols] + 2.0 * pair_start[0:rr, s:s + 1]
                           for s, cols in enumerate(sections)], axis=1)
    pos1 = jnp.sum(jnp.where(is1, pos, 0.0), axis=0, keepdims=True)
    pos2 = jnp.sum(jnp.where(is2, pos, 0.0), axis=0, keepdims=True)

    r8 = lax.broadcasted_iota(I32, (SUBLANES, tm), 0)
    route_t = jnp.where(r8 == 0, pos1, 0.0)
    route_t = jnp.where(r8 == 1, pos2, route_t)
    route_t = jnp.where(r8 == 4, gate1, route_t)
    route_t = jnp.where(r8 == 5, gate2, route_t)
    route_t = jnp.concatenate([route_t, jnp.zeros((LANES - SUBLANES, tm), F32)], axis=0)
    route_ref[...] = route_t.T

    srow = lax.broadcasted_iota(I32, (LOCAL_ROWS, SORT_TM), 0).astype(F32)
    section_rows = LOCAL_PAIRS * (x_new.shape[1] // LANES)
    for s, cols in enumerate(sections):
        perm = jnp.where((srow == pos1[:, cols]) | (srow == pos2[:, cols]), 1.0, 0.0)
        _store_pairs(xsl_ref, s * section_rows, _dot(perm.astype(BF16), hh[cols]).astype(BF16))


def _rows(first_row, n_rows, unit):
    return pl.ds(pl.multiple_of(first_row, unit), n_rows)


def _unpermute(tables, ys_hbm, route_ref, ysl_scr, sem, shape):
    len_ref, loc_ref, glob_ref, tot_ref = tables
    i = pl.program_id(0)
    tm, d = shape
    slab = d // LANES

    def run_copy(tile, glob, loc, rows):
        slot = tile % 2
        return pltpu.make_async_copy(ys_hbm.at[_rows(glob, rows, slab)],
                                     ysl_scr.at[slot, _rows(loc, rows, slab)], sem.at[slot])

    def fetch(tile):
        def body(r, carry):
            k = tile * TILE_RUNS + r
            n = len_ref[k]

            @pl.when(n > 0)
            def _():
                run_copy(tile, glob_ref[k], loc_ref[k], n).start()
            return carry
        lax.fori_loop(0, TILE_RUNS, body, 0, unroll=4)

    @pl.when(i == 0)
    def _():
        ysl_scr[...] = jnp.zeros_like(ysl_scr)
        fetch(i)

    @pl.when(i + 1 < pl.num_programs(0))
    def _():
        fetch(i + 1)

    @pl.when(tot_ref[i] > 0)
    def _():
        run_copy(i, 0, 0, tot_ref[i]).wait()

    ysl_ref = ysl_scr.at[i % 2]
    srow = lax.broadcasted_iota(I32, (SORT_TM, LOCAL_ROWS), 1).astype(F32)
    y_sections = []
    for s in range(tm // SORT_TM):
        rows = slice(s * SORT_TM, (s + 1) * SORT_TM)
        ysl = _load_pairs(ysl_ref, s * LOCAL_PAIRS * slab, LOCAL_PAIRS, d)
        y = jnp.zeros((SORT_TM, d), F32)
        for k in range(2):
            sel = jnp.where(srow == route_ref[rows, k:k + 1], 1.0, 0.0).astype(BF16)
            y = y + route_ref[rows, 4 + k:5 + k] * _dot(sel, ysl)
        y_sections.append(y)
    return jnp.concatenate(y_sections, axis=0)


def _split_prev(fused, refs):
    if not fused:
        return None, refs
    tables, (ys_hbm, route_ref, mod_ref), (ysl_scr, sem) = refs[:4], refs[4:7], refs[-2:]
    return (tables, ys_hbm, route_ref, mod_ref, ysl_scr, sem), refs[7:-2]


def _mixer_input(prev, x_ref):
    if prev is None:
        return x_ref[...]
    tables, ys_hbm, route_ref, mod_ref, ysl_scr, sem = prev
    y = _unpermute(tables, ys_hbm, route_ref, ysl_scr, sem, x_ref.shape)
    return x_ref[...] + mod_ref[0, 5:6, :] * y


def _even_kernel(tiles_per_seq, fused, *refs):
    prev, refs = _split_prev(fused, refs)
    (x_ref, mod_ref, gmix_ref, gffn_ref, win_ref, gv_ref, ws_ref, bs_ref,
     cw_ref, wout_ref, rwt_ref, rbc_ref, earlier_ref, lower_ref,
     xo_ref, xsl_ref, route_ref, cnt_ref,
     zc_scr) = refs
    i = pl.program_id(0)
    tm = x_ref.shape[0]
    aw = gv_ref.shape[0]
    hd = aw // A_HEADS
    n_chunks = tm // CHUNK

    @pl.when(i == 0)
    def _():
        zc_scr[...] = jnp.zeros_like(zc_scr)

    x = _mixer_input(prev, x_ref)
    sh_m = mod_ref[0, 0:1, :]
    sc_m = mod_ref[0, 1:2, :]
    g_m = mod_ref[0, 2:3, :]
    h = _rms(x) * gmix_ref[...] * (1.0 + sc_m) + sh_m
    z = _dot(h.astype(BF16), win_ref[...])
    u = z[:, 0:aw]
    v = z[:, aw:2 * aw]
    b_gate = z[:, 2 * aw:3 * aw]
    c_gate = z[:, 3 * aw:4 * aw]
    x_in = z[:, 4 * aw:5 * aw]

    v_t = v.T
    row = lax.broadcasted_iota(I32, (CHUNK, CHUNK), 0)
    col = lax.broadcasted_iota(I32, (CHUNK, CHUNK), 1)
    causal = col <= row
    head_rows = []
    for hh in range(A_HEADS):
        vh = v_t[hh * hd:(hh + 1) * hd, :]
        msv = jnp.mean(vh * vh, axis=0, keepdims=True)
        vn = (vh * lax.rsqrt(msv + EPS) * gv_ref[hh * hd:(hh + 1) * hd, :]).astype(BF16)
        lhs = jnp.concatenate(
            [vn[:, c * CHUNK:(c + 1) * CHUNK] for c in range(n_chunks)], axis=0)
        w_m = jnp.where(causal, ws_ref[hh], 0.0).astype(BF16)
        sv_h = lax.dot_general(lhs, w_m, (((1,), (1,)), ((), ())),
                               preferred_element_type=F32)
        sv_h = sv_h + bs_ref[hh]
        head_rows.append(jnp.concatenate(
            [sv_h[c * hd:(c + 1) * hd, :] for c in range(n_chunks)], axis=1))
    sv = jnp.concatenate(head_rows, axis=0).T
    y_a = u * sv

    zc = c_gate * x_in
    first = (i % tiles_per_seq) == 0
    halo = zc_scr[tm:tm + SUBLANES, :]
    zc_scr[0:SUBLANES, :] = jnp.where(first, 0.0, halo)
    zc_scr[SUBLANES:SUBLANES + tm, :] = zc
    conv = cw_ref[2:3, :] * zc
    for k in range(CONV_WIDTH - 1):
        shift = CONV_WIDTH - 1 - k
        conv = conv + cw_ref[k:k + 1, :] * zc_scr[SUBLANES - shift:SUBLANES - shift + tm, :]
    y_b = b_gate * conv

    y = _dot(y_a.astype(BF16), wout_ref[0:aw, :]) + _dot(y_b.astype(BF16), wout_ref[aw:, :])
    x_new = x + g_m * y
    xo_ref[...] = x_new
    _route_and_sort(x_new, mod_ref, gffn_ref, rwt_ref, rbc_ref, earlier_ref, lower_ref,
                    xsl_ref, route_ref, cnt_ref)


def _odd_kernel(tiles_per_seq, fused, *refs):
    prev, refs = _split_prev(fused, refs)
    (x_ref, mod_ref, gmix_ref, gffn_ref, wpool_ref, pscale_ref,
     rwt_ref, rbc_ref, earlier_ref, lower_ref,
     xo_ref, xsl_ref, route_ref, cnt_ref) = refs[:14]
    level_scrs = refs[14:]
    i = pl.program_id(0)
    tm, d = x_ref.shape
    halo_rows = max(POOL_WINDOWS)
    gd = d // len(POOL_WINDOWS)

    @pl.when(i == 0)
    def _():
        for scr in level_scrs:
            scr[...] = jnp.zeros_like(scr)

    x = _mixer_input(prev, x_ref)
    sh_m = mod_ref[0, 0:1, :]
    sc_m = mod_ref[0, 1:2, :]
    g_m = mod_ref[0, 2:3, :]
    h = _rms(x) * gmix_ref[...] * (1.0 + sc_m) + sh_m

    tile_in_seq = i % tiles_per_seq
    first = tile_in_seq == 0

    sums = h
    window_sums = []
    for g, scr in enumerate(level_scrs):
        tail = scr[tm:tm + halo_rows, :]
        scr[0:halo_rows, :] = jnp.where(first, 0.0, tail)
        scr[halo_rows:halo_rows + tm, :] = sums
        lag = POOL_WINDOWS[g] // 2
        sums = sums + scr[halo_rows - lag:halo_rows - lag + tm, :]
        window_sums.append(sums[:, 0:gd])
        if g + 1 < len(level_scrs):
            sums = sums[:, gd:]

    pos = (tile_in_seq * tm + lax.broadcasted_iota(I32, (tm, 1), 0)).astype(F32)
    outs = []
    for g, win in enumerate(POOL_WINDOWS):
        cs = slice(g * gd, (g + 1) * gd)
        count = jnp.minimum(pos + 1.0, jnp.float32(win))
        pooled = window_sums[g] / count - h[:, cs]
        outs.append(_dot(pooled.astype(BF16), wpool_ref[g]))
    y = jnp.concatenate(outs, axis=-1) * pscale_ref[...]
    x_new = x + g_m * y
    xo_ref[...] = x_new
    _route_and_sort(x_new, mod_ref, gffn_ref, rwt_ref, rbc_ref, earlier_ref, lower_ref,
                    xsl_ref, route_ref, cnt_ref)


def _mixer_call(kernel_fn, x, mod_l, seq, weights, scratch, prev):
    t_tok, d = x.shape
    n_tiles = t_tok // TM
    tiles_per_seq = seq // TM
    n_slabs = d // LANES

    def const_spec(a):
        return pl.BlockSpec(a.shape, lambda i, *_, nd=a.ndim: (0,) * nd)

    def mod_spec(m):
        return pl.BlockSpec((1,) + m.shape[1:], lambda i, *_: (i // tiles_per_seq, 0, 0))

    tables, prev_inputs, prev_specs, prev_scratch = (), (), [], []
    if prev is not None:
        tables, ys, route_prev, mod_prev = prev
        prev_inputs = (ys, route_prev, mod_prev)
        prev_specs = [pl.BlockSpec(memory_space=pl.ANY),
                      pl.BlockSpec((TM, LANES), lambda i, *_: (i, 0)),
                      mod_spec(mod_prev)]
        prev_scratch = [pltpu.VMEM((2, TILE_PAIRS * n_slabs, LANES), U32),
                        pltpu.SemaphoreType.DMA((2,))]
    in_specs = prev_specs + [
        pl.BlockSpec((TM, d), lambda i, *_: (i, 0)),
        mod_spec(mod_l),
    ] + [const_spec(w) for w in weights]
    out_shape = (
        jax.ShapeDtypeStruct((t_tok, d), F32),
        jax.ShapeDtypeStruct((n_tiles * TILE_PAIRS * n_slabs, LANES), U32),
        jax.ShapeDtypeStruct((t_tok, LANES), F32),
        jax.ShapeDtypeStruct((n_tiles * SECTIONS, ROUTE_ROWS, LANES), F32),
    )
    out_specs = (
        pl.BlockSpec((TM, d), lambda i, *_: (i, 0)),
        pl.BlockSpec((TILE_PAIRS * n_slabs, LANES), lambda i, *_: (i, 0)),
        pl.BlockSpec((TM, LANES), lambda i, *_: (i, 0)),
        pl.BlockSpec((SECTIONS, ROUTE_ROWS, LANES), lambda i, *_: (i, 0, 0)),
    )
    return pl.pallas_call(
        functools.partial(kernel_fn, tiles_per_seq, prev is not None),
        grid_spec=pltpu.PrefetchScalarGridSpec(
            num_scalar_prefetch=len(tables),
            grid=(n_tiles,),
            in_specs=in_specs,
            out_specs=out_specs,
            scratch_shapes=scratch + prev_scratch,
        ),
        out_shape=out_shape,
        compiler_params=pltpu.CompilerParams(
            dimension_semantics=("arbitrary",), vmem_limit_bytes=VMEM_LIMIT),
        name=kernel_fn.__name__.strip("_"),
    )(*tables, *prev_inputs, x, mod_l, *weights)


def _expert_kernel(layer, blk_expert_ref, n_valid_ref, next_ref, slot_ref,
                   blk_first_ref, blk_last_ref, blk_fill_ref, run_len_ref, run_src_ref, run_dst_ref,
                   xsl_hbm, wg_hbm, wu_hbm, wd_hbm, ys_ref,
                   x_scr, zero_scr, wg_stage, wu_stage, wd_stage, wg_scr, wu_scr, wd_scr,
                   x_sem, sem):
    j = pl.program_id(0)
    n_valid = n_valid_ref[0]
    expert = blk_expert_ref[j]
    prev = blk_expert_ref[jnp.maximum(j - 1, 0)]
    d = wg_scr.shape[0]
    slab = d // LANES
    blk_rows = BLK_PAIRS * slab

    def fetch(blk):
        slot = blk % 2
        row0 = blk * blk_rows

        def copy_run(q):
            dst = run_dst_ref[q]
            lo = jnp.maximum(dst, row0)
            hi = jnp.minimum(dst + run_len_ref[q], row0 + blk_rows)

            @pl.when(hi > lo)
            def _():
                pltpu.make_async_copy(
                    xsl_hbm.at[_rows(run_src_ref[q] + (lo - dst), hi - lo, slab)],
                    x_scr.at[slot, _rows(lo - row0, hi - lo, slab)], x_sem.at[slot]).start()

        first = blk_first_ref[blk]

        def body(t, carry):
            copy_run(first + 2 * t)
            copy_run(first + 2 * t + 1)
            return carry

        lax.fori_loop(0, (blk_last_ref[blk] - first + 1) // 2, body, 0)
        fill = blk_fill_ref[blk]

        @pl.when(fill > 0)
        def _():
            pltpu.make_async_copy(zero_scr.at[_rows(0, fill, slab)],
                                  x_scr.at[slot, _rows(blk_rows - fill, fill, slab)],
                                  x_sem.at[slot]).start()

    @pl.when(j == 0)
    def _():
        zero_scr[...] = jnp.zeros_like(zero_scr)
        fetch(j)

    @pl.when(j + 1 < n_valid)
    def _():
        fetch(j + 1)

    def weight_copies(e, slot):
        return [pltpu.make_async_copy(hbm.at[layer, e], stage.at[slot], sem.at[slot])
                for hbm, stage in ((wg_hbm, wg_stage), (wu_hbm, wu_stage), (wd_hbm, wd_stage))]

    @pl.when(j == 0)
    def _():
        for cp in weight_copies(expert, slot_ref[expert]):
            cp.start()

    @pl.when((j == 0) | (expert != prev))
    def _():
        slot = slot_ref[expert]
        for cp in weight_copies(expert, slot):
            cp.wait()
        nxt = next_ref[expert]

        @pl.when(nxt != expert)
        def _():
            for cp in weight_copies(nxt, 1 - slot):
                cp.start()

        wg_scr[...] = wg_stage[slot].astype(BF16)
        wu_scr[...] = wu_stage[slot].astype(BF16)
        wd_scr[...] = wd_stage[slot].astype(BF16)

    slot = j % 2

    @pl.when(j < n_valid)
    def _():
        pltpu.make_async_copy(zero_scr, x_scr.at[slot], x_sem.at[slot]).wait()

    part_pairs = BLK_PAIRS // BLK_PARTS
    part_rows = part_pairs * slab
    n_parts = jnp.where(j < n_valid,
                        (blk_rows - blk_fill_ref[j] + part_rows - 1) // part_rows, 0)
    for p in range(1, BLK_PARTS + 1):
        @pl.when(n_parts == p)
        def _(p=p):
            xb = _load_pairs(x_scr.at[slot], 0, p * part_pairs, d)
            a = _dot(xb, wg_scr[...])
            b = _dot(xb, wu_scr[...])
            hm = (a * jax.nn.sigmoid(a) * b).astype(BF16)
            _store_pairs(ys_ref, 0, _dot(hm, wd_scr[...]).astype(BF16))
            if p < BLK_PARTS:
                ys_ref[p * part_rows:, :] = jnp.zeros((blk_rows - p * part_rows, LANES), U32)

    @pl.when(n_parts == 0)
    def _():
        ys_ref[...] = jnp.zeros_like(ys_ref)


def _experts(tables, xsl, n_blocks, layer, w_gate, w_up, w_down):
    _, _, d, de = w_gate.shape
    blk_rows = BLK_PAIRS * (d // LANES)
    return pl.pallas_call(
        functools.partial(_expert_kernel, layer),
        grid_spec=pltpu.PrefetchScalarGridSpec(
            num_scalar_prefetch=len(tables),
            grid=(n_blocks,),
            in_specs=[pl.BlockSpec(memory_space=pl.ANY)] * 4,
            out_specs=pl.BlockSpec((blk_rows, LANES), lambda j, *_: (j, 0)),
            scratch_shapes=[
                pltpu.VMEM((2, blk_rows, LANES), U32), pltpu.VMEM((blk_rows, LANES), U32),
                pltpu.VMEM((2, d, de), F32), pltpu.VMEM((2, d, de), F32),
                pltpu.VMEM((2, de, d), F32),
                pltpu.VMEM((d, de), BF16), pltpu.VMEM((d, de), BF16), pltpu.VMEM((de, d), BF16),
                pltpu.SemaphoreType.DMA((2,)), pltpu.SemaphoreType.DMA((2,)),
            ],
        ),
        out_shape=jax.ShapeDtypeStruct((n_blocks * blk_rows, LANES), U32),
        compiler_params=pltpu.CompilerParams(
            dimension_semantics=("arbitrary",), vmem_limit_bytes=VMEM_LIMIT),
        name="experts",
    )(*tables, xsl, w_gate, w_up, w_down)


def _final_kernel(len_ref, loc_ref, glob_ref, tot_ref, ys_hbm, x_ref, route_ref,
                  mod_ref, gfin_ref, xo_ref, ysl_scr, sem):
    prev = ((len_ref, loc_ref, glob_ref, tot_ref), ys_hbm, route_ref, mod_ref, ysl_scr, sem)
    xo_ref[...] = _rms(_mixer_input(prev, x_ref)) * gfin_ref[...]


def _final(tables, ys, x, route, mod_l, seq, g_final):
    t_tok, d = x.shape
    tiles_per_seq = seq // TM
    return pl.pallas_call(
        _final_kernel,
        grid_spec=pltpu.PrefetchScalarGridSpec(
            num_scalar_prefetch=4,
            grid=(t_tok // TM,),
            in_specs=[
                pl.BlockSpec(memory_space=pl.ANY),
                pl.BlockSpec((TM, d), lambda i, *_: (i, 0)),
                pl.BlockSpec((TM, LANES), lambda i, *_: (i, 0)),
                pl.BlockSpec((1,) + mod_l.shape[1:], lambda i, *_: (i // tiles_per_seq, 0, 0)),
                pl.BlockSpec((1, d), lambda i, *_: (0, 0)),
            ],
            out_specs=pl.BlockSpec((TM, d), lambda i, *_: (i, 0)),
            scratch_shapes=[pltpu.VMEM((2, TILE_PAIRS * (d // LANES), LANES), U32),
                            pltpu.SemaphoreType.DMA((2,))],
        ),
        out_shape=jax.ShapeDtypeStruct((t_tok, d), F32),
        compiler_params=pltpu.CompilerParams(
            dimension_semantics=("arbitrary",), vmem_limit_bytes=VMEM_LIMIT),
        name="final",
    )(*tables, ys, x, route, mod_l, g_final)


def _moe(x, xsl, cnt, layer, w_gate, w_up, w_down):
    t_tok = x.shape[0]
    n_tiles = t_tok // TM
    n_blocks = -(-(2 * t_tok + n_tiles * TILE_RUNS) // EXPERT_BLK) + N_EXPERTS

    counts = cnt[:, ROUTE_COL0:ROUTE_COL0 + N_EXPERTS, 0].astype(I32)
    run_len = (counts + 1) // 2
    section_base = (jnp.arange(counts.shape[0], dtype=I32) % SECTIONS) * LOCAL_PAIRS
    run_loc = jnp.cumsum(run_len, axis=1) - run_len + section_base[:, None]
    seg_len = jnp.sum(run_len, axis=0)
    seg_pad = (seg_len + BLK_PAIRS - 1) // BLK_PAIRS * BLK_PAIRS
    seg_end = jnp.cumsum(seg_pad)
    seg_start = seg_end - seg_pad
    run_glob = seg_start[None, :] + jnp.cumsum(run_len, axis=0) - run_len
    tile_tot = jnp.sum(run_len.reshape(n_tiles, TILE_RUNS), axis=1)
    slab = x.shape[1] // LANES
    run_tables = tuple((a.reshape(-1) * slab).astype(I32)
                       for a in (run_len, run_loc, run_glob, tile_tot))
    n_valid = (seg_end[-1:] // BLK_PAIRS).astype(I32)
    blk_pair0 = jnp.arange(n_blocks, dtype=I32) * BLK_PAIRS
    blk_expert = jnp.sum((seg_end[None, :] <= blk_pair0[:, None]).astype(I32), axis=1)
    section_pair0 = jnp.arange(counts.shape[0], dtype=I32) * LOCAL_PAIRS
    run_src = jnp.cumsum(run_len, axis=1) - run_len + section_pair0[:, None]
    em_len, em_src, em_dst = (
        jnp.concatenate([a.T.reshape(-1), jnp.full((1,), end, I32)])
        for a, end in ((run_len, 0), (run_src, 0), (run_glob, n_blocks * BLK_PAIRS)))
    blk_first = jnp.sum(((em_dst + em_len)[None, :] <= blk_pair0[:, None]).astype(I32), axis=1)
    blk_last = jnp.sum((em_dst[None, :] < (blk_pair0 + BLK_PAIRS)[:, None]).astype(I32), axis=1)
    seg_stop = jnp.sum(jnp.where(
        jnp.arange(N_EXPERTS, dtype=I32)[None, :] == blk_expert[:, None],
        (seg_start + seg_len)[None, :], 0), axis=1)
    blk_fill = BLK_PAIRS - jnp.clip(seg_stop - blk_pair0, 0, BLK_PAIRS)
    pull_tables = tuple((a * slab).astype(I32) for a in (blk_fill, em_len, em_src, em_dst))
    experts = jnp.arange(N_EXPERTS, dtype=I32)
    used = seg_len > 0
    last_used = jnp.max(jnp.where(used, experts, 0))
    blk_expert = jnp.minimum(blk_expert, last_used).astype(I32)
    later = jnp.where((experts[None, :] > experts[:, None]) & used[None, :],
                      experts[None, :], N_EXPERTS)
    next_used = jnp.min(later, axis=1)
    next_used = jnp.where(next_used == N_EXPERTS, experts, next_used).astype(I32)
    slot_of = ((jnp.cumsum(used.astype(I32)) - 1) % 2).astype(I32)

    expert_tables = ((blk_expert, n_valid, next_used, slot_of,
                      blk_first.astype(I32), blk_last.astype(I32)) + pull_tables)
    return run_tables, _experts(expert_tables, xsl, n_blocks, layer, w_gate, w_up, w_down)


def kernel(x, c, w_ada, b_ada, norm_mix_g, norm_ffn_g, w_in_even, sgu_norm_g, w_spatial, b_spatial, conv_w, w_out_even, w_pool, pool_scale, w_group_router, b_group_router, w_expert_router, b_expert_router, moe_w_gate, moe_w_up, moe_w_down, final_norm_g):
    bsz, seq, d = x.shape
    depth = w_ada.shape[0]
    t_tok = bsz * seq
    assert seq % TM == 0 and d % LANES == 0 and w_spatial.shape[-1] == CHUNK
    assert all(w == 2 ** (g + 1) for g, w in enumerate(POOL_WINDOWS))

    mod = _modulation(c, w_ada, b_ada).reshape(depth, bsz, 6, d)
    tok = jnp.arange(TM)
    earlier = ((tok[:, None] < tok[None, :])
               & (tok[:, None] // SORT_TM == tok[None, :] // SORT_TM)).astype(BF16)
    lower = jnp.tril(jnp.ones((LANES, LANES), F32), -1)
    g_final = final_norm_g.reshape(1, d)

    xf = x.reshape(t_tok, d)
    prev = None
    for l in range(depth):
        i = l // 2
        rw = jnp.concatenate([w_group_router[l], w_expert_router[l]], axis=1).T
        rw = jnp.pad(rw, ((0, ROUTE_ROWS - rw.shape[0]), (0, 0)))
        rw_hi = rw.astype(BF16)
        rw_lo = (rw - rw_hi.astype(F32)).astype(BF16)
        rb = jnp.concatenate([b_group_router[l], b_expert_router[l]])
        rb = jnp.pad(rb, (0, ROUTE_ROWS - rb.shape[0])).reshape(ROUTE_ROWS, 1)
        route_w = [jnp.concatenate([rw_hi, rw_lo], axis=0), rb, earlier, lower]
        gmix = norm_mix_g[l].reshape(1, d)
        gffn = norm_ffn_g[l].reshape(1, d)
        if l % 2 == 0:
            aw = sgu_norm_g.shape[1]
            weights = [gmix, gffn, w_in_even[i].astype(BF16), sgu_norm_g[i].reshape(aw, 1),
                       w_spatial[i], b_spatial[i].reshape(A_HEADS, 1, CHUNK), conv_w[i],
                       w_out_even[i].astype(BF16)] + route_w
            scratch = [pltpu.VMEM((TM + SUBLANES, conv_w.shape[-1]), F32)]
            xf, xsl, route, cnt = _mixer_call(_even_kernel, xf, mod[l], seq, weights, scratch, prev)
        else:
            weights = [gmix, gffn, w_pool[i].astype(BF16), pool_scale[i].reshape(1, d)] + route_w
            gd = d // len(POOL_WINDOWS)
            scratch = [pltpu.VMEM((TM + max(POOL_WINDOWS), d - g * gd), F32)
                       for g in range(len(POOL_WINDOWS))]
            xf, xsl, route, cnt = _mixer_call(_odd_kernel, xf, mod[l], seq, weights, scratch, prev)
        run_tables, ys = _moe(xf, xsl, cnt, l, moe_w_gate, moe_w_up, moe_w_down)
        prev = (run_tables, ys, route, mod[l])
    run_tables, ys, route, mod_l = prev
    return _final(run_tables, ys, xf, route, mod_l, seq, g_final).reshape(bsz, seq, d)
```

```python
import functools

import jax
import jax.numpy as jnp
from jax import lax
from jax.experimental import pallas as pl
from jax.experimental.pallas import tpu as pltpu

F32 = jnp.float32
BF16 = jnp.bfloat16
U32 = jnp.uint32
I32 = jnp.int32

EPS = 1e-6
LANES = 128
SUBLANES = 8
CHUNK = 128
A_HEADS = 8
N_GROUPS = 4
EXPERTS_PER_GROUP = 8
N_EXPERTS = N_GROUPS * EXPERTS_PER_GROUP
POOL_WINDOWS = (2, 4, 8, 16)
CONV_WIDTH = 3
ROUTE_COL0 = N_GROUPS
ROUTE_ROWS = 48

TM = 512
SORT_TM = 256
SECTIONS = TM // SORT_TM
LOCAL_ROWS = 2 * SORT_TM + LANES
LOCAL_PAIRS = LOCAL_ROWS // 2
TILE_PAIRS = SECTIONS * LOCAL_PAIRS
TILE_RUNS = SECTIONS * N_EXPERTS
EXPERT_BLK = 1024
BLK_PAIRS = EXPERT_BLK // 2
BLK_PARTS = 8
MOD_TN = 1536
VMEM_LIMIT = 56 * 1024 * 1024


def _rms(x):
    return x * lax.rsqrt(jnp.mean(x * x, axis=-1, keepdims=True) + EPS)


def _dot(a, b):
    return jnp.dot(a, b, preferred_element_type=F32)


def _store_pairs(ref, base, rows_bf16):
    words = pltpu.bitcast(rows_bf16, U32)
    n_pairs, d = words.shape
    slab = d // LANES
    for c in range(slab):
        ref[pl.ds(base + c, n_pairs, stride=slab), :] = words[:, c * LANES:(c + 1) * LANES]


def _load_pairs(ref, base, n_pairs, d):
    slab = d // LANES
    words = jnp.concatenate(
        [ref[pl.ds(base + c, n_pairs, stride=slab), :] for c in range(slab)], axis=-1)
    return pltpu.bitcast(words, BF16)


def _mod_kernel(c_ref, w_ref, b_ref, o_ref):
    c = c_ref[...]
    ca = c * jax.nn.sigmoid(c)
    o_ref[0] = jnp.dot(ca, w_ref[0], precision=lax.Precision.HIGHEST,
                       preferred_element_type=F32) + b_ref[0]


def _modulation(c, w_ada, b_ada):
    depth, d, n = w_ada.shape
    bsz = c.shape[0]
    return pl.pallas_call(
        _mod_kernel,
        grid=(depth, n // MOD_TN),
        in_specs=[
            pl.BlockSpec((bsz, d), lambda l, j: (0, 0)),
            pl.BlockSpec((1, d, MOD_TN), lambda l, j: (l, 0, j)),
            pl.BlockSpec((1, 1, MOD_TN), lambda l, j: (l, 0, j)),
        ],
        out_specs=pl.BlockSpec((1, bsz, MOD_TN), lambda l, j: (l, 0, j)),
        out_shape=jax.ShapeDtypeStruct((depth, bsz, n), F32),
        compiler_params=pltpu.CompilerParams(
            dimension_semantics=("arbitrary", "arbitrary"),
            vmem_limit_bytes=VMEM_LIMIT),
        name="modulation",
    )(c, w_ada, b_ada.reshape(depth, 1, n))


def _route_and_sort(x_new, mod_ref, gffn_ref, rwt_ref, rbc_ref, earlier_ref, lower_ref,
                    xsl_ref, route_ref, cnt_ref):
    tm = x_new.shape[0]
    sh_f = mod_ref[0, 3:4, :]
    sc_f = mod_ref[0, 4:5, :]
    h2 = _rms(x_new) * (gffn_ref[...] * (1.0 + sc_f)) + sh_f

    rr = ROUTE_ROWS
    nt = (((1,), (1,)), ((), ()))
    hh = h2.astype(BF16)
    hl = (h2 - hh.astype(F32)).astype(BF16)
    both = lax.dot_general(rwt_ref[...], hh, nt, preferred_element_type=F32)
    logits = (both[0:rr] + both[rr:2 * rr]
              + lax.dot_general(rwt_ref[0:rr, :], hl, nt, preferred_element_type=F32)
              + rbc_ref[...])

    row = lax.broadcasted_iota(I32, logits.shape, 0).astype(F32)
    neg = jnp.float32(-jnp.inf)
    big = jnp.float32(LANES)

    gl = jnp.where(row < N_GROUPS, logits, neg)
    gmax = jnp.max(gl, axis=0, keepdims=True)
    g_sel = jnp.min(jnp.where(gl == gmax, row, big), axis=0, keepdims=True)
    g_w = 1.0 / jnp.sum(jnp.exp(gl - gmax), axis=0, keepdims=True)

    lo = ROUTE_COL0 + EXPERTS_PER_GROUP * g_sel
    el = jnp.where((row >= lo) & (row < lo + EXPERTS_PER_GROUP), logits, neg)
    m1 = jnp.max(el, axis=0, keepdims=True)
    i1 = jnp.min(jnp.where(el == m1, row, big), axis=0, keepdims=True)
    el2 = jnp.where(row == i1, neg, el)
    m2 = jnp.max(el2, axis=0, keepdims=True)
    i2 = jnp.min(jnp.where(el2 == m2, row, big), axis=0, keepdims=True)
    t = jnp.exp(m2 - m1)
    gate1 = g_w / (1.0 + t)
    gate2 = g_w * t / (1.0 + t)

    sections = [slice(s * SORT_TM, (s + 1) * SORT_TM) for s in range(tm // SORT_TM)]
    is1 = row == i1
    is2 = row == i2
    onehot = jnp.where(is1 | is2, 1.0, 0.0)
    before = _dot(onehot.astype(BF16), earlier_ref[...])
    lane = lax.broadcasted_iota(I32, (rr, LANES), 1)
    cnts = [jnp.sum(onehot[:, cols], axis=1, keepdims=True) for cols in sections]
    pairs = jnp.zeros((rr, LANES), F32)
    for s, cnt in enumerate(cnts):
        cnt_ref[s] = jnp.broadcast_to(cnt, (rr, LANES))
        pairs = jnp.where(lane == s, jnp.floor((cnt + 1.0) * 0.5), pairs)
    pairs = jnp.concatenate([pairs, jnp.zeros((LANES - rr, LANES), F32)], axis=0)
    pair_start = jnp.dot(lower_ref[...], pairs, precision=lax.Precision.HIGHEST,
                         preferred_element_type=F32)
    pos = jnp.concatenate([before[:, cols] + 2.0 * pair_start[0:rr, s:s + 1]
                           for s, cols in enumerate(sections)], axis=1)
    pos1 = jnp.sum(jnp.where(is1, pos, 0.0), axis=0, keepdims=True)
    pos2 = jnp.sum(jnp.where(is2, pos, 0.0), axis=0, keepdims=True)

    r8 = lax.broadcasted_iota(I32, (SUBLANES, tm), 0)
    route_t = jnp.where(r8 == 0, pos1, 0.0)
    route_t = jnp.where(r8 == 1, pos2, route_t)
    route_t = jnp.where(r8 == 4, gate1, route_t)
    route_t = jnp.where(r8 == 5, gate2, route_t)
    route_t = jnp.concatenate([route_t, jnp.zeros((LANES - SUBLANES, tm), F32)], axis=0)
    route_ref[...] = route_t.T

    srow = lax.broadcasted_iota(I32, (LOCAL_ROWS, SORT_TM), 0).astype(F32)
    section_rows = LOCAL_PAIRS * (x_new.shape[1] // LANES)
    for s, cols in enumerate(sections):
        perm = jnp.where((srow == pos1[:, cols]) | (srow == pos2[:, cols]), 1.0, 0.0)
        _store_pairs(xsl_ref, s * section_rows, _dot(perm.astype(BF16), hh[cols]).astype(BF16))


def _rows(first_row, n_rows, unit):
    return pl.ds(pl.multiple_of(first_row, unit), n_rows)


def _unpermute(tables, ys_hbm, route_ref, ysl_scr, sem, shape):
    len_ref, loc_ref, glob_ref, tot_ref = tables
    i = pl.program_id(0)
    tm, d = shape
    slab = d // LANES

    def run_copy(tile, glob, loc, rows):
        slot = tile % 2
        return pltpu.make_async_copy(ys_hbm.at[_rows(glob, rows, slab)],
                                     ysl_scr.at[slot, _rows(loc, rows, slab)], sem.at[slot])

    def fetch(tile):
        def body(r, carry):
            k = tile * TILE_RUNS + r
            n = len_ref[k]

            @pl.when(n > 0)
            def _():
                run_copy(tile, glob_ref[k], loc_ref[k], n).start()
            return carry
        lax.fori_loop(0, TILE_RUNS, body, 0, unroll=4)

    @pl.when(i == 0)
    def _():
        ysl_scr[...] = jnp.zeros_like(ysl_scr)
        fetch(i)

    @pl.when(i + 1 < pl.num_programs(0))
    def _():
        fetch(i + 1)

    @pl.when(tot_ref[i] > 0)
    def _():
        run_copy(i, 0, 0, tot_ref[i]).wait()

    ysl_ref = ysl_scr.at[i % 2]
    srow = lax.broadcasted_iota(I32, (SORT_TM, LOCAL_ROWS), 1).astype(F32)
    y_sections = []
    for s in range(tm // SORT_TM):
        rows = slice(s * SORT_TM, (s + 1) * SORT_TM)
        ysl = _load_pairs(ysl_ref, s * LOCAL_PAIRS * slab, LOCAL_PAIRS, d)
        sel = jnp.concatenate(
            [jnp.where(srow == route_ref[rows, k:k + 1], 1.0, 0.0).astype(BF16) for k in range(2)],
            axis=0)
        picked = _dot(sel, ysl)
        y_sections.append(route_ref[rows, 4:5] * picked[0:SORT_TM]
                          + route_ref[rows, 5:6] * picked[SORT_TM:2 * SORT_TM])
    return jnp.concatenate(y_sections, axis=0)


def _split_prev(fused, refs):
    if not fused:
        return None, refs
    tables, (ys_hbm, route_ref, mod_ref), (ysl_scr, sem) = refs[:4], refs[4:7], refs[-2:]
    return (tables, ys_hbm, route_ref, mod_ref, ysl_scr, sem), refs[7:-2]


def _mixer_input(prev, x_ref):
    if prev is None:
        return x_ref[...]
    tables, ys_hbm, route_ref, mod_ref, ysl_scr, sem = prev
    y = _unpermute(tables, ys_hbm, route_ref, ysl_scr, sem, x_ref.shape)
    return x_ref[...] + mod_ref[0, 5:6, :] * y


def _even_kernel(tiles_per_seq, fused, *refs):
    prev, refs = _split_prev(fused, refs)
    (x_ref, mod_ref, gmix_ref, gffn_ref, win_ref, gv_ref, ws_ref, bs_ref,
     cw_ref, wout_ref, rwt_ref, rbc_ref, earlier_ref, lower_ref,
     xo_ref, xsl_ref, route_ref, cnt_ref,
     zc_scr) = refs
    i = pl.program_id(0)
    tm = x_ref.shape[0]
    aw = gv_ref.shape[0]
    hd = aw // A_HEADS
    n_chunks = tm // CHUNK

    @pl.when(i == 0)
    def _():
        zc_scr[...] = jnp.zeros_like(zc_scr)

    x = _mixer_input(prev, x_ref)
    sh_m = mod_ref[0, 0:1, :]
    sc_m = mod_ref[0, 1:2, :]
    g_m = mod_ref[0, 2:3, :]
    h = _rms(x) * (gmix_ref[...] * (1.0 + sc_m)) + sh_m
    z = _dot(h.astype(BF16), win_ref[...])
    u = z[:, 0:aw]
    v = z[:, aw:2 * aw]
    b_gate = z[:, 2 * aw:3 * aw]
    c_gate = z[:, 3 * aw:4 * aw]
    x_in = z[:, 4 * aw:5 * aw]

    v_t = v.T
    row = lax.broadcasted_iota(I32, (CHUNK, CHUNK), 0)
    col = lax.broadcasted_iota(I32, (CHUNK, CHUNK), 1)
    causal = col <= row
    head_rows = []
    for hh in range(A_HEADS):
        vh = v_t[hh * hd:(hh + 1) * hd, :]
        msv = jnp.mean(vh * vh, axis=0, keepdims=True)
        vn = (vh * lax.rsqrt(msv + EPS) * gv_ref[hh * hd:(hh + 1) * hd, :]).astype(BF16)
        lhs = jnp.concatenate(
            [vn[:, c * CHUNK:(c + 1) * CHUNK] for c in range(n_chunks)], axis=0)
        w_m = jnp.where(causal, ws_ref[hh], 0.0).astype(BF16)
        sv_h = lax.dot_general(lhs, w_m, (((1,), (1,)), ((), ())),
                               preferred_element_type=F32)
        sv_h = sv_h + bs_ref[hh]
        head_rows.append(jnp.concatenate(
            [sv_h[c * hd:(c + 1) * hd, :] for c in range(n_chunks)], axis=1))
    sv = jnp.concatenate(head_rows, axis=0).T
    y_a = u * sv

    zc = c_gate * x_in
    first = (i % tiles_per_seq) == 0
    halo = zc_scr[tm:tm + SUBLANES, :]
    zc_scr[0:SUBLANES, :] = jnp.where(first, 0.0, halo)
    zc_scr[SUBLANES:SUBLANES + tm, :] = zc
    conv = cw_ref[2:3, :] * zc
    for k in range(CONV_WIDTH - 1):
        shift = CONV_WIDTH - 1 - k
        conv = conv + cw_ref[k:k + 1, :] * zc_scr[SUBLANES - shift:SUBLANES - shift + tm, :]
    y_b = b_gate * conv

    y = _dot(y_a.astype(BF16), wout_ref[0:aw, :]) + _dot(y_b.astype(BF16), wout_ref[aw:, :])
    x_new = x + g_m * y
    xo_ref[...] = x_new
    _route_and_sort(x_new, mod_ref, gffn_ref, rwt_ref, rbc_ref, earlier_ref, lower_ref,
                    xsl_ref, route_ref, cnt_ref)


def _odd_kernel(tiles_per_seq, fused, *refs):
    prev, refs = _split_prev(fused, refs)
    (x_ref, mod_ref, gmix_ref, gffn_ref, wpool_ref, pscale_ref,
     rwt_ref, rbc_ref, earlier_ref, lower_ref,
     xo_ref, xsl_ref, route_ref, cnt_ref) = refs[:14]
    level_scrs = refs[14:]
    i = pl.program_id(0)
    tm, d = x_ref.shape
    halo_rows = max(POOL_WINDOWS)
    gd = d // len(POOL_WINDOWS)

    @pl.when(i == 0)
    def _():
        for scr in level_scrs:
            scr[...] = jnp.zeros_like(scr)

    x = _mixer_input(prev, x_ref)
    sh_m = mod_ref[0, 0:1, :]
    sc_m = mod_ref[0, 1:2, :]
    g_m = mod_ref[0, 2:3, :]
    h = _rms(x) * (gmix_ref[...] * (1.0 + sc_m)) + sh_m

    tile_in_seq = i % tiles_per_seq
    first = tile_in_seq == 0

    sums = h
    window_sums = []
    for g, scr in enumerate(level_scrs):
        tail = scr[tm:tm + halo_rows, :]
        scr[0:halo_rows, :] = jnp.where(first, 0.0, tail)
        scr[halo_rows:halo_rows + tm, :] = sums
        lag = POOL_WINDOWS[g] // 2
        sums = sums + scr[halo_rows - lag:halo_rows - lag + tm, :]
        window_sums.append(sums[:, 0:gd])
        if g + 1 < len(level_scrs):
            sums = sums[:, gd:]

    pos = (tile_in_seq * tm + lax.broadcasted_iota(I32, (tm, 1), 0)).astype(F32)
    outs = []
    for g, win in enumerate(POOL_WINDOWS):
        cs = slice(g * gd, (g + 1) * gd)
        count = jnp.minimum(pos + 1.0, jnp.float32(win))
        pooled = window_sums[g] / count - h[:, cs]
        outs.append(_dot(pooled.astype(BF16), wpool_ref[g]))
    y = jnp.concatenate(outs, axis=-1) * pscale_ref[...]
    x_new = x + g_m * y
    xo_ref[...] = x_new
    _route_and_sort(x_new, mod_ref, gffn_ref, rwt_ref, rbc_ref, earlier_ref, lower_ref,
                    xsl_ref, route_ref, cnt_ref)


def _mixer_call(kernel_fn, x, mod_l, seq, weights, scratch, prev):
    t_tok, d = x.shape
    n_tiles = t_tok // TM
    tiles_per_seq = seq // TM
    n_slabs = d // LANES

    def const_spec(a):
        return pl.BlockSpec(a.shape, lambda i, *_, nd=a.ndim: (0,) * nd)

    def mod_spec(m):
        return pl.BlockSpec((1,) + m.shape[1:], lambda i, *_: (i // tiles_per_seq, 0, 0))

    tables, prev_inputs, prev_specs, prev_scratch = (), (), [], []
    if prev is not None:
        tables, ys, route_prev, mod_prev = prev
        prev_inputs = (ys, route_prev, mod_prev)
        prev_specs = [pl.BlockSpec(memory_space=pl.ANY),
                      pl.BlockSpec((TM, LANES), lambda i, *_: (i, 0)),
                      mod_spec(mod_prev)]
        prev_scratch = [pltpu.VMEM((2, TILE_PAIRS * n_slabs, LANES), U32),
                        pltpu.SemaphoreType.DMA((2,))]
    in_specs = prev_specs + [
        pl.BlockSpec((TM, d), lambda i, *_: (i, 0)),
        mod_spec(mod_l),
    ] + [const_spec(w) for w in weights]
    out_shape = (
        jax.ShapeDtypeStruct((t_tok, d), F32),
        jax.ShapeDtypeStruct((n_tiles * TILE_PAIRS * n_slabs, LANES), U32),
        jax.ShapeDtypeStruct((t_tok, LANES), F32),
        jax.ShapeDtypeStruct((n_tiles * SECTIONS, ROUTE_ROWS, LANES), F32),
    )
    out_specs = (
        pl.BlockSpec((TM, d), lambda i, *_: (i, 0)),
        pl.BlockSpec((TILE_PAIRS * n_slabs, LANES), lambda i, *_: (i, 0)),
        pl.BlockSpec((TM, LANES), lambda i, *_: (i, 0)),
        pl.BlockSpec((SECTIONS, ROUTE_ROWS, LANES), lambda i, *_: (i, 0, 0)),
    )
    return pl.pallas_call(
        functools.partial(kernel_fn, tiles_per_seq, prev is not None),
        grid_spec=pltpu.PrefetchScalarGridSpec(
            num_scalar_prefetch=len(tables),
            grid=(n_tiles,),
            in_specs=in_specs,
            out_specs=out_specs,
            scratch_shapes=scratch + prev_scratch,
        ),
        out_shape=out_shape,
        compiler_params=pltpu.CompilerParams(
            dimension_semantics=("arbitrary",), vmem_limit_bytes=VMEM_LIMIT),
        name=kernel_fn.__name__.strip("_"),
    )(*tables, *prev_inputs, x, mod_l, *weights)


def _expert_kernel(layer, blk_expert_ref, n_valid_ref, next_ref, slot_ref,
                   blk_first_ref, blk_last_ref, blk_fill_ref, run_len_ref, run_src_ref, run_dst_ref,
                   xsl_hbm, wg_hbm, wu_hbm, wd_hbm, ys_ref,
                   x_scr, zero_scr, wg_stage, wu_stage, wd_stage, wg_scr, wu_scr, wd_scr,
                   x_sem, sem):
    j = pl.program_id(0)
    n_valid = n_valid_ref[0]
    expert = blk_expert_ref[j]
    prev = blk_expert_ref[jnp.maximum(j - 1, 0)]
    d = wg_scr.shape[0]
    slab = d // LANES
    blk_rows = BLK_PAIRS * slab

    def fetch(blk):
        slot = blk % 2
        row0 = blk * blk_rows

        def copy_run(q):
            dst = run_dst_ref[q]
            lo = jnp.maximum(dst, row0)
            hi = jnp.minimum(dst + run_len_ref[q], row0 + blk_rows)

            @pl.when(hi > lo)
            def _():
                pltpu.make_async_copy(
                    xsl_hbm.at[_rows(run_src_ref[q] + (lo - dst), hi - lo, slab)],
                    x_scr.at[slot, _rows(lo - row0, hi - lo, slab)], x_sem.at[slot]).start()

        first = blk_first_ref[blk]

        def body(t, carry):
            copy_run(first + 2 * t)
            copy_run(first + 2 * t + 1)
            return carry

        lax.fori_loop(0, (blk_last_ref[blk] - first + 1) // 2, body, 0)
        fill = blk_fill_ref[blk]

        @pl.when(fill > 0)
        def _():
            pltpu.make_async_copy(zero_scr.at[_rows(0, fill, slab)],
                                  x_scr.at[slot, _rows(blk_rows - fill, fill, slab)],
                                  x_sem.at[slot]).start()

    @pl.when(j == 0)
    def _():
        zero_scr[...] = jnp.zeros_like(zero_scr)
        fetch(j)

    @pl.when(j + 1 < n_valid)
    def _():
        fetch(j + 1)

    def weight_copies(e, slot):
        return [pltpu.make_async_copy(hbm.at[layer, e], stage.at[slot], sem.at[slot])
                for hbm, stage in ((wg_hbm, wg_stage), (wu_hbm, wu_stage), (wd_hbm, wd_stage))]

    @pl.when(j == 0)
    def _():
        for cp in weight_copies(expert, slot_ref[expert]):
            cp.start()

    @pl.when((j == 0) | (expert != prev))
    def _():
        slot = slot_ref[expert]
        for cp in weight_copies(expert, slot):
            cp.wait()
        nxt = next_ref[expert]

        @pl.when(nxt != expert)
        def _():
            for cp in weight_copies(nxt, 1 - slot):
                cp.start()

        wg_scr[...] = wg_stage[slot].astype(BF16)
        wu_scr[...] = wu_stage[slot].astype(BF16)
        wd_scr[...] = wd_stage[slot].astype(BF16)

    slot = j % 2

    @pl.when(j < n_valid)
    def _():
        pltpu.make_async_copy(zero_scr, x_scr.at[slot], x_sem.at[slot]).wait()

    part_pairs = BLK_PAIRS // BLK_PARTS
    part_rows = part_pairs * slab
    n_parts = jnp.where(j < n_valid,
                        (blk_rows - blk_fill_ref[j] + part_rows - 1) // part_rows, 0)
    for p in range(1, BLK_PARTS + 1):
        @pl.when(n_parts == p)
        def _(p=p):
            xb = _load_pairs(x_scr.at[slot], 0, p * part_pairs, d)
            a = _dot(xb, wg_scr[...])
            b = _dot(xb, wu_scr[...])
            hm = (a * jax.nn.sigmoid(a) * b).astype(BF16)
            _store_pairs(ys_ref, 0, _dot(hm, wd_scr[...]).astype(BF16))
            if p < BLK_PARTS:
                ys_ref[p * part_rows:, :] = jnp.zeros((blk_rows - p * part_rows, LANES), U32)

    @pl.when(n_parts == 0)
    def _():
        ys_ref[...] = jnp.zeros_like(ys_ref)


def _experts(tables, xsl, n_blocks, layer, w_gate, w_up, w_down):
    _, _, d, de = w_gate.shape
    blk_rows = BLK_PAIRS * (d // LANES)
    return pl.pallas_call(
        functools.partial(_expert_kernel, layer),
        grid_spec=pltpu.PrefetchScalarGridSpec(
            num_scalar_prefetch=len(tables),
            grid=(n_blocks,),
            in_specs=[pl.BlockSpec(memory_space=pl.ANY)] * 4,
            out_specs=pl.BlockSpec((blk_rows, LANES), lambda j, *_: (j, 0)),
            scratch_shapes=[
                pltpu.VMEM((2, blk_rows, LANES), U32), pltpu.VMEM((blk_rows, LANES), U32),
                pltpu.VMEM((2, d, de), F32), pltpu.VMEM((2, d, de), F32),
                pltpu.VMEM((2, de, d), F32),
                pltpu.VMEM((d, de), BF16), pltpu.VMEM((d, de), BF16), pltpu.VMEM((de, d), BF16),
                pltpu.SemaphoreType.DMA((2,)), pltpu.SemaphoreType.DMA((2,)),
            ],
        ),
        out_shape=jax.ShapeDtypeStruct((n_blocks * blk_rows, LANES), U32),
        compiler_params=pltpu.CompilerParams(
            dimension_semantics=("arbitrary",), vmem_limit_bytes=VMEM_LIMIT),
        name="experts",
    )(*tables, xsl, w_gate, w_up, w_down)


def _final_kernel(len_ref, loc_ref, glob_ref, tot_ref, ys_hbm, x_ref, route_ref,
                  mod_ref, gfin_ref, xo_ref, ysl_scr, sem):
    prev = ((len_ref, loc_ref, glob_ref, tot_ref), ys_hbm, route_ref, mod_ref, ysl_scr, sem)
    xo_ref[...] = _rms(_mixer_input(prev, x_ref)) * gfin_ref[...]


def _final(tables, ys, x, route, mod_l, seq, g_final):
    t_tok, d = x.shape
    tiles_per_seq = seq // TM
    return pl.pallas_call(
        _final_kernel,
        grid_spec=pltpu.PrefetchScalarGridSpec(
            num_scalar_prefetch=4,
            grid=(t_tok // TM,),
            in_specs=[
                pl.BlockSpec(memory_space=pl.ANY),
                pl.BlockSpec((TM, d), lambda i, *_: (i, 0)),
                pl.BlockSpec((TM, LANES), lambda i, *_: (i, 0)),
                pl.BlockSpec((1,) + mod_l.shape[1:], lambda i, *_: (i // tiles_per_seq, 0, 0)),
                pl.BlockSpec((1, d), lambda i, *_: (0, 0)),
            ],
            out_specs=pl.BlockSpec((TM, d), lambda i, *_: (i, 0)),
            scratch_shapes=[pltpu.VMEM((2, TILE_PAIRS * (d // LANES), LANES), U32),
                            pltpu.SemaphoreType.DMA((2,))],
        ),
        out_shape=jax.ShapeDtypeStruct((t_tok, d), F32),
        compiler_params=pltpu.CompilerParams(
            dimension_semantics=("arbitrary",), vmem_limit_bytes=VMEM_LIMIT),
        name="final",
    )(*tables, ys, x, route, mod_l, g_final)


def _moe(x, xsl, cnt, layer, w_gate, w_up, w_down):
    t_tok = x.shape[0]
    n_tiles = t_tok // TM
    n_blocks = -(-(2 * t_tok + n_tiles * TILE_RUNS) // EXPERT_BLK) + N_EXPERTS

    counts = cnt[:, ROUTE_COL0:ROUTE_COL0 + N_EXPERTS, 0].astype(I32)
    run_len = (counts + 1) // 2
    section_base = (jnp.arange(counts.shape[0], dtype=I32) % SECTIONS) * LOCAL_PAIRS
    run_loc = jnp.cumsum(run_len, axis=1) - run_len + section_base[:, None]
    seg_len = jnp.sum(run_len, axis=0)
    seg_pad = (seg_len + BLK_PAIRS - 1) // BLK_PAIRS * BLK_PAIRS
    seg_end = jnp.cumsum(seg_pad)
    seg_start = seg_end - seg_pad
    run_glob = seg_start[None, :] + jnp.cumsum(run_len, axis=0) - run_len
    tile_tot = jnp.sum(run_len.reshape(n_tiles, TILE_RUNS), axis=1)
    slab = x.shape[1] // LANES
    run_tables = tuple((a.reshape(-1) * slab).astype(I32)
                       for a in (run_len, run_loc, run_glob, tile_tot))
    n_valid = (seg_end[-1:] // BLK_PAIRS).astype(I32)
    blk_pair0 = jnp.arange(n_blocks, dtype=I32) * BLK_PAIRS
    blk_expert = jnp.sum((seg_end[None, :] <= blk_pair0[:, None]).astype(I32), axis=1)
    section_pair0 = jnp.arange(counts.shape[0], dtype=I32) * LOCAL_PAIRS
    run_src = jnp.cumsum(run_len, axis=1) - run_len + section_pair0[:, None]
    em_len, em_src, em_dst = (
        jnp.concatenate([a.T.reshape(-1), jnp.full((1,), end, I32)])
        for a, end in ((run_len, 0), (run_src, 0), (run_glob, n_blocks * BLK_PAIRS)))
    blk_first = jnp.sum(((em_dst + em_len)[None, :] <= blk_pair0[:, None]).astype(I32), axis=1)
    blk_last = jnp.sum((em_dst[None, :] < (blk_pair0 + BLK_PAIRS)[:, None]).astype(I32), axis=1)
    seg_stop = jnp.sum(jnp.where(
        jnp.arange(N_EXPERTS, dtype=I32)[None, :] == blk_expert[:, None],
        (seg_start + seg_len)[None, :], 0), axis=1)
    blk_fill = BLK_PAIRS - jnp.clip(seg_stop - blk_pair0, 0, BLK_PAIRS)
    pull_tables = tuple((a * slab).astype(I32) for a in (blk_fill, em_len, em_src, em_dst))
    experts = jnp.arange(N_EXPERTS, dtype=I32)
    used = seg_len > 0
    last_used = jnp.max(jnp.where(used, experts, 0))
    blk_expert = jnp.minimum(blk_expert, last_used).astype(I32)
    later = jnp.where((experts[None, :] > experts[:, None]) & used[None, :],
                      experts[None, :], N_EXPERTS)
    next_used = jnp.min(later, axis=1)
    next_used = jnp.where(next_used == N_EXPERTS, experts, next_used).astype(I32)
    slot_of = ((jnp.cumsum(used.astype(I32)) - 1) % 2).astype(I32)

    expert_tables = ((blk_expert, n_valid, next_used, slot_of,
                      blk_first.astype(I32), blk_last.astype(I32)) + pull_tables)
    return run_tables, _experts(expert_tables, xsl, n_blocks, layer, w_gate, w_up, w_down)


def kernel(x, c, w_ada, b_ada, norm_mix_g, norm_ffn_g, w_in_even, sgu_norm_g, w_spatial, b_spatial, conv_w, w_out_even, w_pool, pool_scale, w_group_router, b_group_router, w_expert_router, b_expert_router, moe_w_gate, moe_w_up, moe_w_down, final_norm_g):
    bsz, seq, d = x.shape
    depth = w_ada.shape[0]
    t_tok = bsz * seq
    assert seq % TM == 0 and d % LANES == 0 and w_spatial.shape[-1] == CHUNK
    assert all(w == 2 ** (g + 1) for g, w in enumerate(POOL_WINDOWS))

    mod = _modulation(c, w_ada, b_ada).reshape(depth, bsz, 6, d)
    tok = jnp.arange(TM)
    earlier = ((tok[:, None] < tok[None, :])
               & (tok[:, None] // SORT_TM == tok[None, :] // SORT_TM)).astype(BF16)
    lower = jnp.tril(jnp.ones((LANES, LANES), F32), -1)
    g_final = final_norm_g.reshape(1, d)

    xf = x.reshape(t_tok, d)
    prev = None
    for l in range(depth):
        i = l // 2
        rw = jnp.concatenate([w_group_router[l], w_expert_router[l]], axis=1).T
        rw = jnp.pad(rw, ((0, ROUTE_ROWS - rw.shape[0]), (0, 0)))
        rw_hi = rw.astype(BF16)
        rw_lo = (rw - rw_hi.astype(F32)).astype(BF16)
        rb = jnp.concatenate([b_group_router[l], b_expert_router[l]])
        rb = jnp.pad(rb, (0, ROUTE_ROWS - rb.shape[0])).reshape(ROUTE_ROWS, 1)
        route_w = [jnp.concatenate([rw_hi, rw_lo], axis=0), rb, earlier, lower]
        gmix = norm_mix_g[l].reshape(1, d)
        gffn = norm_ffn_g[l].reshape(1, d)
        if l % 2 == 0:
            aw = sgu_norm_g.shape[1]
            weights = [gmix, gffn, w_in_even[i].astype(BF16), sgu_norm_g[i].reshape(aw, 1),
                       w_spatial[i], b_spatial[i].reshape(A_HEADS, 1, CHUNK), conv_w[i],
                       w_out_even[i].astype(BF16)] + route_w
            scratch = [pltpu.VMEM((TM + SUBLANES, conv_w.shape[-1]), F32)]
            xf, xsl, route, cnt = _mixer_call(_even_kernel, xf, mod[l], seq, weights, scratch, prev)
        else:
            weights = [gmix, gffn, w_pool[i].astype(BF16), pool_scale[i].reshape(1, d)] + route_w
            gd = d // len(POOL_WINDOWS)
            scratch = [pltpu.VMEM((TM + max(POOL_WINDOWS), d - g * gd), F32)
                       for g in range(len(POOL_WINDOWS))]
            xf, xsl, route, cnt = _mixer_call(_odd_kernel, xf, mod[l], seq, weights, scratch, prev)
        run_tables, ys = _moe(xf, xsl, cnt, l, moe_w_gate, moe_w_up, moe_w_down)
        prev = (run_tables, ys, route, mod[l])
    run_tables, ys, route, mod_l = prev
    return _final(run_tables, ys, xf, route, mod_l, seq, g_final).reshape(bsz, seq, d)
```

```python
import functools

import jax
import jax.numpy as jnp
from jax import lax
from jax.experimental import pallas as pl
from jax.experimental.pallas import tpu as pltpu

F32 = jnp.float32
BF16 = jnp.bfloat16
U32 = jnp.uint32
I32 = jnp.int32

EPS = 1e-6
LANES = 128
SUBLANES = 8
CHUNK = 128
A_HEADS = 8
N_GROUPS = 4
EXPERTS_PER_GROUP = 8
N_EXPERTS = N_GROUPS * EXPERTS_PER_GROUP
POOL_WINDOWS = (2, 4, 8, 16)
CONV_WIDTH = 3
ROUTE_COL0 = N_GROUPS
ROUTE_ROWS = 48

TM = 512
SORT_TM = 256
SECTIONS = TM // SORT_TM
LOCAL_ROWS = 2 * SORT_TM + LANES
LOCAL_PAIRS = LOCAL_ROWS // 2
TILE_PAIRS = SECTIONS * LOCAL_PAIRS
TILE_RUNS = SECTIONS * N_EXPERTS
EXPERT_BLK = 1024
BLK_PAIRS = EXPERT_BLK // 2
BLK_PARTS = 8
MOD_TN = 1536
VMEM_LIMIT = 56 * 1024 * 1024


def _rms(x):
    return x * lax.rsqrt(jnp.mean(x * x, axis=-1, keepdims=True) + EPS)


def _dot(a, b):
    return jnp.dot(a, b, preferred_element_type=F32)


def _store_pairs(ref, base, rows_bf16):
    words = pltpu.bitcast(rows_bf16, U32)
    n_pairs, d = words.shape
    slab = d // LANES
    for c in range(slab):
        ref[pl.ds(base + c, n_pairs, stride=slab), :] = words[:, c * LANES:(c + 1) * LANES]


def _load_pairs(ref, base, n_pairs, d):
    slab = d // LANES
    words = jnp.concatenate(
        [ref[pl.ds(base + c, n_pairs, stride=slab), :] for c in range(slab)], axis=-1)
    return pltpu.bitcast(words, BF16)


def _mod_kernel(c_ref, w_ref, b_ref, o_ref):
    c = c_ref[...]
    ca = c * jax.nn.sigmoid(c)
    o_ref[0] = jnp.dot(ca, w_ref[0], precision=lax.Precision.HIGHEST,
                       preferred_element_type=F32) + b_ref[0]


def _modulation(c, w_ada, b_ada):
    depth, d, n = w_ada.shape
    bsz = c.shape[0]
    return pl.pallas_call(
        _mod_kernel,
        grid=(depth, n // MOD_TN),
        in_specs=[
            pl.BlockSpec((bsz, d), lambda l, j: (0, 0)),
            pl.BlockSpec((1, d, MOD_TN), lambda l, j: (l, 0, j)),
            pl.BlockSpec((1, 1, MOD_TN), lambda l, j: (l, 0, j)),
        ],
        out_specs=pl.BlockSpec((1, bsz, MOD_TN), lambda l, j: (l, 0, j)),
        out_shape=jax.ShapeDtypeStruct((depth, bsz, n), F32),
        compiler_params=pltpu.CompilerParams(
            dimension_semantics=("arbitrary", "arbitrary"),
            vmem_limit_bytes=VMEM_LIMIT),
        name="modulation",
    )(c, w_ada, b_ada.reshape(depth, 1, n))


def _route_and_sort(x_new, mod_ref, gffn_ref, rwt_ref, rbc_ref, earlier_ref, lower_ref,
                    xsl_ref, route_ref, cnt_ref):
    tm = x_new.shape[0]
    sh_f = mod_ref[0, 3:4, :]
    sc_f = mod_ref[0, 4:5, :]
    h2 = _rms(x_new) * (gffn_ref[...] * (1.0 + sc_f)) + sh_f

    rr = ROUTE_ROWS
    hh = h2.astype(BF16)
    logits = lax.dot_general(rwt_ref[...], hh, (((1,), (1,)), ((), ())),
                             preferred_element_type=F32) + rbc_ref[...]

    row = lax.broadcasted_iota(I32, logits.shape, 0).astype(F32)
    neg = jnp.float32(-jnp.inf)
    big = jnp.float32(LANES)

    gl = jnp.where(row < N_GROUPS, logits, neg)
    gmax = jnp.max(gl, axis=0, keepdims=True)
    g_sel = jnp.min(jnp.where(gl == gmax, row, big), axis=0, keepdims=True)
    g_w = 1.0 / jnp.sum(jnp.exp(gl - gmax), axis=0, keepdims=True)

    lo = ROUTE_COL0 + EXPERTS_PER_GROUP * g_sel
    el = jnp.where((row >= lo) & (row < lo + EXPERTS_PER_GROUP), logits, neg)
    m1 = jnp.max(el, axis=0, keepdims=True)
    i1 = jnp.min(jnp.where(el == m1, row, big), axis=0, keepdims=True)
    el2 = jnp.where(row == i1, neg, el)
    m2 = jnp.max(el2, axis=0, keepdims=True)
    i2 = jnp.min(jnp.where(el2 == m2, row, big), axis=0, keepdims=True)
    t = jnp.exp(m2 - m1)
    gate1 = g_w / (1.0 + t)
    gate2 = g_w * t / (1.0 + t)

    sections = [slice(s * SORT_TM, (s + 1) * SORT_TM) for s in range(tm // SORT_TM)]
    is1 = row == i1
    is2 = row == i2
    onehot = jnp.where(is1 | is2, 1.0, 0.0)
    before = _dot(onehot.astype(BF16), earlier_ref[...])
    lane = lax.broadcasted_iota(I32, (rr, LANES), 1)
    cnts = [jnp.sum(onehot[:, cols], axis=1, keepdims=True) for cols in sections]
    pairs = jnp.zeros((rr, LANES), F32)
    for s, cnt in enumerate(cnts):
        cnt_ref[s] = jnp.broadcast_to(cnt, (rr, LANES))
        pairs = jnp.where(lane == s, jnp.floor((cnt + 1.0) * 0.5), pairs)
    pairs = jnp.concatenate([pairs, jnp.zeros((LANES - rr, LANES), F32)], axis=0)
    pair_start = jnp.dot(lower_ref[...], pairs, precision=lax.Precision.HIGHEST,
                         preferred_element_type=F32)
    pos = jnp.concatenate([before[:, cols] + 2.0 * pair_start[0:rr, s:s + 1]
                           for s, cols in enumerate(sections)], axis=1)
    pos1 = jnp.sum(jnp.where(is1, pos, 0.0), axis=0, keepdims=True)
    pos2 = jnp.sum(jnp.where(is2, pos, 0.0), axis=0, keepdims=True)

    r8 = lax.broadcasted_iota(I32, (SUBLANES, tm), 0)
    route_t = jnp.where(r8 == 0, pos1, 0.0)
    route_t = jnp.where(r8 == 1, pos2, route_t)
    route_t = jnp.where(r8 == 4, gate1, route_t)
    route_t = jnp.where(r8 == 5, gate2, route_t)
    route_t = jnp.concatenate([route_t, jnp.zeros((LANES - SUBLANES, tm), F32)], axis=0)
    route_ref[...] = route_t.T

    srow = lax.broadcasted_iota(I32, (LOCAL_ROWS, SORT_TM), 0).astype(F32)
    section_rows = LOCAL_PAIRS * (x_new.shape[1] // LANES)
    for s, cols in enumerate(sections):
        perm = jnp.where((srow == pos1[:, cols]) | (srow == pos2[:, cols]), 1.0, 0.0)
        _store_pairs(xsl_ref, s * section_rows, _dot(perm.astype(BF16), hh[cols]).astype(BF16))


def _rows(first_row, n_rows, unit):
    return pl.ds(pl.multiple_of(first_row, unit), n_rows)


def _unpermute(tables, ys_hbm, route_ref, ysl_scr, sem, shape):
    len_ref, loc_ref, glob_ref, tot_ref = tables
    i = pl.program_id(0)
    tm, d = shape
    slab = d // LANES

    def run_copy(tile, glob, loc, rows):
        slot = tile % 2
        return pltpu.make_async_copy(ys_hbm.at[_rows(glob, rows, slab)],
                                     ysl_scr.at[slot, _rows(loc, rows, slab)], sem.at[slot])

    def fetch(tile):
        def body(r, carry):
            k = tile * TILE_RUNS + r
            n = len_ref[k]

            @pl.when(n > 0)
            def _():
                run_copy(tile, glob_ref[k], loc_ref[k], n).start()
            return carry
        lax.fori_loop(0, TILE_RUNS, body, 0, unroll=4)

    @pl.when(i == 0)
    def _():
        ysl_scr[...] = jnp.zeros_like(ysl_scr)
        fetch(i)

    @pl.when(i + 1 < pl.num_programs(0))
    def _():
        fetch(i + 1)

    @pl.when(tot_ref[i] > 0)
    def _():
        run_copy(i, 0, 0, tot_ref[i]).wait()

    ysl_ref = ysl_scr.at[i % 2]
    srow = lax.broadcasted_iota(I32, (SORT_TM, LOCAL_ROWS), 1).astype(F32)
    y_sections = []
    for s in range(tm // SORT_TM):
        rows = slice(s * SORT_TM, (s + 1) * SORT_TM)
        ysl = _load_pairs(ysl_ref, s * LOCAL_PAIRS * slab, LOCAL_PAIRS, d)
        sel = jnp.concatenate(
            [jnp.where(srow == route_ref[rows, k:k + 1], 1.0, 0.0).astype(BF16) for k in range(2)],
            axis=0)
        picked = _dot(sel, ysl)
        y_sections.append(route_ref[rows, 4:5] * picked[0:SORT_TM]
                          + route_ref[rows, 5:6] * picked[SORT_TM:2 * SORT_TM])
    return jnp.concatenate(y_sections, axis=0)


def _split_prev(fused, refs):
    if not fused:
        return None, refs
    tables, (ys_hbm, route_ref, mod_ref), (ysl_scr, sem) = refs[:4], refs[4:7], refs[-2:]
    return (tables, ys_hbm, route_ref, mod_ref, ysl_scr, sem), refs[7:-2]


def _mixer_input(prev, x_ref):
    if prev is None:
        return x_ref[...]
    tables, ys_hbm, route_ref, mod_ref, ysl_scr, sem = prev
    y = _unpermute(tables, ys_hbm, route_ref, ysl_scr, sem, x_ref.shape)
    return x_ref[...] + mod_ref[0, 5:6, :] * y


def _even_kernel(tiles_per_seq, fused, *refs):
    prev, refs = _split_prev(fused, refs)
    (x_ref, mod_ref, gmix_ref, gffn_ref, win_ref, gv_ref, ws_ref, bs_ref,
     cw_ref, wout_ref, rwt_ref, rbc_ref, earlier_ref, lower_ref,
     xo_ref, xsl_ref, route_ref, cnt_ref,
     zc_scr) = refs
    i = pl.program_id(0)
    tm = x_ref.shape[0]
    aw = gv_ref.shape[0]
    hd = aw // A_HEADS
    n_chunks = tm // CHUNK

    @pl.when(i == 0)
    def _():
        zc_scr[...] = jnp.zeros_like(zc_scr)

    x = _mixer_input(prev, x_ref)
    sh_m = mod_ref[0, 0:1, :]
    sc_m = mod_ref[0, 1:2, :]
    g_m = mod_ref[0, 2:3, :]
    h = _rms(x) * (gmix_ref[...] * (1.0 + sc_m)) + sh_m
    z = _dot(h.astype(BF16), win_ref[...])
    u = z[:, 0:aw]
    v = z[:, aw:2 * aw]
    b_gate = z[:, 2 * aw:3 * aw]
    c_gate = z[:, 3 * aw:4 * aw]
    x_in = z[:, 4 * aw:5 * aw]

    v_t = v.T
    row = lax.broadcasted_iota(I32, (CHUNK, CHUNK), 0)
    col = lax.broadcasted_iota(I32, (CHUNK, CHUNK), 1)
    causal = col <= row
    head_rows = []
    for hh in range(A_HEADS):
        vh = v_t[hh * hd:(hh + 1) * hd, :]
        msv = jnp.mean(vh * vh, axis=0, keepdims=True)
        vn = (vh * lax.rsqrt(msv + EPS) * gv_ref[hh * hd:(hh + 1) * hd, :]).astype(BF16)
        lhs = jnp.concatenate(
            [vn[:, c * CHUNK:(c + 1) * CHUNK] for c in range(n_chunks)], axis=0)
        w_m = jnp.where(causal, ws_ref[hh], 0.0).astype(BF16)
        sv_h = lax.dot_general(lhs, w_m, (((1,), (1,)), ((), ())),
                               preferred_element_type=F32)
        sv_h = sv_h + bs_ref[hh]
        head_rows.append(jnp.concatenate(
            [sv_h[c * hd:(c + 1) * hd, :] for c in range(n_chunks)], axis=1))
    sv = jnp.concatenate(head_rows, axis=0).T
    y_a = u * sv

    zc = c_gate * x_in
    first = (i % tiles_per_seq) == 0
    halo = zc_scr[tm:tm + SUBLANES, :]
    zc_scr[0:SUBLANES, :] = jnp.where(first, 0.0, halo)
    zc_scr[SUBLANES:SUBLANES + tm, :] = zc
    conv = cw_ref[2:3, :] * zc
    for k in range(CONV_WIDTH - 1):
        shift = CONV_WIDTH - 1 - k
        conv = conv + cw_ref[k:k + 1, :] * zc_scr[SUBLANES - shift:SUBLANES - shift + tm, :]
    y_b = b_gate * conv

    y = _dot(y_a.astype(BF16), wout_ref[0:aw, :]) + _dot(y_b.astype(BF16), wout_ref[aw:, :])
    x_new = x + g_m * y
    xo_ref[...] = x_new
    _route_and_sort(x_new, mod_ref, gffn_ref, rwt_ref, rbc_ref, earlier_ref, lower_ref,
                    xsl_ref, route_ref, cnt_ref)


def _odd_kernel(tiles_per_seq, fused, *refs):
    prev, refs = _split_prev(fused, refs)
    (x_ref, mod_ref, gmix_ref, gffn_ref, wpool_ref, pscale_ref,
     rwt_ref, rbc_ref, earlier_ref, lower_ref,
     xo_ref, xsl_ref, route_ref, cnt_ref) = refs[:14]
    level_scrs = refs[14:]
    i = pl.program_id(0)
    tm, d = x_ref.shape
    halo_rows = max(POOL_WINDOWS)
    gd = d // len(POOL_WINDOWS)

    @pl.when(i == 0)
    def _():
        for scr in level_scrs:
            scr[...] = jnp.zeros_like(scr)

    x = _mixer_input(prev, x_ref)
    sh_m = mod_ref[0, 0:1, :]
    sc_m = mod_ref[0, 1:2, :]
    g_m = mod_ref[0, 2:3, :]
    h = _rms(x) * (gmix_ref[...] * (1.0 + sc_m)) + sh_m

    tile_in_seq = i % tiles_per_seq
    first = tile_in_seq == 0

    sums = h
    window_sums = []
    for g, scr in enumerate(level_scrs):
        tail = scr[tm:tm + halo_rows, :]
        scr[0:halo_rows, :] = jnp.where(first, 0.0, tail)
        scr[halo_rows:halo_rows + tm, :] = sums
        lag = POOL_WINDOWS[g] // 2
        sums = sums + scr[halo_rows - lag:halo_rows - lag + tm, :]
        window_sums.append(sums[:, 0:gd])
        if g + 1 < len(level_scrs):
            sums = sums[:, gd:]

    pos = (tile_in_seq * tm + lax.broadcasted_iota(I32, (tm, 1), 0)).astype(F32)
    outs = []
    for g, win in enumerate(POOL_WINDOWS):
        cs = slice(g * gd, (g + 1) * gd)
        count = jnp.minimum(pos + 1.0, jnp.float32(win))
        pooled = window_sums[g] / count - h[:, cs]
        outs.append(_dot(pooled.astype(BF16), wpool_ref[g]))
    y = jnp.concatenate(outs, axis=-1) * pscale_ref[...]
    x_new = x + g_m * y
    xo_ref[...] = x_new
    _route_and_sort(x_new, mod_ref, gffn_ref, rwt_ref, rbc_ref, earlier_ref, lower_ref,
                    xsl_ref, route_ref, cnt_ref)


def _mixer_call(kernel_fn, x, mod_l, seq, weights, scratch, prev):
    t_tok, d = x.shape
    n_tiles = t_tok // TM
    tiles_per_seq = seq // TM
    n_slabs = d // LANES

    def const_spec(a):
        return pl.BlockSpec(a.shape, lambda i, *_, nd=a.ndim: (0,) * nd)

    def mod_spec(m):
        return pl.BlockSpec((1,) + m.shape[1:], lambda i, *_: (i // tiles_per_seq, 0, 0))

    tables, prev_inputs, prev_specs, prev_scratch = (), (), [], []
    if prev is not None:
        tables, ys, route_prev, mod_prev = prev
        prev_inputs = (ys, route_prev, mod_prev)
        prev_specs = [pl.BlockSpec(memory_space=pl.ANY),
                      pl.BlockSpec((TM, LANES), lambda i, *_: (i, 0)),
                      mod_spec(mod_prev)]
        prev_scratch = [pltpu.VMEM((2, TILE_PAIRS * n_slabs, LANES), U32),
                        pltpu.SemaphoreType.DMA((2,))]
    in_specs = prev_specs + [
        pl.BlockSpec((TM, d), lambda i, *_: (i, 0)),
        mod_spec(mod_l),
    ] + [const_spec(w) for w in weights]
    out_shape = (
        jax.ShapeDtypeStruct((t_tok, d), F32),
        jax.ShapeDtypeStruct((n_tiles * TILE_PAIRS * n_slabs, LANES), U32),
        jax.ShapeDtypeStruct((t_tok, LANES), F32),
        jax.ShapeDtypeStruct((n_tiles * SECTIONS, ROUTE_ROWS, LANES), F32),
    )
    out_specs = (
        pl.BlockSpec((TM, d), lambda i, *_: (i, 0)),
        pl.BlockSpec((TILE_PAIRS * n_slabs, LANES), lambda i, *_: (i, 0)),
        pl.BlockSpec((TM, LANES), lambda i, *_: (i, 0)),
        pl.BlockSpec((SECTIONS, ROUTE_ROWS, LANES), lambda i, *_: (i, 0, 0)),
    )
    return pl.pallas_call(
        functools.partial(kernel_fn, tiles_per_seq, prev is not None),
        grid_spec=pltpu.PrefetchScalarGridSpec(
            num_scalar_prefetch=len(tables),
            grid=(n_tiles,),
            in_specs=in_specs,
            out_specs=out_specs,
            scratch_shapes=scratch + prev_scratch,
        ),
        out_shape=out_shape,
        compiler_params=pltpu.CompilerParams(
            dimension_semantics=("arbitrary",), vmem_limit_bytes=VMEM_LIMIT),
        name=kernel_fn.__name__.strip("_"),
    )(*tables, *prev_inputs, x, mod_l, *weights)


def _expert_kernel(layer, blk_expert_ref, n_valid_ref, next_ref, slot_ref,
                   blk_first_ref, blk_last_ref, blk_fill_ref, run_len_ref, run_src_ref, run_dst_ref,
                   xsl_hbm, wg_hbm, wu_hbm, wd_hbm, ys_ref,
                   x_scr, zero_scr, wg_stage, wu_stage, wd_stage, wg_scr, wu_scr, wd_scr,
                   x_sem, sem):
    j = pl.program_id(0)
    n_valid = n_valid_ref[0]
    expert = blk_expert_ref[j]
    prev = blk_expert_ref[jnp.maximum(j - 1, 0)]
    d = wg_scr.shape[0]
    slab = d // LANES
    blk_rows = BLK_PAIRS * slab

    def fetch(blk):
        slot = blk % 2
        row0 = blk * blk_rows

        def copy_run(q):
            dst = run_dst_ref[q]
            lo = jnp.maximum(dst, row0)
            hi = jnp.minimum(dst + run_len_ref[q], row0 + blk_rows)

            @pl.when(hi > lo)
            def _():
                pltpu.make_async_copy(
                    xsl_hbm.at[_rows(run_src_ref[q] + (lo - dst), hi - lo, slab)],
                    x_scr.at[slot, _rows(lo - row0, hi - lo, slab)], x_sem.at[slot]).start()

        first = blk_first_ref[blk]

        def body(t, carry):
            copy_run(first + 2 * t)
            copy_run(first + 2 * t + 1)
            return carry

        lax.fori_loop(0, (blk_last_ref[blk] - first + 1) // 2, body, 0)
        fill = blk_fill_ref[blk]

        @pl.when(fill > 0)
        def _():
            pltpu.make_async_copy(zero_scr.at[_rows(0, fill, slab)],
                                  x_scr.at[slot, _rows(blk_rows - fill, fill, slab)],
                                  x_sem.at[slot]).start()

    @pl.when(j == 0)
    def _():
        zero_scr[...] = jnp.zeros_like(zero_scr)
        fetch(j)

    @pl.when(j + 1 < n_valid)
    def _():
        fetch(j + 1)

    def weight_copies(e, slot):
        return [pltpu.make_async_copy(hbm.at[layer, e], stage.at[slot], sem.at[slot])
                for hbm, stage in ((wg_hbm, wg_stage), (wu_hbm, wu_stage), (wd_hbm, wd_stage))]

    @pl.when(j == 0)
    def _():
        for cp in weight_copies(expert, slot_ref[expert]):
            cp.start()

    @pl.when((j == 0) | (expert != prev))
    def _():
        slot = slot_ref[expert]
        for cp in weight_copies(expert, slot):
            cp.wait()
        nxt = next_ref[expert]

        @pl.when(nxt != expert)
        def _():
            for cp in weight_copies(nxt, 1 - slot):
                cp.start()

        wg_scr[...] = wg_stage[slot].astype(BF16)
        wu_scr[...] = wu_stage[slot].astype(BF16)
        wd_scr[...] = wd_stage[slot].astype(BF16)

    slot = j % 2

    @pl.when(j < n_valid)
    def _():
        pltpu.make_async_copy(zero_scr, x_scr.at[slot], x_sem.at[slot]).wait()

    part_pairs = BLK_PAIRS // BLK_PARTS
    part_rows = part_pairs * slab
    n_parts = jnp.where(j < n_valid,
                        (blk_rows - blk_fill_ref[j] + part_rows - 1) // part_rows, 0)
    for p in range(1, BLK_PARTS + 1):
        @pl.when(n_parts == p)
        def _(p=p):
            xb = _load_pairs(x_scr.at[slot], 0, p * part_pairs, d)
            a = _dot(xb, wg_scr[...])
            b = _dot(xb, wu_scr[...])
            hm = (a * jax.nn.sigmoid(a) * b).astype(BF16)
            _store_pairs(ys_ref, 0, _dot(hm, wd_scr[...]).astype(BF16))
            if p < BLK_PARTS:
                ys_ref[p * part_rows:, :] = jnp.zeros((blk_rows - p * part_rows, LANES), U32)

    @pl.when(n_parts == 0)
    def _():
        ys_ref[...] = jnp.zeros_like(ys_ref)


def _experts(tables, xsl, n_blocks, layer, w_gate, w_up, w_down):
    _, _, d, de = w_gate.shape
    blk_rows = BLK_PAIRS * (d // LANES)
    return pl.pallas_call(
        functools.partial(_expert_kernel, layer),
        grid_spec=pltpu.PrefetchScalarGridSpec(
            num_scalar_prefetch=len(tables),
            grid=(n_blocks,),
            in_specs=[pl.BlockSpec(memory_space=pl.ANY)] * 4,
            out_specs=pl.BlockSpec((blk_rows, LANES), lambda j, *_: (j, 0)),
            scratch_shapes=[
                pltpu.VMEM((2, blk_rows, LANES), U32), pltpu.VMEM((blk_rows, LANES), U32),
                pltpu.VMEM((2, d, de), F32), pltpu.VMEM((2, d, de), F32),
                pltpu.VMEM((2, de, d), F32),
                pltpu.VMEM((d, de), BF16), pltpu.VMEM((d, de), BF16), pltpu.VMEM((de, d), BF16),
                pltpu.SemaphoreType.DMA((2,)), pltpu.SemaphoreType.DMA((2,)),
            ],
        ),
        out_shape=jax.ShapeDtypeStruct((n_blocks * blk_rows, LANES), U32),
        compiler_params=pltpu.CompilerParams(
            dimension_semantics=("arbitrary",), vmem_limit_bytes=VMEM_LIMIT),
        name="experts",
    )(*tables, xsl, w_gate, w_up, w_down)


def _final_kernel(len_ref, loc_ref, glob_ref, tot_ref, ys_hbm, x_ref, route_ref,
                  mod_ref, gfin_ref, xo_ref, ysl_scr, sem):
    prev = ((len_ref, loc_ref, glob_ref, tot_ref), ys_hbm, route_ref, mod_ref, ysl_scr, sem)
    xo_ref[...] = _rms(_mixer_input(prev, x_ref)) * gfin_ref[...]


def _final(tables, ys, x, route, mod_l, seq, g_final):
    t_tok, d = x.shape
    tiles_per_seq = seq // TM
    return pl.pallas_call(
        _final_kernel,
        grid_spec=pltpu.PrefetchScalarGridSpec(
            num_scalar_prefetch=4,
            grid=(t_tok // TM,),
            in_specs=[
                pl.BlockSpec(memory_space=pl.ANY),
                pl.BlockSpec((TM, d), lambda i, *_: (i, 0)),
                pl.BlockSpec((TM, LANES), lambda i, *_: (i, 0)),
                pl.BlockSpec((1,) + mod_l.shape[1:], lambda i, *_: (i // tiles_per_seq, 0, 0)),
                pl.BlockSpec((1, d), lambda i, *_: (0, 0)),
            ],
            out_specs=pl.BlockSpec((TM, d), lambda i, *_: (i, 0)),
            scratch_shapes=[pltpu.VMEM((2, TILE_PAIRS * (d // LANES), LANES), U32),
                            pltpu.SemaphoreType.DMA((2,))],
        ),
        out_shape=jax.ShapeDtypeStruct((t_tok, d), F32),
        compiler_params=pltpu.CompilerParams(
            dimension_semantics=("arbitrary",), vmem_limit_bytes=VMEM_LIMIT),
        name="final",
    )(*tables, ys, x, route, mod_l, g_final)


def _moe(x, xsl, cnt, layer, w_gate, w_up, w_down):
    t_tok = x.shape[0]
    n_tiles = t_tok // TM
    n_blocks = -(-(2 * t_tok + n_tiles * TILE_RUNS) // EXPERT_BLK) + N_EXPERTS

    counts = cnt[:, ROUTE_COL0:ROUTE_COL0 + N_EXPERTS, 0].astype(I32)
    run_len = (counts + 1) // 2
    section_base = (jnp.arange(counts.shape[0], dtype=I32) % SECTIONS) * LOCAL_PAIRS
    run_loc = jnp.cumsum(run_len, axis=1) - run_len + section_base[:, None]
    seg_len = jnp.sum(run_len, axis=0)
    seg_pad = (seg_len + BLK_PAIRS - 1) // BLK_PAIRS * BLK_PAIRS
    seg_end = jnp.cumsum(seg_pad)
    seg_start = seg_end - seg_pad
    run_glob = seg_start[None, :] + jnp.cumsum(run_len, axis=0) - run_len
    tile_tot = jnp.sum(run_len.reshape(n_tiles, TILE_RUNS), axis=1)
    slab = x.shape[1] // LANES
    run_tables = tuple((a.reshape(-1) * slab).astype(I32)
                       for a in (run_len, run_loc, run_glob, tile_tot))
    n_valid = (seg_end[-1:] // BLK_PAIRS).astype(I32)
    blk_pair0 = jnp.arange(n_blocks, dtype=I32) * BLK_PAIRS
    blk_expert = jnp.sum((seg_end[None, :] <= blk_pair0[:, None]).astype(I32), axis=1)
    section_pair0 = jnp.arange(counts.shape[0], dtype=I32) * LOCAL_PAIRS
    run_src = jnp.cumsum(run_len, axis=1) - run_len + section_pair0[:, None]
    em_len, em_src, em_dst = (
        jnp.concatenate([a.T.reshape(-1), jnp.full((1,), end, I32)])
        for a, end in ((run_len, 0), (run_src, 0), (run_glob, n_blocks * BLK_PAIRS)))
    blk_first = jnp.sum(((em_dst + em_len)[None, :] <= blk_pair0[:, None]).astype(I32), axis=1)
    blk_last = jnp.sum((em_dst[None, :] < (blk_pair0 + BLK_PAIRS)[:, None]).astype(I32), axis=1)
    seg_stop = jnp.sum(jnp.where(
        jnp.arange(N_EXPERTS, dtype=I32)[None, :] == blk_expert[:, None],
        (seg_start + seg_len)[None, :], 0), axis=1)
    blk_fill = BLK_PAIRS - jnp.clip(seg_stop - blk_pair0, 0, BLK_PAIRS)
    pull_tables = tuple((a * slab).astype(I32) for a in (blk_fill, em_len, em_src, em_dst))
    experts = jnp.arange(N_EXPERTS, dtype=I32)
    used = seg_len > 0
    last_used = jnp.max(jnp.where(used, experts, 0))
    blk_expert = jnp.minimum(blk_expert, last_used).astype(I32)
    later = jnp.where((experts[None, :] > experts[:, None]) & used[None, :],
                      experts[None, :], N_EXPERTS)
    next_used = jnp.min(later, axis=1)
    next_used = jnp.where(next_used == N_EXPERTS, experts, next_used).astype(I32)
    slot_of = ((jnp.cumsum(used.astype(I32)) - 1) % 2).astype(I32)

    expert_tables = ((blk_expert, n_valid, next_used, slot_of,
                      blk_first.astype(I32), blk_last.astype(I32)) + pull_tables)
    return run_tables, _experts(expert_tables, xsl, n_blocks, layer, w_gate, w_up, w_down)


def kernel(x, c, w_ada, b_ada, norm_mix_g, norm_ffn_g, w_in_even, sgu_norm_g, w_spatial, b_spatial, conv_w, w_out_even, w_pool, pool_scale, w_group_router, b_group_router, w_expert_router, b_expert_router, moe_w_gate, moe_w_up, moe_w_down, final_norm_g):
    bsz, seq, d = x.shape
    depth = w_ada.shape[0]
    t_tok = bsz * seq
    assert seq % TM == 0 and d % LANES == 0 and w_spatial.shape[-1] == CHUNK
    assert all(w == 2 ** (g + 1) for g, w in enumerate(POOL_WINDOWS))

    mod = _modulation(c, w_ada, b_ada).reshape(depth, bsz, 6, d)
    tok = jnp.arange(TM)
    earlier = ((tok[:, None] < tok[None, :])
               & (tok[:, None] // SORT_TM == tok[None, :] // SORT_TM)).astype(BF16)
    lower = jnp.tril(jnp.ones((LANES, LANES), F32), -1)
    g_final = final_norm_g.reshape(1, d)

    xf = x.reshape(t_tok, d)
    prev = None
    for l in range(depth):
        i = l // 2
        rw = jnp.concatenate([w_group_router[l], w_expert_router[l]], axis=1).T
        rw = jnp.pad(rw, ((0, ROUTE_ROWS - rw.shape[0]), (0, 0))).astype(BF16)
        rb = jnp.concatenate([b_group_router[l], b_expert_router[l]])
        rb = jnp.pad(rb, (0, ROUTE_ROWS - rb.shape[0])).reshape(ROUTE_ROWS, 1)
        route_w = [rw, rb, earlier, lower]
        gmix = norm_mix_g[l].reshape(1, d)
        gffn = norm_ffn_g[l].reshape(1, d)
        if l % 2 == 0:
            aw = sgu_norm_g.shape[1]
            weights = [gmix, gffn, w_in_even[i].astype(BF16), sgu_norm_g[i].reshape(aw, 1),
                       w_spatial[i], b_spatial[i].reshape(A_HEADS, 1, CHUNK), conv_w[i],
                       w_out_even[i].astype(BF16)] + route_w
            scratch = [pltpu.VMEM((TM + SUBLANES, conv_w.shape[-1]), F32)]
            xf, xsl, route, cnt = _mixer_call(_even_kernel, xf, mod[l], seq, weights, scratch, prev)
        else:
            weights = [gmix, gffn, w_pool[i].astype(BF16), pool_scale[i].reshape(1, d)] + route_w
            gd = d // len(POOL_WINDOWS)
            scratch = [pltpu.VMEM((TM + max(POOL_WINDOWS), d - g * gd), F32)
                       for g in range(len(POOL_WINDOWS))]
            xf, xsl, route, cnt = _mixer_call(_odd_kernel, xf, mod[l], seq, weights, scratch, prev)
        run_tables, ys = _moe(xf, xsl, cnt, l, moe_w_gate, moe_w_up, moe_w_down)
        prev = (run_tables, ys, route, mod[l])
    run_tables, ys, route, mod_l = prev
    return _final(run_tables, ys, xf, route, mod_l, seq, g_final).reshape(bsz, seq, d)
```

```python
import functools

import jax
import jax.numpy as jnp
from jax import lax
from jax.experimental import pallas as pl
from jax.experimental.pallas import tpu as pltpu

F32 = jnp.float32
BF16 = jnp.bfloat16
U32 = jnp.uint32
I32 = jnp.int32

EPS = 1e-6
LANES = 128
SUBLANES = 8
CHUNK = 128
A_HEADS = 8
N_GROUPS = 4
EXPERTS_PER_GROUP = 8
N_EXPERTS = N_GROUPS * EXPERTS_PER_GROUP
POOL_WINDOWS = (2, 4, 8, 16)
CONV_WIDTH = 3
ROUTE_COL0 = N_GROUPS
ROUTE_ROWS = 48

TM = 512
SORT_TM = 256
SECTIONS = TM // SORT_TM
LOCAL_ROWS = 2 * SORT_TM + LANES
LOCAL_PAIRS = LOCAL_ROWS // 2
TILE_PAIRS = SECTIONS * LOCAL_PAIRS
TILE_RUNS = SECTIONS * N_EXPERTS
EXPERT_BLK = 2048
BLK_PAIRS = EXPERT_BLK // 2
BLK_PARTS = 8
MOD_TN = 1536
VMEM_LIMIT = 56 * 1024 * 1024


def _rms(x):
    return x * lax.rsqrt(jnp.mean(x * x, axis=-1, keepdims=True) + EPS)


def _dot(a, b):
    return jnp.dot(a, b, preferred_element_type=F32)


def _store_pairs(ref, base, rows_bf16):
    words = pltpu.bitcast(rows_bf16, U32)
    n_pairs, d = words.shape
    slab = d // LANES
    for c in range(slab):
        ref[pl.ds(base + c, n_pairs, stride=slab), :] = words[:, c * LANES:(c + 1) * LANES]


def _load_pairs(ref, base, n_pairs, d):
    slab = d // LANES
    words = jnp.concatenate(
        [ref[pl.ds(base + c, n_pairs, stride=slab), :] for c in range(slab)], axis=-1)
    return pltpu.bitcast(words, BF16)


def _mod_kernel(c_ref, w_ref, b_ref, o_ref):
    c = c_ref[...]
    ca = c * jax.nn.sigmoid(c)
    o_ref[0] = jnp.dot(ca, w_ref[0], precision=lax.Precision.HIGHEST,
                       preferred_element_type=F32) + b_ref[0]


def _modulation(c, w_ada, b_ada):
    depth, d, n = w_ada.shape
    bsz = c.shape[0]
    return pl.pallas_call(
        _mod_kernel,
        grid=(depth, n // MOD_TN),
        in_specs=[
            pl.BlockSpec((bsz, d), lambda l, j: (0, 0)),
            pl.BlockSpec((1, d, MOD_TN), lambda l, j: (l, 0, j)),
            pl.BlockSpec((1, 1, MOD_TN), lambda l, j: (l, 0, j)),
        ],
        out_specs=pl.BlockSpec((1, bsz, MOD_TN), lambda l, j: (l, 0, j)),
        out_shape=jax.ShapeDtypeStruct((depth, bsz, n), F32),
        compiler_params=pltpu.CompilerParams(
            dimension_semantics=("arbitrary", "arbitrary"),
            vmem_limit_bytes=VMEM_LIMIT),
        name="modulation",
    )(c, w_ada, b_ada.reshape(depth, 1, n))


def _route_and_sort(x_new, mod_ref, gffn_ref, rwt_ref, rbc_ref, earlier_ref, lower_ref,
                    xsl_ref, route_ref, cnt_ref):
    tm = x_new.shape[0]
    sh_f = mod_ref[0, 3:4, :]
    sc_f = mod_ref[0, 4:5, :]
    h2 = _rms(x_new) * (gffn_ref[...] * (1.0 + sc_f)) + sh_f

    rr = ROUTE_ROWS
    hh = h2.astype(BF16)
    logits = lax.dot_general(rwt_ref[...], hh, (((1,), (1,)), ((), ())),
                             preferred_element_type=F32) + rbc_ref[...]

    row = lax.broadcasted_iota(I32, logits.shape, 0).astype(F32)
    neg = jnp.float32(-jnp.inf)
    big = jnp.float32(LANES)

    gl = jnp.where(row < N_GROUPS, logits, neg)
    gmax = jnp.max(gl, axis=0, keepdims=True)
    g_sel = jnp.min(jnp.where(gl == gmax, row, big), axis=0, keepdims=True)
    g_w = 1.0 / jnp.sum(jnp.exp(gl - gmax), axis=0, keepdims=True)

    lo = ROUTE_COL0 + EXPERTS_PER_GROUP * g_sel
    el = jnp.where((row >= lo) & (row < lo + EXPERTS_PER_GROUP), logits, neg)
    m1 = jnp.max(el, axis=0, keepdims=True)
    i1 = jnp.min(jnp.where(el == m1, row, big), axis=0, keepdims=True)
    el2 = jnp.where(row == i1, neg, el)
    m2 = jnp.max(el2, axis=0, keepdims=True)
    i2 = jnp.min(jnp.where(el2 == m2, row, big), axis=0, keepdims=True)
    t = jnp.exp(m2 - m1)
    gate1 = g_w / (1.0 + t)
    gate2 = g_w * t / (1.0 + t)

    sections = [slice(s * SORT_TM, (s + 1) * SORT_TM) for s in range(tm // SORT_TM)]
    is1 = row == i1
    is2 = row == i2
    onehot = jnp.where(is1 | is2, 1.0, 0.0)
    before = _dot(onehot.astype(BF16), earlier_ref[...])
    lane = lax.broadcasted_iota(I32, (rr, LANES), 1)
    cnts = [jnp.sum(onehot[:, cols], axis=1, keepdims=True) for cols in sections]
    pairs = jnp.zeros((rr, LANES), F32)
    for s, cnt in enumerate(cnts):
        cnt_ref[s] = jnp.broadcast_to(cnt, (rr, LANES))
        pairs = jnp.where(lane == s, jnp.floor((cnt + 1.0) * 0.5), pairs)
    pairs = jnp.concatenate([pairs, jnp.zeros((LANES - rr, LANES), F32)], axis=0)
    pair_start = jnp.dot(lower_ref[...], pairs, precision=lax.Precision.HIGHEST,
                         preferred_element_type=F32)
    pos = jnp.concatenate([before[:, cols] + 2.0 * pair_start[0:rr, s:s + 1]
                           for s, cols in enumerate(sections)], axis=1)
    pos1 = jnp.sum(jnp.where(is1, pos, 0.0), axis=0, keepdims=True)
    pos2 = jnp.sum(jnp.where(is2, pos, 0.0), axis=0, keepdims=True)

    r8 = lax.broadcasted_iota(I32, (SUBLANES, tm), 0)
    route_t = jnp.where(r8 == 0, pos1, 0.0)
    route_t = jnp.where(r8 == 1, pos2, route_t)
    route_t = jnp.where(r8 == 4, gate1, route_t)
    route_t = jnp.where(r8 == 5, gate2, route_t)
    route_t = jnp.concatenate([route_t, jnp.zeros((LANES - SUBLANES, tm), F32)], axis=0)
    route_ref[...] = route_t.T

    srow = lax.broadcasted_iota(I32, (LOCAL_ROWS, SORT_TM), 0).astype(F32)
    section_rows = LOCAL_PAIRS * (x_new.shape[1] // LANES)
    for s, cols in enumerate(sections):
        perm = jnp.where((srow == pos1[:, cols]) | (srow == pos2[:, cols]), 1.0, 0.0)
        _store_pairs(xsl_ref, s * section_rows, _dot(perm.astype(BF16), hh[cols]).astype(BF16))


def _rows(first_row, n_rows, unit):
    return pl.ds(pl.multiple_of(first_row, unit), n_rows)


def _unpermute(tables, ys_hbm, route_ref, ysl_scr, sem, shape):
    len_ref, loc_ref, glob_ref, tot_ref = tables
    i = pl.program_id(0)
    tm, d = shape
    slab = d // LANES

    def run_copy(tile, glob, loc, rows):
        slot = tile % 2
        return pltpu.make_async_copy(ys_hbm.at[_rows(glob, rows, slab)],
                                     ysl_scr.at[slot, _rows(loc, rows, slab)], sem.at[slot])

    def fetch(tile):
        def body(r, carry):
            k = tile * TILE_RUNS + r
            n = len_ref[k]

            @pl.when(n > 0)
            def _():
                run_copy(tile, glob_ref[k], loc_ref[k], n).start()
            return carry
        lax.fori_loop(0, TILE_RUNS, body, 0, unroll=4)

    @pl.when(i == 0)
    def _():
        ysl_scr[...] = jnp.zeros_like(ysl_scr)
        fetch(i)

    @pl.when(i + 1 < pl.num_programs(0))
    def _():
        fetch(i + 1)

    @pl.when(tot_ref[i] > 0)
    def _():
        run_copy(i, 0, 0, tot_ref[i]).wait()

    ysl_ref = ysl_scr.at[i % 2]
    srow = lax.broadcasted_iota(I32, (SORT_TM, LOCAL_ROWS), 1).astype(F32)
    y_sections = []
    for s in range(tm // SORT_TM):
        rows = slice(s * SORT_TM, (s + 1) * SORT_TM)
        ysl = _load_pairs(ysl_ref, s * LOCAL_PAIRS * slab, LOCAL_PAIRS, d)
        sel = jnp.concatenate(
            [jnp.where(srow == route_ref[rows, k:k + 1], 1.0, 0.0).astype(BF16) for k in range(2)],
            axis=0)
        picked = _dot(sel, ysl)
        y_sections.append(route_ref[rows, 4:5] * picked[0:SORT_TM]
                          + route_ref[rows, 5:6] * picked[SORT_TM:2 * SORT_TM])
    return jnp.concatenate(y_sections, axis=0)


def _split_prev(fused, refs):
    if not fused:
        return None, refs
    tables, (ys_hbm, route_ref, mod_ref), (ysl_scr, sem) = refs[:4], refs[4:7], refs[-2:]
    return (tables, ys_hbm, route_ref, mod_ref, ysl_scr, sem), refs[7:-2]


def _mixer_input(prev, x_ref):
    if prev is None:
        return x_ref[...]
    tables, ys_hbm, route_ref, mod_ref, ysl_scr, sem = prev
    y = _unpermute(tables, ys_hbm, route_ref, ysl_scr, sem, x_ref.shape)
    return x_ref[...] + mod_ref[0, 5:6, :] * y


def _even_kernel(tiles_per_seq, fused, *refs):
    prev, refs = _split_prev(fused, refs)
    (x_ref, mod_ref, gmix_ref, gffn_ref, win_ref, gv_ref, ws_ref, bs_ref,
     cw_ref, wout_ref, rwt_ref, rbc_ref, earlier_ref, lower_ref,
     xo_ref, xsl_ref, route_ref, cnt_ref,
     zc_scr) = refs
    i = pl.program_id(0)
    tm = x_ref.shape[0]
    aw = gv_ref.shape[0]
    hd = aw // A_HEADS
    n_chunks = tm // CHUNK

    @pl.when(i == 0)
    def _():
        zc_scr[...] = jnp.zeros_like(zc_scr)

    x = _mixer_input(prev, x_ref)
    sh_m = mod_ref[0, 0:1, :]
    sc_m = mod_ref[0, 1:2, :]
    g_m = mod_ref[0, 2:3, :]
    h = _rms(x) * (gmix_ref[...] * (1.0 + sc_m)) + sh_m
    z = _dot(h.astype(BF16), win_ref[...])
    u = z[:, 0:aw]
    v = z[:, aw:2 * aw]
    b_gate = z[:, 2 * aw:3 * aw]
    c_gate = z[:, 3 * aw:4 * aw]
    x_in = z[:, 4 * aw:5 * aw]

    v_t = v.T
    row = lax.broadcasted_iota(I32, (CHUNK, CHUNK), 0)
    col = lax.broadcasted_iota(I32, (CHUNK, CHUNK), 1)
    causal = col <= row
    head_rows = []
    for hh in range(A_HEADS):
        vh = v_t[hh * hd:(hh + 1) * hd, :]
        msv = jnp.mean(vh * vh, axis=0, keepdims=True)
        vn = (vh * lax.rsqrt(msv + EPS) * gv_ref[hh * hd:(hh + 1) * hd, :]).astype(BF16)
        lhs = jnp.concatenate(
            [vn[:, c * CHUNK:(c + 1) * CHUNK] for c in range(n_chunks)], axis=0)
        w_m = jnp.where(causal, ws_ref[hh], 0.0).astype(BF16)
        sv_h = lax.dot_general(lhs, w_m, (((1,), (1,)), ((), ())),
                               preferred_element_type=F32)
        sv_h = sv_h + bs_ref[hh]
        head_rows.append(jnp.concatenate(
            [sv_h[c * hd:(c + 1) * hd, :] for c in range(n_chunks)], axis=1))
    sv = jnp.concatenate(head_rows, axis=0).T
    y_a = u * sv

    zc = c_gate * x_in
    first = (i % tiles_per_seq) == 0
    halo = zc_scr[tm:tm + SUBLANES, :]
    zc_scr[0:SUBLANES, :] = jnp.where(first, 0.0, halo)
    zc_scr[SUBLANES:SUBLANES + tm, :] = zc
    conv = cw_ref[2:3, :] * zc
    for k in range(CONV_WIDTH - 1):
        shift = CONV_WIDTH - 1 - k
        conv = conv + cw_ref[k:k + 1, :] * zc_scr[SUBLANES - shift:SUBLANES - shift + tm, :]
    y_b = b_gate * conv

    y = _dot(y_a.astype(BF16), wout_ref[0:aw, :]) + _dot(y_b.astype(BF16), wout_ref[aw:, :])
    x_new = x + g_m * y
    xo_ref[...] = x_new
    _route_and_sort(x_new, mod_ref, gffn_ref, rwt_ref, rbc_ref, earlier_ref, lower_ref,
                    xsl_ref, route_ref, cnt_ref)


def _odd_kernel(tiles_per_seq, fused, *refs):
    prev, refs = _split_prev(fused, refs)
    (x_ref, mod_ref, gmix_ref, gffn_ref, wpool_ref, pscale_ref,
     rwt_ref, rbc_ref, earlier_ref, lower_ref,
     xo_ref, xsl_ref, route_ref, cnt_ref) = refs[:14]
    level_scrs = refs[14:]
    i = pl.program_id(0)
    tm, d = x_ref.shape
    halo_rows = max(POOL_WINDOWS)
    gd = d // len(POOL_WINDOWS)

    @pl.when(i == 0)
    def _():
        for scr in level_scrs:
            scr[...] = jnp.zeros_like(scr)

    x = _mixer_input(prev, x_ref)
    sh_m = mod_ref[0, 0:1, :]
    sc_m = mod_ref[0, 1:2, :]
    g_m = mod_ref[0, 2:3, :]
    h = _rms(x) * (gmix_ref[...] * (1.0 + sc_m)) + sh_m

    tile_in_seq = i % tiles_per_seq
    first = tile_in_seq == 0

    sums = h
    window_sums = []
    for g, scr in enumerate(level_scrs):
        tail = scr[tm:tm + halo_rows, :]
        scr[0:halo_rows, :] = jnp.where(first, 0.0, tail)
        scr[halo_rows:halo_rows + tm, :] = sums
        lag = POOL_WINDOWS[g] // 2
        sums = sums + scr[halo_rows - lag:halo_rows - lag + tm, :]
        window_sums.append(sums[:, 0:gd])
        if g + 1 < len(level_scrs):
            sums = sums[:, gd:]

    pos = (tile_in_seq * tm + lax.broadcasted_iota(I32, (tm, 1), 0)).astype(F32)
    outs = []
    for g, win in enumerate(POOL_WINDOWS):
        cs = slice(g * gd, (g + 1) * gd)
        count = jnp.minimum(pos + 1.0, jnp.float32(win))
        pooled = window_sums[g] / count - h[:, cs]
        outs.append(_dot(pooled.astype(BF16), wpool_ref[g]))
    y = jnp.concatenate(outs, axis=-1) * pscale_ref[...]
    x_new = x + g_m * y
    xo_ref[...] = x_new
    _route_and_sort(x_new, mod_ref, gffn_ref, rwt_ref, rbc_ref, earlier_ref, lower_ref,
                    xsl_ref, route_ref, cnt_ref)


def _mixer_call(kernel_fn, x, mod_l, seq, weights, scratch, prev):
    t_tok, d = x.shape
    n_tiles = t_tok // TM
    tiles_per_seq = seq // TM
    n_slabs = d // LANES

    def const_spec(a):
        return pl.BlockSpec(a.shape, lambda i, *_, nd=a.ndim: (0,) * nd)

    def mod_spec(m):
        return pl.BlockSpec((1,) + m.shape[1:], lambda i, *_: (i // tiles_per_seq, 0, 0))

    tables, prev_inputs, prev_specs, prev_scratch = (), (), [], []
    if prev is not None:
        tables, ys, route_prev, mod_prev = prev
        prev_inputs = (ys, route_prev, mod_prev)
        prev_specs = [pl.BlockSpec(memory_space=pl.ANY),
                      pl.BlockSpec((TM, LANES), lambda i, *_: (i, 0)),
                      mod_spec(mod_prev)]
        prev_scratch = [pltpu.VMEM((2, TILE_PAIRS * n_slabs, LANES), U32),
                        pltpu.SemaphoreType.DMA((2,))]
    in_specs = prev_specs + [
        pl.BlockSpec((TM, d), lambda i, *_: (i, 0)),
        mod_spec(mod_l),
    ] + [const_spec(w) for w in weights]
    out_shape = (
        jax.ShapeDtypeStruct((t_tok, d), F32),
        jax.ShapeDtypeStruct((n_tiles * TILE_PAIRS * n_slabs, LANES), U32),
        jax.ShapeDtypeStruct((t_tok, LANES), F32),
        jax.ShapeDtypeStruct((n_tiles * SECTIONS, ROUTE_ROWS, LANES), F32),
    )
    out_specs = (
        pl.BlockSpec((TM, d), lambda i, *_: (i, 0)),
        pl.BlockSpec((TILE_PAIRS * n_slabs, LANES), lambda i, *_: (i, 0)),
        pl.BlockSpec((TM, LANES), lambda i, *_: (i, 0)),
        pl.BlockSpec((SECTIONS, ROUTE_ROWS, LANES), lambda i, *_: (i, 0, 0)),
    )
    return pl.pallas_call(
        functools.partial(kernel_fn, tiles_per_seq, prev is not None),
        grid_spec=pltpu.PrefetchScalarGridSpec(
            num_scalar_prefetch=len(tables),
            grid=(n_tiles,),
            in_specs=in_specs,
            out_specs=out_specs,
            scratch_shapes=scratch + prev_scratch,
        ),
        out_shape=out_shape,
        compiler_params=pltpu.CompilerParams(
            dimension_semantics=("arbitrary",), vmem_limit_bytes=VMEM_LIMIT),
        name=kernel_fn.__name__.strip("_"),
    )(*tables, *prev_inputs, x, mod_l, *weights)


def _expert_kernel(layer, blk_expert_ref, n_valid_ref, next_ref, slot_ref,
                   blk_first_ref, blk_last_ref, blk_fill_ref, run_len_ref, run_src_ref, run_dst_ref,
                   xsl_hbm, wg_hbm, wu_hbm, wd_hbm, ys_ref,
                   x_scr, zero_scr, wg_stage, wu_stage, wd_stage, wg_scr, wu_scr, wd_scr,
                   x_sem, sem):
    j = pl.program_id(0)
    n_valid = n_valid_ref[0]
    expert = blk_expert_ref[j]
    prev = blk_expert_ref[jnp.maximum(j - 1, 0)]
    d = wg_scr.shape[0]
    slab = d // LANES
    blk_rows = BLK_PAIRS * slab

    def fetch(blk):
        slot = blk % 2
        row0 = blk * blk_rows

        def copy_run(q):
            dst = run_dst_ref[q]
            lo = jnp.maximum(dst, row0)
            hi = jnp.minimum(dst + run_len_ref[q], row0 + blk_rows)

            @pl.when(hi > lo)
            def _():
                pltpu.make_async_copy(
                    xsl_hbm.at[_rows(run_src_ref[q] + (lo - dst), hi - lo, slab)],
                    x_scr.at[slot, _rows(lo - row0, hi - lo, slab)], x_sem.at[slot]).start()

        first = blk_first_ref[blk]

        def body(t, carry):
            copy_run(first + 2 * t)
            copy_run(first + 2 * t + 1)
            return carry

        lax.fori_loop(0, (blk_last_ref[blk] - first + 1) // 2, body, 0)
        fill = blk_fill_ref[blk]

        @pl.when(fill > 0)
        def _():
            pltpu.make_async_copy(zero_scr.at[_rows(0, fill, slab)],
                                  x_scr.at[slot, _rows(blk_rows - fill, fill, slab)],
                                  x_sem.at[slot]).start()

    @pl.when(j == 0)
    def _():
        zero_scr[...] = jnp.zeros_like(zero_scr)
        fetch(j)

    @pl.when(j + 1 < n_valid)
    def _():
        fetch(j + 1)

    def weight_copies(e, slot):
        return [pltpu.make_async_copy(hbm.at[layer, e], stage.at[slot], sem.at[slot])
                for hbm, stage in ((wg_hbm, wg_stage), (wu_hbm, wu_stage), (wd_hbm, wd_stage))]

    @pl.when(j == 0)
    def _():
        for cp in weight_copies(expert, slot_ref[expert]):
            cp.start()

    @pl.when((j == 0) | (expert != prev))
    def _():
        slot = slot_ref[expert]
        for cp in weight_copies(expert, slot):
            cp.wait()
        nxt = next_ref[expert]

        @pl.when(nxt != expert)
        def _():
            for cp in weight_copies(nxt, 1 - slot):
                cp.start()

        wg_scr[...] = wg_stage[slot].astype(BF16)
        wu_scr[...] = wu_stage[slot].astype(BF16)
        wd_scr[...] = wd_stage[slot].astype(BF16)

    slot = j % 2

    @pl.when(j < n_valid)
    def _():
        pltpu.make_async_copy(zero_scr, x_scr.at[slot], x_sem.at[slot]).wait()

    part_pairs = BLK_PAIRS // BLK_PARTS
    part_rows = part_pairs * slab
    n_parts = jnp.where(j < n_valid,
                        (blk_rows - blk_fill_ref[j] + part_rows - 1) // part_rows, 0)
    for p in range(1, BLK_PARTS + 1):
        @pl.when(n_parts == p)
        def _(p=p):
            xb = _load_pairs(x_scr.at[slot], 0, p * part_pairs, d)
            a = _dot(xb, wg_scr[...])
            b = _dot(xb, wu_scr[...])
            hm = (a * jax.nn.sigmoid(a) * b).astype(BF16)
            _store_pairs(ys_ref, 0, _dot(hm, wd_scr[...]).astype(BF16))
            if p < BLK_PARTS:
                ys_ref[p * part_rows:, :] = jnp.zeros((blk_rows - p * part_rows, LANES), U32)

    @pl.when(n_parts == 0)
    def _():
        ys_ref[...] = jnp.zeros_like(ys_ref)


def _experts(tables, xsl, n_blocks, layer, w_gate, w_up, w_down):
    _, _, d, de = w_gate.shape
    blk_rows = BLK_PAIRS * (d // LANES)
    return pl.pallas_call(
        functools.partial(_expert_kernel, layer),
        grid_spec=pltpu.PrefetchScalarGridSpec(
            num_scalar_prefetch=len(tables),
            grid=(n_blocks,),
            in_specs=[pl.BlockSpec(memory_space=pl.ANY)] * 4,
            out_specs=pl.BlockSpec((blk_rows, LANES), lambda j, *_: (j, 0)),
            scratch_shapes=[
                pltpu.VMEM((2, blk_rows, LANES), U32), pltpu.VMEM((blk_rows, LANES), U32),
                pltpu.VMEM((2, d, de), F32), pltpu.VMEM((2, d, de), F32),
                pltpu.VMEM((2, de, d), F32),
                pltpu.VMEM((d, de), BF16), pltpu.VMEM((d, de), BF16), pltpu.VMEM((de, d), BF16),
                pltpu.SemaphoreType.DMA((2,)), pltpu.SemaphoreType.DMA((2,)),
            ],
        ),
        out_shape=jax.ShapeDtypeStruct((n_blocks * blk_rows, LANES), U32),
        compiler_params=pltpu.CompilerParams(
            dimension_semantics=("arbitrary",), vmem_limit_bytes=VMEM_LIMIT),
        name="experts",
    )(*tables, xsl, w_gate, w_up, w_down)


def _final_kernel(len_ref, loc_ref, glob_ref, tot_ref, ys_hbm, x_ref, route_ref,
                  mod_ref, gfin_ref, xo_ref, ysl_scr, sem):
    prev = ((len_ref, loc_ref, glob_ref, tot_ref), ys_hbm, route_ref, mod_ref, ysl_scr, sem)
    xo_ref[...] = _rms(_mixer_input(prev, x_ref)) * gfin_ref[...]


def _final(tables, ys, x, route, mod_l, seq, g_final):
    t_tok, d = x.shape
    tiles_per_seq = seq // TM
    return pl.pallas_call(
        _final_kernel,
        grid_spec=pltpu.PrefetchScalarGridSpec(
            num_scalar_prefetch=4,
            grid=(t_tok // TM,),
            in_specs=[
                pl.BlockSpec(memory_space=pl.ANY),
                pl.BlockSpec((TM, d), lambda i, *_: (i, 0)),
                pl.BlockSpec((TM, LANES), lambda i, *_: (i, 0)),
                pl.BlockSpec((1,) + mod_l.shape[1:], lambda i, *_: (i // tiles_per_seq, 0, 0)),
                pl.BlockSpec((1, d), lambda i, *_: (0, 0)),
            ],
            out_specs=pl.BlockSpec((TM, d), lambda i, *_: (i, 0)),
            scratch_shapes=[pltpu.VMEM((2, TILE_PAIRS * (d // LANES), LANES), U32),
                            pltpu.SemaphoreType.DMA((2,))],
        ),
        out_shape=jax.ShapeDtypeStruct((t_tok, d), F32),
        compiler_params=pltpu.CompilerParams(
            dimension_semantics=("arbitrary",), vmem_limit_bytes=VMEM_LIMIT),
        name="final",
    )(*tables, ys, x, route, mod_l, g_final)


def _moe(x, xsl, cnt, layer, w_gate, w_up, w_down):
    t_tok = x.shape[0]
    n_tiles = t_tok // TM
    n_blocks = -(-(2 * t_tok + n_tiles * TILE_RUNS) // EXPERT_BLK) + N_EXPERTS

    counts = cnt[:, ROUTE_COL0:ROUTE_COL0 + N_EXPERTS, 0].astype(I32)
    run_len = (counts + 1) // 2
    section_base = (jnp.arange(counts.shape[0], dtype=I32) % SECTIONS) * LOCAL_PAIRS
    run_loc = jnp.cumsum(run_len, axis=1) - run_len + section_base[:, None]
    seg_len = jnp.sum(run_len, axis=0)
    seg_pad = (seg_len + BLK_PAIRS - 1) // BLK_PAIRS * BLK_PAIRS
    seg_end = jnp.cumsum(seg_pad)
    seg_start = seg_end - seg_pad
    run_glob = seg_start[None, :] + jnp.cumsum(run_len, axis=0) - run_len
    tile_tot = jnp.sum(run_len.reshape(n_tiles, TILE_RUNS), axis=1)
    slab = x.shape[1] // LANES
    run_tables = tuple((a.reshape(-1) * slab).astype(I32)
                       for a in (run_len, run_loc, run_glob, tile_tot))
    n_valid = (seg_end[-1:] // BLK_PAIRS).astype(I32)
    blk_pair0 = jnp.arange(n_blocks, dtype=I32) * BLK_PAIRS
    blk_expert = jnp.sum((seg_end[None, :] <= blk_pair0[:, None]).astype(I32), axis=1)
    section_pair0 = jnp.arange(counts.shape[0], dtype=I32) * LOCAL_PAIRS
    run_src = jnp.cumsum(run_len, axis=1) - run_len + section_pair0[:, None]
    em_len, em_src, em_dst = (
        jnp.concatenate([a.T.reshape(-1), jnp.full((1,), end, I32)])
        for a, end in ((run_len, 0), (run_src, 0), (run_glob, n_blocks * BLK_PAIRS)))
    blk_first = jnp.sum(((em_dst + em_len)[None, :] <= blk_pair0[:, None]).astype(I32), axis=1)
    blk_last = jnp.sum((em_dst[None, :] < (blk_pair0 + BLK_PAIRS)[:, None]).astype(I32), axis=1)
    seg_stop = jnp.sum(jnp.where(
        jnp.arange(N_EXPERTS, dtype=I32)[None, :] == blk_expert[:, None],
        (seg_start + seg_len)[None, :], 0), axis=1)
    blk_fill = BLK_PAIRS - jnp.clip(seg_stop - blk_pair0, 0, BLK_PAIRS)
    pull_tables = tuple((a * slab).astype(I32) for a in (blk_fill, em_len, em_src, em_dst))
    experts = jnp.arange(N_EXPERTS, dtype=I32)
    used = seg_len > 0
    last_used = jnp.max(jnp.where(used, experts, 0))
    blk_expert = jnp.minimum(blk_expert, last_used).astype(I32)
    later = jnp.where((experts[None, :] > experts[:, None]) & used[None, :],
                      experts[None, :], N_EXPERTS)
    next_used = jnp.min(later, axis=1)
    next_used = jnp.where(next_used == N_EXPERTS, experts, next_used).astype(I32)
    slot_of = ((jnp.cumsum(used.astype(I32)) - 1) % 2).astype(I32)

    expert_tables = ((blk_expert, n_valid, next_used, slot_of,
                      blk_first.astype(I32), blk_last.astype(I32)) + pull_tables)
    return run_tables, _experts(expert_tables, xsl, n_blocks, layer, w_gate, w_up, w_down)


def kernel(x, c, w_ada, b_ada, norm_mix_g, norm_ffn_g, w_in_even, sgu_norm_g, w_spatial, b_spatial, conv_w, w_out_even, w_pool, pool_scale, w_group_router, b_group_router, w_expert_router, b_expert_router, moe_w_gate, moe_w_up, moe_w_down, final_norm_g):
    bsz, seq, d = x.shape
    depth = w_ada.shape[0]
    t_tok = bsz * seq
    assert seq % TM == 0 and d % LANES == 0 and w_spatial.shape[-1] == CHUNK
    assert all(w == 2 ** (g + 1) for g, w in enumerate(POOL_WINDOWS))

    mod = _modulation(c, w_ada, b_ada).reshape(depth, bsz, 6, d)
    tok = jnp.arange(TM)
    earlier = ((tok[:, None] < tok[None, :])
               & (tok[:, None] // SORT_TM == tok[None, :] // SORT_TM)).astype(BF16)
    lower = jnp.tril(jnp.ones((LANES, LANES), F32), -1)
    g_final = final_norm_g.reshape(1, d)

    xf = x.reshape(t_tok, d)
    prev = None
    for l in range(depth):
        i = l // 2
        rw = jnp.concatenate([w_group_router[l], w_expert_router[l]], axis=1).T
        rw = jnp.pad(rw, ((0, ROUTE_ROWS - rw.shape[0]), (0, 0))).astype(BF16)
        rb = jnp.concatenate([b_group_router[l], b_expert_router[l]])
        rb = jnp.pad(rb, (0, ROUTE_ROWS - rb.shape[0])).reshape(ROUTE_ROWS, 1)
        route_w = [rw, rb, earlier, lower]
        gmix = norm_mix_g[l].reshape(1, d)
        gffn = norm_ffn_g[l].reshape(1, d)
        if l % 2 == 0:
            aw = sgu_norm_g.shape[1]
            weights = [gmix, gffn, w_in_even[i].astype(BF16), sgu_norm_g[i].reshape(aw, 1),
                       w_spatial[i], b_spatial[i].reshape(A_HEADS, 1, CHUNK), conv_w[i],
                       w_out_even[i].astype(BF16)] + route_w
            scratch = [pltpu.VMEM((TM + SUBLANES, conv_w.shape[-1]), F32)]
            xf, xsl, route, cnt = _mixer_call(_even_kernel, xf, mod[l], seq, weights, scratch, prev)
        else:
            weights = [gmix, gffn, w_pool[i].astype(BF16), pool_scale[i].reshape(1, d)] + route_w
            gd = d // len(POOL_WINDOWS)
            scratch = [pltpu.VMEM((TM + max(POOL_WINDOWS), d - g * gd), F32)
                       for g in range(len(POOL_WINDOWS))]
            xf, xsl, route, cnt = _mixer_call(_odd_kernel, xf, mod[l], seq, weights, scratch, prev)
        run_tables, ys = _moe(xf, xsl, cnt, l, moe_w_gate, moe_w_up, moe_w_down)
        prev = (run_tables, ys, route, mod[l])
    run_tables, ys, route, mod_l = prev
    return _final(run_tables, ys, xf, route, mod_l, seq, g_final).reshape(bsz, seq, d)
```

```python
import functools

import jax
import jax.numpy as jnp
from jax import lax
from jax.experimental import pallas as pl
from jax.experimental.pallas import tpu as pltpu

F32 = jnp.float32
BF16 = jnp.bfloat16
U32 = jnp.uint32
I32 = jnp.int32

EPS = 1e-6
LANES = 128
SUBLANES = 8
CHUNK = 128
A_HEADS = 8
N_GROUPS = 4
EXPERTS_PER_GROUP = 8
N_EXPERTS = N_GROUPS * EXPERTS_PER_GROUP
POOL_WINDOWS = (2, 4, 8, 16)
CONV_WIDTH = 3
ROUTE_COL0 = N_GROUPS
ROUTE_ROWS = 48

TM = 512
SORT_TM = 256
SECTIONS = TM // SORT_TM
LOCAL_ROWS = 2 * SORT_TM + LANES
LOCAL_PAIRS = LOCAL_ROWS // 2
TILE_PAIRS = SECTIONS * LOCAL_PAIRS
TILE_RUNS = SECTIONS * N_EXPERTS
EXPERT_BLK = 1024
BLK_PAIRS = EXPERT_BLK // 2
BLK_PARTS = 8
MOD_TN = 3072
VMEM_LIMIT = 56 * 1024 * 1024


def _rms(x):
    return x * lax.rsqrt(jnp.mean(x * x, axis=-1, keepdims=True) + EPS)


def _dot(a, b):
    return jnp.dot(a, b, preferred_element_type=F32)


def _store_pairs(ref, base, rows_bf16):
    words = pltpu.bitcast(rows_bf16, U32)
    n_pairs, d = words.shape
    slab = d // LANES
    for c in range(slab):
        ref[pl.ds(base + c, n_pairs, stride=slab), :] = words[:, c * LANES:(c + 1) * LANES]


def _load_pairs(ref, base, n_pairs, d):
    slab = d // LANES
    words = jnp.concatenate(
        [ref[pl.ds(base + c, n_pairs, stride=slab), :] for c in range(slab)], axis=-1)
    return pltpu.bitcast(words, BF16)


def _mod_kernel(c_ref, w_ref, b_ref, o_ref):
    c = c_ref[...]
    ca = c * jax.nn.sigmoid(c)
    o_ref[0] = jnp.dot(ca, w_ref[0], precision=lax.Precision.HIGHEST,
                       preferred_element_type=F32) + b_ref[0]


def _modulation(c, w_ada, b_ada):
    depth, d, n = w_ada.shape
    bsz = c.shape[0]
    return pl.pallas_call(
        _mod_kernel,
        grid=(depth, n // MOD_TN),
        in_specs=[
            pl.BlockSpec((bsz, d), lambda l, j: (0, 0)),
            pl.BlockSpec((1, d, MOD_TN), lambda l, j: (l, 0, j)),
            pl.BlockSpec((1, 1, MOD_TN), lambda l, j: (l, 0, j)),
        ],
        out_specs=pl.BlockSpec((1, bsz, MOD_TN), lambda l, j: (l, 0, j)),
        out_shape=jax.ShapeDtypeStruct((depth, bsz, n), F32),
        compiler_params=pltpu.CompilerParams(
            dimension_semantics=("arbitrary", "arbitrary"),
            vmem_limit_bytes=VMEM_LIMIT),
        name="modulation",
    )(c, w_ada, b_ada.reshape(depth, 1, n))


def _route_and_sort(x_new, mod_ref, gffn_ref, rwt_ref, rbc_ref, earlier_ref, lower_ref,
                    xsl_ref, route_ref, cnt_ref):
    tm = x_new.shape[0]
    sh_f = mod_ref[0, 3:4, :]
    sc_f = mod_ref[0, 4:5, :]
    h2 = _rms(x_new) * (gffn_ref[...] * (1.0 + sc_f)) + sh_f

    rr = ROUTE_ROWS
    hh = h2.astype(BF16)
    logits = lax.dot_general(rwt_ref[...], hh, (((1,), (1,)), ((), ())),
                             preferred_element_type=F32) + rbc_ref[...]

    row = lax.broadcasted_iota(I32, logits.shape, 0).astype(F32)
    neg = jnp.float32(-jnp.inf)
    big = jnp.float32(LANES)

    gl = jnp.where(row < N_GROUPS, logits, neg)
    gmax = jnp.max(gl, axis=0, keepdims=True)
    g_sel = jnp.min(jnp.where(gl == gmax, row, big), axis=0, keepdims=True)
    g_w = 1.0 / jnp.sum(jnp.exp(gl - gmax), axis=0, keepdims=True)

    lo = ROUTE_COL0 + EXPERTS_PER_GROUP * g_sel
    el = jnp.where((row >= lo) & (row < lo + EXPERTS_PER_GROUP), logits, neg)
    m1 = jnp.max(el, axis=0, keepdims=True)
    i1 = jnp.min(jnp.where(el == m1, row, big), axis=0, keepdims=True)
    el2 = jnp.where(row == i1, neg, el)
    m2 = jnp.max(el2, axis=0, keepdims=True)
    i2 = jnp.min(jnp.where(el2 == m2, row, big), axis=0, keepdims=True)
    t = jnp.exp(m2 - m1)
    gate1 = g_w / (1.0 + t)
    gate2 = g_w * t / (1.0 + t)

    sections = [slice(s * SORT_TM, (s + 1) * SORT_TM) for s in range(tm // SORT_TM)]
    is1 = row == i1
    is2 = row == i2
    onehot = jnp.where(is1 | is2, 1.0, 0.0)
    before = _dot(onehot.astype(BF16), earlier_ref[...])
    lane = lax.broadcasted_iota(I32, (rr, LANES), 1)
    cnts = [jnp.sum(onehot[:, cols], axis=1, keepdims=True) for cols in sections]
    pairs = jnp.zeros((rr, LANES), F32)
    for s, cnt in enumerate(cnts):
        cnt_ref[s] = jnp.broadcast_to(cnt, (rr, LANES))
        pairs = jnp.where(lane == s, jnp.floor((cnt + 1.0) * 0.5), pairs)
    pairs = jnp.concatenate([pairs, jnp.zeros((LANES - rr, LANES), F32)], axis=0)
    pair_start = jnp.dot(lower_ref[...], pairs, precision=lax.Precision.HIGHEST,
                         preferred_element_type=F32)
    pos = jnp.concatenate([before[:, cols] + 2.0 * pair_start[0:rr, s:s + 1]
                           for s, cols in enumerate(sections)], axis=1)
    pos1 = jnp.sum(jnp.where(is1, pos, 0.0), axis=0, keepdims=True)
    pos2 = jnp.sum(jnp.where(is2, pos, 0.0), axis=0, keepdims=True)

    r8 = lax.broadcasted_iota(I32, (SUBLANES, tm), 0)
    route_t = jnp.where(r8 == 0, pos1, 0.0)
    route_t = jnp.where(r8 == 1, pos2, route_t)
    route_t = jnp.where(r8 == 4, gate1, route_t)
    route_t = jnp.where(r8 == 5, gate2, route_t)
    route_t = jnp.concatenate([route_t, jnp.zeros((LANES - SUBLANES, tm), F32)], axis=0)
    route_ref[...] = route_t.T

    srow = lax.broadcasted_iota(I32, (LOCAL_ROWS, SORT_TM), 0).astype(F32)
    section_rows = LOCAL_PAIRS * (x_new.shape[1] // LANES)
    for s, cols in enumerate(sections):
        perm = jnp.where((srow == pos1[:, cols]) | (srow == pos2[:, cols]), 1.0, 0.0)
        _store_pairs(xsl_ref, s * section_rows, _dot(perm.astype(BF16), hh[cols]).astype(BF16))


def _rows(first_row, n_rows, unit):
    return pl.ds(pl.multiple_of(first_row, unit), n_rows)


def _unpermute(tables, ys_hbm, route_ref, ysl_scr, sem, shape):
    len_ref, loc_ref, glob_ref, tot_ref = tables
    i = pl.program_id(0)
    tm, d = shape
    slab = d // LANES

    def run_copy(tile, glob, loc, rows):
        slot = tile % 2
        return pltpu.make_async_copy(ys_hbm.at[_rows(glob, rows, slab)],
                                     ysl_scr.at[slot, _rows(loc, rows, slab)], sem.at[slot])

    def fetch(tile):
        def body(r, carry):
            k = tile * TILE_RUNS + r
            n = len_ref[k]

            @pl.when(n > 0)
            def _():
                run_copy(tile, glob_ref[k], loc_ref[k], n).start()
            return carry
        lax.fori_loop(0, TILE_RUNS, body, 0, unroll=4)

    @pl.when(i == 0)
    def _():
        ysl_scr[...] = jnp.zeros_like(ysl_scr)
        fetch(i)

    @pl.when(i + 1 < pl.num_programs(0))
    def _():
        fetch(i + 1)

    @pl.when(tot_ref[i] > 0)
    def _():
        run_copy(i, 0, 0, tot_ref[i]).wait()

    ysl_ref = ysl_scr.at[i % 2]
    srow = lax.broadcasted_iota(I32, (SORT_TM, LOCAL_ROWS), 1).astype(F32)
    y_sections = []
    for s in range(tm // SORT_TM):
        rows = slice(s * SORT_TM, (s + 1) * SORT_TM)
        ysl = _load_pairs(ysl_ref, s * LOCAL_PAIRS * slab, LOCAL_PAIRS, d)
        sel = jnp.concatenate(
            [jnp.where(srow == route_ref[rows, k:k + 1], 1.0, 0.0).astype(BF16) for k in range(2)],
            axis=0)
        picked = _dot(sel, ysl)
        y_sections.append(route_ref[rows, 4:5] * picked[0:SORT_TM]
                          + route_ref[rows, 5:6] * picked[SORT_TM:2 * SORT_TM])
    return jnp.concatenate(y_sections, axis=0)


def _split_prev(fused, refs):
    if not fused:
        return None, refs
    tables, (ys_hbm, route_ref, mod_ref), (ysl_scr, sem) = refs[:4], refs[4:7], refs[-2:]
    return (tables, ys_hbm, route_ref, mod_ref, ysl_scr, sem), refs[7:-2]


def _mixer_input(prev, x_ref):
    if prev is None:
        return x_ref[...]
    tables, ys_hbm, route_ref, mod_ref, ysl_scr, sem = prev
    y = _unpermute(tables, ys_hbm, route_ref, ysl_scr, sem, x_ref.shape)
    return x_ref[...] + mod_ref[0, 5:6, :] * y


def _even_kernel(tiles_per_seq, fused, *refs):
    prev, refs = _split_prev(fused, refs)
    (x_ref, mod_ref, gmix_ref, gffn_ref, win_ref, gv_ref, ws_ref, bs_ref,
     cw_ref, wout_ref, rwt_ref, rbc_ref, earlier_ref, lower_ref,
     xo_ref, xsl_ref, route_ref, cnt_ref,
     zc_scr) = refs
    i = pl.program_id(0)
    tm = x_ref.shape[0]
    aw = gv_ref.shape[0]
    hd = aw // A_HEADS
    n_chunks = tm // CHUNK

    @pl.when(i == 0)
    def _():
        zc_scr[...] = jnp.zeros_like(zc_scr)

    x = _mixer_input(prev, x_ref)
    sh_m = mod_ref[0, 0:1, :]
    sc_m = mod_ref[0, 1:2, :]
    g_m = mod_ref[0, 2:3, :]
    h = _rms(x) * (gmix_ref[...] * (1.0 + sc_m)) + sh_m
    z = _dot(h.astype(BF16), win_ref[...])
    u = z[:, 0:aw]
    v = z[:, aw:2 * aw]
    b_gate = z[:, 2 * aw:3 * aw]
    c_gate = z[:, 3 * aw:4 * aw]
    x_in = z[:, 4 * aw:5 * aw]

    v_t = v.T
    row = lax.broadcasted_iota(I32, (CHUNK, CHUNK), 0)
    col = lax.broadcasted_iota(I32, (CHUNK, CHUNK), 1)
    causal = col <= row
    head_rows = []
    for hh in range(A_HEADS):
        vh = v_t[hh * hd:(hh + 1) * hd, :]
        msv = jnp.mean(vh * vh, axis=0, keepdims=True)
        vn = (vh * lax.rsqrt(msv + EPS) * gv_ref[hh * hd:(hh + 1) * hd, :]).astype(BF16)
        lhs = jnp.concatenate(
            [vn[:, c * CHUNK:(c + 1) * CHUNK] for c in range(n_chunks)], axis=0)
        w_m = jnp.where(causal, ws_ref[hh], 0.0).astype(BF16)
        sv_h = lax.dot_general(lhs, w_m, (((1,), (1,)), ((), ())),
                               preferred_element_type=F32)
        sv_h = sv_h + bs_ref[hh]
        head_rows.append(jnp.concatenate(
            [sv_h[c * hd:(c + 1) * hd, :] for c in range(n_chunks)], axis=1))
    sv = jnp.concatenate(head_rows, axis=0).T
    y_a = u * sv

    zc = c_gate * x_in
    first = (i % tiles_per_seq) == 0
    halo = zc_scr[tm:tm + SUBLANES, :]
    zc_scr[0:SUBLANES, :] = jnp.where(first, 0.0, halo)
    zc_scr[SUBLANES:SUBLANES + tm, :] = zc
    conv = cw_ref[2:3, :] * zc
    for k in range(CONV_WIDTH - 1):
        shift = CONV_WIDTH - 1 - k
        conv = conv + cw_ref[k:k + 1, :] * zc_scr[SUBLANES - shift:SUBLANES - shift + tm, :]
    y_b = b_gate * conv

    y = _dot(y_a.astype(BF16), wout_ref[0:aw, :]) + _dot(y_b.astype(BF16), wout_ref[aw:, :])
    x_new = x + g_m * y
    xo_ref[...] = x_new
    _route_and_sort(x_new, mod_ref, gffn_ref, rwt_ref, rbc_ref, earlier_ref, lower_ref,
                    xsl_ref, route_ref, cnt_ref)


def _odd_kernel(tiles_per_seq, fused, *refs):
    prev, refs = _split_prev(fused, refs)
    (x_ref, mod_ref, gmix_ref, gffn_ref, wpool_ref, pscale_ref,
     rwt_ref, rbc_ref, earlier_ref, lower_ref,
     xo_ref, xsl_ref, route_ref, cnt_ref) = refs[:14]
    level_scrs = refs[14:]
    i = pl.program_id(0)
    tm, d = x_ref.shape
    halo_rows = max(POOL_WINDOWS)
    gd = d // len(POOL_WINDOWS)

    @pl.when(i == 0)
    def _():
        for scr in level_scrs:
            scr[...] = jnp.zeros_like(scr)

    x = _mixer_input(prev, x_ref)
    sh_m = mod_ref[0, 0:1, :]
    sc_m = mod_ref[0, 1:2, :]
    g_m = mod_ref[0, 2:3, :]
    h = _rms(x) * (gmix_ref[...] * (1.0 + sc_m)) + sh_m

    tile_in_seq = i % tiles_per_seq
    first = tile_in_seq == 0

    sums = h
    window_sums = []
    for g, scr in enumerate(level_scrs):
        tail = scr[tm:tm + halo_rows, :]
        scr[0:halo_rows, :] = jnp.where(first, 0.0, tail)
        scr[halo_rows:halo_rows + tm, :] = sums
        lag = POOL_WINDOWS[g] // 2
        sums = sums + scr[halo_rows - lag:halo_rows - lag + tm, :]
        window_sums.append(sums[:, 0:gd])
        if g + 1 < len(level_scrs):
            sums = sums[:, gd:]

    pos = (tile_in_seq * tm + lax.broadcasted_iota(I32, (tm, 1), 0)).astype(F32)
    outs = []
    for g, win in enumerate(POOL_WINDOWS):
        cs = slice(g * gd, (g + 1) * gd)
        count = jnp.minimum(pos + 1.0, jnp.float32(win))
        pooled = window_sums[g] / count - h[:, cs]
        outs.append(_dot(pooled.astype(BF16), wpool_ref[g]))
    y = jnp.concatenate(outs, axis=-1) * pscale_ref[...]
    x_new = x + g_m * y
    xo_ref[...] = x_new
    _route_and_sort(x_new, mod_ref, gffn_ref, rwt_ref, rbc_ref, earlier_ref, lower_ref,
                    xsl_ref, route_ref, cnt_ref)


def _mixer_call(kernel_fn, x, mod_l, seq, weights, scratch, prev):
    t_tok, d = x.shape
    n_tiles = t_tok // TM
    tiles_per_seq = seq // TM
    n_slabs = d // LANES

    def const_spec(a):
        return pl.BlockSpec(a.shape, lambda i, *_, nd=a.ndim: (0,) * nd)

    def mod_spec(m):
        return pl.BlockSpec((1,) + m.shape[1:], lambda i, *_: (i // tiles_per_seq, 0, 0))

    tables, prev_inputs, prev_specs, prev_scratch = (), (), [], []
    if prev is not None:
        tables, ys, route_prev, mod_prev = prev
        prev_inputs = (ys, route_prev, mod_prev)
        prev_specs = [pl.BlockSpec(memory_space=pl.ANY),
                      pl.BlockSpec((TM, LANES), lambda i, *_: (i, 0)),
                      mod_spec(mod_prev)]
        prev_scratch = [pltpu.VMEM((2, TILE_PAIRS * n_slabs, LANES), U32),
                        pltpu.SemaphoreType.DMA((2,))]
    in_specs = prev_specs + [
        pl.BlockSpec((TM, d), lambda i, *_: (i, 0)),
        mod_spec(mod_l),
    ] + [const_spec(w) for w in weights]
    out_shape = (
        jax.ShapeDtypeStruct((t_tok, d), F32),
        jax.ShapeDtypeStruct((n_tiles * TILE_PAIRS * n_slabs, LANES), U32),
        jax.ShapeDtypeStruct((t_tok, LANES), F32),
        jax.ShapeDtypeStruct((n_tiles * SECTIONS, ROUTE_ROWS, LANES), F32),
    )
    out_specs = (
        pl.BlockSpec((TM, d), lambda i, *_: (i, 0)),
        pl.BlockSpec((TILE_PAIRS * n_slabs, LANES), lambda i, *_: (i, 0)),
        pl.BlockSpec((TM, LANES), lambda i, *_: (i, 0)),
        pl.BlockSpec((SECTIONS, ROUTE_ROWS, LANES), lambda i, *_: (i, 0, 0)),
    )
    return pl.pallas_call(
        functools.partial(kernel_fn, tiles_per_seq, prev is not None),
        grid_spec=pltpu.PrefetchScalarGridSpec(
            num_scalar_prefetch=len(tables),
            grid=(n_tiles,),
            in_specs=in_specs,
            out_specs=out_specs,
            scratch_shapes=scratch + prev_scratch,
        ),
        out_shape=out_shape,
        compiler_params=pltpu.CompilerParams(
            dimension_semantics=("arbitrary",), vmem_limit_bytes=VMEM_LIMIT),
        name=kernel_fn.__name__.strip("_"),
    )(*tables, *prev_inputs, x, mod_l, *weights)


def _expert_kernel(layer, blk_expert_ref, n_valid_ref, next_ref, slot_ref,
                   blk_first_ref, blk_last_ref, blk_fill_ref, run_len_ref, run_src_ref, run_dst_ref,
                   xsl_hbm, wg_hbm, wu_hbm, wd_hbm, ys_ref,
                   x_scr, zero_scr, wg_stage, wu_stage, wd_stage, wg_scr, wu_scr, wd_scr,
                   x_sem, sem):
    j = pl.program_id(0)
    n_valid = n_valid_ref[0]
    expert = blk_expert_ref[j]
    prev = blk_expert_ref[jnp.maximum(j - 1, 0)]
    d = wg_scr.shape[0]
    slab = d // LANES
    blk_rows = BLK_PAIRS * slab

    def fetch(blk):
        slot = blk % 2
        row0 = blk * blk_rows

        def copy_run(q):
            dst = run_dst_ref[q]
            lo = jnp.maximum(dst, row0)
            hi = jnp.minimum(dst + run_len_ref[q], row0 + blk_rows)

            @pl.when(hi > lo)
            def _():
                pltpu.make_async_copy(
                    xsl_hbm.at[_rows(run_src_ref[q] + (lo - dst), hi - lo, slab)],
                    x_scr.at[slot, _rows(lo - row0, hi - lo, slab)], x_sem.at[slot]).start()

        first = blk_first_ref[blk]

        def body(t, carry):
            copy_run(first + 2 * t)
            copy_run(first + 2 * t + 1)
            return carry

        lax.fori_loop(0, (blk_last_ref[blk] - first + 1) // 2, body, 0)
        fill = blk_fill_ref[blk]

        @pl.when(fill > 0)
        def _():
            pltpu.make_async_copy(zero_scr.at[_rows(0, fill, slab)],
                                  x_scr.at[slot, _rows(blk_rows - fill, fill, slab)],
                                  x_sem.at[slot]).start()

    @pl.when(j == 0)
    def _():
        zero_scr[...] = jnp.zeros_like(zero_scr)
        fetch(j)

    @pl.when(j + 1 < n_valid)
    def _():
        fetch(j + 1)

    def weight_copies(e, slot):
        return [pltpu.make_async_copy(hbm.at[layer, e], stage.at[slot], sem.at[slot])
                for hbm, stage in ((wg_hbm, wg_stage), (wu_hbm, wu_stage), (wd_hbm, wd_stage))]

    @pl.when(j == 0)
    def _():
        for cp in weight_copies(expert, slot_ref[expert]):
            cp.start()

    @pl.when((j == 0) | (expert != prev))
    def _():
        slot = slot_ref[expert]
        for cp in weight_copies(expert, slot):
            cp.wait()
        nxt = next_ref[expert]

        @pl.when(nxt != expert)
        def _():
            for cp in weight_copies(nxt, 1 - slot):
                cp.start()

        wg_scr[...] = wg_stage[slot].astype(BF16)
        wu_scr[...] = wu_stage[slot].astype(BF16)
        wd_scr[...] = wd_stage[slot].astype(BF16)

    slot = j % 2

    @pl.when(j < n_valid)
    def _():
        pltpu.make_async_copy(zero_scr, x_scr.at[slot], x_sem.at[slot]).wait()

    part_pairs = BLK_PAIRS // BLK_PARTS
    part_rows = part_pairs * slab
    n_parts = jnp.where(j < n_valid,
                        (blk_rows - blk_fill_ref[j] + part_rows - 1) // part_rows, 0)
    for p in range(1, BLK_PARTS + 1):
        @pl.when(n_parts == p)
        def _(p=p):
            xb = _load_pairs(x_scr.at[slot], 0, p * part_pairs, d)
            a = _dot(xb, wg_scr[...])
            b = _dot(xb, wu_scr[...])
            hm = (a * jax.nn.sigmoid(a) * b).astype(BF16)
            _store_pairs(ys_ref, 0, _dot(hm, wd_scr[...]).astype(BF16))
            if p < BLK_PARTS:
                ys_ref[p * part_rows:, :] = jnp.zeros((blk_rows - p * part_rows, LANES), U32)

    @pl.when(n_parts == 0)
    def _():
        ys_ref[...] = jnp.zeros_like(ys_ref)


def _experts(tables, xsl, n_blocks, layer, w_gate, w_up, w_down):
    _, _, d, de = w_gate.shape
    blk_rows = BLK_PAIRS * (d // LANES)
    return pl.pallas_call(
        functools.partial(_expert_kernel, layer),
        grid_spec=pltpu.PrefetchScalarGridSpec(
            num_scalar_prefetch=len(tables),
            grid=(n_blocks,),
            in_specs=[pl.BlockSpec(memory_space=pl.ANY)] * 4,
            out_specs=pl.BlockSpec((blk_rows, LANES), lambda j, *_: (j, 0)),
            scratch_shapes=[
                pltpu.VMEM((2, blk_rows, LANES), U32), pltpu.VMEM((blk_rows, LANES), U32),
                pltpu.VMEM((2, d, de), F32), pltpu.VMEM((2, d, de), F32),
                pltpu.VMEM((2, de, d), F32),
                pltpu.VMEM((d, de), BF16), pltpu.VMEM((d, de), BF16), pltpu.VMEM((de, d), BF16),
                pltpu.SemaphoreType.DMA((2,)), pltpu.SemaphoreType.DMA((2,)),
            ],
        ),
        out_shape=jax.ShapeDtypeStruct((n_blocks * blk_rows, LANES), U32),
        compiler_params=pltpu.CompilerParams(
            dimension_semantics=("arbitrary",), vmem_limit_bytes=VMEM_LIMIT),
        name="experts",
    )(*tables, xsl, w_gate, w_up, w_down)


def _final_kernel(len_ref, loc_ref, glob_ref, tot_ref, ys_hbm, x_ref, route_ref,
                  mod_ref, gfin_ref, xo_ref, ysl_scr, sem):
    prev = ((len_ref, loc_ref, glob_ref, tot_ref), ys_hbm, route_ref, mod_ref, ysl_scr, sem)
    xo_ref[...] = _rms(_mixer_input(prev, x_ref)) * gfin_ref[...]


def _final(tables, ys, x, route, mod_l, seq, g_final):
    t_tok, d = x.shape
    tiles_per_seq = seq // TM
    return pl.pallas_call(
        _final_kernel,
        grid_spec=pltpu.PrefetchScalarGridSpec(
            num_scalar_prefetch=4,
            grid=(t_tok // TM,),
            in_specs=[
                pl.BlockSpec(memory_space=pl.ANY),
                pl.BlockSpec((TM, d), lambda i, *_: (i, 0)),
                pl.BlockSpec((TM, LANES), lambda i, *_: (i, 0)),
                pl.BlockSpec((1,) + mod_l.shape[1:], lambda i, *_: (i // tiles_per_seq, 0, 0)),
                pl.BlockSpec((1, d), lambda i, *_: (0, 0)),
            ],
            out_specs=pl.BlockSpec((TM, d), lambda i, *_: (i, 0)),
            scratch_shapes=[pltpu.VMEM((2, TILE_PAIRS * (d // LANES), LANES), U32),
                            pltpu.SemaphoreType.DMA((2,))],
        ),
        out_shape=jax.ShapeDtypeStruct((t_tok, d), F32),
        compiler_params=pltpu.CompilerParams(
            dimension_semantics=("arbitrary",), vmem_limit_bytes=VMEM_LIMIT),
        name="final",
    )(*tables, ys, x, route, mod_l, g_final)


def _moe(x, xsl, cnt, layer, w_gate, w_up, w_down):
    t_tok = x.shape[0]
    n_tiles = t_tok // TM
    n_blocks = -(-(2 * t_tok + n_tiles * TILE_RUNS) // EXPERT_BLK) + N_EXPERTS

    counts = cnt[:, ROUTE_COL0:ROUTE_COL0 + N_EXPERTS, 0].astype(I32)
    run_len = (counts + 1) // 2
    section_base = (jnp.arange(counts.shape[0], dtype=I32) % SECTIONS) * LOCAL_PAIRS
    run_loc = jnp.cumsum(run_len, axis=1) - run_len + section_base[:, None]
    seg_len = jnp.sum(run_len, axis=0)
    seg_pad = (seg_len + BLK_PAIRS - 1) // BLK_PAIRS * BLK_PAIRS
    seg_end = jnp.cumsum(seg_pad)
    seg_start = seg_end - seg_pad
    run_glob = seg_start[None, :] + jnp.cumsum(run_len, axis=0) - run_len
    tile_tot = jnp.sum(run_len.reshape(n_tiles, TILE_RUNS), axis=1)
    slab = x.shape[1] // LANES
    run_tables = tuple((a.reshape(-1) * slab).astype(I32)
                       for a in (run_len, run_loc, run_glob, tile_tot))
    n_valid = (seg_end[-1:] // BLK_PAIRS).astype(I32)
    blk_pair0 = jnp.arange(n_blocks, dtype=I32) * BLK_PAIRS
    blk_expert = jnp.sum((seg_end[None, :] <= blk_pair0[:, None]).astype(I32), axis=1)
    section_pair0 = jnp.arange(counts.shape[0], dtype=I32) * LOCAL_PAIRS
    run_src = jnp.cumsum(run_len, axis=1) - run_len + section_pair0[:, None]
    em_len, em_src, em_dst = (
        jnp.concatenate([a.T.reshape(-1), jnp.full((1,), end, I32)])
        for a, end in ((run_len, 0), (run_src, 0), (run_glob, n_blocks * BLK_PAIRS)))
    blk_first = jnp.sum(((em_dst + em_len)[None, :] <= blk_pair0[:, None]).astype(I32), axis=1)
    blk_last = jnp.sum((em_dst[None, :] < (blk_pair0 + BLK_PAIRS)[:, None]).astype(I32), axis=1)
    seg_stop = jnp.sum(jnp.where(
        jnp.arange(N_EXPERTS, dtype=I32)[None, :] == blk_expert[:, None],
        (seg_start + seg_len)[None, :], 0), axis=1)
    blk_fill = BLK_PAIRS - jnp.clip(seg_stop - blk_pair0, 0, BLK_PAIRS)
    pull_tables = tuple((a * slab).astype(I32) for a in (blk_fill, em_len, em_src, em_dst))
    experts = jnp.arange(N_EXPERTS, dtype=I32)
    used = seg_len > 0
    last_used = jnp.max(jnp.where(used, experts, 0))
    blk_expert = jnp.minimum(blk_expert, last_used).astype(I32)
    later = jnp.where((experts[None, :] > experts[:, None]) & used[None, :],
                      experts[None, :], N_EXPERTS)
    next_used = jnp.min(later, axis=1)
    next_used = jnp.where(next_used == N_EXPERTS, experts, next_used).astype(I32)
    slot_of = ((jnp.cumsum(used.astype(I32)) - 1) % 2).astype(I32)

    expert_tables = ((blk_expert, n_valid, next_used, slot_of,
                      blk_first.astype(I32), blk_last.astype(I32)) + pull_tables)
    return run_tables, _experts(expert_tables, xsl, n_blocks, layer, w_gate, w_up, w_down)


def kernel(x, c, w_ada, b_ada, norm_mix_g, norm_ffn_g, w_in_even, sgu_norm_g, w_spatial, b_spatial, conv_w, w_out_even, w_pool, pool_scale, w_group_router, b_group_router, w_expert_router, b_expert_router, moe_w_gate, moe_w_up, moe_w_down, final_norm_g):
    bsz, seq, d = x.shape
    depth = w_ada.shape[0]
    t_tok = bsz * seq
    assert seq % TM == 0 and d % LANES == 0 and w_spatial.shape[-1] == CHUNK
    assert all(w == 2 ** (g + 1) for g, w in enumerate(POOL_WINDOWS))

    mod = _modulation(c, w_ada, b_ada).reshape(depth, bsz, 6, d)
    tok = jnp.arange(TM)
    earlier = ((tok[:, None] < tok[None, :])
               & (tok[:, None] // SORT_TM == tok[None, :] // SORT_TM)).astype(BF16)
    lower = jnp.tril(jnp.ones((LANES, LANES), F32), -1)
    g_final = final_norm_g.reshape(1, d)

    xf = x.reshape(t_tok, d)
    prev = None
    for l in range(depth):
        i = l // 2
        rw = jnp.concatenate([w_group_router[l], w_expert_router[l]], axis=1).T
        rw = jnp.pad(rw, ((0, ROUTE_ROWS - rw.shape[0]), (0, 0))).astype(BF16)
        rb = jnp.concatenate([b_group_router[l], b_expert_router[l]])
        rb = jnp.pad(rb, (0, ROUTE_ROWS - rb.shape[0])).reshape(ROUTE_ROWS, 1)
        route_w = [rw, rb, earlier, lower]
        gmix = norm_mix_g[l].reshape(1, d)
        gffn = norm_ffn_g[l].reshape(1, d)
        if l % 2 == 0:
            aw = sgu_norm_g.shape[1]
            weights = [gmix, gffn, w_in_even[i].astype(BF16), sgu_norm_g[i].reshape(aw, 1),
                       w_spatial[i], b_spatial[i].reshape(A_HEADS, 1, CHUNK), conv_w[i],
                       w_out_even[i].astype(BF16)] + route_w
            scratch = [pltpu.VMEM((TM + SUBLANES, conv_w.shape[-1]), F32)]
            xf, xsl, route, cnt = _mixer_call(_even_kernel, xf, mod[l], seq, weights, scratch, prev)
        else:
            weights = [gmix, gffn, w_pool[i].astype(BF16), pool_scale[i].reshape(1, d)] + route_w
            gd = d // len(POOL_WINDOWS)
            scratch = [pltpu.VMEM((TM + max(POOL_WINDOWS), d - g * gd), F32)
                       for g in range(len(POOL_WINDOWS))]
            xf, xsl, route, cnt = _mixer_call(_odd_kernel, xf, mod[l], seq, weights, scratch, prev)
        run_tables, ys = _moe(xf, xsl, cnt, l, moe_w_gate, moe_w_up, moe_w_down)
        prev = (run_tables, ys, route, mod[l])
    run_tables, ys, route, mod_l = prev
    return _final(run_tables, ys, xf, route, mod_l, seq, g_final).reshape(bsz, seq, d)
```

```python
import functools

import jax
import jax.numpy as jnp
from jax import lax
from jax.experimental import pallas as pl
from jax.experimental.pallas import tpu as pltpu

F32 = jnp.float32
BF16 = jnp.bfloat16
U32 = jnp.uint32
I32 = jnp.int32

EPS = 1e-6
LANES = 128
SUBLANES = 8
CHUNK = 128
A_HEADS = 8
N_GROUPS = 4
EXPERTS_PER_GROUP = 8
N_EXPERTS = N_GROUPS * EXPERTS_PER_GROUP
POOL_WINDOWS = (2, 4, 8, 16)
CONV_WIDTH = 3
ROUTE_COL0 = N_GROUPS
ROUTE_ROWS = 48

TM = 512
SORT_TM = 256
SECTIONS = TM // SORT_TM
LOCAL_ROWS = 2 * SORT_TM + LANES
LOCAL_PAIRS = LOCAL_ROWS // 2
TILE_PAIRS = SECTIONS * LOCAL_PAIRS
TILE_RUNS = SECTIONS * N_EXPERTS
EXPERT_BLK = 1024
BLK_PAIRS = EXPERT_BLK // 2
BLK_PARTS = 8
MOD_TN = 3072
VMEM_LIMIT = 56 * 1024 * 1024


def _rms(x):
    return x * lax.rsqrt(jnp.mean(x * x, axis=-1, keepdims=True) + EPS)


def _dot(a, b):
    return jnp.dot(a, b, preferred_element_type=F32)


def _store_pairs(ref, base, rows_bf16):
    words = pltpu.bitcast(rows_bf16, U32)
    n_pairs, d = words.shape
    slab = d // LANES
    for c in range(slab):
        ref[pl.ds(base + c, n_pairs, stride=slab), :] = words[:, c * LANES:(c + 1) * LANES]


def _load_pairs(ref, base, n_pairs, d):
    slab = d // LANES
    words = jnp.concatenate(
        [ref[pl.ds(base + c, n_pairs, stride=slab), :] for c in range(slab)], axis=-1)
    return pltpu.bitcast(words, BF16)


def _mod_kernel(c_ref, w_ref, b_ref, o_ref):
    c = c_ref[...]
    ca = c * jax.nn.sigmoid(c)
    o_ref[0] = jnp.dot(ca, w_ref[0], precision=lax.Precision.HIGHEST,
                       preferred_element_type=F32) + b_ref[0]


def _modulation(c, w_ada, b_ada):
    depth, d, n = w_ada.shape
    bsz = c.shape[0]
    return pl.pallas_call(
        _mod_kernel,
        grid=(depth, n // MOD_TN),
        in_specs=[
            pl.BlockSpec((bsz, d), lambda l, j: (0, 0)),
            pl.BlockSpec((1, d, MOD_TN), lambda l, j: (l, 0, j)),
            pl.BlockSpec((1, 1, MOD_TN), lambda l, j: (l, 0, j)),
        ],
        out_specs=pl.BlockSpec((1, bsz, MOD_TN), lambda l, j: (l, 0, j)),
        out_shape=jax.ShapeDtypeStruct((depth, bsz, n), F32),
        compiler_params=pltpu.CompilerParams(
            dimension_semantics=("arbitrary", "arbitrary"),
            vmem_limit_bytes=VMEM_LIMIT),
        name="modulation",
    )(c, w_ada, b_ada.reshape(depth, 1, n))


def _route_and_sort(x_new, mod_ref, gffn_ref, rwt_ref, rbc_ref, earlier_ref, lower_ref,
                    xsl_ref, route_ref, cnt_ref):
    tm = x_new.shape[0]
    sh_f = mod_ref[0, 3:4, :]
    sc_f = mod_ref[0, 4:5, :]
    h2 = _rms(x_new) * (gffn_ref[...] * (1.0 + sc_f)) + sh_f

    rr = ROUTE_ROWS
    hh = h2.astype(BF16)
    logits = lax.dot_general(rwt_ref[...], hh, (((1,), (1,)), ((), ())),
                             preferred_element_type=F32) + rbc_ref[...]

    row = lax.broadcasted_iota(I32, logits.shape, 0).astype(F32)
    neg = jnp.float32(-jnp.inf)
    big = jnp.float32(LANES)

    gl = jnp.where(row < N_GROUPS, logits, neg)
    gmax = jnp.max(gl, axis=0, keepdims=True)
    g_sel = jnp.min(jnp.where(gl == gmax, row, big), axis=0, keepdims=True)
    g_w = 1.0 / jnp.sum(jnp.exp(gl - gmax), axis=0, keepdims=True)

    lo = ROUTE_COL0 + EXPERTS_PER_GROUP * g_sel
    el = jnp.where((row >= lo) & (row < lo + EXPERTS_PER_GROUP), logits, neg)
    m1 = jnp.max(el, axis=0, keepdims=True)
    i1 = jnp.min(jnp.where(el == m1, row, big), axis=0, keepdims=True)
    el2 = jnp.where(row == i1, neg, el)
    m2 = jnp.max(el2, axis=0, keepdims=True)
    i2 = jnp.min(jnp.where(el2 == m2, row, big), axis=0, keepdims=True)
    t = jnp.exp(m2 - m1)
    gate1 = g_w / (1.0 + t)
    gate2 = g_w * t / (1.0 + t)

    sections = [slice(s * SORT_TM, (s + 1) * SORT_TM) for s in range(tm // SORT_TM)]
    is1 = row == i1
    is2 = row == i2
    onehot = jnp.where(is1 | is2, 1.0, 0.0)
    stacked = jnp.concatenate([onehot[:, cols] for cols in sections], axis=0).astype(BF16)
    before = _dot(stacked, earlier_ref[...])
    before = jnp.concatenate([before[s * rr:(s + 1) * rr] for s in range(len(sections))], axis=1)
    lane = lax.broadcasted_iota(I32, (rr, LANES), 1)
    cnts = [jnp.sum(onehot[:, cols], axis=1, keepdims=True) for cols in sections]
    pairs = jnp.zeros((rr, LANES), F32)
    for s, cnt in enumerate(cnts):
        cnt_ref[s] = jnp.broadcast_to(cnt, (rr, LANES))
        pairs = jnp.where(lane == s, jnp.floor((cnt + 1.0) * 0.5), pairs)
    pairs = jnp.concatenate([pairs, jnp.zeros((LANES - rr, LANES), F32)], axis=0)
    pair_start = jnp.dot(lower_ref[...], pairs, precision=lax.Precision.HIGHEST,
                         preferred_element_type=F32)
    pos = jnp.concatenate([before[:, cols] + 2.0 * pair_start[0:rr, s:s + 1]
                           for s, cols in enumerate(sections)], axis=1)
    pos1 = jnp.sum(jnp.where(is1, pos, 0.0), axis=0, keepdims=True)
    pos2 = jnp.sum(jnp.where(is2, pos, 0.0), axis=0, keepdims=True)

    r8 = lax.broadcasted_iota(I32, (SUBLANES, tm), 0)
    route_t = jnp.where(r8 == 0, pos1, 0.0)
    route_t = jnp.where(r8 == 1, pos2, route_t)
    route_t = jnp.where(r8 == 4, gate1, route_t)
    route_t = jnp.where(r8 == 5, gate2, route_t)
    route_t = jnp.concatenate([route_t, jnp.zeros((LANES - SUBLANES, tm), F32)], axis=0)
    route_ref[...] = route_t.T

    srow = lax.broadcasted_iota(I32, (LOCAL_ROWS, SORT_TM), 0).astype(F32)
    section_rows = LOCAL_PAIRS * (x_new.shape[1] // LANES)
    for s, cols in enumerate(sections):
        perm = jnp.where((srow == pos1[:, cols]) | (srow == pos2[:, cols]), 1.0, 0.0)
        _store_pairs(xsl_ref, s * section_rows, _dot(perm.astype(BF16), hh[cols]).astype(BF16))


def _rows(first_row, n_rows, unit):
    return pl.ds(pl.multiple_of(first_row, unit), n_rows)


def _unpermute(tables, ys_hbm, route_ref, ysl_scr, sem, shape):
    len_ref, loc_ref, glob_ref, tot_ref = tables
    i = pl.program_id(0)
    tm, d = shape
    slab = d // LANES

    def run_copy(tile, glob, loc, rows):
        slot = tile % 2
        return pltpu.make_async_copy(ys_hbm.at[_rows(glob, rows, slab)],
                                     ysl_scr.at[slot, _rows(loc, rows, slab)], sem.at[slot])

    def fetch(tile):
        def body(r, carry):
            k = tile * TILE_RUNS + r
            n = len_ref[k]

            @pl.when(n > 0)
            def _():
                run_copy(tile, glob_ref[k], loc_ref[k], n).start()
            return carry
        lax.fori_loop(0, TILE_RUNS, body, 0, unroll=4)

    @pl.when(i == 0)
    def _():
        ysl_scr[...] = jnp.zeros_like(ysl_scr)
        fetch(i)

    @pl.when(i + 1 < pl.num_programs(0))
    def _():
        fetch(i + 1)

    @pl.when(tot_ref[i] > 0)
    def _():
        run_copy(i, 0, 0, tot_ref[i]).wait()

    ysl_ref = ysl_scr.at[i % 2]
    srow = lax.broadcasted_iota(I32, (SORT_TM, LOCAL_ROWS), 1).astype(F32)
    y_sections = []
    for s in range(tm // SORT_TM):
        rows = slice(s * SORT_TM, (s + 1) * SORT_TM)
        ysl = _load_pairs(ysl_ref, s * LOCAL_PAIRS * slab, LOCAL_PAIRS, d)
        sel = jnp.concatenate(
            [jnp.where(srow == route_ref[rows, k:k + 1], 1.0, 0.0).astype(BF16) for k in range(2)],
            axis=0)
        picked = _dot(sel, ysl)
        y_sections.append(route_ref[rows, 4:5] * picked[0:SORT_TM]
                          + route_ref[rows, 5:6] * picked[SORT_TM:2 * SORT_TM])
    return jnp.concatenate(y_sections, axis=0)


def _split_prev(fused, refs):
    if not fused:
        return None, refs
    tables, (ys_hbm, route_ref, mod_ref), (ysl_scr, sem) = refs[:4], refs[4:7], refs[-2:]
    return (tables, ys_hbm, route_ref, mod_ref, ysl_scr, sem), refs[7:-2]


def _mixer_input(prev, x_ref):
    if prev is None:
        return x_ref[...]
    tables, ys_hbm, route_ref, mod_ref, ysl_scr, sem = prev
    y = _unpermute(tables, ys_hbm, route_ref, ysl_scr, sem, x_ref.shape)
    return x_ref[...] + mod_ref[0, 5:6, :] * y


def _even_kernel(tiles_per_seq, fused, *refs):
    prev, refs = _split_prev(fused, refs)
    (x_ref, mod_ref, gmix_ref, gffn_ref, win_ref, gv_ref, ws_ref, bs_ref,
     cw_ref, wout_ref, rwt_ref, rbc_ref, earlier_ref, lower_ref,
     xo_ref, xsl_ref, route_ref, cnt_ref,
     zc_scr) = refs
    i = pl.program_id(0)
    tm = x_ref.shape[0]
    aw = gv_ref.shape[0]
    hd = aw // A_HEADS
    n_chunks = tm // CHUNK

    @pl.when(i == 0)
    def _():
        zc_scr[...] = jnp.zeros_like(zc_scr)

    x = _mixer_input(prev, x_ref)
    sh_m = mod_ref[0, 0:1, :]
    sc_m = mod_ref[0, 1:2, :]
    g_m = mod_ref[0, 2:3, :]
    h = _rms(x) * (gmix_ref[...] * (1.0 + sc_m)) + sh_m
    z = _dot(h.astype(BF16), win_ref[...])
    u = z[:, 0:aw]
    v = z[:, aw:2 * aw]
    b_gate = z[:, 2 * aw:3 * aw]
    c_gate = z[:, 3 * aw:4 * aw]
    x_in = z[:, 4 * aw:5 * aw]

    v_t = v.T
    row = lax.broadcasted_iota(I32, (CHUNK, CHUNK), 0)
    col = lax.broadcasted_iota(I32, (CHUNK, CHUNK), 1)
    causal = col <= row
    head_rows = []
    for hh in range(A_HEADS):
        vh = v_t[hh * hd:(hh + 1) * hd, :]
        msv = jnp.mean(vh * vh, axis=0, keepdims=True)
        vn = (vh * lax.rsqrt(msv + EPS) * gv_ref[hh * hd:(hh + 1) * hd, :]).astype(BF16)
        lhs = jnp.concatenate(
            [vn[:, c * CHUNK:(c + 1) * CHUNK] for c in range(n_chunks)], axis=0)
        w_m = jnp.where(causal, ws_ref[hh], 0.0).astype(BF16)
        sv_h = lax.dot_general(lhs, w_m, (((1,), (1,)), ((), ())),
                               preferred_element_type=F32)
        sv_h = sv_h + bs_ref[hh]
        head_rows.append(jnp.concatenate(
            [sv_h[c * hd:(c + 1) * hd, :] for c in range(n_chunks)], axis=1))
    sv = jnp.concatenate(head_rows, axis=0).T
    y_a = u * sv

    zc = c_gate * x_in
    first = (i % tiles_per_seq) == 0
    halo = zc_scr[tm:tm + SUBLANES, :]
    zc_scr[0:SUBLANES, :] = jnp.where(first, 0.0, halo)
    zc_scr[SUBLANES:SUBLANES + tm, :] = zc
    conv = cw_ref[2:3, :] * zc
    for k in range(CONV_WIDTH - 1):
        shift = CONV_WIDTH - 1 - k
        conv = conv + cw_ref[k:k + 1, :] * zc_scr[SUBLANES - shift:SUBLANES - shift + tm, :]
    y_b = b_gate * conv

    y = _dot(y_a.astype(BF16), wout_ref[0:aw, :]) + _dot(y_b.astype(BF16), wout_ref[aw:, :])
    x_new = x + g_m * y
    xo_ref[...] = x_new
    _route_and_sort(x_new, mod_ref, gffn_ref, rwt_ref, rbc_ref, earlier_ref, lower_ref,
                    xsl_ref, route_ref, cnt_ref)


def _odd_kernel(tiles_per_seq, fused, *refs):
    prev, refs = _split_prev(fused, refs)
    (x_ref, mod_ref, gmix_ref, gffn_ref, wpool_ref, pscale_ref,
     rwt_ref, rbc_ref, earlier_ref, lower_ref,
     xo_ref, xsl_ref, route_ref, cnt_ref) = refs[:14]
    level_scrs = refs[14:]
    i = pl.program_id(0)
    tm, d = x_ref.shape
    halo_rows = max(POOL_WINDOWS)
    gd = d // len(POOL_WINDOWS)

    @pl.when(i == 0)
    def _():
        for scr in level_scrs:
            scr[...] = jnp.zeros_like(scr)

    x = _mixer_input(prev, x_ref)
    sh_m = mod_ref[0, 0:1, :]
    sc_m = mod_ref[0, 1:2, :]
    g_m = mod_ref[0, 2:3, :]
    h = _rms(x) * (gmix_ref[...] * (1.0 + sc_m)) + sh_m

    tile_in_seq = i % tiles_per_seq
    first = tile_in_seq == 0

    sums = h
    window_sums = []
    for g, scr in enumerate(level_scrs):
        tail = scr[tm:tm + halo_rows, :]
        scr[0:halo_rows, :] = jnp.where(first, 0.0, tail)
        scr[halo_rows:halo_rows + tm, :] = sums
        lag = POOL_WINDOWS[g] // 2
        sums = sums + scr[halo_rows - lag:halo_rows - lag + tm, :]
        window_sums.append(sums[:, 0:gd])
        if g + 1 < len(level_scrs):
            sums = sums[:, gd:]

    pos = (tile_in_seq * tm + lax.broadcasted_iota(I32, (tm, 1), 0)).astype(F32)
    outs = []
    for g, win in enumerate(POOL_WINDOWS):
        cs = slice(g * gd, (g + 1) * gd)
        count = jnp.minimum(pos + 1.0, jnp.float32(win))
        pooled = window_sums[g] / count - h[:, cs]
        outs.append(_dot(pooled.astype(BF16), wpool_ref[g]))
    y = jnp.concatenate(outs, axis=-1) * pscale_ref[...]
    x_new = x + g_m * y
    xo_ref[...] = x_new
    _route_and_sort(x_new, mod_ref, gffn_ref, rwt_ref, rbc_ref, earlier_ref, lower_ref,
                    xsl_ref, route_ref, cnt_ref)


def _mixer_call(kernel_fn, x, mod_l, seq, weights, scratch, prev):
    t_tok, d = x.shape
    n_tiles = t_tok // TM
    tiles_per_seq = seq // TM
    n_slabs = d // LANES

    def const_spec(a):
        return pl.BlockSpec(a.shape, lambda i, *_, nd=a.ndim: (0,) * nd)

    def mod_spec(m):
        return pl.BlockSpec((1,) + m.shape[1:], lambda i, *_: (i // tiles_per_seq, 0, 0))

    tables, prev_inputs, prev_specs, prev_scratch = (), (), [], []
    if prev is not None:
        tables, ys, route_prev, mod_prev = prev
        prev_inputs = (ys, route_prev, mod_prev)
        prev_specs = [pl.BlockSpec(memory_space=pl.ANY),
                      pl.BlockSpec((TM, LANES), lambda i, *_: (i, 0)),
                      mod_spec(mod_prev)]
        prev_scratch = [pltpu.VMEM((2, TILE_PAIRS * n_slabs, LANES), U32),
                        pltpu.SemaphoreType.DMA((2,))]
    in_specs = prev_specs + [
        pl.BlockSpec((TM, d), lambda i, *_: (i, 0)),
        mod_spec(mod_l),
    ] + [const_spec(w) for w in weights]
    out_shape = (
        jax.ShapeDtypeStruct((t_tok, d), F32),
        jax.ShapeDtypeStruct((n_tiles * TILE_PAIRS * n_slabs, LANES), U32),
        jax.ShapeDtypeStruct((t_tok, LANES), F32),
        jax.ShapeDtypeStruct((n_tiles * SECTIONS, ROUTE_ROWS, LANES), F32),
    )
    out_specs = (
        pl.BlockSpec((TM, d), lambda i, *_: (i, 0)),
        pl.BlockSpec((TILE_PAIRS * n_slabs, LANES), lambda i, *_: (i, 0)),
        pl.BlockSpec((TM, LANES), lambda i, *_: (i, 0)),
        pl.BlockSpec((SECTIONS, ROUTE_ROWS, LANES), lambda i, *_: (i, 0, 0)),
    )
    return pl.pallas_call(
        functools.partial(kernel_fn, tiles_per_seq, prev is not None),
        grid_spec=pltpu.PrefetchScalarGridSpec(
            num_scalar_prefetch=len(tables),
            grid=(n_tiles,),
            in_specs=in_specs,
            out_specs=out_specs,
            scratch_shapes=scratch + prev_scratch,
        ),
        out_shape=out_shape,
        compiler_params=pltpu.CompilerParams(
            dimension_semantics=("arbitrary",), vmem_limit_bytes=VMEM_LIMIT),
        name=kernel_fn.__name__.strip("_"),
    )(*tables, *prev_inputs, x, mod_l, *weights)


def _expert_kernel(layer, blk_expert_ref, n_valid_ref, next_ref, slot_ref,
                   blk_first_ref, blk_last_ref, blk_fill_ref, run_len_ref, run_src_ref, run_dst_ref,
                   xsl_hbm, wg_hbm, wu_hbm, wd_hbm, ys_ref,
                   x_scr, zero_scr, wg_stage, wu_stage, wd_stage, wg_scr, wu_scr, wd_scr,
                   x_sem, sem):
    j = pl.program_id(0)
    n_valid = n_valid_ref[0]
    expert = blk_expert_ref[j]
    prev = blk_expert_ref[jnp.maximum(j - 1, 0)]
    d = wg_scr.shape[0]
    slab = d // LANES
    blk_rows = BLK_PAIRS * slab

    def fetch(blk):
        slot = blk % 2
        row0 = blk * blk_rows

        def copy_run(q):
            dst = run_dst_ref[q]
            lo = jnp.maximum(dst, row0)
            hi = jnp.minimum(dst + run_len_ref[q], row0 + blk_rows)

            @pl.when(hi > lo)
            def _():
                pltpu.make_async_copy(
                    xsl_hbm.at[_rows(run_src_ref[q] + (lo - dst), hi - lo, slab)],
                    x_scr.at[slot, _rows(lo - row0, hi - lo, slab)], x_sem.at[slot]).start()

        first = blk_first_ref[blk]

        def body(t, carry):
            copy_run(first + 2 * t)
            copy_run(first + 2 * t + 1)
            return carry

        lax.fori_loop(0, (blk_last_ref[blk] - first + 1) // 2, body, 0)
        fill = blk_fill_ref[blk]

        @pl.when(fill > 0)
        def _():
            pltpu.make_async_copy(zero_scr.at[_rows(0, fill, slab)],
                                  x_scr.at[slot, _rows(blk_rows - fill, fill, slab)],
                                  x_sem.at[slot]).start()

    @pl.when(j == 0)
    def _():
        zero_scr[...] = jnp.zeros_like(zero_scr)
        fetch(j)

    @pl.when(j + 1 < n_valid)
    def _():
        fetch(j + 1)

    def weight_copies(e, slot):
        return [pltpu.make_async_copy(hbm.at[layer, e], stage.at[slot], sem.at[slot])
                for hbm, stage in ((wg_hbm, wg_stage), (wu_hbm, wu_stage), (wd_hbm, wd_stage))]

    @pl.when(j == 0)
    def _():
        for cp in weight_copies(expert, slot_ref[expert]):
            cp.start()

    @pl.when((j == 0) | (expert != prev))
    def _():
        slot = slot_ref[expert]
        for cp in weight_copies(expert, slot):
            cp.wait()
        nxt = next_ref[expert]

        @pl.when(nxt != expert)
        def _():
            for cp in weight_copies(nxt, 1 - slot):
                cp.start()

        wg_scr[...] = wg_stage[slot].astype(BF16)
        wu_scr[...] = wu_stage[slot].astype(BF16)
        wd_scr[...] = wd_stage[slot].astype(BF16)

    slot = j % 2

    @pl.when(j < n_valid)
    def _():
        pltpu.make_async_copy(zero_scr, x_scr.at[slot], x_sem.at[slot]).wait()

    part_pairs = BLK_PAIRS // BLK_PARTS
    part_rows = part_pairs * slab
    n_parts = jnp.where(j < n_valid,
                        (blk_rows - blk_fill_ref[j] + part_rows - 1) // part_rows, 0)
    for p in range(1, BLK_PARTS + 1):
        @pl.when(n_parts == p)
        def _(p=p):
            xb = _load_pairs(x_scr.at[slot], 0, p * part_pairs, d)
            a = _dot(xb, wg_scr[...])
            b = _dot(xb, wu_scr[...])
            hm = (a * jax.nn.sigmoid(a) * b).astype(BF16)
            _store_pairs(ys_ref, 0, _dot(hm, wd_scr[...]).astype(BF16))
            if p < BLK_PARTS:
                ys_ref[p * part_rows:, :] = jnp.zeros((blk_rows - p * part_rows, LANES), U32)

    @pl.when(n_parts == 0)
    def _():
        ys_ref[...] = jnp.zeros_like(ys_ref)


def _experts(tables, xsl, n_blocks, layer, w_gate, w_up, w_down):
    _, _, d, de = w_gate.shape
    blk_rows = BLK_PAIRS * (d // LANES)
    return pl.pallas_call(
        functools.partial(_expert_kernel, layer),
        grid_spec=pltpu.PrefetchScalarGridSpec(
            num_scalar_prefetch=len(tables),
            grid=(n_blocks,),
            in_specs=[pl.BlockSpec(memory_space=pl.ANY)] * 4,
            out_specs=pl.BlockSpec((blk_rows, LANES), lambda j, *_: (j, 0)),
            scratch_shapes=[
                pltpu.VMEM((2, blk_rows, LANES), U32), pltpu.VMEM((blk_rows, LANES), U32),
                pltpu.VMEM((2, d, de), F32), pltpu.VMEM((2, d, de), F32),
                pltpu.VMEM((2, de, d), F32),
                pltpu.VMEM((d, de), BF16), pltpu.VMEM((d, de), BF16), pltpu.VMEM((de, d), BF16),
                pltpu.SemaphoreType.DMA((2,)), pltpu.SemaphoreType.DMA((2,)),
            ],
        ),
        out_shape=jax.ShapeDtypeStruct((n_blocks * blk_rows, LANES), U32),
        compiler_params=pltpu.CompilerParams(
            dimension_semantics=("arbitrary",), vmem_limit_bytes=VMEM_LIMIT),
        name="experts",
    )(*tables, xsl, w_gate, w_up, w_down)


def _final_kernel(len_ref, loc_ref, glob_ref, tot_ref, ys_hbm, x_ref, route_ref,
                  mod_ref, gfin_ref, xo_ref, ysl_scr, sem):
    prev = ((len_ref, loc_ref, glob_ref, tot_ref), ys_hbm, route_ref, mod_ref, ysl_scr, sem)
    xo_ref[...] = _rms(_mixer_input(prev, x_ref)) * gfin_ref[...]


def _final(tables, ys, x, route, mod_l, seq, g_final):
    t_tok, d = x.shape
    tiles_per_seq = seq // TM
    return pl.pallas_call(
        _final_kernel,
        grid_spec=pltpu.PrefetchScalarGridSpec(
            num_scalar_prefetch=4,
            grid=(t_tok // TM,),
            in_specs=[
                pl.BlockSpec(memory_space=pl.ANY),
                pl.BlockSpec((TM, d), lambda i, *_: (i, 0)),
                pl.BlockSpec((TM, LANES), lambda i, *_: (i, 0)),
                pl.BlockSpec((1,) + mod_l.shape[1:], lambda i, *_: (i // tiles_per_seq, 0, 0)),
                pl.BlockSpec((1, d), lambda i, *_: (0, 0)),
            ],
            out_specs=pl.BlockSpec((TM, d), lambda i, *_: (i, 0)),
            scratch_shapes=[pltpu.VMEM((2, TILE_PAIRS * (d // LANES), LANES), U32),
                            pltpu.SemaphoreType.DMA((2,))],
        ),
        out_shape=jax.ShapeDtypeStruct((t_tok, d), F32),
        compiler_params=pltpu.CompilerParams(
            dimension_semantics=("arbitrary",), vmem_limit_bytes=VMEM_LIMIT),
        name="final",
    )(*tables, ys, x, route, mod_l, g_final)


def _moe(x, xsl, cnt, layer, w_gate, w_up, w_down):
    t_tok = x.shape[0]
    n_tiles = t_tok // TM
    n_blocks = -(-(2 * t_tok + n_tiles * TILE_RUNS) // EXPERT_BLK) + N_EXPERTS

    counts = cnt[:, ROUTE_COL0:ROUTE_COL0 + N_EXPERTS, 0].astype(I32)
    run_len = (counts + 1) // 2
    section_base = (jnp.arange(counts.shape[0], dtype=I32) % SECTIONS) * LOCAL_PAIRS
    run_loc = jnp.cumsum(run_len, axis=1) - run_len + section_base[:, None]
    seg_len = jnp.sum(run_len, axis=0)
    seg_pad = (seg_len + BLK_PAIRS - 1) // BLK_PAIRS * BLK_PAIRS
    seg_end = jnp.cumsum(seg_pad)
    seg_start = seg_end - seg_pad
    run_glob = seg_start[None, :] + jnp.cumsum(run_len, axis=0) - run_len
    tile_tot = jnp.sum(run_len.reshape(n_tiles, TILE_RUNS), axis=1)
    slab = x.shape[1] // LANES
    run_tables = tuple((a.reshape(-1) * slab).astype(I32)
                       for a in (run_len, run_loc, run_glob, tile_tot))
    n_valid = (seg_end[-1:] // BLK_PAIRS).astype(I32)
    blk_pair0 = jnp.arange(n_blocks, dtype=I32) * BLK_PAIRS
    blk_expert = jnp.sum((seg_end[None, :] <= blk_pair0[:, None]).astype(I32), axis=1)
    section_pair0 = jnp.arange(counts.shape[0], dtype=I32) * LOCAL_PAIRS
    run_src = jnp.cumsum(run_len, axis=1) - run_len + section_pair0[:, None]
    em_len, em_src, em_dst = (
        jnp.concatenate([a.T.reshape(-1), jnp.full((1,), end, I32)])
        for a, end in ((run_len, 0), (run_src, 0), (run_glob, n_blocks * BLK_PAIRS)))
    blk_first = jnp.sum(((em_dst + em_len)[None, :] <= blk_pair0[:, None]).astype(I32), axis=1)
    blk_last = jnp.sum((em_dst[None, :] < (blk_pair0 + BLK_PAIRS)[:, None]).astype(I32), axis=1)
    seg_stop = jnp.sum(jnp.where(
        jnp.arange(N_EXPERTS, dtype=I32)[None, :] == blk_expert[:, None],
        (seg_start + seg_len)[None, :], 0), axis=1)
    blk_fill = BLK_PAIRS - jnp.clip(seg_stop - blk_pair0, 0, BLK_PAIRS)
    pull_tables = tuple((a * slab).astype(I32) for a in (blk_fill, em_len, em_src, em_dst))
    experts = jnp.arange(N_EXPERTS, dtype=I32)
    used = seg_len > 0
    last_used = jnp.max(jnp.where(used, experts, 0))
    blk_expert = jnp.minimum(blk_expert, last_used).astype(I32)
    later = jnp.where((experts[None, :] > experts[:, None]) & used[None, :],
                      experts[None, :], N_EXPERTS)
    next_used = jnp.min(later, axis=1)
    next_used = jnp.where(next_used == N_EXPERTS, experts, next_used).astype(I32)
    slot_of = ((jnp.cumsum(used.astype(I32)) - 1) % 2).astype(I32)

    expert_tables = ((blk_expert, n_valid, next_used, slot_of,
                      blk_first.astype(I32), blk_last.astype(I32)) + pull_tables)
    return run_tables, _experts(expert_tables, xsl, n_blocks, layer, w_gate, w_up, w_down)


def kernel(x, c, w_ada, b_ada, norm_mix_g, norm_ffn_g, w_in_even, sgu_norm_g, w_spatial, b_spatial, conv_w, w_out_even, w_pool, pool_scale, w_group_router, b_group_router, w_expert_router, b_expert_router, moe_w_gate, moe_w_up, moe_w_down, final_norm_g):
    bsz, seq, d = x.shape
    depth = w_ada.shape[0]
    t_tok = bsz * seq
    assert seq % TM == 0 and d % LANES == 0 and w_spatial.shape[-1] == CHUNK
    assert all(w == 2 ** (g + 1) for g, w in enumerate(POOL_WINDOWS))

    mod = _modulation(c, w_ada, b_ada).reshape(depth, bsz, 6, d)
    earlier = jnp.triu(jnp.ones((SORT_TM, SORT_TM), BF16), 1)
    lower = jnp.tril(jnp.ones((LANES, LANES), F32), -1)
    g_final = final_norm_g.reshape(1, d)

    xf = x.reshape(t_tok, d)
    prev = None
    for l in range(depth):
        i = l // 2
        rw = jnp.concatenate([w_group_router[l], w_expert_router[l]], axis=1).T
        rw = jnp.pad(rw, ((0, ROUTE_ROWS - rw.shape[0]), (0, 0))).astype(BF16)
        rb = jnp.concatenate([b_group_router[l], b_expert_router[l]])
        rb = jnp.pad(rb, (0, ROUTE_ROWS - rb.shape[0])).reshape(ROUTE_ROWS, 1)
        route_w = [rw, rb, earlier, lower]
        gmix = norm_mix_g[l].reshape(1, d)
        gffn = norm_ffn_g[l].reshape(1, d)
        if l % 2 == 0:
            aw = sgu_norm_g.shape[1]
            weights = [gmix, gffn, w_in_even[i].astype(BF16), sgu_norm_g[i].reshape(aw, 1),
                       w_spatial[i], b_spatial[i].reshape(A_HEADS, 1, CHUNK), conv_w[i],
                       w_out_even[i].astype(BF16)] + route_w
            scratch = [pltpu.VMEM((TM + SUBLANES, conv_w.shape[-1]), F32)]
            xf, xsl, route, cnt = _mixer_call(_even_kernel, xf, mod[l], seq, weights, scratch, prev)
        else:
            weights = [gmix, gffn, w_pool[i].astype(BF16), pool_scale[i].reshape(1, d)] + route_w
            gd = d // len(POOL_WINDOWS)
            scratch = [pltpu.VMEM((TM + max(POOL_WINDOWS), d - g * gd), F32)
                       for g in range(len(POOL_WINDOWS))]
            xf, xsl, route, cnt = _mixer_call(_odd_kernel, xf, mod[l], seq, weights, scratch, prev)
        run_tables, ys = _moe(xf, xsl, cnt, l, moe_w_gate, moe_w_up, moe_w_down)
        prev = (run_tables, ys, route, mod[l])
    run_tables, ys, route, mod_l = prev
    return _final(run_tables, ys, xf, route, mod_l, seq, g_final).reshape(bsz, seq, d)
```

```python
import functools

import jax
import jax.numpy as jnp
from jax import lax
from jax.experimental import pallas as pl
from jax.experimental.pallas import tpu as pltpu

F32 = jnp.float32
BF16 = jnp.bfloat16
U32 = jnp.uint32
I32 = jnp.int32

EPS = 1e-6
LANES = 128
SUBLANES = 8
CHUNK = 128
A_HEADS = 8
N_GROUPS = 4
EXPERTS_PER_GROUP = 8
N_EXPERTS = N_GROUPS * EXPERTS_PER_GROUP
POOL_WINDOWS = (2, 4, 8, 16)
CONV_WIDTH = 3
ROUTE_COL0 = N_GROUPS
ROUTE_ROWS = 48

TM = 512
SORT_TM = 256
SECTIONS = TM // SORT_TM
LOCAL_ROWS = 2 * SORT_TM + LANES
LOCAL_PAIRS = LOCAL_ROWS // 2
TILE_PAIRS = SECTIONS * LOCAL_PAIRS
TILE_RUNS = SECTIONS * N_EXPERTS
EXPERT_BLK = 1024
BLK_PAIRS = EXPERT_BLK // 2
BLK_PARTS = 8
FETCH_UNROLL = 4
MOD_TN = 3072
VMEM_LIMIT = 56 * 1024 * 1024


def _rms(x):
    return x * lax.rsqrt(jnp.mean(x * x, axis=-1, keepdims=True) + EPS)


def _dot(a, b):
    return jnp.dot(a, b, preferred_element_type=F32)


def _store_pairs(ref, base, rows_bf16):
    words = pltpu.bitcast(rows_bf16, U32)
    n_pairs, d = words.shape
    slab = d // LANES
    for c in range(slab):
        ref[pl.ds(base + c, n_pairs, stride=slab), :] = words[:, c * LANES:(c + 1) * LANES]


def _load_pairs(ref, base, n_pairs, d):
    slab = d // LANES
    words = jnp.concatenate(
        [ref[pl.ds(base + c, n_pairs, stride=slab), :] for c in range(slab)], axis=-1)
    return pltpu.bitcast(words, BF16)


def _mod_kernel(c_ref, w_ref, b_ref, o_ref):
    c = c_ref[...]
    ca = c * jax.nn.sigmoid(c)
    o_ref[0] = jnp.dot(ca, w_ref[0], precision=lax.Precision.HIGHEST,
                       preferred_element_type=F32) + b_ref[0]


def _modulation(c, w_ada, b_ada):
    depth, d, n = w_ada.shape
    bsz = c.shape[0]
    return pl.pallas_call(
        _mod_kernel,
        grid=(depth, n // MOD_TN),
        in_specs=[
            pl.BlockSpec((bsz, d), lambda l, j: (0, 0)),
            pl.BlockSpec((1, d, MOD_TN), lambda l, j: (l, 0, j)),
            pl.BlockSpec((1, 1, MOD_TN), lambda l, j: (l, 0, j)),
        ],
        out_specs=pl.BlockSpec((1, bsz, MOD_TN), lambda l, j: (l, 0, j)),
        out_shape=jax.ShapeDtypeStruct((depth, bsz, n), F32),
        compiler_params=pltpu.CompilerParams(
            dimension_semantics=("arbitrary", "arbitrary"),
            vmem_limit_bytes=VMEM_LIMIT),
        name="modulation",
    )(c, w_ada, b_ada.reshape(depth, 1, n))


def _route_and_sort(x_new, mod_ref, gffn_ref, rwt_ref, rbc_ref, earlier_ref, lower_ref,
                    xsl_ref, route_ref, cnt_ref):
    tm = x_new.shape[0]
    sh_f = mod_ref[0, 3:4, :]
    sc_f = mod_ref[0, 4:5, :]
    h2 = _rms(x_new) * (gffn_ref[...] * (1.0 + sc_f)) + sh_f

    rr = ROUTE_ROWS
    hh = h2.astype(BF16)
    logits = lax.dot_general(rwt_ref[...], hh, (((1,), (1,)), ((), ())),
                             preferred_element_type=F32) + rbc_ref[...]

    row = lax.broadcasted_iota(I32, logits.shape, 0).astype(F32)
    neg = jnp.float32(-jnp.inf)
    big = jnp.float32(LANES)

    gl = jnp.where(row < N_GROUPS, logits, neg)
    gmax = jnp.max(gl, axis=0, keepdims=True)
    g_sel = jnp.min(jnp.where(gl == gmax, row, big), axis=0, keepdims=True)
    g_w = 1.0 / jnp.sum(jnp.exp(gl - gmax), axis=0, keepdims=True)

    lo = ROUTE_COL0 + EXPERTS_PER_GROUP * g_sel
    el = jnp.where((row >= lo) & (row < lo + EXPERTS_PER_GROUP), logits, neg)
    m1 = jnp.max(el, axis=0, keepdims=True)
    i1 = jnp.min(jnp.where(el == m1, row, big), axis=0, keepdims=True)
    el2 = jnp.where(row == i1, neg, el)
    m2 = jnp.max(el2, axis=0, keepdims=True)
    i2 = jnp.min(jnp.where(el2 == m2, row, big), axis=0, keepdims=True)
    t = jnp.exp(m2 - m1)
    gate1 = g_w / (1.0 + t)
    gate2 = g_w * t / (1.0 + t)

    sections = [slice(s * SORT_TM, (s + 1) * SORT_TM) for s in range(tm // SORT_TM)]
    is1 = row == i1
    is2 = row == i2
    onehot = jnp.where(is1 | is2, 1.0, 0.0)
    before = _dot(onehot.astype(BF16), earlier_ref[...])
    lane = lax.broadcasted_iota(I32, (rr, LANES), 1)
    cnts = [jnp.sum(onehot[:, cols], axis=1, keepdims=True) for cols in sections]
    pairs = jnp.zeros((rr, LANES), F32)
    for s, cnt in enumerate(cnts):
        cnt_ref[s] = jnp.broadcast_to(cnt, (rr, LANES))
        pairs = jnp.where(lane == s, jnp.floor((cnt + 1.0) * 0.5), pairs)
    pairs = jnp.concatenate([pairs, jnp.zeros((LANES - rr, LANES), F32)], axis=0)
    pair_start = jnp.dot(lower_ref[...], pairs, precision=lax.Precision.HIGHEST,
                         preferred_element_type=F32)
    pos = jnp.concatenate([before[:, cols] + 2.0 * pair_start[0:rr, s:s + 1]
                           for s, cols in enumerate(sections)], axis=1)
    pos1 = jnp.sum(jnp.where(is1, pos, 0.0), axis=0, keepdims=True)
    pos2 = jnp.sum(jnp.where(is2, pos, 0.0), axis=0, keepdims=True)

    r8 = lax.broadcasted_iota(I32, (SUBLANES, tm), 0)
    route_t = jnp.where(r8 == 0, pos1, 0.0)
    route_t = jnp.where(r8 == 1, pos2, route_t)
    route_t = jnp.where(r8 == 4, gate1, route_t)
    route_t = jnp.where(r8 == 5, gate2, route_t)
    route_t = jnp.concatenate([route_t, jnp.zeros((LANES - SUBLANES, tm), F32)], axis=0)
    route_ref[...] = route_t.T

    srow = lax.broadcasted_iota(I32, (LOCAL_ROWS, SORT_TM), 0).astype(F32)
    section_rows = LOCAL_PAIRS * (x_new.shape[1] // LANES)
    for s, cols in enumerate(sections):
        perm = jnp.where((srow == pos1[:, cols]) | (srow == pos2[:, cols]), 1.0, 0.0)
        _store_pairs(xsl_ref, s * section_rows, _dot(perm.astype(BF16), hh[cols]).astype(BF16))


def _rows(first_row, n_rows, unit):
    return pl.ds(pl.multiple_of(first_row, unit), n_rows)


def _unpermute(tables, ys_hbm, route_ref, ysl_scr, sem, shape):
    len_ref, loc_ref, glob_ref, tot_ref = tables
    i = pl.program_id(0)
    tm, d = shape
    slab = d // LANES

    def run_copy(tile, glob, loc, rows):
        slot = tile % 2
        return pltpu.make_async_copy(ys_hbm.at[_rows(glob, rows, slab)],
                                     ysl_scr.at[slot, _rows(loc, rows, slab)], sem.at[slot])

    def fetch(tile):
        def body(r, carry):
            k = tile * TILE_RUNS + r
            n = len_ref[k]

            @pl.when(n > 0)
            def _():
                run_copy(tile, glob_ref[k], loc_ref[k], n).start()
            return carry
        lax.fori_loop(0, TILE_RUNS, body, 0, unroll=4)

    @pl.when(i == 0)
    def _():
        ysl_scr[...] = jnp.zeros_like(ysl_scr)
        fetch(i)

    @pl.when(i + 1 < pl.num_programs(0))
    def _():
        fetch(i + 1)

    @pl.when(tot_ref[i] > 0)
    def _():
        run_copy(i, 0, 0, tot_ref[i]).wait()

    ysl_ref = ysl_scr.at[i % 2]
    srow = lax.broadcasted_iota(I32, (SORT_TM, LOCAL_ROWS), 1).astype(F32)
    y_sections = []
    for s in range(tm // SORT_TM):
        rows = slice(s * SORT_TM, (s + 1) * SORT_TM)
        ysl = _load_pairs(ysl_ref, s * LOCAL_PAIRS * slab, LOCAL_PAIRS, d)
        sel = jnp.concatenate(
            [jnp.where(srow == route_ref[rows, k:k + 1], 1.0, 0.0).astype(BF16) for k in range(2)],
            axis=0)
        picked = _dot(sel, ysl)
        y_sections.append(route_ref[rows, 4:5] * picked[0:SORT_TM]
                          + route_ref[rows, 5:6] * picked[SORT_TM:2 * SORT_TM])
    return jnp.concatenate(y_sections, axis=0)


def _split_prev(fused, refs):
    if not fused:
        return None, refs
    tables, (ys_hbm, route_ref, mod_ref), (ysl_scr, sem) = refs[:4], refs[4:7], refs[-2:]
    return (tables, ys_hbm, route_ref, mod_ref, ysl_scr, sem), refs[7:-2]


def _mixer_input(prev, x_ref):
    if prev is None:
        return x_ref[...]
    tables, ys_hbm, route_ref, mod_ref, ysl_scr, sem = prev
    y = _unpermute(tables, ys_hbm, route_ref, ysl_scr, sem, x_ref.shape)
    return x_ref[...] + mod_ref[0, 5:6, :] * y


def _even_kernel(tiles_per_seq, fused, *refs):
    prev, refs = _split_prev(fused, refs)
    (x_ref, mod_ref, gmix_ref, gffn_ref, win_ref, gv_ref, ws_ref, bs_ref,
     cw_ref, wout_ref, rwt_ref, rbc_ref, earlier_ref, lower_ref,
     xo_ref, xsl_ref, route_ref, cnt_ref,
     zc_scr) = refs
    i = pl.program_id(0)
    tm = x_ref.shape[0]
    aw = gv_ref.shape[0]
    hd = aw // A_HEADS
    n_chunks = tm // CHUNK

    @pl.when(i == 0)
    def _():
        zc_scr[...] = jnp.zeros_like(zc_scr)

    x = _mixer_input(prev, x_ref)
    sh_m = mod_ref[0, 0:1, :]
    sc_m = mod_ref[0, 1:2, :]
    g_m = mod_ref[0, 2:3, :]
    h = _rms(x) * (gmix_ref[...] * (1.0 + sc_m)) + sh_m
    z = _dot(h.astype(BF16), win_ref[...])
    u = z[:, 0:aw]
    v = z[:, aw:2 * aw]
    b_gate = z[:, 2 * aw:3 * aw]
    c_gate = z[:, 3 * aw:4 * aw]
    x_in = z[:, 4 * aw:5 * aw]

    v_t = v.T
    row = lax.broadcasted_iota(I32, (CHUNK, CHUNK), 0)
    col = lax.broadcasted_iota(I32, (CHUNK, CHUNK), 1)
    causal = col <= row
    head_rows = []
    for hh in range(A_HEADS):
        vh = v_t[hh * hd:(hh + 1) * hd, :]
        msv = jnp.mean(vh * vh, axis=0, keepdims=True)
        vn = (vh * lax.rsqrt(msv + EPS) * gv_ref[hh * hd:(hh + 1) * hd, :]).astype(BF16)
        lhs = jnp.concatenate(
            [vn[:, c * CHUNK:(c + 1) * CHUNK] for c in range(n_chunks)], axis=0)
        w_m = jnp.where(causal, ws_ref[hh], 0.0).astype(BF16)
        sv_h = lax.dot_general(lhs, w_m, (((1,), (1,)), ((), ())),
                               preferred_element_type=F32)
        sv_h = sv_h + bs_ref[hh]
        head_rows.append(jnp.concatenate(
            [sv_h[c * hd:(c + 1) * hd, :] for c in range(n_chunks)], axis=1))
    sv = jnp.concatenate(head_rows, axis=0).T
    y_a = u * sv

    zc = c_gate * x_in
    first = (i % tiles_per_seq) == 0
    halo = zc_scr[tm:tm + SUBLANES, :]
    zc_scr[0:SUBLANES, :] = jnp.where(first, 0.0, halo)
    zc_scr[SUBLANES:SUBLANES + tm, :] = zc
    conv = cw_ref[2:3, :] * zc
    for k in range(CONV_WIDTH - 1):
        shift = CONV_WIDTH - 1 - k
        conv = conv + cw_ref[k:k + 1, :] * zc_scr[SUBLANES - shift:SUBLANES - shift + tm, :]
    y_b = b_gate * conv

    y = _dot(y_a.astype(BF16), wout_ref[0:aw, :]) + _dot(y_b.astype(BF16), wout_ref[aw:, :])
    x_new = x + g_m * y
    xo_ref[...] = x_new
    _route_and_sort(x_new, mod_ref, gffn_ref, rwt_ref, rbc_ref, earlier_ref, lower_ref,
                    xsl_ref, route_ref, cnt_ref)


def _odd_kernel(tiles_per_seq, fused, *refs):
    prev, refs = _split_prev(fused, refs)
    (x_ref, mod_ref, gmix_ref, gffn_ref, wpool_ref, pscale_ref,
     rwt_ref, rbc_ref, earlier_ref, lower_ref,
     xo_ref, xsl_ref, route_ref, cnt_ref) = refs[:14]
    level_scrs = refs[14:]
    i = pl.program_id(0)
    tm, d = x_ref.shape
    halo_rows = max(POOL_WINDOWS)
    gd = d // len(POOL_WINDOWS)

    @pl.when(i == 0)
    def _():
        for scr in level_scrs:
            scr[...] = jnp.zeros_like(scr)

    x = _mixer_input(prev, x_ref)
    sh_m = mod_ref[0, 0:1, :]
    sc_m = mod_ref[0, 1:2, :]
    g_m = mod_ref[0, 2:3, :]
    h = _rms(x) * (gmix_ref[...] * (1.0 + sc_m)) + sh_m

    tile_in_seq = i % tiles_per_seq
    first = tile_in_seq == 0

    sums = h
    window_sums = []
    for g, scr in enumerate(level_scrs):
        tail = scr[tm:tm + halo_rows, :]
        scr[0:halo_rows, :] = jnp.where(first, 0.0, tail)
        scr[halo_rows:halo_rows + tm, :] = sums
        lag = POOL_WINDOWS[g] // 2
        sums = sums + scr[halo_rows - lag:halo_rows - lag + tm, :]
        window_sums.append(sums[:, 0:gd])
        if g + 1 < len(level_scrs):
            sums = sums[:, gd:]

    pos = (tile_in_seq * tm + lax.broadcasted_iota(I32, (tm, 1), 0)).astype(F32)
    outs = []
    for g, win in enumerate(POOL_WINDOWS):
        cs = slice(g * gd, (g + 1) * gd)
        count = jnp.minimum(pos + 1.0, jnp.float32(win))
        pooled = window_sums[g] / count - h[:, cs]
        outs.append(_dot(pooled.astype(BF16), wpool_ref[g]))
    y = jnp.concatenate(outs, axis=-1) * pscale_ref[...]
    x_new = x + g_m * y
    xo_ref[...] = x_new
    _route_and_sort(x_new, mod_ref, gffn_ref, rwt_ref, rbc_ref, earlier_ref, lower_ref,
                    xsl_ref, route_ref, cnt_ref)


def _mixer_call(kernel_fn, x, mod_l, seq, weights, scratch, prev):
    t_tok, d = x.shape
    n_tiles = t_tok // TM
    tiles_per_seq = seq // TM
    n_slabs = d // LANES

    def const_spec(a):
        return pl.BlockSpec(a.shape, lambda i, *_, nd=a.ndim: (0,) * nd)

    def mod_spec(m):
        return pl.BlockSpec((1,) + m.shape[1:], lambda i, *_: (i // tiles_per_seq, 0, 0))

    tables, prev_inputs, prev_specs, prev_scratch = (), (), [], []
    if prev is not None:
        tables, ys, route_prev, mod_prev = prev
        prev_inputs = (ys, route_prev, mod_prev)
        prev_specs = [pl.BlockSpec(memory_space=pl.ANY),
                      pl.BlockSpec((TM, LANES), lambda i, *_: (i, 0)),
                      mod_spec(mod_prev)]
        prev_scratch = [pltpu.VMEM((2, TILE_PAIRS * n_slabs, LANES), U32),
                        pltpu.SemaphoreType.DMA((2,))]
    in_specs = prev_specs + [
        pl.BlockSpec((TM, d), lambda i, *_: (i, 0)),
        mod_spec(mod_l),
    ] + [const_spec(w) for w in weights]
    out_shape = (
        jax.ShapeDtypeStruct((t_tok, d), F32),
        jax.ShapeDtypeStruct((n_tiles * TILE_PAIRS * n_slabs, LANES), U32),
        jax.ShapeDtypeStruct((t_tok, LANES), F32),
        jax.ShapeDtypeStruct((n_tiles * SECTIONS, ROUTE_ROWS, LANES), F32),
    )
    out_specs = (
        pl.BlockSpec((TM, d), lambda i, *_: (i, 0)),
        pl.BlockSpec((TILE_PAIRS * n_slabs, LANES), lambda i, *_: (i, 0)),
        pl.BlockSpec((TM, LANES), lambda i, *_: (i, 0)),
        pl.BlockSpec((SECTIONS, ROUTE_ROWS, LANES), lambda i, *_: (i, 0, 0)),
    )
    return pl.pallas_call(
        functools.partial(kernel_fn, tiles_per_seq, prev is not None),
        grid_spec=pltpu.PrefetchScalarGridSpec(
            num_scalar_prefetch=len(tables),
            grid=(n_tiles,),
            in_specs=in_specs,
            out_specs=out_specs,
            scratch_shapes=scratch + prev_scratch,
        ),
        out_shape=out_shape,
        compiler_params=pltpu.CompilerParams(
            dimension_semantics=("arbitrary",), vmem_limit_bytes=VMEM_LIMIT),
        name=kernel_fn.__name__.strip("_"),
    )(*tables, *prev_inputs, x, mod_l, *weights)


def _expert_kernel(layer, blk_expert_ref, n_valid_ref, next_ref, slot_ref,
                   blk_first_ref, blk_last_ref, blk_fill_ref, run_len_ref, run_src_ref, run_dst_ref,
                   xsl_hbm, wg_hbm, wu_hbm, wd_hbm, ys_ref,
                   x_scr, zero_scr, wg_stage, wu_stage, wd_stage, wg_scr, wu_scr, wd_scr,
                   x_sem, sem):
    j = pl.program_id(0)
    n_valid = n_valid_ref[0]
    expert = blk_expert_ref[j]
    prev = blk_expert_ref[jnp.maximum(j - 1, 0)]
    d = wg_scr.shape[0]
    slab = d // LANES
    blk_rows = BLK_PAIRS * slab

    def fetch(blk):
        slot = blk % 2
        row0 = blk * blk_rows

        def copy_run(q):
            dst = run_dst_ref[q]
            lo = jnp.maximum(dst, row0)
            hi = jnp.minimum(dst + run_len_ref[q], row0 + blk_rows)

            @pl.when(hi > lo)
            def _():
                pltpu.make_async_copy(
                    xsl_hbm.at[_rows(run_src_ref[q] + (lo - dst), hi - lo, slab)],
                    x_scr.at[slot, _rows(lo - row0, hi - lo, slab)], x_sem.at[slot]).start()

        first = blk_first_ref[blk]

        def body(t, carry):
            for u in range(FETCH_UNROLL):
                copy_run(first + FETCH_UNROLL * t + u)
            return carry

        lax.fori_loop(0, (blk_last_ref[blk] - first + FETCH_UNROLL - 1) // FETCH_UNROLL, body, 0)
        fill = blk_fill_ref[blk]

        @pl.when(fill > 0)
        def _():
            pltpu.make_async_copy(zero_scr.at[_rows(0, fill, slab)],
                                  x_scr.at[slot, _rows(blk_rows - fill, fill, slab)],
                                  x_sem.at[slot]).start()

    @pl.when(j == 0)
    def _():
        zero_scr[...] = jnp.zeros_like(zero_scr)
        fetch(j)

    @pl.when(j + 1 < n_valid)
    def _():
        fetch(j + 1)

    def weight_copies(e, slot):
        return [pltpu.make_async_copy(hbm.at[layer, e], stage.at[slot], sem.at[slot])
                for hbm, stage in ((wg_hbm, wg_stage), (wu_hbm, wu_stage), (wd_hbm, wd_stage))]

    @pl.when(j == 0)
    def _():
        for cp in weight_copies(expert, slot_ref[expert]):
            cp.start()

    @pl.when((j == 0) | (expert != prev))
    def _():
        slot = slot_ref[expert]
        for cp in weight_copies(expert, slot):
            cp.wait()
        nxt = next_ref[expert]

        @pl.when(nxt != expert)
        def _():
            for cp in weight_copies(nxt, 1 - slot):
                cp.start()

        wg_scr[...] = wg_stage[slot].astype(BF16)
        wu_scr[...] = wu_stage[slot].astype(BF16)
        wd_scr[...] = wd_stage[slot].astype(BF16)

    slot = j % 2

    @pl.when(j < n_valid)
    def _():
        pltpu.make_async_copy(zero_scr, x_scr.at[slot], x_sem.at[slot]).wait()

    part_pairs = BLK_PAIRS // BLK_PARTS
    part_rows = part_pairs * slab
    n_parts = jnp.where(j < n_valid,
                        (blk_rows - blk_fill_ref[j] + part_rows - 1) // part_rows, 0)
    for p in range(1, BLK_PARTS + 1):
        @pl.when(n_parts == p)
        def _(p=p):
            xb = _load_pairs(x_scr.at[slot], 0, p * part_pairs, d)
            a = _dot(xb, wg_scr[...])
            b = _dot(xb, wu_scr[...])
            hm = (a * jax.nn.sigmoid(a) * b).astype(BF16)
            _store_pairs(ys_ref, 0, _dot(hm, wd_scr[...]).astype(BF16))
            if p < BLK_PARTS:
                ys_ref[p * part_rows:, :] = jnp.zeros((blk_rows - p * part_rows, LANES), U32)

    @pl.when(n_parts == 0)
    def _():
        ys_ref[...] = jnp.zeros_like(ys_ref)


def _experts(tables, xsl, n_blocks, layer, w_gate, w_up, w_down):
    _, _, d, de = w_gate.shape
    blk_rows = BLK_PAIRS * (d // LANES)
    return pl.pallas_call(
        functools.partial(_expert_kernel, layer),
        grid_spec=pltpu.PrefetchScalarGridSpec(
            num_scalar_prefetch=len(tables),
            grid=(n_blocks,),
            in_specs=[pl.BlockSpec(memory_space=pl.ANY)] * 4,
            out_specs=pl.BlockSpec((blk_rows, LANES), lambda j, *_: (j, 0)),
            scratch_shapes=[
                pltpu.VMEM((2, blk_rows, LANES), U32), pltpu.VMEM((blk_rows, LANES), U32),
                pltpu.VMEM((2, d, de), F32), pltpu.VMEM((2, d, de), F32),
                pltpu.VMEM((2, de, d), F32),
                pltpu.VMEM((d, de), BF16), pltpu.VMEM((d, de), BF16), pltpu.VMEM((de, d), BF16),
                pltpu.SemaphoreType.DMA((2,)), pltpu.SemaphoreType.DMA((2,)),
            ],
        ),
        out_shape=jax.ShapeDtypeStruct((n_blocks * blk_rows, LANES), U32),
        compiler_params=pltpu.CompilerParams(
            dimension_semantics=("arbitrary",), vmem_limit_bytes=VMEM_LIMIT),
        name="experts",
    )(*tables, xsl, w_gate, w_up, w_down)


def _final_kernel(len_ref, loc_ref, glob_ref, tot_ref, ys_hbm, x_ref, route_ref,
                  mod_ref, gfin_ref, xo_ref, ysl_scr, sem):
    prev = ((len_ref, loc_ref, glob_ref, tot_ref), ys_hbm, route_ref, mod_ref, ysl_scr, sem)
    xo_ref[...] = _rms(_mixer_input(prev, x_ref)) * gfin_ref[...]


def _final(tables, ys, x, route, mod_l, seq, g_final):
    t_tok, d = x.shape
    tiles_per_seq = seq // TM
    return pl.pallas_call(
        _final_kernel,
        grid_spec=pltpu.PrefetchScalarGridSpec(
            num_scalar_prefetch=4,
            grid=(t_tok // TM,),
            in_specs=[
                pl.BlockSpec(memory_space=pl.ANY),
                pl.BlockSpec((TM, d), lambda i, *_: (i, 0)),
                pl.BlockSpec((TM, LANES), lambda i, *_: (i, 0)),
                pl.BlockSpec((1,) + mod_l.shape[1:], lambda i, *_: (i // tiles_per_seq, 0, 0)),
                pl.BlockSpec((1, d), lambda i, *_: (0, 0)),
            ],
            out_specs=pl.BlockSpec((TM, d), lambda i, *_: (i, 0)),
            scratch_shapes=[pltpu.VMEM((2, TILE_PAIRS * (d // LANES), LANES), U32),
                            pltpu.SemaphoreType.DMA((2,))],
        ),
        out_shape=jax.ShapeDtypeStruct((t_tok, d), F32),
        compiler_params=pltpu.CompilerParams(
            dimension_semantics=("arbitrary",), vmem_limit_bytes=VMEM_LIMIT),
        name="final",
    )(*tables, ys, x, route, mod_l, g_final)


def _moe(x, xsl, cnt, layer, w_gate, w_up, w_down):
    t_tok = x.shape[0]
    n_tiles = t_tok // TM
    n_blocks = -(-(2 * t_tok + n_tiles * TILE_RUNS) // EXPERT_BLK) + N_EXPERTS

    counts = cnt[:, ROUTE_COL0:ROUTE_COL0 + N_EXPERTS, 0].astype(I32)
    run_len = (counts + 1) // 2
    section_base = (jnp.arange(counts.shape[0], dtype=I32) % SECTIONS) * LOCAL_PAIRS
    run_loc = jnp.cumsum(run_len, axis=1) - run_len + section_base[:, None]
    seg_len = jnp.sum(run_len, axis=0)
    seg_pad = (seg_len + BLK_PAIRS - 1) // BLK_PAIRS * BLK_PAIRS
    seg_end = jnp.cumsum(seg_pad)
    seg_start = seg_end - seg_pad
    run_glob = seg_start[None, :] + jnp.cumsum(run_len, axis=0) - run_len
    tile_tot = jnp.sum(run_len.reshape(n_tiles, TILE_RUNS), axis=1)
    slab = x.shape[1] // LANES
    run_tables = tuple((a.reshape(-1) * slab).astype(I32)
                       for a in (run_len, run_loc, run_glob, tile_tot))
    n_valid = (seg_end[-1:] // BLK_PAIRS).astype(I32)
    blk_pair0 = jnp.arange(n_blocks, dtype=I32) * BLK_PAIRS
    blk_expert = jnp.sum((seg_end[None, :] <= blk_pair0[:, None]).astype(I32), axis=1)
    section_pair0 = jnp.arange(counts.shape[0], dtype=I32) * LOCAL_PAIRS
    run_src = jnp.cumsum(run_len, axis=1) - run_len + section_pair0[:, None]
    em_len, em_src, em_dst = (
        jnp.concatenate([a.T.reshape(-1), jnp.full((FETCH_UNROLL - 1,), end, I32)])
        for a, end in ((run_len, 0), (run_src, 0), (run_glob, n_blocks * BLK_PAIRS)))
    blk_first = jnp.sum(((em_dst + em_len)[None, :] <= blk_pair0[:, None]).astype(I32), axis=1)
    blk_last = jnp.sum((em_dst[None, :] < (blk_pair0 + BLK_PAIRS)[:, None]).astype(I32), axis=1)
    seg_stop = jnp.sum(jnp.where(
        jnp.arange(N_EXPERTS, dtype=I32)[None, :] == blk_expert[:, None],
        (seg_start + seg_len)[None, :], 0), axis=1)
    blk_fill = BLK_PAIRS - jnp.clip(seg_stop - blk_pair0, 0, BLK_PAIRS)
    pull_tables = tuple((a * slab).astype(I32) for a in (blk_fill, em_len, em_src, em_dst))
    experts = jnp.arange(N_EXPERTS, dtype=I32)
    used = seg_len > 0
    last_used = jnp.max(jnp.where(used, experts, 0))
    blk_expert = jnp.minimum(blk_expert, last_used).astype(I32)
    later = jnp.where((experts[None, :] > experts[:, None]) & used[None, :],
                      experts[None, :], N_EXPERTS)
    next_used = jnp.min(later, axis=1)
    next_used = jnp.where(next_used == N_EXPERTS, experts, next_used).astype(I32)
    slot_of = ((jnp.cumsum(used.astype(I32)) - 1) % 2).astype(I32)

    expert_tables = ((blk_expert, n_valid, next_used, slot_of,
                      blk_first.astype(I32), blk_last.astype(I32)) + pull_tables)
    return run_tables, _experts(expert_tables, xsl, n_blocks, layer, w_gate, w_up, w_down)


def kernel(x, c, w_ada, b_ada, norm_mix_g, norm_ffn_g, w_in_even, sgu_norm_g, w_spatial, b_spatial, conv_w, w_out_even, w_pool, pool_scale, w_group_router, b_group_router, w_expert_router, b_expert_router, moe_w_gate, moe_w_up, moe_w_down, final_norm_g):
    bsz, seq, d = x.shape
    depth = w_ada.shape[0]
    t_tok = bsz * seq
    assert seq % TM == 0 and d % LANES == 0 and w_spatial.shape[-1] == CHUNK
    assert all(w == 2 ** (g + 1) for g, w in enumerate(POOL_WINDOWS))

    mod = _modulation(c, w_ada, b_ada).reshape(depth, bsz, 6, d)
    tok = jnp.arange(TM)
    earlier = ((tok[:, None] < tok[None, :])
               & (tok[:, None] // SORT_TM == tok[None, :] // SORT_TM)).astype(BF16)
    lower = jnp.tril(jnp.ones((LANES, LANES), F32), -1)
    g_final = final_norm_g.reshape(1, d)

    xf = x.reshape(t_tok, d)
    prev = None
    for l in range(depth):
        i = l // 2
        rw = jnp.concatenate([w_group_router[l], w_expert_router[l]], axis=1).T
        rw = jnp.pad(rw, ((0, ROUTE_ROWS - rw.shape[0]), (0, 0))).astype(BF16)
        rb = jnp.concatenate([b_group_router[l], b_expert_router[l]])
        rb = jnp.pad(rb, (0, ROUTE_ROWS - rb.shape[0])).reshape(ROUTE_ROWS, 1)
        route_w = [rw, rb, earlier, lower]
        gmix = norm_mix_g[l].reshape(1, d)
        gffn = norm_ffn_g[l].reshape(1, d)
        if l % 2 == 0:
            aw = sgu_norm_g.shape[1]
            weights = [gmix, gffn, w_in_even[i].astype(BF16), sgu_norm_g[i].reshape(aw, 1),
                       w_spatial[i], b_spatial[i].reshape(A_HEADS, 1, CHUNK), conv_w[i],
                       w_out_even[i].astype(BF16)] + route_w
            scratch = [pltpu.VMEM((TM + SUBLANES, conv_w.shape[-1]), F32)]
            xf, xsl, route, cnt = _mixer_call(_even_kernel, xf, mod[l], seq, weights, scratch, prev)
        else:
            weights = [gmix, gffn, w_pool[i].astype(BF16), pool_scale[i].reshape(1, d)] + route_w
            gd = d // len(POOL_WINDOWS)
            scratch = [pltpu.VMEM((TM + max(POOL_WINDOWS), d - g * gd), F32)
                       for g in range(len(POOL_WINDOWS))]
            xf, xsl, route, cnt = _mixer_call(_odd_kernel, xf, mod[l], seq, weights, scratch, prev)
        run_tables, ys = _moe(xf, xsl, cnt, l, moe_w_gate, moe_w_up, moe_w_down)
        prev = (run_tables, ys, route, mod[l])
    run_tables, ys, route, mod_l = prev
    return _final(run_tables, ys, xf, route, mod_l, seq, g_final).reshape(bsz, seq, d)
```

```python
import functools

import jax
import jax.numpy as jnp
from jax import lax
from jax.experimental import pallas as pl
from jax.experimental.pallas import tpu as pltpu

F32 = jnp.float32
BF16 = jnp.bfloat16
U32 = jnp.uint32
I32 = jnp.int32

EPS = 1e-6
LANES = 128
SUBLANES = 8
CHUNK = 128
A_HEADS = 8
N_GROUPS = 4
EXPERTS_PER_GROUP = 8
N_EXPERTS = N_GROUPS * EXPERTS_PER_GROUP
POOL_WINDOWS = (2, 4, 8, 16)
CONV_WIDTH = 3
ROUTE_COL0 = N_GROUPS
ROUTE_ROWS = 48

TM = 512
SORT_TM = 256
SECTIONS = TM // SORT_TM
LOCAL_ROWS = 2 * SORT_TM + LANES
LOCAL_PAIRS = LOCAL_ROWS // 2
TILE_PAIRS = SECTIONS * LOCAL_PAIRS
TILE_RUNS = SECTIONS * N_EXPERTS
EXPERT_BLK = 1024
BLK_PAIRS = EXPERT_BLK // 2
BLK_PARTS = 8
FETCH_UNROLL = 8
MOD_TN = 3072
VMEM_LIMIT = 56 * 1024 * 1024


def _rms(x):
    return x * lax.rsqrt(jnp.mean(x * x, axis=-1, keepdims=True) + EPS)


def _dot(a, b):
    return jnp.dot(a, b, preferred_element_type=F32)


def _store_pairs(ref, base, rows_bf16):
    words = pltpu.bitcast(rows_bf16, U32)
    n_pairs, d = words.shape
    slab = d // LANES
    for c in range(slab):
        ref[pl.ds(base + c, n_pairs, stride=slab), :] = words[:, c * LANES:(c + 1) * LANES]


def _load_pairs(ref, base, n_pairs, d):
    slab = d // LANES
    words = jnp.concatenate(
        [ref[pl.ds(base + c, n_pairs, stride=slab), :] for c in range(slab)], axis=-1)
    return pltpu.bitcast(words, BF16)


def _mod_kernel(c_ref, w_ref, b_ref, o_ref):
    c = c_ref[...]
    ca = c * jax.nn.sigmoid(c)
    o_ref[0] = jnp.dot(ca, w_ref[0], precision=lax.Precision.HIGHEST,
                       preferred_element_type=F32) + b_ref[0]


def _modulation(c, w_ada, b_ada):
    depth, d, n = w_ada.shape
    bsz = c.shape[0]
    return pl.pallas_call(
        _mod_kernel,
        grid=(depth, n // MOD_TN),
        in_specs=[
            pl.BlockSpec((bsz, d), lambda l, j: (0, 0)),
            pl.BlockSpec((1, d, MOD_TN), lambda l, j: (l, 0, j)),
            pl.BlockSpec((1, 1, MOD_TN), lambda l, j: (l, 0, j)),
        ],
        out_specs=pl.BlockSpec((1, bsz, MOD_TN), lambda l, j: (l, 0, j)),
        out_shape=jax.ShapeDtypeStruct((depth, bsz, n), F32),
        compiler_params=pltpu.CompilerParams(
            dimension_semantics=("arbitrary", "arbitrary"),
            vmem_limit_bytes=VMEM_LIMIT),
        name="modulation",
    )(c, w_ada, b_ada.reshape(depth, 1, n))


def _route_and_sort(x_new, mod_ref, gffn_ref, rwt_ref, rbc_ref, earlier_ref, lower_ref,
                    xsl_ref, route_ref, cnt_ref):
    tm = x_new.shape[0]
    sh_f = mod_ref[0, 3:4, :]
    sc_f = mod_ref[0, 4:5, :]
    h2 = _rms(x_new) * (gffn_ref[...] * (1.0 + sc_f)) + sh_f

    rr = ROUTE_ROWS
    hh = h2.astype(BF16)
    logits = lax.dot_general(rwt_ref[...], hh, (((1,), (1,)), ((), ())),
                             preferred_element_type=F32) + rbc_ref[...]

    row = lax.broadcasted_iota(I32, logits.shape, 0).astype(F32)
    neg = jnp.float32(-jnp.inf)
    big = jnp.float32(LANES)

    gl = jnp.where(row < N_GROUPS, logits, neg)
    gmax = jnp.max(gl, axis=0, keepdims=True)
    g_sel = jnp.min(jnp.where(gl == gmax, row, big), axis=0, keepdims=True)
    g_w = 1.0 / jnp.sum(jnp.exp(gl - gmax), axis=0, keepdims=True)

    lo = ROUTE_COL0 + EXPERTS_PER_GROUP * g_sel
    el = jnp.where((row >= lo) & (row < lo + EXPERTS_PER_GROUP), logits, neg)
    m1 = jnp.max(el, axis=0, keepdims=True)
    i1 = jnp.min(jnp.where(el == m1, row, big), axis=0, keepdims=True)
    el2 = jnp.where(row == i1, neg, el)
    m2 = jnp.max(el2, axis=0, keepdims=True)
    i2 = jnp.min(jnp.where(el2 == m2, row, big), axis=0, keepdims=True)
    t = jnp.exp(m2 - m1)
    gate1 = g_w / (1.0 + t)
    gate2 = g_w * t / (1.0 + t)

    sections = [slice(s * SORT_TM, (s + 1) * SORT_TM) for s in range(tm // SORT_TM)]
    is1 = row == i1
    is2 = row == i2
    onehot = jnp.where(is1 | is2, 1.0, 0.0)
    before = _dot(onehot.astype(BF16), earlier_ref[...])
    lane = lax.broadcasted_iota(I32, (rr, LANES), 1)
    cnts = [jnp.sum(onehot[:, cols], axis=1, keepdims=True) for cols in sections]
    pairs = jnp.zeros((rr, LANES), F32)
    for s, cnt in enumerate(cnts):
        cnt_ref[s] = jnp.broadcast_to(cnt, (rr, LANES))
        pairs = jnp.where(lane == s, jnp.floor((cnt + 1.0) * 0.5), pairs)
    pairs = jnp.concatenate([pairs, jnp.zeros((LANES - rr, LANES), F32)], axis=0)
    pair_start = jnp.dot(lower_ref[...], pairs, precision=lax.Precision.HIGHEST,
                         preferred_element_type=F32)
    pos = jnp.concatenate([before[:, cols] + 2.0 * pair_start[0:rr, s:s + 1]
                           for s, cols in enumerate(sections)], axis=1)
    pos1 = jnp.sum(jnp.where(is1, pos, 0.0), axis=0, keepdims=True)
    pos2 = jnp.sum(jnp.where(is2, pos, 0.0), axis=0, keepdims=True)

    r8 = lax.broadcasted_iota(I32, (SUBLANES, tm), 0)
    route_t = jnp.where(r8 == 0, pos1, 0.0)
    route_t = jnp.where(r8 == 1, pos2, route_t)
    route_t = jnp.where(r8 == 4, gate1, route_t)
    route_t = jnp.where(r8 == 5, gate2, route_t)
    route_t = jnp.concatenate([route_t, jnp.zeros((LANES - SUBLANES, tm), F32)], axis=0)
    route_ref[...] = route_t.T

    srow = lax.broadcasted_iota(I32, (LOCAL_ROWS, SORT_TM), 0).astype(F32)
    section_rows = LOCAL_PAIRS * (x_new.shape[1] // LANES)
    for s, cols in enumerate(sections):
        perm = jnp.where((srow == pos1[:, cols]) | (srow == pos2[:, cols]), 1.0, 0.0)
        _store_pairs(xsl_ref, s * section_rows, _dot(perm.astype(BF16), hh[cols]).astype(BF16))


def _rows(first_row, n_rows, unit):
    return pl.ds(pl.multiple_of(first_row, unit), n_rows)


def _unpermute(tables, ys_hbm, route_ref, ysl_scr, sem, shape):
    len_ref, loc_ref, glob_ref, tot_ref = tables
    i = pl.program_id(0)
    tm, d = shape
    slab = d // LANES

    def run_copy(tile, glob, loc, rows):
        slot = tile % 2
        return pltpu.make_async_copy(ys_hbm.at[_rows(glob, rows, slab)],
                                     ysl_scr.at[slot, _rows(loc, rows, slab)], sem.at[slot])

    def fetch(tile):
        def body(r, carry):
            k = tile * TILE_RUNS + r
            n = len_ref[k]

            @pl.when(n > 0)
            def _():
                run_copy(tile, glob_ref[k], loc_ref[k], n).start()
            return carry
        lax.fori_loop(0, TILE_RUNS, body, 0, unroll=8)

    @pl.when(i == 0)
    def _():
        ysl_scr[...] = jnp.zeros_like(ysl_scr)
        fetch(i)

    @pl.when(i + 1 < pl.num_programs(0))
    def _():
        fetch(i + 1)

    @pl.when(tot_ref[i] > 0)
    def _():
        run_copy(i, 0, 0, tot_ref[i]).wait()

    ysl_ref = ysl_scr.at[i % 2]
    srow = lax.broadcasted_iota(I32, (SORT_TM, LOCAL_ROWS), 1).astype(F32)
    y_sections = []
    for s in range(tm // SORT_TM):
        rows = slice(s * SORT_TM, (s + 1) * SORT_TM)
        ysl = _load_pairs(ysl_ref, s * LOCAL_PAIRS * slab, LOCAL_PAIRS, d)
        sel = jnp.concatenate(
            [jnp.where(srow == route_ref[rows, k:k + 1], 1.0, 0.0).astype(BF16) for k in range(2)],
            axis=0)
        picked = _dot(sel, ysl)
        y_sections.append(route_ref[rows, 4:5] * picked[0:SORT_TM]
                          + route_ref[rows, 5:6] * picked[SORT_TM:2 * SORT_TM])
    return jnp.concatenate(y_sections, axis=0)


def _split_prev(fused, refs):
    if not fused:
        return None, refs
    tables, (ys_hbm, route_ref, mod_ref), (ysl_scr, sem) = refs[:4], refs[4:7], refs[-2:]
    return (tables, ys_hbm, route_ref, mod_ref, ysl_scr, sem), refs[7:-2]


def _mixer_input(prev, x_ref):
    if prev is None:
        return x_ref[...]
    tables, ys_hbm, route_ref, mod_ref, ysl_scr, sem = prev
    y = _unpermute(tables, ys_hbm, route_ref, ysl_scr, sem, x_ref.shape)
    return x_ref[...] + mod_ref[0, 5:6, :] * y


def _even_kernel(tiles_per_seq, fused, *refs):
    prev, refs = _split_prev(fused, refs)
    (x_ref, mod_ref, gmix_ref, gffn_ref, win_ref, gv_ref, ws_ref, bs_ref,
     cw_ref, wout_ref, rwt_ref, rbc_ref, earlier_ref, lower_ref,
     xo_ref, xsl_ref, route_ref, cnt_ref,
     zc_scr) = refs
    i = pl.program_id(0)
    tm = x_ref.shape[0]
    aw = gv_ref.shape[0]
    hd = aw // A_HEADS
    n_chunks = tm // CHUNK

    @pl.when(i == 0)
    def _():
        zc_scr[...] = jnp.zeros_like(zc_scr)

    x = _mixer_input(prev, x_ref)
    sh_m = mod_ref[0, 0:1, :]
    sc_m = mod_ref[0, 1:2, :]
    g_m = mod_ref[0, 2:3, :]
    h = _rms(x) * (gmix_ref[...] * (1.0 + sc_m)) + sh_m
    z = _dot(h.astype(BF16), win_ref[...])
    u = z[:, 0:aw]
    v = z[:, aw:2 * aw]
    b_gate = z[:, 2 * aw:3 * aw]
    c_gate = z[:, 3 * aw:4 * aw]
    x_in = z[:, 4 * aw:5 * aw]

    v_t = v.T
    row = lax.broadcasted_iota(I32, (CHUNK, CHUNK), 0)
    col = lax.broadcasted_iota(I32, (CHUNK, CHUNK), 1)
    causal = col <= row
    head_rows = []
    for hh in range(A_HEADS):
        vh = v_t[hh * hd:(hh + 1) * hd, :]
        msv = jnp.mean(vh * vh, axis=0, keepdims=True)
        vn = (vh * lax.rsqrt(msv + EPS) * gv_ref[hh * hd:(hh + 1) * hd, :]).astype(BF16)
        lhs = jnp.concatenate(
            [vn[:, c * CHUNK:(c + 1) * CHUNK] for c in range(n_chunks)], axis=0)
        w_m = jnp.where(causal, ws_ref[hh], 0.0).astype(BF16)
        sv_h = lax.dot_general(lhs, w_m, (((1,), (1,)), ((), ())),
                               preferred_element_type=F32)
        sv_h = sv_h + bs_ref[hh]
        head_rows.append(jnp.concatenate(
            [sv_h[c * hd:(c + 1) * hd, :] for c in range(n_chunks)], axis=1))
    sv = jnp.concatenate(head_rows, axis=0).T
    y_a = u * sv

    zc = c_gate * x_in
    first = (i % tiles_per_seq) == 0
    halo = zc_scr[tm:tm + SUBLANES, :]
    zc_scr[0:SUBLANES, :] = jnp.where(first, 0.0, halo)
    zc_scr[SUBLANES:SUBLANES + tm, :] = zc
    conv = cw_ref[2:3, :] * zc
    for k in range(CONV_WIDTH - 1):
        shift = CONV_WIDTH - 1 - k
        conv = conv + cw_ref[k:k + 1, :] * zc_scr[SUBLANES - shift:SUBLANES - shift + tm, :]
    y_b = b_gate * conv

    y = _dot(y_a.astype(BF16), wout_ref[0:aw, :]) + _dot(y_b.astype(BF16), wout_ref[aw:, :])
    x_new = x + g_m * y
    xo_ref[...] = x_new
    _route_and_sort(x_new, mod_ref, gffn_ref, rwt_ref, rbc_ref, earlier_ref, lower_ref,
                    xsl_ref, route_ref, cnt_ref)


def _odd_kernel(tiles_per_seq, fused, *refs):
    prev, refs = _split_prev(fused, refs)
    (x_ref, mod_ref, gmix_ref, gffn_ref, wpool_ref, pscale_ref,
     rwt_ref, rbc_ref, earlier_ref, lower_ref,
     xo_ref, xsl_ref, route_ref, cnt_ref) = refs[:14]
    level_scrs = refs[14:]
    i = pl.program_id(0)
    tm, d = x_ref.shape
    halo_rows = max(POOL_WINDOWS)
    gd = d // len(POOL_WINDOWS)

    @pl.when(i == 0)
    def _():
        for scr in level_scrs:
            scr[...] = jnp.zeros_like(scr)

    x = _mixer_input(prev, x_ref)
    sh_m = mod_ref[0, 0:1, :]
    sc_m = mod_ref[0, 1:2, :]
    g_m = mod_ref[0, 2:3, :]
    h = _rms(x) * (gmix_ref[...] * (1.0 + sc_m)) + sh_m

    tile_in_seq = i % tiles_per_seq
    first = tile_in_seq == 0

    sums = h
    window_sums = []
    for g, scr in enumerate(level_scrs):
        tail = scr[tm:tm + halo_rows, :]
        scr[0:halo_rows, :] = jnp.where(first, 0.0, tail)
        scr[halo_rows:halo_rows + tm, :] = sums
        lag = POOL_WINDOWS[g] // 2
        sums = sums + scr[halo_rows - lag:halo_rows - lag + tm, :]
        window_sums.append(sums[:, 0:gd])
        if g + 1 < len(level_scrs):
            sums = sums[:, gd:]

    pos = (tile_in_seq * tm + lax.broadcasted_iota(I32, (tm, 1), 0)).astype(F32)
    outs = []
    for g, win in enumerate(POOL_WINDOWS):
        cs = slice(g * gd, (g + 1) * gd)
        count = jnp.minimum(pos + 1.0, jnp.float32(win))
        pooled = window_sums[g] / count - h[:, cs]
        outs.append(_dot(pooled.astype(BF16), wpool_ref[g]))
    y = jnp.concatenate(outs, axis=-1) * pscale_ref[...]
    x_new = x + g_m * y
    xo_ref[...] = x_new
    _route_and_sort(x_new, mod_ref, gffn_ref, rwt_ref, rbc_ref, earlier_ref, lower_ref,
                    xsl_ref, route_ref, cnt_ref)


def _mixer_call(kernel_fn, x, mod_l, seq, weights, scratch, prev):
    t_tok, d = x.shape
    n_tiles = t_tok // TM
    tiles_per_seq = seq // TM
    n_slabs = d // LANES

    def const_spec(a):
        return pl.BlockSpec(a.shape, lambda i, *_, nd=a.ndim: (0,) * nd)

    def mod_spec(m):
        return pl.BlockSpec((1,) + m.shape[1:], lambda i, *_: (i // tiles_per_seq, 0, 0))

    tables, prev_inputs, prev_specs, prev_scratch = (), (), [], []
    if prev is not None:
        tables, ys, route_prev, mod_prev = prev
        prev_inputs = (ys, route_prev, mod_prev)
        prev_specs = [pl.BlockSpec(memory_space=pl.ANY),
                      pl.BlockSpec((TM, LANES), lambda i, *_: (i, 0)),
                      mod_spec(mod_prev)]
        prev_scratch = [pltpu.VMEM((2, TILE_PAIRS * n_slabs, LANES), U32),
                        pltpu.SemaphoreType.DMA((2,))]
    in_specs = prev_specs + [
        pl.BlockSpec((TM, d), lambda i, *_: (i, 0)),
        mod_spec(mod_l),
    ] + [const_spec(w) for w in weights]
    out_shape = (
        jax.ShapeDtypeStruct((t_tok, d), F32),
        jax.ShapeDtypeStruct((n_tiles * TILE_PAIRS * n_slabs, LANES), U32),
        jax.ShapeDtypeStruct((t_tok, LANES), F32),
        jax.ShapeDtypeStruct((n_tiles * SECTIONS, ROUTE_ROWS, LANES), F32),
    )
    out_specs = (
        pl.BlockSpec((TM, d), lambda i, *_: (i, 0)),
        pl.BlockSpec((TILE_PAIRS * n_slabs, LANES), lambda i, *_: (i, 0)),
        pl.BlockSpec((TM, LANES), lambda i, *_: (i, 0)),
        pl.BlockSpec((SECTIONS, ROUTE_ROWS, LANES), lambda i, *_: (i, 0, 0)),
    )
    return pl.pallas_call(
        functools.partial(kernel_fn, tiles_per_seq, prev is not None),
        grid_spec=pltpu.PrefetchScalarGridSpec(
            num_scalar_prefetch=len(tables),
            grid=(n_tiles,),
            in_specs=in_specs,
            out_specs=out_specs,
            scratch_shapes=scratch + prev_scratch,
        ),
        out_shape=out_shape,
        compiler_params=pltpu.CompilerParams(
            dimension_semantics=("arbitrary",), vmem_limit_bytes=VMEM_LIMIT),
        name=kernel_fn.__name__.strip("_"),
    )(*tables, *prev_inputs, x, mod_l, *weights)


def _expert_kernel(layer, blk_expert_ref, n_valid_ref, next_ref, slot_ref,
                   blk_first_ref, blk_last_ref, blk_fill_ref, run_len_ref, run_src_ref, run_dst_ref,
                   xsl_hbm, wg_hbm, wu_hbm, wd_hbm, ys_ref,
                   x_scr, zero_scr, wg_stage, wu_stage, wd_stage, wg_scr, wu_scr, wd_scr,
                   x_sem, sem):
    j = pl.program_id(0)
    n_valid = n_valid_ref[0]
    expert = blk_expert_ref[j]
    prev = blk_expert_ref[jnp.maximum(j - 1, 0)]
    d = wg_scr.shape[0]
    slab = d // LANES
    blk_rows = BLK_PAIRS * slab

    def fetch(blk):
        slot = blk % 2
        row0 = blk * blk_rows

        def copy_run(q):
            dst = run_dst_ref[q]
            lo = jnp.maximum(dst, row0)
            hi = jnp.minimum(dst + run_len_ref[q], row0 + blk_rows)

            @pl.when(hi > lo)
            def _():
                pltpu.make_async_copy(
                    xsl_hbm.at[_rows(run_src_ref[q] + (lo - dst), hi - lo, slab)],
                    x_scr.at[slot, _rows(lo - row0, hi - lo, slab)], x_sem.at[slot]).start()

        first = blk_first_ref[blk]

        def body(t, carry):
            for u in range(FETCH_UNROLL):
                copy_run(first + FETCH_UNROLL * t + u)
            return carry

        lax.fori_loop(0, (blk_last_ref[blk] - first + FETCH_UNROLL - 1) // FETCH_UNROLL, body, 0)
        fill = blk_fill_ref[blk]

        @pl.when(fill > 0)
        def _():
            pltpu.make_async_copy(zero_scr.at[_rows(0, fill, slab)],
                                  x_scr.at[slot, _rows(blk_rows - fill, fill, slab)],
                                  x_sem.at[slot]).start()

    @pl.when(j == 0)
    def _():
        zero_scr[...] = jnp.zeros_like(zero_scr)
        fetch(j)

    @pl.when(j + 1 < n_valid)
    def _():
        fetch(j + 1)

    def weight_copies(e, slot):
        return [pltpu.make_async_copy(hbm.at[layer, e], stage.at[slot], sem.at[slot])
                for hbm, stage in ((wg_hbm, wg_stage), (wu_hbm, wu_stage), (wd_hbm, wd_stage))]

    @pl.when(j == 0)
    def _():
        for cp in weight_copies(expert, slot_ref[expert]):
            cp.start()

    @pl.when((j == 0) | (expert != prev))
    def _():
        slot = slot_ref[expert]
        for cp in weight_copies(expert, slot):
            cp.wait()
        nxt = next_ref[expert]

        @pl.when(nxt != expert)
        def _():
            for cp in weight_copies(nxt, 1 - slot):
                cp.start()

        wg_scr[...] = wg_stage[slot].astype(BF16)
        wu_scr[...] = wu_stage[slot].astype(BF16)
        wd_scr[...] = wd_stage[slot].astype(BF16)

    slot = j % 2

    @pl.when(j < n_valid)
    def _():
        pltpu.make_async_copy(zero_scr, x_scr.at[slot], x_sem.at[slot]).wait()

    part_pairs = BLK_PAIRS // BLK_PARTS
    part_rows = part_pairs * slab
    n_parts = jnp.where(j < n_valid,
                        (blk_rows - blk_fill_ref[j] + part_rows - 1) // part_rows, 0)
    for p in range(1, BLK_PARTS + 1):
        @pl.when(n_parts == p)
        def _(p=p):
            xb = _load_pairs(x_scr.at[slot], 0, p * part_pairs, d)
            a = _dot(xb, wg_scr[...])
            b = _dot(xb, wu_scr[...])
            hm = (a * jax.nn.sigmoid(a) * b).astype(BF16)
            _store_pairs(ys_ref, 0, _dot(hm, wd_scr[...]).astype(BF16))
            if p < BLK_PARTS:
                ys_ref[p * part_rows:, :] = jnp.zeros((blk_rows - p * part_rows, LANES), U32)

    @pl.when(n_parts == 0)
    def _():
        ys_ref[...] = jnp.zeros_like(ys_ref)


def _experts(tables, xsl, n_blocks, layer, w_gate, w_up, w_down):
    _, _, d, de = w_gate.shape
    blk_rows = BLK_PAIRS * (d // LANES)
    return pl.pallas_call(
        functools.partial(_expert_kernel, layer),
        grid_spec=pltpu.PrefetchScalarGridSpec(
            num_scalar_prefetch=len(tables),
            grid=(n_blocks,),
            in_specs=[pl.BlockSpec(memory_space=pl.ANY)] * 4,
            out_specs=pl.BlockSpec((blk_rows, LANES), lambda j, *_: (j, 0)),
            scratch_shapes=[
                pltpu.VMEM((2, blk_rows, LANES), U32), pltpu.VMEM((blk_rows, LANES), U32),
                pltpu.VMEM((2, d, de), F32), pltpu.VMEM((2, d, de), F32),
                pltpu.VMEM((2, de, d), F32),
                pltpu.VMEM((d, de), BF16), pltpu.VMEM((d, de), BF16), pltpu.VMEM((de, d), BF16),
                pltpu.SemaphoreType.DMA((2,)), pltpu.SemaphoreType.DMA((2,)),
            ],
        ),
        out_shape=jax.ShapeDtypeStruct((n_blocks * blk_rows, LANES), U32),
        compiler_params=pltpu.CompilerParams(
            dimension_semantics=("arbitrary",), vmem_limit_bytes=VMEM_LIMIT),
        name="experts",
    )(*tables, xsl, w_gate, w_up, w_down)


def _final_kernel(len_ref, loc_ref, glob_ref, tot_ref, ys_hbm, x_ref, route_ref,
                  mod_ref, gfin_ref, xo_ref, ysl_scr, sem):
    prev = ((len_ref, loc_ref, glob_ref, tot_ref), ys_hbm, route_ref, mod_ref, ysl_scr, sem)
    xo_ref[...] = _rms(_mixer_input(prev, x_ref)) * gfin_ref[...]


def _final(tables, ys, x, route, mod_l, seq, g_final):
    t_tok, d = x.shape
    tiles_per_seq = seq // TM
    return pl.pallas_call(
        _final_kernel,
        grid_spec=pltpu.PrefetchScalarGridSpec(
            num_scalar_prefetch=4,
            grid=(t_tok // TM,),
            in_specs=[
                pl.BlockSpec(memory_space=pl.ANY),
                pl.BlockSpec((TM, d), lambda i, *_: (i, 0)),
                pl.BlockSpec((TM, LANES), lambda i, *_: (i, 0)),
                pl.BlockSpec((1,) + mod_l.shape[1:], lambda i, *_: (i // tiles_per_seq, 0, 0)),
                pl.BlockSpec((1, d), lambda i, *_: (0, 0)),
            ],
            out_specs=pl.BlockSpec((TM, d), lambda i, *_: (i, 0)),
            scratch_shapes=[pltpu.VMEM((2, TILE_PAIRS * (d // LANES), LANES), U32),
                            pltpu.SemaphoreType.DMA((2,))],
        ),
        out_shape=jax.ShapeDtypeStruct((t_tok, d), F32),
        compiler_params=pltpu.CompilerParams(
            dimension_semantics=("arbitrary",), vmem_limit_bytes=VMEM_LIMIT),
        name="final",
    )(*tables, ys, x, route, mod_l, g_final)


def _moe(x, xsl, cnt, layer, w_gate, w_up, w_down):
    t_tok = x.shape[0]
    n_tiles = t_tok // TM
    n_blocks = -(-(2 * t_tok + n_tiles * TILE_RUNS) // EXPERT_BLK) + N_EXPERTS

    counts = cnt[:, ROUTE_COL0:ROUTE_COL0 + N_EXPERTS, 0].astype(I32)
    run_len = (counts + 1) // 2
    section_base = (jnp.arange(counts.shape[0], dtype=I32) % SECTIONS) * LOCAL_PAIRS
    run_loc = jnp.cumsum(run_len, axis=1) - run_len + section_base[:, None]
    seg_len = jnp.sum(run_len, axis=0)
    seg_pad = (seg_len + BLK_PAIRS - 1) // BLK_PAIRS * BLK_PAIRS
    seg_end = jnp.cumsum(seg_pad)
    seg_start = seg_end - seg_pad
    run_glob = seg_start[None, :] + jnp.cumsum(run_len, axis=0) - run_len
    tile_tot = jnp.sum(run_len.reshape(n_tiles, TILE_RUNS), axis=1)
    slab = x.shape[1] // LANES
    run_tables = tuple((a.reshape(-1) * slab).astype(I32)
                       for a in (run_len, run_loc, run_glob, tile_tot))
    n_valid = (seg_end[-1:] // BLK_PAIRS).astype(I32)
    blk_pair0 = jnp.arange(n_blocks, dtype=I32) * BLK_PAIRS
    blk_expert = jnp.sum((seg_end[None, :] <= blk_pair0[:, None]).astype(I32), axis=1)
    section_pair0 = jnp.arange(counts.shape[0], dtype=I32) * LOCAL_PAIRS
    run_src = jnp.cumsum(run_len, axis=1) - run_len + section_pair0[:, None]
    em_len, em_src, em_dst = (
        jnp.concatenate([a.T.reshape(-1), jnp.full((FETCH_UNROLL - 1,), end, I32)])
        for a, end in ((run_len, 0), (run_src, 0), (run_glob, n_blocks * BLK_PAIRS)))
    blk_first = jnp.sum(((em_dst + em_len)[None, :] <= blk_pair0[:, None]).astype(I32), axis=1)
    blk_last = jnp.sum((em_dst[None, :] < (blk_pair0 + BLK_PAIRS)[:, None]).astype(I32), axis=1)
    seg_stop = jnp.sum(jnp.where(
        jnp.arange(N_EXPERTS, dtype=I32)[None, :] == blk_expert[:, None],
        (seg_start + seg_len)[None, :], 0), axis=1)
    blk_fill = BLK_PAIRS - jnp.clip(seg_stop - blk_pair0, 0, BLK_PAIRS)
    pull_tables = tuple((a * slab).astype(I32) for a in (blk_fill, em_len, em_src, em_dst))
    experts = jnp.arange(N_EXPERTS, dtype=I32)
    used = seg_len > 0
    last_used = jnp.max(jnp.where(used, experts, 0))
    blk_expert = jnp.minimum(blk_expert, last_used).astype(I32)
    later = jnp.where((experts[None, :] > experts[:, None]) & used[None, :],
                      experts[None, :], N_EXPERTS)
    next_used = jnp.min(later, axis=1)
    next_used = jnp.where(next_used == N_EXPERTS, experts, next_used).astype(I32)
    slot_of = ((jnp.cumsum(used.astype(I32)) - 1) % 2).astype(I32)

    expert_tables = ((blk_expert, n_valid, next_used, slot_of,
                      blk_first.astype(I32), blk_last.astype(I32)) + pull_tables)
    return run_tables, _experts(expert_tables, xsl, n_blocks, layer, w_gate, w_up, w_down)


def kernel(x, c, w_ada, b_ada, norm_mix_g, norm_ffn_g, w_in_even, sgu_norm_g, w_spatial, b_spatial, conv_w, w_out_even, w_pool, pool_scale, w_group_router, b_group_router, w_expert_router, b_expert_router, moe_w_gate, moe_w_up, moe_w_down, final_norm_g):
    bsz, seq, d = x.shape
    depth = w_ada.shape[0]
    t_tok = bsz * seq
    assert seq % TM == 0 and d % LANES == 0 and w_spatial.shape[-1] == CHUNK
    assert all(w == 2 ** (g + 1) for g, w in enumerate(POOL_WINDOWS))

    mod = _modulation(c, w_ada, b_ada).reshape(depth, bsz, 6, d)
    tok = jnp.arange(TM)
    earlier = ((tok[:, None] < tok[None, :])
               & (tok[:, None] // SORT_TM == tok[None, :] // SORT_TM)).astype(BF16)
    lower = jnp.tril(jnp.ones((LANES, LANES), F32), -1)
    g_final = final_norm_g.reshape(1, d)

    xf = x.reshape(t_tok, d)
    prev = None
    for l in range(depth):
        i = l // 2
        rw = jnp.concatenate([w_group_router[l], w_expert_router[l]], axis=1).T
        rw = jnp.pad(rw, ((0, ROUTE_ROWS - rw.shape[0]), (0, 0))).astype(BF16)
        rb = jnp.concatenate([b_group_router[l], b_expert_router[l]])
        rb = jnp.pad(rb, (0, ROUTE_ROWS - rb.shape[0])).reshape(ROUTE_ROWS, 1)
        route_w = [rw, rb, earlier, lower]
        gmix = norm_mix_g[l].reshape(1, d)
        gffn = norm_ffn_g[l].reshape(1, d)
        if l % 2 == 0:
            aw = sgu_norm_g.shape[1]
            weights = [gmix, gffn, w_in_even[i].astype(BF16), sgu_norm_g[i].reshape(aw, 1),
                       w_spatial[i], b_spatial[i].reshape(A_HEADS, 1, CHUNK), conv_w[i],
                       w_out_even[i].astype(BF16)] + route_w
            scratch = [pltpu.VMEM((TM + SUBLANES, conv_w.shape[-1]), F32)]
            xf, xsl, route, cnt = _mixer_call(_even_kernel, xf, mod[l], seq, weights, scratch, prev)
        else:
            weights = [gmix, gffn, w_pool[i].astype(BF16), pool_scale[i].reshape(1, d)] + route_w
            gd = d // len(POOL_WINDOWS)
            scratch = [pltpu.VMEM((TM + max(POOL_WINDOWS), d - g * gd), F32)
                       for g in range(len(POOL_WINDOWS))]
            xf, xsl, route, cnt = _mixer_call(_odd_kernel, xf, mod[l], seq, weights, scratch, prev)
        run_tables, ys = _moe(xf, xsl, cnt, l, moe_w_gate, moe_w_up, moe_w_down)
        prev = (run_tables, ys, route, mod[l])
    run_tables, ys, route, mod_l = prev
    return _final(run_tables, ys, xf, route, mod_l, seq, g_final).reshape(bsz, seq, d)
```

```python
import functools

import jax
import jax.numpy as jnp
from jax import lax
from jax.experimental import pallas as pl
from jax.experimental.pallas import tpu as pltpu

F32 = jnp.float32
BF16 = jnp.bfloat16
U32 = jnp.uint32
I32 = jnp.int32

EPS = 1e-6
LANES = 128
SUBLANES = 8
CHUNK = 128
A_HEADS = 8
N_GROUPS = 4
EXPERTS_PER_GROUP = 8
N_EXPERTS = N_GROUPS * EXPERTS_PER_GROUP
POOL_WINDOWS = (2, 4, 8, 16)
CONV_WIDTH = 3
ROUTE_COL0 = N_GROUPS
ROUTE_ROWS = 48

TM = 512
SORT_TM = 256
SECTIONS = TM // SORT_TM
LOCAL_ROWS = 2 * SORT_TM + LANES
LOCAL_PAIRS = LOCAL_ROWS // 2
TILE_PAIRS = SECTIONS * LOCAL_PAIRS
TILE_RUNS = SECTIONS * N_EXPERTS
EXPERT_BLK = 1024
BLK_PAIRS = EXPERT_BLK // 2
BLK_PARTS = 8
FETCH_UNROLL = 8
MOD_TN = 3072
VMEM_LIMIT = 56 * 1024 * 1024


def _rms(x):
    return x * lax.rsqrt(jnp.mean(x * x, axis=-1, keepdims=True) + EPS)


def _dot(a, b):
    return jnp.dot(a, b, preferred_element_type=F32)


def _store_pairs(ref, base, rows_bf16):
    words = pltpu.bitcast(rows_bf16, U32)
    n_pairs, d = words.shape
    slab = d // LANES
    for c in range(slab):
        ref[pl.ds(base + c, n_pairs, stride=slab), :] = words[:, c * LANES:(c + 1) * LANES]


def _load_pairs(ref, base, n_pairs, d):
    slab = d // LANES
    words = jnp.concatenate(
        [ref[pl.ds(base + c, n_pairs, stride=slab), :] for c in range(slab)], axis=-1)
    return pltpu.bitcast(words, BF16)


def _mod_kernel(c_ref, w_ref, b_ref, o_ref):
    c = c_ref[...]
    ca = c * jax.nn.sigmoid(c)
    o_ref[0] = jnp.dot(ca, w_ref[0], precision=lax.Precision.HIGHEST,
                       preferred_element_type=F32) + b_ref[0]


def _modulation(c, w_ada, b_ada):
    depth, d, n = w_ada.shape
    bsz = c.shape[0]
    return pl.pallas_call(
        _mod_kernel,
        grid=(depth, n // MOD_TN),
        in_specs=[
            pl.BlockSpec((bsz, d), lambda l, j: (0, 0)),
            pl.BlockSpec((1, d, MOD_TN), lambda l, j: (l, 0, j)),
            pl.BlockSpec((1, 1, MOD_TN), lambda l, j: (l, 0, j)),
        ],
        out_specs=pl.BlockSpec((1, bsz, MOD_TN), lambda l, j: (l, 0, j)),
        out_shape=jax.ShapeDtypeStruct((depth, bsz, n), F32),
        compiler_params=pltpu.CompilerParams(
            dimension_semantics=("arbitrary", "arbitrary"),
            vmem_limit_bytes=VMEM_LIMIT),
        name="modulation",
    )(c, w_ada, b_ada.reshape(depth, 1, n))


def _route_and_sort(x_new, mod_ref, gffn_ref, rwt_ref, rbc_ref, earlier_ref, lower_ref,
                    xsl_ref, route_ref, cnt_ref):
    tm = x_new.shape[0]
    sh_f = mod_ref[0, 3:4, :]
    sc_f = mod_ref[0, 4:5, :]
    h2 = _rms(x_new) * (gffn_ref[...] * (1.0 + sc_f)) + sh_f

    rr = ROUTE_ROWS
    hh = h2.astype(BF16)
    logits = lax.dot_general(rwt_ref[...], hh, (((1,), (1,)), ((), ())),
                             preferred_element_type=F32) + rbc_ref[...]

    row = lax.broadcasted_iota(I32, logits.shape, 0).astype(F32)
    neg = jnp.float32(-jnp.inf)
    big = jnp.float32(LANES)

    gl = jnp.where(row < N_GROUPS, logits, neg)
    gmax = jnp.max(gl, axis=0, keepdims=True)
    g_sel = jnp.min(jnp.where(gl == gmax, row, big), axis=0, keepdims=True)
    g_w = 1.0 / jnp.sum(jnp.exp(gl - gmax), axis=0, keepdims=True)

    lo = ROUTE_COL0 + EXPERTS_PER_GROUP * g_sel
    el = jnp.where((row >= lo) & (row < lo + EXPERTS_PER_GROUP), logits, neg)
    m1 = jnp.max(el, axis=0, keepdims=True)
    i1 = jnp.min(jnp.where(el == m1, row, big), axis=0, keepdims=True)
    el2 = jnp.where(row == i1, neg, el)
    m2 = jnp.max(el2, axis=0, keepdims=True)
    i2 = jnp.min(jnp.where(el2 == m2, row, big), axis=0, keepdims=True)
    t = jnp.exp(m2 - m1)
    gate1 = g_w / (1.0 + t)
    gate2 = g_w * t / (1.0 + t)

    sections = [slice(s * SORT_TM, (s + 1) * SORT_TM) for s in range(tm // SORT_TM)]
    is1 = row == i1
    is2 = row == i2
    onehot = jnp.where(is1 | is2, 1.0, 0.0)
    before = _dot(onehot.astype(BF16), earlier_ref[...])
    lane = lax.broadcasted_iota(I32, (rr, LANES), 1)
    cnts = [jnp.sum(onehot[:, cols], axis=1, keepdims=True) for cols in sections]
    pairs = jnp.zeros((rr, LANES), F32)
    for s, cnt in enumerate(cnts):
        cnt_ref[s] = jnp.broadcast_to(cnt, (rr, LANES))
        pairs = jnp.where(lane == s, jnp.floor((cnt + 1.0) * 0.5), pairs)
    pairs = jnp.concatenate([pairs, jnp.zeros((LANES - rr, LANES), F32)], axis=0)
    pair_start = jnp.dot(lower_ref[...], pairs, precision=lax.Precision.HIGHEST,
                         preferred_element_type=F32)
    pos = jnp.concatenate([before[:, cols] + 2.0 * pair_start[0:rr, s:s + 1]
                           for s, cols in enumerate(sections)], axis=1)
    pos1 = jnp.sum(jnp.where(is1, pos, 0.0), axis=0, keepdims=True)
    pos2 = jnp.sum(jnp.where(is2, pos, 0.0), axis=0, keepdims=True)

    r8 = lax.broadcasted_iota(I32, (SUBLANES, tm), 0)
    route_t = jnp.where(r8 == 0, pos1, 0.0)
    route_t = jnp.where(r8 == 1, pos2, route_t)
    route_t = jnp.where(r8 == 4, gate1, route_t)
    route_t = jnp.where(r8 == 5, gate2, route_t)
    route_t = jnp.concatenate([route_t, jnp.zeros((LANES - SUBLANES, tm), F32)], axis=0)
    route_ref[...] = route_t.T

    srow = lax.broadcasted_iota(I32, (LOCAL_ROWS, SORT_TM), 0).astype(F32)
    section_rows = LOCAL_PAIRS * (x_new.shape[1] // LANES)
    for s, cols in enumerate(sections):
        perm = jnp.where((srow == pos1[:, cols]) | (srow == pos2[:, cols]), 1.0, 0.0)
        _store_pairs(xsl_ref, s * section_rows, _dot(perm.astype(BF16), hh[cols]).astype(BF16))


def _rows(first_row, n_rows, unit):
    return pl.ds(pl.multiple_of(first_row, unit), n_rows)


def _unpermute(tables, ys_hbm, route_ref, ysl_scr, sem, shape):
    len_ref, loc_ref, glob_ref, tot_ref = tables
    i = pl.program_id(0)
    tm, d = shape
    slab = d // LANES

    def run_copy(tile, glob, loc, rows):
        slot = tile % 2
        return pltpu.make_async_copy(ys_hbm.at[_rows(glob, rows, slab)],
                                     ysl_scr.at[slot, _rows(loc, rows, slab)], sem.at[slot])

    def fetch(tile):
        def body(r, carry):
            k = tile * TILE_RUNS + r
            n = len_ref[k]

            @pl.when(n > 0)
            def _():
                run_copy(tile, glob_ref[k], loc_ref[k], n).start()
            return carry
        lax.fori_loop(0, TILE_RUNS, body, 0, unroll=16)

    @pl.when(i == 0)
    def _():
        ysl_scr[...] = jnp.zeros_like(ysl_scr)
        fetch(i)

    @pl.when(i + 1 < pl.num_programs(0))
    def _():
        fetch(i + 1)

    @pl.when(tot_ref[i] > 0)
    def _():
        run_copy(i, 0, 0, tot_ref[i]).wait()

    ysl_ref = ysl_scr.at[i % 2]
    srow = lax.broadcasted_iota(I32, (SORT_TM, LOCAL_ROWS), 1).astype(F32)
    y_sections = []
    for s in range(tm // SORT_TM):
        rows = slice(s * SORT_TM, (s + 1) * SORT_TM)
        ysl = _load_pairs(ysl_ref, s * LOCAL_PAIRS * slab, LOCAL_PAIRS, d)
        sel = jnp.concatenate(
            [jnp.where(srow == route_ref[rows, k:k + 1], 1.0, 0.0).astype(BF16) for k in range(2)],
            axis=0)
        picked = _dot(sel, ysl)
        y_sections.append(route_ref[rows, 4:5] * picked[0:SORT_TM]
                          + route_ref[rows, 5:6] * picked[SORT_TM:2 * SORT_TM])
    return jnp.concatenate(y_sections, axis=0)


def _split_prev(fused, refs):
    if not fused:
        return None, refs
    tables, (ys_hbm, route_ref, mod_ref), (ysl_scr, sem) = refs[:4], refs[4:7], refs[-2:]
    return (tables, ys_hbm, route_ref, mod_ref, ysl_scr, sem), refs[7:-2]


def _mixer_input(prev, x_ref):
    if prev is None:
        return x_ref[...]
    tables, ys_hbm, route_ref, mod_ref, ysl_scr, sem = prev
    y = _unpermute(tables, ys_hbm, route_ref, ysl_scr, sem, x_ref.shape)
    return x_ref[...] + mod_ref[0, 5:6, :] * y


def _even_kernel(tiles_per_seq, fused, *refs):
    prev, refs = _split_prev(fused, refs)
    (x_ref, mod_ref, gmix_ref, gffn_ref, win_ref, gv_ref, ws_ref, bs_ref,
     cw_ref, wout_ref, rwt_ref, rbc_ref, earlier_ref, lower_ref,
     xo_ref, xsl_ref, route_ref, cnt_ref,
     zc_scr) = refs
    i = pl.program_id(0)
    tm = x_ref.shape[0]
    aw = gv_ref.shape[0]
    hd = aw // A_HEADS
    n_chunks = tm // CHUNK

    @pl.when(i == 0)
    def _():
        zc_scr[...] = jnp.zeros_like(zc_scr)

    x = _mixer_input(prev, x_ref)
    sh_m = mod_ref[0, 0:1, :]
    sc_m = mod_ref[0, 1:2, :]
    g_m = mod_ref[0, 2:3, :]
    h = _rms(x) * (gmix_ref[...] * (1.0 + sc_m)) + sh_m
    z = _dot(h.astype(BF16), win_ref[...])
    u = z[:, 0:aw]
    v = z[:, aw:2 * aw]
    b_gate = z[:, 2 * aw:3 * aw]
    c_gate = z[:, 3 * aw:4 * aw]
    x_in = z[:, 4 * aw:5 * aw]

    v_t = v.T
    row = lax.broadcasted_iota(I32, (CHUNK, CHUNK), 0)
    col = lax.broadcasted_iota(I32, (CHUNK, CHUNK), 1)
    causal = col <= row
    head_rows = []
    for hh in range(A_HEADS):
        vh = v_t[hh * hd:(hh + 1) * hd, :]
        msv = jnp.mean(vh * vh, axis=0, keepdims=True)
        vn = (vh * lax.rsqrt(msv + EPS) * gv_ref[hh * hd:(hh + 1) * hd, :]).astype(BF16)
        lhs = jnp.concatenate(
            [vn[:, c * CHUNK:(c + 1) * CHUNK] for c in range(n_chunks)], axis=0)
        w_m = jnp.where(causal, ws_ref[hh], 0.0).astype(BF16)
        sv_h = lax.dot_general(lhs, w_m, (((1,), (1,)), ((), ())),
                               preferred_element_type=F32)
        sv_h = sv_h + bs_ref[hh]
        head_rows.append(jnp.concatenate(
            [sv_h[c * hd:(c + 1) * hd, :] for c in range(n_chunks)], axis=1))
    sv = jnp.concatenate(head_rows, axis=0).T
    y_a = u * sv

    zc = c_gate * x_in
    first = (i % tiles_per_seq) == 0
    halo = zc_scr[tm:tm + SUBLANES, :]
    zc_scr[0:SUBLANES, :] = jnp.where(first, 0.0, halo)
    zc_scr[SUBLANES:SUBLANES + tm, :] = zc
    conv = cw_ref[2:3, :] * zc
    for k in range(CONV_WIDTH - 1):
        shift = CONV_WIDTH - 1 - k
        conv = conv + cw_ref[k:k + 1, :] * zc_scr[SUBLANES - shift:SUBLANES - shift + tm, :]
    y_b = b_gate * conv

    y = _dot(y_a.astype(BF16), wout_ref[0:aw, :]) + _dot(y_b.astype(BF16), wout_ref[aw:, :])
    x_new = x + g_m * y
    xo_ref[...] = x_new
    _route_and_sort(x_new, mod_ref, gffn_ref, rwt_ref, rbc_ref, earlier_ref, lower_ref,
                    xsl_ref, route_ref, cnt_ref)


def _odd_kernel(tiles_per_seq, fused, *refs):
    prev, refs = _split_prev(fused, refs)
    (x_ref, mod_ref, gmix_ref, gffn_ref, wpool_ref, pscale_ref,
     rwt_ref, rbc_ref, earlier_ref, lower_ref,
     xo_ref, xsl_ref, route_ref, cnt_ref) = refs[:14]
    level_scrs = refs[14:]
    i = pl.program_id(0)
    tm, d = x_ref.shape
    halo_rows = max(POOL_WINDOWS)
    gd = d // len(POOL_WINDOWS)

    @pl.when(i == 0)
    def _():
        for scr in level_scrs:
            scr[...] = jnp.zeros_like(scr)

    x = _mixer_input(prev, x_ref)
    sh_m = mod_ref[0, 0:1, :]
    sc_m = mod_ref[0, 1:2, :]
    g_m = mod_ref[0, 2:3, :]
    h = _rms(x) * (gmix_ref[...] * (1.0 + sc_m)) + sh_m

    tile_in_seq = i % tiles_per_seq
    first = tile_in_seq == 0

    sums = h
    window_sums = []
    for g, scr in enumerate(level_scrs):
        tail = scr[tm:tm + halo_rows, :]
        scr[0:halo_rows, :] = jnp.where(first, 0.0, tail)
        scr[halo_rows:halo_rows + tm, :] = sums
        lag = POOL_WINDOWS[g] // 2
        sums = sums + scr[halo_rows - lag:halo_rows - lag + tm, :]
        window_sums.append(sums[:, 0:gd])
        if g + 1 < len(level_scrs):
            sums = sums[:, gd:]

    pos = (tile_in_seq * tm + lax.broadcasted_iota(I32, (tm, 1), 0)).astype(F32)
    outs = []
    for g, win in enumerate(POOL_WINDOWS):
        cs = slice(g * gd, (g + 1) * gd)
        count = jnp.minimum(pos + 1.0, jnp.float32(win))
        pooled = window_sums[g] / count - h[:, cs]
        outs.append(_dot(pooled.astype(BF16), wpool_ref[g]))
    y = jnp.concatenate(outs, axis=-1) * pscale_ref[...]
    x_new = x + g_m * y
    xo_ref[...] = x_new
    _route_and_sort(x_new, mod_ref, gffn_ref, rwt_ref, rbc_ref, earlier_ref, lower_ref,
                    xsl_ref, route_ref, cnt_ref)


def _mixer_call(kernel_fn, x, mod_l, seq, weights, scratch, prev):
    t_tok, d = x.shape
    n_tiles = t_tok // TM
    tiles_per_seq = seq // TM
    n_slabs = d // LANES

    def const_spec(a):
        return pl.BlockSpec(a.shape, lambda i, *_, nd=a.ndim: (0,) * nd)

    def mod_spec(m):
        return pl.BlockSpec((1,) + m.shape[1:], lambda i, *_: (i // tiles_per_seq, 0, 0))

    tables, prev_inputs, prev_specs, prev_scratch = (), (), [], []
    if prev is not None:
        tables, ys, route_prev, mod_prev = prev
        prev_inputs = (ys, route_prev, mod_prev)
        prev_specs = [pl.BlockSpec(memory_space=pl.ANY),
                      pl.BlockSpec((TM, LANES), lambda i, *_: (i, 0)),
                      mod_spec(mod_prev)]
        prev_scratch = [pltpu.VMEM((2, TILE_PAIRS * n_slabs, LANES), U32),
                        pltpu.SemaphoreType.DMA((2,))]
    in_specs = prev_specs + [
        pl.BlockSpec((TM, d), lambda i, *_: (i, 0)),
        mod_spec(mod_l),
    ] + [const_spec(w) for w in weights]
    out_shape = (
        jax.ShapeDtypeStruct((t_tok, d), F32),
        jax.ShapeDtypeStruct((n_tiles * TILE_PAIRS * n_slabs, LANES), U32),
        jax.ShapeDtypeStruct((t_tok, LANES), F32),
        jax.ShapeDtypeStruct((n_tiles * SECTIONS, ROUTE_ROWS, LANES), F32),
    )
    out_specs = (
        pl.BlockSpec((TM, d), lambda i, *_: (i, 0)),
        pl.BlockSpec((TILE_PAIRS * n_slabs, LANES), lambda i, *_: (i, 0)),
        pl.BlockSpec((TM, LANES), lambda i, *_: (i, 0)),
        pl.BlockSpec((SECTIONS, ROUTE_ROWS, LANES), lambda i, *_: (i, 0, 0)),
    )
    return pl.pallas_call(
        functools.partial(kernel_fn, tiles_per_seq, prev is not None),
        grid_spec=pltpu.PrefetchScalarGridSpec(
            num_scalar_prefetch=len(tables),
            grid=(n_tiles,),
            in_specs=in_specs,
            out_specs=out_specs,
            scratch_shapes=scratch + prev_scratch,
        ),
        out_shape=out_shape,
        compiler_params=pltpu.CompilerParams(
            dimension_semantics=("arbitrary",), vmem_limit_bytes=VMEM_LIMIT),
        name=kernel_fn.__name__.strip("_"),
    )(*tables, *prev_inputs, x, mod_l, *weights)


def _expert_kernel(layer, blk_expert_ref, n_valid_ref, next_ref, slot_ref,
                   blk_first_ref, blk_last_ref, blk_fill_ref, run_len_ref, run_src_ref, run_dst_ref,
                   xsl_hbm, wg_hbm, wu_hbm, wd_hbm, ys_ref,
                   x_scr, zero_scr, wg_stage, wu_stage, wd_stage, wg_scr, wu_scr, wd_scr,
                   x_sem, sem):
    j = pl.program_id(0)
    n_valid = n_valid_ref[0]
    expert = blk_expert_ref[j]
    prev = blk_expert_ref[jnp.maximum(j - 1, 0)]
    d = wg_scr.shape[0]
    slab = d // LANES
    blk_rows = BLK_PAIRS * slab

    def fetch(blk):
        slot = blk % 2
        row0 = blk * blk_rows

        def copy_run(q):
            dst = run_dst_ref[q]
            lo = jnp.maximum(dst, row0)
            hi = jnp.minimum(dst + run_len_ref[q], row0 + blk_rows)

            @pl.when(hi > lo)
            def _():
                pltpu.make_async_copy(
                    xsl_hbm.at[_rows(run_src_ref[q] + (lo - dst), hi - lo, slab)],
                    x_scr.at[slot, _rows(lo - row0, hi - lo, slab)], x_sem.at[slot]).start()

        first = blk_first_ref[blk]

        def body(t, carry):
            for u in range(FETCH_UNROLL):
                copy_run(first + FETCH_UNROLL * t + u)
            return carry

        lax.fori_loop(0, (blk_last_ref[blk] - first + FETCH_UNROLL - 1) // FETCH_UNROLL, body, 0)
        fill = blk_fill_ref[blk]

        @pl.when(fill > 0)
        def _():
            pltpu.make_async_copy(zero_scr.at[_rows(0, fill, slab)],
                                  x_scr.at[slot, _rows(blk_rows - fill, fill, slab)],
                                  x_sem.at[slot]).start()

    @pl.when(j == 0)
    def _():
        zero_scr[...] = jnp.zeros_like(zero_scr)
        fetch(j)

    @pl.when(j + 1 < n_valid)
    def _():
        fetch(j + 1)

    def weight_copies(e, slot):
        return [pltpu.make_async_copy(hbm.at[layer, e], stage.at[slot], sem.at[slot])
                for hbm, stage in ((wg_hbm, wg_stage), (wu_hbm, wu_stage), (wd_hbm, wd_stage))]

    @pl.when(j == 0)
    def _():
        for cp in weight_copies(expert, slot_ref[expert]):
            cp.start()

    @pl.when((j == 0) | (expert != prev))
    def _():
        slot = slot_ref[expert]
        for cp in weight_copies(expert, slot):
            cp.wait()
        nxt = next_ref[expert]

        @pl.when(nxt != expert)
        def _():
            for cp in weight_copies(nxt, 1 - slot):
                cp.start()

        wg_scr[...] = wg_stage[slot].astype(BF16)
        wu_scr[...] = wu_stage[slot].astype(BF16)
        wd_scr[...] = wd_stage[slot].astype(BF16)

    slot = j % 2

    @pl.when(j < n_valid)
    def _():
        pltpu.make_async_copy(zero_scr, x_scr.at[slot], x_sem.at[slot]).wait()

    part_pairs = BLK_PAIRS // BLK_PARTS
    part_rows = part_pairs * slab
    n_parts = jnp.where(j < n_valid,
                        (blk_rows - blk_fill_ref[j] + part_rows - 1) // part_rows, 0)
    for p in range(1, BLK_PARTS + 1):
        @pl.when(n_parts == p)
        def _(p=p):
            xb = _load_pairs(x_scr.at[slot], 0, p * part_pairs, d)
            a = _dot(xb, wg_scr[...])
            b = _dot(xb, wu_scr[...])
            hm = (a * jax.nn.sigmoid(a) * b).astype(BF16)
            _store_pairs(ys_ref, 0, _dot(hm, wd_scr[...]).astype(BF16))
            if p < BLK_PARTS:
                ys_ref[p * part_rows:, :] = jnp.zeros((blk_rows - p * part_rows, LANES), U32)

    @pl.when(n_parts == 0)
    def _():
        ys_ref[...] = jnp.zeros_like(ys_ref)


def _experts(tables, xsl, n_blocks, layer, w_gate, w_up, w_down):
    _, _, d, de = w_gate.shape
    blk_rows = BLK_PAIRS * (d // LANES)
    return pl.pallas_call(
        functools.partial(_expert_kernel, layer),
        grid_spec=pltpu.PrefetchScalarGridSpec(
            num_scalar_prefetch=len(tables),
            grid=(n_blocks,),
            in_specs=[pl.BlockSpec(memory_space=pl.ANY)] * 4,
            out_specs=pl.BlockSpec((blk_rows, LANES), lambda j, *_: (j, 0)),
            scratch_shapes=[
                pltpu.VMEM((2, blk_rows, LANES), U32), pltpu.VMEM((blk_rows, LANES), U32),
                pltpu.VMEM((2, d, de), F32), pltpu.VMEM((2, d, de), F32),
                pltpu.VMEM((2, de, d), F32),
                pltpu.VMEM((d, de), BF16), pltpu.VMEM((d, de), BF16), pltpu.VMEM((de, d), BF16),
                pltpu.SemaphoreType.DMA((2,)), pltpu.SemaphoreType.DMA((2,)),
            ],
        ),
        out_shape=jax.ShapeDtypeStruct((n_blocks * blk_rows, LANES), U32),
        compiler_params=pltpu.CompilerParams(
            dimension_semantics=("arbitrary",), vmem_limit_bytes=VMEM_LIMIT),
        name="experts",
    )(*tables, xsl, w_gate, w_up, w_down)


def _final_kernel(len_ref, loc_ref, glob_ref, tot_ref, ys_hbm, x_ref, route_ref,
                  mod_ref, gfin_ref, xo_ref, ysl_scr, sem):
    prev = ((len_ref, loc_ref, glob_ref, tot_ref), ys_hbm, route_ref, mod_ref, ysl_scr, sem)
    xo_ref[...] = _rms(_mixer_input(prev, x_ref)) * gfin_ref[...]


def _final(tables, ys, x, route, mod_l, seq, g_final):
    t_tok, d = x.shape
    tiles_per_seq = seq // TM
    return pl.pallas_call(
        _final_kernel,
        grid_spec=pltpu.PrefetchScalarGridSpec(
            num_scalar_prefetch=4,
            grid=(t_tok // TM,),
            in_specs=[
                pl.BlockSpec(memory_space=pl.ANY),
                pl.BlockSpec((TM, d), lambda i, *_: (i, 0)),
                pl.BlockSpec((TM, LANES), lambda i, *_: (i, 0)),
                pl.BlockSpec((1,) + mod_l.shape[1:], lambda i, *_: (i // tiles_per_seq, 0, 0)),
                pl.BlockSpec((1, d), lambda i, *_: (0, 0)),
            ],
            out_specs=pl.BlockSpec((TM, d), lambda i, *_: (i, 0)),
            scratch_shapes=[pltpu.VMEM((2, TILE_PAIRS * (d // LANES), LANES), U32),
                            pltpu.SemaphoreType.DMA((2,))],
        ),
        out_shape=jax.ShapeDtypeStruct((t_tok, d), F32),
        compiler_params=pltpu.CompilerParams(
            dimension_semantics=("arbitrary",), vmem_limit_bytes=VMEM_LIMIT),
        name="final",
    )(*tables, ys, x, route, mod_l, g_final)


def _moe(x, xsl, cnt, layer, w_gate, w_up, w_down):
    t_tok = x.shape[0]
    n_tiles = t_tok // TM
    n_blocks = -(-(2 * t_tok + n_tiles * TILE_RUNS) // EXPERT_BLK) + N_EXPERTS

    counts = cnt[:, ROUTE_COL0:ROUTE_COL0 + N_EXPERTS, 0].astype(I32)
    run_len = (counts + 1) // 2
    section_base = (jnp.arange(counts.shape[0], dtype=I32) % SECTIONS) * LOCAL_PAIRS
    run_loc = jnp.cumsum(run_len, axis=1) - run_len + section_base[:, None]
    seg_len = jnp.sum(run_len, axis=0)
    seg_pad = (seg_len + BLK_PAIRS - 1) // BLK_PAIRS * BLK_PAIRS
    seg_end = jnp.cumsum(seg_pad)
    seg_start = seg_end - seg_pad
    run_glob = seg_start[None, :] + jnp.cumsum(run_len, axis=0) - run_len
    tile_tot = jnp.sum(run_len.reshape(n_tiles, TILE_RUNS), axis=1)
    slab = x.shape[1] // LANES
    run_tables = tuple((a.reshape(-1) * slab).astype(I32)
                       for a in (run_len, run_loc, run_glob, tile_tot))
    n_valid = (seg_end[-1:] // BLK_PAIRS).astype(I32)
    blk_pair0 = jnp.arange(n_blocks, dtype=I32) * BLK_PAIRS
    blk_expert = jnp.sum((seg_end[None, :] <= blk_pair0[:, None]).astype(I32), axis=1)
    section_pair0 = jnp.arange(counts.shape[0], dtype=I32) * LOCAL_PAIRS
    run_src = jnp.cumsum(run_len, axis=1) - run_len + section_pair0[:, None]
    em_len, em_src, em_dst = (
        jnp.concatenate([a.T.reshape(-1), jnp.full((FETCH_UNROLL - 1,), end, I32)])
        for a, end in ((run_len, 0), (run_src, 0), (run_glob, n_blocks * BLK_PAIRS)))
    blk_first = jnp.sum(((em_dst + em_len)[None, :] <= blk_pair0[:, None]).astype(I32), axis=1)
    blk_last = jnp.sum((em_dst[None, :] < (blk_pair0 + BLK_PAIRS)[:, None]).astype(I32), axis=1)
    seg_stop = jnp.sum(jnp.where(
        jnp.arange(N_EXPERTS, dtype=I32)[None, :] == blk_expert[:, None],
        (seg_start + seg_len)[None, :], 0), axis=1)
    blk_fill = BLK_PAIRS - jnp.clip(seg_stop - blk_pair0, 0, BLK_PAIRS)
    pull_tables = tuple((a * slab).astype(I32) for a in (blk_fill, em_len, em_src, em_dst))
    experts = jnp.arange(N_EXPERTS, dtype=I32)
    used = seg_len > 0
    last_used = jnp.max(jnp.where(used, experts, 0))
    blk_expert = jnp.minimum(blk_expert, last_used).astype(I32)
    later = jnp.where((experts[None, :] > experts[:, None]) & used[None, :],
                      experts[None, :], N_EXPERTS)
    next_used = jnp.min(later, axis=1)
    next_used = jnp.where(next_used == N_EXPERTS, experts, next_used).astype(I32)
    slot_of = ((jnp.cumsum(used.astype(I32)) - 1) % 2).astype(I32)

    expert_tables = ((blk_expert, n_valid, next_used, slot_of,
                      blk_first.astype(I32), blk_last.astype(I32)) + pull_tables)
    return run_tables, _experts(expert_tables, xsl, n_blocks, layer, w_gate, w_up, w_down)


def kernel(x, c, w_ada, b_ada, norm_mix_g, norm_ffn_g, w_in_even, sgu_norm_g, w_spatial, b_spatial, conv_w, w_out_even, w_pool, pool_scale, w_group_router, b_group_router, w_expert_router, b_expert_router, moe_w_gate, moe_w_up, moe_w_down, final_norm_g):
    bsz, seq, d = x.shape
    depth = w_ada.shape[0]
    t_tok = bsz * seq
    assert seq % TM == 0 and d % LANES == 0 and w_spatial.shape[-1] == CHUNK
    assert all(w == 2 ** (g + 1) for g, w in enumerate(POOL_WINDOWS))

    mod = _modulation(c, w_ada, b_ada).reshape(depth, bsz, 6, d)
    tok = jnp.arange(TM)
    earlier = ((tok[:, None] < tok[None, :])
               & (tok[:, None] // SORT_TM == tok[None, :] // SORT_TM)).astype(BF16)
    lower = jnp.tril(jnp.ones((LANES, LANES), F32), -1)
    g_final = final_norm_g.reshape(1, d)

    xf = x.reshape(t_tok, d)
    prev = None
    for l in range(depth):
        i = l // 2
        rw = jnp.concatenate([w_group_router[l], w_expert_router[l]], axis=1).T
        rw = jnp.pad(rw, ((0, ROUTE_ROWS - rw.shape[0]), (0, 0))).astype(BF16)
        rb = jnp.concatenate([b_group_router[l], b_expert_router[l]])
        rb = jnp.pad(rb, (0, ROUTE_ROWS - rb.shape[0])).reshape(ROUTE_ROWS, 1)
        route_w = [rw, rb, earlier, lower]
        gmix = norm_mix_g[l].reshape(1, d)
        gffn = norm_ffn_g[l].reshape(1, d)
        if l % 2 == 0:
            aw = sgu_norm_g.shape[1]
            weights = [gmix, gffn, w_in_even[i].astype(BF16), sgu_norm_g[i].reshape(aw, 1),
                       w_spatial[i], b_spatial[i].reshape(A_HEADS, 1, CHUNK), conv_w[i],
                       w_out_even[i].astype(BF16)] + route_w
            scratch = [pltpu.VMEM((TM + SUBLANES, conv_w.shape[-1]), F32)]
            xf, xsl, route, cnt = _mixer_call(_even_kernel, xf, mod[l], seq, weights, scratch, prev)
        else:
            weights = [gmix, gffn, w_pool[i].astype(BF16), pool_scale[i].reshape(1, d)] + route_w
            gd = d // len(POOL_WINDOWS)
            scratch = [pltpu.VMEM((TM + max(POOL_WINDOWS), d - g * gd), F32)
                       for g in range(len(POOL_WINDOWS))]
            xf, xsl, route, cnt = _mixer_call(_odd_kernel, xf, mod[l], seq, weights, scratch, prev)
        run_tables, ys = _moe(xf, xsl, cnt, l, moe_w_gate, moe_w_up, moe_w_down)
        prev = (run_tables, ys, route, mod[l])
    run_tables, ys, route, mod_l = prev
    return _final(run_tables, ys, xf, route, mod_l, seq, g_final).reshape(bsz, seq, d)
```
